```python
import jax, jax.numpy as jnp
from jax import lax
import numpy as np

D_MODEL = 1024
BATCH = 8
SEQ = 16384
DEPTH = 2

D_MIX = D_MODEL
N_GROUPS = 4
D_GROUP = D_MIX // N_GROUPS
HEAD_DIM = 64
N_HEADS_GROUP = D_GROUP // HEAD_DIM
CHUNK = 128
Q_BLOCK = 128
SHORT_CONV_W = 3
CONF_CONV_W = 31
D_FF = 4 * D_MODEL
EPS = 1e-6
D_IN_PROJ = 10 * D_GROUP
SPLIT_POINTS = (2 * D_GROUP, 5 * D_GROUP, 8 * D_GROUP)

kernel_name = 'hymba_parallel_gmlp_shortconv_stickbreak_conformer'


def rms_norm(x, g=None):
    xf = x.astype(jnp.float32)
    y = xf * lax.rsqrt(jnp.mean(xf * xf, axis=-1, keepdims=True) + EPS)
    if g is not None:
        y = y * g.astype(jnp.float32)
    return y.astype(x.dtype)


def layer_norm(x, g, b):
    xf = x.astype(jnp.float32)
    xc = xf - jnp.mean(xf, axis=-1, keepdims=True)
    var = jnp.mean(xc * xc, axis=-1, keepdims=True)
    y = xc * lax.rsqrt(var + EPS) * g.astype(jnp.float32) + b.astype(jnp.float32)
    return y.astype(x.dtype)


def causal_depthwise_conv(x, w):
    k_w, c = w.shape
    return lax.conv_general_dilated(
        x, w[:, None, :].astype(x.dtype), window_strides=(1,), padding=[(k_w - 1, 0)],
        dimension_numbers=('NWC', 'WIO', 'NWC'), feature_group_count=c)


def spatial_gating_mixer(z, v_gain, w_s, b_s):
    z = jax.nn.gelu(z)
    u, v = jnp.split(z, 2, axis=-1)
    v = rms_norm(v, v_gain)
    b, s, _ = v.shape
    v = v.reshape(b, s // CHUNK, CHUNK, N_HEADS_GROUP, HEAD_DIM)
    mask = jnp.tril(jnp.ones((CHUNK, CHUNK), dtype=bool))
    w = jnp.where(mask, w_s, jnp.zeros_like(w_s))
    f = jnp.einsum('hts,bnshd->bnthd', w, v) + b_s.T[:, :, None]
    return u * f.reshape(b, s, D_GROUP)


def short_conv_mixer(z, w_conv):
    gate_b, gate_c, h = jnp.split(z, 3, axis=-1)
    return gate_b * causal_depthwise_conv(gate_c * h, w_conv)


def stick_breaking_attention(z):
    q, k, v = jnp.split(z, 3, axis=-1)
    b, s, _ = q.shape
    q = q.reshape(b, s, N_HEADS_GROUP, HEAD_DIM)
    k = k.reshape(b, s, N_HEADS_GROUP, HEAD_DIM)
    v = v.reshape(b, s, N_HEADS_GROUP, HEAD_DIM)
    scale = HEAD_DIM ** -0.5
    n_blocks = s // Q_BLOCK
    q_blocks = q.reshape(b, n_blocks, Q_BLOCK, N_HEADS_GROUP, HEAD_DIM).transpose(1, 0, 2, 3, 4)
    key_pos = jnp.arange(s)

    def one_block(args):
        q_blk, blk_idx = args
        logits = jnp.einsum('bqhd,bkhd->bhqk', q_blk, k,
                            preferred_element_type=jnp.float32) * scale
        q_pos = blk_idx * Q_BLOCK + jnp.arange(Q_BLOCK)
        causal = key_pos[None, :] < q_pos[:, None]
        log_beta = jax.nn.log_sigmoid(logits)
        log_one_minus = jnp.where(causal, jax.nn.log_sigmoid(-logits), 0.0)
        log_stick = lax.cumsum(log_one_minus, axis=3, reverse=True) - log_one_minus
        weights = jnp.where(causal, jnp.exp(log_beta + log_stick), 0.0)
        return jnp.einsum('bhqk,bkhd->bqhd', weights.astype(v.dtype), v)

    out = lax.map(one_block, (q_blocks, jnp.arange(n_blocks)))
    return out.transpose(1, 0, 2, 3, 4).reshape(b, s, D_GROUP)


def conformer_conv_mixer(z, w_conv, ln_g, ln_b):
    a, g = jnp.split(z, 2, axis=-1)
    h = a * jax.nn.sigmoid(g)
    h = causal_depthwise_conv(h, w_conv)
    h = layer_norm(h, ln_g, ln_b)
    return jax.nn.silu(h)


def _fwd_setup_inputs(seed: int = 0) -> dict:
    key = jax.random.key(seed)
    ks = jax.random.split(key, 17)
    nrm = jax.random.normal
    x = nrm(ks[0], (BATCH, SEQ, D_MODEL), jnp.float32)
    norm_mix_g = 1.0 + 0.05 * nrm(ks[1], (DEPTH, D_MODEL), jnp.float32)
    w_in = nrm(ks[2], (DEPTH, D_MODEL, D_IN_PROJ), jnp.float32) * D_MODEL ** -0.5
    gmlp_v_g = 1.0 + 0.05 * nrm(ks[3], (DEPTH, D_GROUP), jnp.float32)
    gmlp_w_s = nrm(ks[4], (DEPTH, N_HEADS_GROUP, CHUNK, CHUNK), jnp.float32) * CHUNK ** -0.5
    gmlp_b_s = 1.0 + 0.05 * nrm(ks[5], (DEPTH, N_HEADS_GROUP, CHUNK), jnp.float32)
    short_conv_w = nrm(ks[6], (DEPTH, SHORT_CONV_W, D_GROUP), jnp.float32) * SHORT_CONV_W ** -0.5
    conf_conv_w = nrm(ks[7], (DEPTH, CONF_CONV_W, D_GROUP), jnp.float32) * CONF_CONV_W ** -0.5
    conf_ln_g = 1.0 + 0.05 * nrm(ks[8], (DEPTH, D_GROUP), jnp.float32)
    conf_ln_b = 0.02 * nrm(ks[9], (DEPTH, D_GROUP), jnp.float32)
    mix_out_g = 1.0 + 0.05 * nrm(ks[10], (DEPTH, D_MIX), jnp.float32)
    w_out = nrm(ks[11], (DEPTH, D_MIX, D_MODEL), jnp.float32) * D_MIX ** -0.5
    norm_ffn_g = 1.0 + 0.05 * nrm(ks[12], (DEPTH, D_MODEL), jnp.float32)
    w_up = nrm(ks[13], (DEPTH, D_MODEL, D_FF), jnp.float32) * D_MODEL ** -0.5
    w_down = nrm(ks[14], (DEPTH, D_FF, D_MODEL), jnp.float32) * D_FF ** -0.5
    final_norm_g = 1.0 + 0.05 * nrm(ks[15], (D_MODEL,), jnp.float32)
    return {'x': x, 'norm_mix_g': norm_mix_g, 'w_in': w_in, 'gmlp_v_g': gmlp_v_g,
            'gmlp_w_s': gmlp_w_s, 'gmlp_b_s': gmlp_b_s, 'short_conv_w': short_conv_w,
            'conf_conv_w': conf_conv_w, 'conf_ln_g': conf_ln_g, 'conf_ln_b': conf_ln_b,
            'mix_out_g': mix_out_g, 'w_out': w_out, 'norm_ffn_g': norm_ffn_g,
            'w_up': w_up, 'w_down': w_down, 'final_norm_g': final_norm_g}


def _fwd_reference(x, norm_mix_g, w_in, gmlp_v_g, gmlp_w_s, gmlp_b_s, short_conv_w, conf_conv_w,
              conf_ln_g, conf_ln_b, mix_out_g, w_out, norm_ffn_g, w_up, w_down, final_norm_g):
    for l in range(DEPTH):
        h = rms_norm(x, norm_mix_g[l])
        z = jnp.einsum('bsd,de->bse', h, w_in[l])
        z_a, z_b, z_c, z_d = jnp.split(z, SPLIT_POINTS, axis=-1)
        y_a = spatial_gating_mixer(z_a, gmlp_v_g[l], gmlp_w_s[l], gmlp_b_s[l])
        y_b = short_conv_mixer(z_b, short_conv_w[l])
        y_c = stick_breaking_attention(z_c)
        y_d = conformer_conv_mixer(z_d, conf_conv_w[l], conf_ln_g[l], conf_ln_b[l])
        y = jnp.concatenate([rms_norm(y_a), rms_norm(y_b), rms_norm(y_c), rms_norm(y_d)],
                            axis=-1) * mix_out_g[l]
        x = x + jnp.einsum('bse,ed->bsd', y, w_out[l])
        h = rms_norm(x, norm_ffn_g[l])
        a = jax.nn.relu(jnp.einsum('bsd,df->bsf', h, w_up[l]))
        x = x + jnp.einsum('bsf,fd->bsd', a * a, w_down[l])
    return rms_norm(x, final_norm_g)


import jax as _jax
import jax.numpy as _jnp

TWIN_FORMAT = 'train_step'
FWD_PARAMS = ['x', 'norm_mix_g', 'w_in', 'gmlp_v_g', 'gmlp_w_s', 'gmlp_b_s', 'short_conv_w', 'conf_conv_w', 'conf_ln_g', 'conf_ln_b', 'mix_out_g', 'w_out', 'norm_ffn_g', 'w_up', 'w_down', 'final_norm_g']
TWIN_WEIGHTS = ['norm_mix_g', 'w_in', 'gmlp_v_g', 'gmlp_w_s', 'gmlp_b_s', 'short_conv_w', 'conf_conv_w', 'conf_ln_g', 'conf_ln_b', 'mix_out_g', 'w_out', 'norm_ffn_g', 'w_up', 'w_down', 'final_norm_g']
TWIN_DIFF_INPUT = 'x'
TWIN_INPUTS = ['x', 'norm_mix_g', 'w_in', 'gmlp_v_g', 'gmlp_w_s', 'gmlp_b_s', 'short_conv_w', 'conf_conv_w', 'conf_ln_g', 'conf_ln_b', 'mix_out_g', 'w_out', 'norm_ffn_g', 'w_up', 'w_down', 'final_norm_g', 'loss_target', 'm_norm_mix_g', 'm_w_in', 'm_gmlp_v_g', 'm_gmlp_w_s', 'm_gmlp_b_s', 'm_short_conv_w', 'm_conf_conv_w', 'm_conf_ln_g', 'm_conf_ln_b', 'm_mix_out_g', 'm_w_out', 'm_norm_ffn_g', 'm_w_up', 'm_w_down', 'm_final_norm_g', 'v_norm_mix_g', 'v_w_in', 'v_gmlp_v_g', 'v_gmlp_w_s', 'v_gmlp_b_s', 'v_short_conv_w', 'v_conf_conv_w', 'v_conf_ln_g', 'v_conf_ln_b', 'v_mix_out_g', 'v_w_out', 'v_norm_ffn_g', 'v_w_up', 'v_w_down', 'v_final_norm_g']
TWIN_OUTPUTS = ['loss', 'grad_x', 'grad_norm_mix_g', 'grad_w_in', 'grad_gmlp_v_g', 'grad_gmlp_w_s', 'grad_gmlp_b_s', 'grad_short_conv_w', 'grad_conf_conv_w', 'grad_conf_ln_g', 'grad_conf_ln_b', 'grad_mix_out_g', 'grad_w_out', 'grad_norm_ffn_g', 'grad_w_up', 'grad_w_down', 'grad_final_norm_g', 'delta_norm_mix_g', 'delta_w_in', 'delta_gmlp_v_g', 'delta_gmlp_w_s', 'delta_gmlp_b_s', 'delta_short_conv_w', 'delta_conf_conv_w', 'delta_conf_ln_g', 'delta_conf_ln_b', 'delta_mix_out_g', 'delta_w_out', 'delta_norm_ffn_g', 'delta_w_up', 'delta_w_down', 'delta_final_norm_g', 'new_m_norm_mix_g', 'new_m_w_in', 'new_m_gmlp_v_g', 'new_m_gmlp_w_s', 'new_m_gmlp_b_s', 'new_m_short_conv_w', 'new_m_conf_conv_w', 'new_m_conf_ln_g', 'new_m_conf_ln_b', 'new_m_mix_out_g', 'new_m_w_out', 'new_m_norm_ffn_g', 'new_m_w_up', 'new_m_w_down', 'new_m_final_norm_g', 'new_v_norm_mix_g', 'new_v_w_in', 'new_v_gmlp_v_g', 'new_v_gmlp_w_s', 'new_v_gmlp_b_s', 'new_v_short_conv_w', 'new_v_conf_conv_w', 'new_v_conf_ln_g', 'new_v_conf_ln_b', 'new_v_mix_out_g', 'new_v_w_out', 'new_v_norm_ffn_g', 'new_v_w_up', 'new_v_w_down', 'new_v_final_norm_g']
TWIN_LEAF_KINDS = {'loss': 'loss', 'grad_x': 'grad_x', 'grad_norm_mix_g': 'grad_w', 'grad_w_in': 'grad_w', 'grad_gmlp_v_g': 'grad_w', 'grad_gmlp_w_s': 'grad_w', 'grad_gmlp_b_s': 'grad_w', 'grad_short_conv_w': 'grad_w', 'grad_conf_conv_w': 'grad_w', 'grad_conf_ln_g': 'grad_w', 'grad_conf_ln_b': 'grad_w', 'grad_mix_out_g': 'grad_w', 'grad_w_out': 'grad_w', 'grad_norm_ffn_g': 'grad_w', 'grad_w_up': 'grad_w', 'grad_w_down': 'grad_w', 'grad_final_norm_g': 'grad_w', 'delta_norm_mix_g': 'delta_w', 'delta_w_in': 'delta_w', 'delta_gmlp_v_g': 'delta_w', 'delta_gmlp_w_s': 'delta_w', 'delta_gmlp_b_s': 'delta_w', 'delta_short_conv_w': 'delta_w', 'delta_conf_conv_w': 'delta_w', 'delta_conf_ln_g': 'delta_w', 'delta_conf_ln_b': 'delta_w', 'delta_mix_out_g': 'delta_w', 'delta_w_out': 'delta_w', 'delta_norm_ffn_g': 'delta_w', 'delta_w_up': 'delta_w', 'delta_w_down': 'delta_w', 'delta_final_norm_g': 'delta_w', 'new_m_norm_mix_g': 'new_m', 'new_m_w_in': 'new_m', 'new_m_gmlp_v_g': 'new_m', 'new_m_gmlp_w_s': 'new_m', 'new_m_gmlp_b_s': 'new_m', 'new_m_short_conv_w': 'new_m', 'new_m_conf_conv_w': 'new_m', 'new_m_conf_ln_g': 'new_m', 'new_m_conf_ln_b': 'new_m', 'new_m_mix_out_g': 'new_m', 'new_m_w_out': 'new_m', 'new_m_norm_ffn_g': 'new_m', 'new_m_w_up': 'new_m', 'new_m_w_down': 'new_m', 'new_m_final_norm_g': 'new_m', 'new_v_norm_mix_g': 'new_v', 'new_v_w_in': 'new_v', 'new_v_gmlp_v_g': 'new_v', 'new_v_gmlp_w_s': 'new_v', 'new_v_gmlp_b_s': 'new_v', 'new_v_short_conv_w': 'new_v', 'new_v_conf_conv_w': 'new_v', 'new_v_conf_ln_g': 'new_v', 'new_v_conf_ln_b': 'new_v', 'new_v_mix_out_g': 'new_v', 'new_v_w_out': 'new_v', 'new_v_norm_ffn_g': 'new_v', 'new_v_w_up': 'new_v', 'new_v_w_down': 'new_v', 'new_v_final_norm_g': 'new_v'}


def _forward(args):
    return _fwd_reference(*[args[k] for k in FWD_PARAMS])


def _output_shape():
    def fwd():
        inp = _fwd_setup_inputs(0)
        return _fwd_reference(*[inp[k] for k in FWD_PARAMS])
    out = _jax.eval_shape(fwd)
    return out.shape, out.dtype

N_MICROBATCH = 1
ADAM_LR = 0.001
ADAM_B1 = 0.9
ADAM_B2 = 0.999
ADAM_EPS = 1e-08
ADAM_WD = 0.01
ADAM_STEP = 10
PER_EXAMPLE_BATCH_AXIS = {'x': 0, 'loss_target': 0}
SHARED_INPUTS = []
_WEIGHT_DTYPES = {'norm_mix_g': _jnp.float32, 'w_in': _jnp.float32, 'gmlp_v_g': _jnp.float32, 'gmlp_w_s': _jnp.float32, 'gmlp_b_s': _jnp.float32, 'short_conv_w': _jnp.float32, 'conf_conv_w': _jnp.float32, 'conf_ln_g': _jnp.float32, 'conf_ln_b': _jnp.float32, 'mix_out_g': _jnp.float32, 'w_out': _jnp.float32, 'norm_ffn_g': _jnp.float32, 'w_up': _jnp.float32, 'w_down': _jnp.float32, 'final_norm_g': _jnp.float32}
MOMENT_SCALE = {'norm_mix_g': 4.030136e-01, 'w_in': 2.363072e-01, 'gmlp_v_g': 1.400555e-01, 'gmlp_w_s': 9.925765e-02, 'gmlp_b_s': 1.350072e-01, 'short_conv_w': 2.660202e-01, 'conf_conv_w': 3.087387e-01, 'conf_ln_g': 7.677876e-01, 'conf_ln_b': 1.275152e+00, 'mix_out_g': 4.149816e-01, 'w_out': 4.042979e-01, 'norm_ffn_g': 3.271408e-01, 'w_up': 1.648837e-01, 'w_down': 6.808766e-01, 'final_norm_g': 1.298893e+02}


def _to_microbatches(a, axis):
    t = _jnp.moveaxis(a, axis, 0)
    t = t.reshape((N_MICROBATCH, t.shape[0] // N_MICROBATCH) + t.shape[1:])
    return _jnp.moveaxis(t, 1, axis + 1)


def setup_inputs(seed: int = 0) -> dict:
    inp = _fwd_setup_inputs(seed)
    key = _jax.random.fold_in(_jax.random.key(seed), 7919)
    shape, _ = _output_shape()
    out = dict(inp)
    out["loss_target"] = _jax.random.normal(_jax.random.fold_in(key, 0), shape, _jnp.float32)
    for i, name in enumerate(TWIN_WEIGHTS):
        w = inp[name].astype(_jnp.float32)
        if MOMENT_SCALE is None:
            s = _jnp.sqrt(_jnp.mean(_jnp.square(w)) + 1e-30)
        else:
            s = MOMENT_SCALE[name]
        km, kv = _jax.random.split(_jax.random.fold_in(key, i + 1))
        out[name] = w
        out["m_" + name] = s * _jax.random.normal(km, w.shape, _jnp.float32)
        out["v_" + name] = (s * s) * _jax.random.uniform(kv, w.shape, _jnp.float32, 0.5, 1.5)
    if N_MICROBATCH > 1:
        for name, axis in PER_EXAMPLE_BATCH_AXIS.items():
            out[name] = _to_microbatches(out[name], axis)
    return {'x': out['x'], 'norm_mix_g': out['norm_mix_g'], 'w_in': out['w_in'], 'gmlp_v_g': out['gmlp_v_g'], 'gmlp_w_s': out['gmlp_w_s'], 'gmlp_b_s': out['gmlp_b_s'], 'short_conv_w': out['short_conv_w'], 'conf_conv_w': out['conf_conv_w'], 'conf_ln_g': out['conf_ln_g'], 'conf_ln_b': out['conf_ln_b'], 'mix_out_g': out['mix_out_g'], 'w_out': out['w_out'], 'norm_ffn_g': out['norm_ffn_g'], 'w_up': out['w_up'], 'w_down': out['w_down'], 'final_norm_g': out['final_norm_g'], 'loss_target': out['loss_target'], 'm_norm_mix_g': out['m_norm_mix_g'], 'm_w_in': out['m_w_in'], 'm_gmlp_v_g': out['m_gmlp_v_g'], 'm_gmlp_w_s': out['m_gmlp_w_s'], 'm_gmlp_b_s': out['m_gmlp_b_s'], 'm_short_conv_w': out['m_short_conv_w'], 'm_conf_conv_w': out['m_conf_conv_w'], 'm_conf_ln_g': out['m_conf_ln_g'], 'm_conf_ln_b': out['m_conf_ln_b'], 'm_mix_out_g': out['m_mix_out_g'], 'm_w_out': out['m_w_out'], 'm_norm_ffn_g': out['m_norm_ffn_g'], 'm_w_up': out['m_w_up'], 'm_w_down': out['m_w_down'], 'm_final_norm_g': out['m_final_norm_g'], 'v_norm_mix_g': out['v_norm_mix_g'], 'v_w_in': out['v_w_in'], 'v_gmlp_v_g': out['v_gmlp_v_g'], 'v_gmlp_w_s': out['v_gmlp_w_s'], 'v_gmlp_b_s': out['v_gmlp_b_s'], 'v_short_conv_w': out['v_short_conv_w'], 'v_conf_conv_w': out['v_conf_conv_w'], 'v_conf_ln_g': out['v_conf_ln_g'], 'v_conf_ln_b': out['v_conf_ln_b'], 'v_mix_out_g': out['v_mix_out_g'], 'v_w_out': out['v_w_out'], 'v_norm_ffn_g': out['v_norm_ffn_g'], 'v_w_up': out['v_w_up'], 'v_w_down': out['v_w_down'], 'v_final_norm_g': out['v_final_norm_g']}


def _loss(weights, diff, rest, loss_target):
    with _jax.named_scope("forward"):
        args = {**rest, TWIN_DIFF_INPUT: diff, **{k: w.astype(_WEIGHT_DTYPES[k]) for k, w in weights.items()}}
        y = _forward(args)
    with _jax.named_scope("loss_head"):
        err = _jnp.square(y.astype(_jnp.float32) - loss_target)
        return 0.5 * _jnp.sum(_jnp.mean(err, axis=-1)) if err.ndim else 0.5 * err


def _adamw(w, g, m, v):
    m = ADAM_B1 * m + (1.0 - ADAM_B1) * g
    v = ADAM_B2 * v + (1.0 - ADAM_B2) * _jnp.square(g)
    m_hat = m / (1.0 - ADAM_B1 ** ADAM_STEP)
    v_hat = v / (1.0 - ADAM_B2 ** ADAM_STEP)
    delta = -ADAM_LR * (m_hat / (_jnp.sqrt(v_hat) + ADAM_EPS) + ADAM_WD * w)
    return delta, m, v


def reference(x, norm_mix_g, w_in, gmlp_v_g, gmlp_w_s, gmlp_b_s, short_conv_w, conf_conv_w, conf_ln_g, conf_ln_b, mix_out_g, w_out, norm_ffn_g, w_up, w_down, final_norm_g, loss_target, m_norm_mix_g, m_w_in, m_gmlp_v_g, m_gmlp_w_s, m_gmlp_b_s, m_short_conv_w, m_conf_conv_w, m_conf_ln_g, m_conf_ln_b, m_mix_out_g, m_w_out, m_norm_ffn_g, m_w_up, m_w_down, m_final_norm_g, v_norm_mix_g, v_w_in, v_gmlp_v_g, v_gmlp_w_s, v_gmlp_b_s, v_short_conv_w, v_conf_conv_w, v_conf_ln_g, v_conf_ln_b, v_mix_out_g, v_w_out, v_norm_ffn_g, v_w_up, v_w_down, v_final_norm_g):
    given = dict(x=x, norm_mix_g=norm_mix_g, w_in=w_in, gmlp_v_g=gmlp_v_g, gmlp_w_s=gmlp_w_s, gmlp_b_s=gmlp_b_s, short_conv_w=short_conv_w, conf_conv_w=conf_conv_w, conf_ln_g=conf_ln_g, conf_ln_b=conf_ln_b, mix_out_g=mix_out_g, w_out=w_out, norm_ffn_g=norm_ffn_g, w_up=w_up, w_down=w_down, final_norm_g=final_norm_g, loss_target=loss_target, m_norm_mix_g=m_norm_mix_g, m_w_in=m_w_in, m_gmlp_v_g=m_gmlp_v_g, m_gmlp_w_s=m_gmlp_w_s, m_gmlp_b_s=m_gmlp_b_s, m_short_conv_w=m_short_conv_w, m_conf_conv_w=m_conf_conv_w, m_conf_ln_g=m_conf_ln_g, m_conf_ln_b=m_conf_ln_b, m_mix_out_g=m_mix_out_g, m_w_out=m_w_out, m_norm_ffn_g=m_norm_ffn_g, m_w_up=m_w_up, m_w_down=m_w_down, m_final_norm_g=m_final_norm_g, v_norm_mix_g=v_norm_mix_g, v_w_in=v_w_in, v_gmlp_v_g=v_gmlp_v_g, v_gmlp_w_s=v_gmlp_w_s, v_gmlp_b_s=v_gmlp_b_s, v_short_conv_w=v_short_conv_w, v_conf_conv_w=v_conf_conv_w, v_conf_ln_g=v_conf_ln_g, v_conf_ln_b=v_conf_ln_b, v_mix_out_g=v_mix_out_g, v_w_out=v_w_out, v_norm_ffn_g=v_norm_ffn_g, v_w_up=v_w_up, v_w_down=v_w_down, v_final_norm_g=v_final_norm_g)
    weights = {n: given[n] for n in TWIN_WEIGHTS}
    shared = {n: given[n] for n in SHARED_INPUTS}
    per_example = {n: given[n] for n in ['x']}
    grad_fn = _jax.value_and_grad(_loss, argnums=(0, 1))

    def one_microbatch(ex, loss_target):
        ex = dict(ex)
        diff = ex.pop(TWIN_DIFF_INPUT)
        return grad_fn(weights, diff, {**shared, **ex}, loss_target)

    if N_MICROBATCH == 1:
        loss, (grad_w, grad_x) = one_microbatch(per_example, given["loss_target"])
    else:
        def body(carry, xs):
            loss_sum, grad_sum = carry
            l_k, (gw_k, gx_k) = one_microbatch(xs[0], xs[1])
            with _jax.named_scope("update"):
                return (loss_sum + l_k, _jax.tree.map(_jnp.add, grad_sum, gw_k)), gx_k

        init = (_jnp.zeros((), _jnp.float32), _jax.tree.map(_jnp.zeros_like, weights))
        (loss, grad_w), grad_x = _jax.lax.scan(body, init, (per_example, given["loss_target"]))
    with _jax.named_scope("update"):
        delta_w, new_m, new_v = {}, {}, {}
        for n in TWIN_WEIGHTS:
            delta_w[n], new_m[n], new_v[n] = _adamw(weights[n], grad_w[n], given["m_" + n], given["v_" + n])
    return (loss, grad_x, *[grad_w[n] for n in TWIN_WEIGHTS], *[delta_w[n] for n in TWIN_WEIGHTS],
            *[new_m[n] for n in TWIN_WEIGHTS], *[new_v[n] for n in TWIN_WEIGHTS])
```

```python
import jax
import jax.numpy as jnp
from jax import lax
from jax.experimental import pallas as pl
from jax.experimental.pallas import tpu as pltpu

F32 = jnp.float32
BF16 = jnp.bfloat16

D_MODEL = 1024
D_GROUP = 256
D_IN = 10 * D_GROUP
D_FF = 4 * D_MODEL
N_DEV = 8
N_LAYERS = 2
HEAD_DIM = 64
HEADS_PER_PAIR = 2
PAIR = HEADS_PER_PAIR * HEAD_DIM
CHUNK = 128
K_SHORT = 3
K_CONF = 31
HALO = 32
EPS = 1e-6
ATT_SCALE = HEAD_DIM ** -0.5
LOG_CUT = -104.0
MIB = 2 ** 20

ADAM_LR = 0.001
ADAM_B1 = 0.9
ADAM_B2 = 0.999
ADAM_EPS = 1e-08
ADAM_WD = 0.01
ADAM_STEP = 10

MESH_AXES = ("x", "y", "c")
GELU_C = 0.7978845608028654
GELU_A = 0.044715


def _mm(a, b):
    return jnp.dot(a, b, preferred_element_type=F32)


def _mm_nt(a, b):
    return lax.dot_general(a, b, (((1,), (1,)), ((), ())), preferred_element_type=F32)


def _mm_tn(a, b):
    return lax.dot_general(a, b, (((0,), (0,)), ((), ())), preferred_element_type=F32)


def _rms(x):
    return lax.rsqrt(jnp.mean(x * x, axis=-1, keepdims=True) + EPS)


def _rms_bwd(dy, xh, r):
    return r * (dy - xh * jnp.mean(dy * xh, axis=-1, keepdims=True))


def _sigmoid(x):
    return 1.0 / (1.0 + jnp.exp(-x))


def _whole(shape):
    return pl.BlockSpec(shape, lambda *_: (0,) * len(shape))


def _rows(tb, width, col=0):
    return pl.BlockSpec((tb, width), lambda i: (i, col))


def _params(semantics, vmem_mib):
    return pltpu.CompilerParams(dimension_semantics=semantics, vmem_limit_bytes=vmem_mib * MIB)


def _sds(shape, dtype=F32):
    return jax.ShapeDtypeStruct(shape, dtype)


def _split_bf16(v):
    hi = v.astype(BF16)
    lo = (v - hi.astype(F32)).astype(BF16)
    return hi, lo


def _exchange(arrays, scatter, name):
    n = len(arrays)

    def body(*refs):
        ins, outs = refs[:n], refs[n:2 * n]
        send_sems, recv_sems, local_sems = refs[2 * n:]
        x, y, c = lax.axis_index("x"), lax.axis_index("y"), lax.axis_index("c")
        me = 4 * x + 2 * y + c
        peers = []
        for k in range(1, N_DEV):
            px = 1 - x if (k >> 2) & 1 else x
            py = 1 - y if (k >> 1) & 1 else y
            pc = 1 - c if k & 1 else c
            peers.append((px, py, pc))

        local = []
        for a in range(n):
            cp = pltpu.make_async_copy(ins[a].at[me] if scatter else ins[a], outs[a].at[me], local_sems.at[a])
            cp.start()
            local.append(cp)
        remote = []
        for k, peer in enumerate(peers):
            pf = 4 * peer[0] + 2 * peer[1] + peer[2]
            for a in range(n):
                cp = pltpu.make_async_remote_copy(
                    src_ref=ins[a].at[pf] if scatter else ins[a], dst_ref=outs[a].at[me],
                    send_sem=send_sems.at[a, k], recv_sem=recv_sems.at[a, k],
                    device_id=peer, device_id_type=pl.DeviceIdType.MESH)
                cp.start()
                remote.append(cp)
        for cp in remote:
            cp.wait_send()
        for k, peer in enumerate(peers):
            pf = 4 * peer[0] + 2 * peer[1] + peer[2]
            for a in range(n):
                pltpu.make_async_remote_copy(
                    src_ref=ins[a].at[pf] if scatter else ins[a], dst_ref=outs[a].at[pf],
                    send_sem=send_sems.at[a, k], recv_sem=recv_sems.at[a, k],
                    device_id=peer, device_id_type=pl.DeviceIdType.MESH).wait_recv()
        for cp in local:
            cp.wait()

    any_spec = pl.BlockSpec(memory_space=pl.ANY)
    out_shape = [_sds(a.shape if scatter else (N_DEV,) + a.shape, a.dtype) for a in arrays]
    return pl.pallas_call(
        body, name=name, out_shape=out_shape,
        in_specs=[any_spec] * n, out_specs=[any_spec] * n,
        scratch_shapes=[pltpu.SemaphoreType.DMA((n, N_DEV - 1)), pltpu.SemaphoreType.DMA((n, N_DEV - 1)),
                        pltpu.SemaphoreType.DMA((n,))],
    )(*arrays)


def _in_proj_fwd(x, g, wint, tb):
    s = x.shape[0]

    def body(x_ref, g_ref, w_ref, z_ref, qkv_ref):
        xv = x_ref[...]
        h = (xv * _rms(xv) * g_ref[...]).astype(BF16)
        z = _mm_nt(h, w_ref[...])
        z_ref[...] = z
        qkv_ref[:, 0:D_GROUP] = (z[:, 5 * D_GROUP:6 * D_GROUP] * ATT_SCALE).astype(BF16)
        qkv_ref[:, D_GROUP:3 * D_GROUP] = z[:, 6 * D_GROUP:8 * D_GROUP].astype(BF16)

    return pl.pallas_call(
        body, name="in_proj_fwd", grid=(s // tb,),
        in_specs=[_rows(tb, D_MODEL), _whole((1, D_MODEL)), _whole((D_IN, D_MODEL))],
        out_specs=[_rows(tb, D_IN), _rows(tb, 3 * D_GROUP)],
        out_shape=[_sds((s, D_IN)), _sds((s, 3 * D_GROUP), BF16)],
        compiler_params=_params(("parallel",), 48),
    )(x, g, wint)


def _gelu(x):
    return 0.5 * x * (1.0 + jnp.tanh(GELU_C * (x + GELU_A * x * x * x)))


def _gelu_grad(x):
    t = jnp.tanh(GELU_C * (x + GELU_A * x * x * x))
    return 0.5 * (1.0 + t) + 0.5 * x * (1.0 - t * t) * GELU_C * (1.0 + 3.0 * GELU_A * x * x)


def _head_lane(width):
    return lax.broadcasted_iota(jnp.int32, (1, width), 1) // HEAD_DIM


def _gating_chunk(wm_ref, bexp_ref, vc, lane_h):
    f = bexp_ref[...]
    for h in range(D_GROUP // HEAD_DIM):
        f = f + _mm(wm_ref[h], jnp.where(lane_h == h, vc, 0))
    return f


def _halo_specs(s, tb, width_blocks):
    per = tb // HALO
    prev = pl.BlockSpec((HALO, width_blocks), lambda i: (jnp.maximum(i * per - 1, 0), 0))
    nxt = pl.BlockSpec((HALO, width_blocks), lambda i: (jnp.minimum((i + 1) * per, s // HALO - 1), 0))
    return prev, nxt


def _mix_fwd(z, vg, wm, bexp, scw, ccw, lng, lnb, tb):
    s = z.shape[0]
    prev_spec, _ = _halo_specs(s, tb, D_IN)

    def body(z_ref, zp_ref, vg_ref, wm_ref, bexp_ref, scw_ref, ccw_ref, lng_ref, lnb_ref,
             ya_ref, yb_ref, yd_ref, pbuf, hbuf):
        keep = (pl.program_id(0) > 0).astype(F32)
        lane_h = _head_lane(D_GROUP)
        ga = _gelu(z_ref[:, 0:2 * D_GROUP])
        u, v = ga[:, :D_GROUP], ga[:, D_GROUP:]
        vn = (v * _rms(v) * vg_ref[...]).astype(BF16)
        for n in range(tb // CHUNK):
            rows = slice(n * CHUNK, (n + 1) * CHUNK)
            ya_ref[rows, :] = u[rows] * _gating_chunk(wm_ref, bexp_ref, vn[rows], lane_h)
        p = z_ref[:, 3 * D_GROUP:4 * D_GROUP] * z_ref[:, 4 * D_GROUP:5 * D_GROUP]
        pbuf[0:HALO, :] = zp_ref[:, 3 * D_GROUP:4 * D_GROUP] * zp_ref[:, 4 * D_GROUP:5 * D_GROUP] * keep
        pbuf[HALO:HALO + tb, :] = p
        cv = scw_ref[K_SHORT - 1:K_SHORT, :] * p
        for k in range(K_SHORT - 1):
            cv = cv + scw_ref[k:k + 1, :] * pbuf[pl.ds(HALO - (K_SHORT - 1) + k, tb), :]
        yb_ref[...] = z_ref[:, 2 * D_GROUP:3 * D_GROUP] * cv
        hbuf[0:HALO, :] = zp_ref[:, 8 * D_GROUP:9 * D_GROUP] * _sigmoid(zp_ref[:, 9 * D_GROUP:10 * D_GROUP]) * keep
        hbuf[HALO:HALO + tb, :] = z_ref[:, 8 * D_GROUP:9 * D_GROUP] * _sigmoid(z_ref[:, 9 * D_GROUP:10 * D_GROUP])
        c = jnp.zeros((tb, D_GROUP), F32)
        for k in range(K_CONF):
            c = c + ccw_ref[k:k + 1, :] * hbuf[pl.ds(HALO - (K_CONF - 1) + k, tb), :]
        xc = c - jnp.mean(c, axis=-1, keepdims=True)
        ln = xc * lax.rsqrt(jnp.mean(xc * xc, axis=-1, keepdims=True) + EPS) * lng_ref[...] + lnb_ref[...]
        yd_ref[...] = ln * _sigmoid(ln)

    grp = _rows(tb, D_GROUP)
    return pl.pallas_call(
        body, name="mix_fwd", grid=(s // tb,),
        in_specs=[_rows(tb, D_IN), prev_spec, _whole((1, D_GROUP)), _whole(wm.shape), _whole(bexp.shape),
                  _whole(scw.shape), _whole(ccw.shape), _whole((1, D_GROUP)), _whole((1, D_GROUP))],
        out_specs=[grp, grp, grp],
        out_shape=[_sds((s, D_GROUP))] * 3,
        scratch_shapes=[pltpu.VMEM((HALO + tb, D_GROUP), F32), pltpu.VMEM((HALO + tb, D_GROUP), F32)],
        compiler_params=_params(("parallel",), 40),
    )(z, z, vg, wm, bexp, scw, ccw, lng, lnb)


def _stick_tile(qh, kt, qpos, k0, c, upper):
    tk = kt.shape[0]
    x = _mm_nt(qh, kt)
    kpos = k0 + lax.broadcasted_iota(jnp.int32, (1, tk), 1)
    causal = kpos < qpos
    soft = jnp.log1p(jnp.exp(-jnp.abs(x)))
    lb = jnp.minimum(x, 0.0) - soft
    lom = jnp.where(causal, -jnp.maximum(x, 0.0) - soft, 0.0)
    hi, lo = _split_bf16(lom)
    stick = c + _mm(hi, upper) + _mm(lo, upper)
    w = jnp.where(causal, jnp.exp(lb + stick), 0.0)
    return w, lb, lom, causal


def _attn_specs(s, tq):
    q_spec = pl.BlockSpec((tq, PAIR), lambda hp, qi: (qi, hp))
    k_spec = pl.BlockSpec((s, PAIR), lambda hp, qi: (0, 2 + hp))
    v_spec = pl.BlockSpec((s, PAIR), lambda hp, qi: (0, 4 + hp))
    return q_spec, k_spec, v_spec


def _attn_fwd(qkv, tq, tk):
    s = qkv.shape[0]

    def body(q_ref, k_ref, v_ref, o_ref):
        qi = pl.program_id(1)
        q = q_ref[...]
        lane_h = _head_lane(PAIR)
        qpos = qi * tq + lax.broadcasted_iota(jnp.int32, (tq, 1), 0)
        upper = (lax.broadcasted_iota(jnp.int32, (tk, tk), 0) > lax.broadcasted_iota(jnp.int32, (tk, tk), 1)).astype(BF16)
        out = jnp.zeros((tq, PAIR), F32)
        for h in range(HEADS_PER_PAIR):
            qh = jnp.where(lane_h == h, q, 0)

            def step(carry, h=h, qh=qh):
                kb, _, c, acc = carry
                k0 = pl.multiple_of(kb * tk, tk)
                kt = k_ref[pl.ds(k0, tk), :]
                vt = jnp.where(lane_h == h, v_ref[pl.ds(k0, tk), :], 0)
                w, _, lom, _ = _stick_tile(qh, kt, qpos, k0, c, upper)
                acc = acc + _mm(w.astype(BF16), vt)
                c = c + jnp.sum(lom, axis=1, keepdims=True)
                return kb - 1, (jnp.max(c) > LOG_CUT).astype(jnp.int32), c, acc

            init = (((qi + 1) * tq - 1) // tk, jnp.int32(1), jnp.zeros((tq, 1), F32), jnp.zeros((tq, PAIR), F32))
            out = out + lax.while_loop(lambda cr: jnp.logical_and(cr[0] >= 0, cr[1] > 0), step, init)[3]
        o_ref[...] = out

    return pl.pallas_call(
        body, name="attn_fwd", grid=(D_GROUP // PAIR, s // tq),
        in_specs=list(_attn_specs(s, tq)),
        out_specs=pl.BlockSpec((tq, PAIR), lambda hp, qi: (qi, hp)),
        out_shape=_sds((s, D_GROUP)),
        compiler_params=_params(("parallel", "parallel"), 40),
    )(qkv, qkv, qkv)


def _out_proj_fwd(ys, x, mg, wout, tb):
    s = x.shape[0]

    def body(ya_ref, yb_ref, yc_ref, yd_ref, x_ref, mg_ref, w_ref, o_ref):
        acc = x_ref[...]
        for gi, y_ref in enumerate((ya_ref, yb_ref, yc_ref, yd_ref)):
            cols = slice(gi * D_GROUP, (gi + 1) * D_GROUP)
            y = y_ref[...]
            acc = acc + _mm((y * _rms(y) * mg_ref[:, cols]).astype(BF16), w_ref[cols, :])
        o_ref[...] = acc

    grp = _rows(tb, D_GROUP)
    return pl.pallas_call(
        body, name="out_proj_fwd", grid=(s // tb,),
        in_specs=[grp, grp, grp, grp, _rows(tb, D_MODEL), _whole((1, D_MODEL)), _whole((D_MODEL, D_MODEL))],
        out_specs=_rows(tb, D_MODEL), out_shape=_sds((s, D_MODEL)),
        compiler_params=_params(("parallel",), 32),
    )(*ys, x, mg, wout)


def _ffn_fwd(x, g, wup, wdn, tb):
    s = x.shape[0]
    ff = D_FF // N_DEV

    def body(x_ref, g_ref, wu_ref, wd_ref, o_ref):
        xv = x_ref[...]
        h = (xv * _rms(xv) * g_ref[...]).astype(BF16)
        acc = xv
        for d in range(N_DEV):
            a = jnp.maximum(_mm(h, wu_ref[d]), 0.0)
            acc = acc + _mm((a * a).astype(BF16), wd_ref[d])
        o_ref[...] = acc

    return pl.pallas_call(
        body, name="ffn_fwd", grid=(s // tb,),
        in_specs=[_rows(tb, D_MODEL), _whole((1, D_MODEL)), _whole((N_DEV, D_MODEL, ff)), _whole((N_DEV, ff, D_MODEL))],
        out_specs=_rows(tb, D_MODEL), out_shape=_sds((s, D_MODEL)),
        compiler_params=_params(("parallel",), 56),
    )(x, g, wup, wdn)


def _loss_head(x, g, tgt, tb):
    s = x.shape[0]

    def body(x_ref, g_ref, t_ref, dx_ref, dg_ref, loss_ref):
        @pl.when(pl.program_id(0) == 0)
        def _():
            dg_ref[...] = jnp.zeros_like(dg_ref)
            loss_ref[...] = jnp.zeros_like(loss_ref)

        xv = x_ref[...]
        r = _rms(xv)
        xh = xv * r
        err = xh * g_ref[...] - t_ref[...]
        loss_ref[...] += 0.5 * jnp.sum(jnp.mean(err * err, axis=-1, keepdims=True))
        dy = err * (1.0 / D_MODEL)
        dg_ref[...] += jnp.sum(dy * xh, axis=0, keepdims=True)
        dx_ref[...] = _rms_bwd(dy * g_ref[...], xh, r)

    return pl.pallas_call(
        body, name="loss_head", grid=(s // tb,),
        in_specs=[_rows(tb, D_MODEL), _whole((1, D_MODEL)), _rows(tb, D_MODEL)],
        out_specs=[_rows(tb, D_MODEL), _whole((1, D_MODEL)), _whole((8, 128))],
        out_shape=[_sds((s, D_MODEL)), _sds((1, D_MODEL)), _sds((8, 128))],
        compiler_params=_params(("arbitrary",), 32),
    )(x, g, tgt)


def _ffn_bwd(x1, dx2, g, wup, wdn, tb):
    s = x1.shape[0]
    ff = D_FF // N_DEV

    def body(x_ref, dy_ref, g_ref, wu_ref, wd_ref, dx_ref, hb_ref, dpre_ref, sq_ref, dg_ref):
        @pl.when(pl.program_id(0) == 0)
        def _():
            dg_ref[...] = jnp.zeros_like(dg_ref)

        xv = x_ref[...]
        r = _rms(xv)
        xh = xv * r
        hb = (xh * g_ref[...]).astype(BF16)
        hb_ref[...] = hb
        dyv = dy_ref[...]
        dyb = dyv.astype(BF16)
        dh = jnp.zeros((tb, D_MODEL), F32)
        for d in range(N_DEV):
            cols = slice(d * ff, (d + 1) * ff)
            a = jnp.maximum(_mm(hb, wu_ref[d]), 0.0)
            sq_ref[:, cols] = (a * a).astype(BF16)
            dpre = (_mm_nt(dyb, wd_ref[d]) * (2.0 * a)).astype(BF16)
            dpre_ref[:, cols] = dpre
            dh = dh + _mm_nt(dpre, wu_ref[d])
        dg_ref[...] += jnp.sum(dh * xh, axis=0, keepdims=True)
        dx_ref[...] = dyv + _rms_bwd(dh * g_ref[...], xh, r)

    return pl.pallas_call(
        body, name="ffn_bwd", grid=(s // tb,),
        in_specs=[_rows(tb, D_MODEL), _rows(tb, D_MODEL), _whole((1, D_MODEL)),
                  _whole((N_DEV, D_MODEL, ff)), _whole((N_DEV, ff, D_MODEL))],
        out_specs=[_rows(tb, D_MODEL), _rows(tb, D_MODEL), _rows(tb, D_FF), _rows(tb, D_FF), _whole((1, D_MODEL))],
        out_shape=[_sds((s, D_MODEL)), _sds((s, D_MODEL), BF16), _sds((s, D_FF), BF16), _sds((s, D_FF), BF16),
                   _sds((1, D_MODEL))],
        compiler_params=_params(("arbitrary",), 60),
    )(x1, dx2, g, wup, wdn)


def _tn_matmul(a, b, nd, a_blocked, b_blocked, tb, name):
    s = a.shape[0]
    ka = a.shape[1] // nd if a_blocked else a.shape[1]
    nb = b.shape[1] // nd if b_blocked else b.shape[1]

    def body(a_ref, b_ref, o_ref):
        @pl.when(pl.program_id(1) == 0)
        def _():
            o_ref[...] = jnp.zeros_like(o_ref)

        o_ref[0] += _mm_tn(a_ref[...].astype(BF16), b_ref[...].astype(BF16))

    return pl.pallas_call(
        body, name=name, grid=(nd, s // tb),
        in_specs=[pl.BlockSpec((tb, ka), (lambda d, i: (i, d)) if a_blocked else (lambda d, i: (i, 0))),
                  pl.BlockSpec((tb, nb), (lambda d, i: (i, d)) if b_blocked else (lambda d, i: (i, 0)))],
        out_specs=pl.BlockSpec((1, ka, nb), lambda d, i: (d, 0, 0)),
        out_shape=_sds((nd, ka, nb)),
        compiler_params=_params(("parallel", "arbitrary"), 56),
    )(a, b)


def _out_proj_bwd(dx1, ys, mg, wout, tb):
    s = dx1.shape[0]

    def body(dx_ref, ya_ref, yb_ref, yc_ref, yd_ref, mg_ref, w_ref,
             dya_ref, dyb_ref, dyc_ref, dyd_ref, yn_ref, dmg_ref):
        @pl.when(pl.program_id(0) == 0)
        def _():
            dmg_ref[...] = jnp.zeros_like(dmg_ref)

        dyn = _mm_nt(dx_ref[...].astype(BF16), w_ref[...])
        groups = ((ya_ref, dya_ref), (yb_ref, dyb_ref), (yc_ref, dyc_ref), (yd_ref, dyd_ref))
        for gi, (y_ref, dy_ref) in enumerate(groups):
            cols = slice(gi * D_GROUP, (gi + 1) * D_GROUP)
            y = y_ref[...]
            r = _rms(y)
            n = y * r
            gain = mg_ref[:, cols]
            dn = dyn[:, cols]
            yn_ref[:, cols] = (n * gain).astype(BF16)
            dmg_ref[:, cols] += jnp.sum(dn * n, axis=0, keepdims=True)
            dy_ref[...] = _rms_bwd(dn * gain, n, r)

    grp = _rows(tb, D_GROUP)
    return pl.pallas_call(
        body, name="out_proj_bwd", grid=(s // tb,),
        in_specs=[_rows(tb, D_MODEL), grp, grp, grp, grp, _whole((1, D_MODEL)), _whole((D_MODEL, D_MODEL))],
        out_specs=[grp, grp, grp, grp, _rows(tb, D_MODEL), _whole((1, D_MODEL))],
        out_shape=[_sds((s, D_GROUP))] * 4 + [_sds((s, D_MODEL), BF16), _sds((1, D_MODEL))],
        compiler_params=_params(("arbitrary",), 32),
    )(dx1, *ys, mg, wout)


def _attn_bwd(qkv, do, o, tq, tk):
    s = qkv.shape[0]
    nq = s // tq
    n_pairs = D_GROUP // PAIR

    def body(q_ref, k_ref, v_ref, do_ref, o_ref, dq_ref, dk_hbm, dv_hbm, dk_acc, dv_acc, sems):
        hp, qi = pl.program_id(0), pl.program_id(1)

        @pl.when(qi == 0)
        def _():
            dk_acc[...] = jnp.zeros_like(dk_acc)
            dv_acc[...] = jnp.zeros_like(dv_acc)

        q = q_ref[...]
        dob = do_ref[...].astype(BF16)
        prod = dob.astype(F32) * o_ref[...]
        lane_h = _head_lane(PAIR)
        qpos = qi * tq + lax.broadcasted_iota(jnp.int32, (tq, 1), 0)
        row = lax.broadcasted_iota(jnp.int32, (tk, tk), 0)
        col = lax.broadcasted_iota(jnp.int32, (tk, tk), 1)
        upper = (row > col).astype(BF16)
        upper_eq = (row >= col).astype(BF16)
        dq = jnp.zeros((tq, PAIR), F32)
        for h in range(HEADS_PER_PAIR):
            in_head = lane_h == h
            qh = jnp.where(in_head, q, 0)
            doh = jnp.where(in_head, dob, 0)
            total = jnp.sum(jnp.where(in_head, prod, 0.0), axis=1, keepdims=True)

            def step(carry, in_head=in_head, qh=qh, doh=doh, total=total):
                kb, _, c, near, acc = carry
                k0 = pl.multiple_of(kb * tk, tk)
                kt = k_ref[pl.ds(k0, tk), :]
                vt = v_ref[pl.ds(k0, tk), :]
                w, lb, lom, causal = _stick_tile(qh, kt, qpos, k0, c, upper)
                wb = w.astype(BF16)
                gw = _mm_nt(doh, vt) * wb.astype(F32)
                hi, lo = _split_bf16(gw)
                far = total - near - _mm(hi, upper_eq) - _mm(lo, upper_eq)
                beta = jnp.exp(lb)
                dxb = jnp.where(causal, gw * (1.0 - beta) - far * beta, 0.0).astype(BF16)
                acc = acc + _mm(dxb, jnp.where(in_head, kt, 0))
                dk_acc[pl.ds(k0, tk), :] += _mm_tn(dxb, qh)
                dv_acc[pl.ds(k0, tk), :] += _mm_tn(wb, doh)
                c = c + jnp.sum(lom, axis=1, keepdims=True)
                near = near + jnp.sum(gw, axis=1, keepdims=True)
                return kb - 1, (jnp.max(c) > LOG_CUT).astype(jnp.int32), c, near, acc

            init = (((qi + 1) * tq - 1) // tk, jnp.int32(1), jnp.zeros((tq, 1), F32), jnp.zeros((tq, 1), F32),
                    jnp.zeros((tq, PAIR), F32))
            dq = dq + lax.while_loop(lambda cr: jnp.logical_and(cr[0] >= 0, cr[1] > 0), step, init)[4]
        dq_ref[...] = dq

        @pl.when(qi == nq - 1)
        def _():
            ck = pltpu.make_async_copy(dk_acc, dk_hbm.at[hp], sems.at[0])
            cv = pltpu.make_async_copy(dv_acc, dv_hbm.at[hp], sems.at[1])
            ck.start()
            cv.start()
            ck.wait()
            cv.wait()

    blk = pl.BlockSpec((tq, PAIR), lambda hp, qi: (qi, hp))
    any_spec = pl.BlockSpec(memory_space=pl.ANY)
    return pl.pallas_call(
        body, name="attn_bwd", grid=(n_pairs, nq),
        in_specs=list(_attn_specs(s, tq)) + [blk, blk],
        out_specs=[blk, any_spec, any_spec],
        out_shape=[_sds((s, D_GROUP)), _sds((n_pairs, s, PAIR)), _sds((n_pairs, s, PAIR))],
        scratch_shapes=[pltpu.VMEM((s, PAIR), F32), pltpu.VMEM((s, PAIR), F32), pltpu.SemaphoreType.DMA((2,))],
        compiler_params=_params(("arbitrary", "arbitrary"), 56),
    )(qkv, qkv, qkv, do, o)


def _mix_bwd(z, dya, dyb, dyd, dq, dk, dv, vg, wm, wmt, bexp, scw, ccw, lng, lnb, tb):
    s = z.shape[0]
    n_steps = s // tb
    prev_spec, next_spec = _halo_specs(s, tb, D_IN)
    _, next_grp = _halo_specs(s, tb, D_GROUP)
    ext = tb + HALO

    def body(z_ref, zp_ref, zn_ref, dya_ref, dyb_ref, dybn_ref, dyd_ref, dydn_ref, dq_ref, dk0_ref, dk1_ref,
             dv0_ref, dv1_ref, vg_ref, wm_ref, wmt_ref, bexp_ref, scw_ref, ccw_ref, lng_ref, lnb_ref,
             dz_ref, dvg_ref, dws_ref, dbs_ref, dscw_ref, dccw_ref, dlng_ref, dlnb_ref,
             pbuf, hbuf, gbuf, cbuf, dubuf, dvnbuf):
        i = pl.program_id(0)

        @pl.when(i == 0)
        def _():
            for ref in (dvg_ref, dws_ref, dbs_ref, dscw_ref, dccw_ref, dlng_ref, dlnb_ref):
                ref[...] = jnp.zeros_like(ref)

        keep_prev = (i > 0).astype(F32)
        keep_next = (i < n_steps - 1).astype(F32)
        lane_h = _head_lane(D_GROUP)

        za = z_ref[:, 0:2 * D_GROUP]
        ga = _gelu(za)
        u, v = ga[:, :D_GROUP], ga[:, D_GROUP:]
        r = _rms(v)
        vh = v * r
        vn = (vh * vg_ref[...]).astype(BF16)
        tril = lax.broadcasted_iota(jnp.int32, (CHUNK, CHUNK), 0) >= lax.broadcasted_iota(jnp.int32, (CHUNK, CHUNK), 1)
        dbias = jnp.zeros((CHUNK, D_GROUP), F32)
        for n in range(tb // CHUNK):
            rows = slice(n * CHUNK, (n + 1) * CHUNK)
            vc = vn[rows]
            dy = dya_ref[rows, :]
            dubuf[rows, :] = dy * _gating_chunk(wm_ref, bexp_ref, vc, lane_h)
            df = dy * u[rows]
            dfb = df.astype(BF16)
            dvn = jnp.zeros((CHUNK, D_GROUP), F32)
            for h in range(D_GROUP // HEAD_DIM):
                dfh = jnp.where(lane_h == h, dfb, 0)
                dvn = dvn + _mm(wmt_ref[h], dfh)
                dws_ref[h] += jnp.where(tril, _mm_nt(dfh, vc), 0.0)
            dvnbuf[rows, :] = dvn
            dbias = dbias + df
        for h in range(D_GROUP // HEAD_DIM):
            per_head = jnp.sum(jnp.where(lane_h == h, dbias, 0.0), axis=1, keepdims=True)
            dbs_ref[...] += per_head * (lax.broadcasted_iota(jnp.int32, (1, CHUNK), 1) == h).astype(F32)
        dvn = dvnbuf[...]
        dvg_ref[...] += jnp.sum(dvn * vh, axis=0, keepdims=True)
        dgelu = _gelu_grad(za)
        dz_ref[:, 0:D_GROUP] = (dubuf[...] * dgelu[:, :D_GROUP]).astype(BF16)
        dz_ref[:, D_GROUP:2 * D_GROUP] = (_rms_bwd(dvn * vg_ref[...], vh, r) * dgelu[:, D_GROUP:]).astype(BF16)

        gate_b = z_ref[:, 2 * D_GROUP:3 * D_GROUP]
        gate_c = z_ref[:, 3 * D_GROUP:4 * D_GROUP]
        hh = z_ref[:, 4 * D_GROUP:5 * D_GROUP]
        p = gate_c * hh
        pbuf[0:HALO, :] = zp_ref[:, 3 * D_GROUP:4 * D_GROUP] * zp_ref[:, 4 * D_GROUP:5 * D_GROUP] * keep_prev
        pbuf[HALO:HALO + tb, :] = p
        dyb_v = dyb_ref[...]
        dcv = dyb_v * gate_b
        gbuf[0:tb, :] = dcv
        gbuf[tb:ext, :] = dybn_ref[...] * zn_ref[:, 2 * D_GROUP:3 * D_GROUP] * keep_next
        cv = scw_ref[K_SHORT - 1:K_SHORT, :] * p
        dp = scw_ref[K_SHORT - 1:K_SHORT, :] * dcv
        dscw_ref[K_SHORT - 1:K_SHORT, :] += jnp.sum(dcv * p, axis=0, keepdims=True)
        for k in range(K_SHORT - 1):
            shifted = pbuf[pl.ds(HALO - (K_SHORT - 1) + k, tb), :]
            cv = cv + scw_ref[k:k + 1, :] * shifted
            dp = dp + scw_ref[k:k + 1, :] * gbuf[pl.ds(K_SHORT - 1 - k, tb), :]
            dscw_ref[k:k + 1, :] += jnp.sum(dcv * shifted, axis=0, keepdims=True)
        dz_ref[:, 2 * D_GROUP:3 * D_GROUP] = (dyb_v * cv).astype(BF16)
        dz_ref[:, 3 * D_GROUP:4 * D_GROUP] = (dp * hh).astype(BF16)
        dz_ref[:, 4 * D_GROUP:5 * D_GROUP] = (dp * gate_c).astype(BF16)

        dz_ref[:, 5 * D_GROUP:6 * D_GROUP] = (dq_ref[...] * ATT_SCALE).astype(BF16)
        dz_ref[:, 6 * D_GROUP:6 * D_GROUP + PAIR] = dk0_ref[...].astype(BF16)
        dz_ref[:, 6 * D_GROUP + PAIR:7 * D_GROUP] = dk1_ref[...].astype(BF16)
        dz_ref[:, 7 * D_GROUP:7 * D_GROUP + PAIR] = dv0_ref[...].astype(BF16)
        dz_ref[:, 7 * D_GROUP + PAIR:8 * D_GROUP] = dv1_ref[...].astype(BF16)

        a = z_ref[:, 8 * D_GROUP:9 * D_GROUP]
        sg = _sigmoid(z_ref[:, 9 * D_GROUP:10 * D_GROUP])
        hbuf[0:HALO, :] = zp_ref[:, 8 * D_GROUP:9 * D_GROUP] * _sigmoid(zp_ref[:, 9 * D_GROUP:10 * D_GROUP]) * keep_prev
        hbuf[HALO:HALO + tb, :] = a * sg
        hbuf[HALO + tb:HALO + ext, :] = zn_ref[:, 8 * D_GROUP:9 * D_GROUP] * _sigmoid(zn_ref[:, 9 * D_GROUP:10 * D_GROUP])
        c = jnp.zeros((ext, D_GROUP), F32)
        for k in range(K_CONF):
            c = c + ccw_ref[k:k + 1, :] * hbuf[pl.ds(HALO - (K_CONF - 1) + k, ext), :]
        xc = c - jnp.mean(c, axis=-1, keepdims=True)
        rs = lax.rsqrt(jnp.mean(xc * xc, axis=-1, keepdims=True) + EPS)
        xh = xc * rs
        ln = xh * lng_ref[...] + lnb_ref[...]
        sl = _sigmoid(ln)
        dy_ext = jnp.concatenate([dyd_ref[...], dydn_ref[...] * keep_next], axis=0)
        dln = dy_ext * sl * (1.0 + ln * (1.0 - sl))
        dlng_ref[...] += jnp.sum(dln[:tb] * xh[:tb], axis=0, keepdims=True)
        dlnb_ref[...] += jnp.sum(dln[:tb], axis=0, keepdims=True)
        dxh = dln * lng_ref[...]
        dc = rs * (dxh - jnp.mean(dxh, axis=-1, keepdims=True) - xh * jnp.mean(dxh * xh, axis=-1, keepdims=True))
        cbuf[...] = dc
        dc_blk = dc[:tb]
        dhd = jnp.zeros((tb, D_GROUP), F32)
        for k in range(K_CONF):
            dhd = dhd + ccw_ref[k:k + 1, :] * cbuf[pl.ds(K_CONF - 1 - k, tb), :]
            dccw_ref[k:k + 1, :] += jnp.sum(dc_blk * hbuf[pl.ds(HALO - (K_CONF - 1) + k, tb), :], axis=0, keepdims=True)
        dz_ref[:, 8 * D_GROUP:9 * D_GROUP] = (dhd * sg).astype(BF16)
        dz_ref[:, 9 * D_GROUP:10 * D_GROUP] = (dhd * a * sg * (1.0 - sg)).astype(BF16)

    grp = _rows(tb, D_GROUP)
    pair0 = pl.BlockSpec((None, tb, PAIR), lambda i: (0, i, 0))
    pair1 = pl.BlockSpec((None, tb, PAIR), lambda i: (1, i, 0))
    small = [_sds((1, D_GROUP)), _sds((4, CHUNK, CHUNK)), _sds((CHUNK, CHUNK)), _sds((8, D_GROUP)),
             _sds((HALO, D_GROUP)), _sds((1, D_GROUP)), _sds((1, D_GROUP))]
    return pl.pallas_call(
        body, name="mix_bwd", grid=(n_steps,),
        in_specs=[_rows(tb, D_IN), prev_spec, next_spec, grp, grp, next_grp, grp, next_grp, grp, pair0, pair1, pair0, pair1,
                  _whole((1, D_GROUP)), _whole(wm.shape), _whole(wmt.shape), _whole(bexp.shape), _whole(scw.shape),
                  _whole(ccw.shape), _whole((1, D_GROUP)), _whole((1, D_GROUP))],
        out_specs=[_rows(tb, D_IN)] + [_whole(t.shape) for t in small],
        out_shape=[_sds((s, D_IN), BF16)] + small,
        scratch_shapes=[pltpu.VMEM((HALO + tb, D_GROUP), F32), pltpu.VMEM((HALO + ext, D_GROUP), F32),
                        pltpu.VMEM((ext, D_GROUP), F32), pltpu.VMEM((ext, D_GROUP), F32),
                        pltpu.VMEM((tb, D_GROUP), F32), pltpu.VMEM((tb, D_GROUP), F32)],
        compiler_params=_params(("arbitrary",), 48),
    )(z, z, z, dya, dyb, dyb, dyd, dyd, dq, dk, dk, dv, dv, vg, wm, wmt, bexp, scw, ccw, lng, lnb)


def _in_proj_bwd(x, dz, dres, g, wint, tb):
    s = x.shape[0]

    def body(x_ref, dz_ref, dr_ref, g_ref, w_ref, dx_ref, hb_ref, dg_ref):
        @pl.when(pl.program_id(0) == 0)
        def _():
            dg_ref[...] = jnp.zeros_like(dg_ref)

        xv = x_ref[...]
        r = _rms(xv)
        xh = xv * r
        hb_ref[...] = (xh * g_ref[...]).astype(BF16)
        dh = _mm(dz_ref[...], w_ref[...])
        dg_ref[...] += jnp.sum(dh * xh, axis=0, keepdims=True)
        dx_ref[...] = dr_ref[...] + _rms_bwd(dh * g_ref[...], xh, r)

    return pl.pallas_call(
        body, name="in_proj_bwd", grid=(s // tb,),
        in_specs=[_rows(tb, D_MODEL), _rows(tb, D_IN), _rows(tb, D_MODEL), _whole((1, D_MODEL)), _whole((D_IN, D_MODEL))],
        out_specs=[_rows(tb, D_MODEL), _rows(tb, D_MODEL), _whole((1, D_MODEL))],
        out_shape=[_sds((s, D_MODEL)), _sds((s, D_MODEL), BF16), _sds((1, D_MODEL))],
        compiler_params=_params(("arbitrary",), 48),
    )(x, dz, dres, g, wint)


def _adamw(w, g, m, v):
    m = ADAM_B1 * m + (1.0 - ADAM_B1) * g
    v = ADAM_B2 * v + (1.0 - ADAM_B2) * (g * g)
    m_hat = m / (1.0 - ADAM_B1 ** ADAM_STEP)
    v_hat = v / (1.0 - ADAM_B2 ** ADAM_STEP)
    delta = -ADAM_LR * (m_hat / (jnp.sqrt(v_hat) + ADAM_EPS) + ADAM_WD * w)
    return delta, m, v


def _reduce_adamw(parts, w, m, v, tb, name):
    rows, cols = w.shape

    def body(p_ref, w_ref, m_ref, v_ref, g_ref, d_ref, m2_ref, v2_ref):
        g = p_ref[0]
        for j in range(1, N_DEV):
            g = g + p_ref[j]
        g_ref[...] = g
        d_ref[...], m2_ref[...], v2_ref[...] = _adamw(w_ref[...], g, m_ref[...], v_ref[...])

    blk = _rows(tb, cols)
    return pl.pallas_call(
        body, name=name, grid=(rows // tb,),
        in_specs=[pl.BlockSpec((N_DEV, tb, cols), lambda i: (0, i, 0)), blk, blk, blk],
        out_specs=[blk] * 4, out_shape=[_sds((rows, cols))] * 4,
        compiler_params=_params(("parallel",), 32),
    )(parts, w, m, v)


LANES = 128
PACK_ALIGN = 8 * LANES


def _pack(arrays):
    pieces = []
    for a in arrays:
        flat = a.reshape(-1)
        pieces.append(jnp.pad(flat, (0, -flat.shape[0] % PACK_ALIGN)).reshape(-1, LANES))
    return jnp.concatenate(pieces, axis=0)


def _unpack(packed, shapes):
    out, row = [], 0
    for shape in shapes:
        size = 1
        for dim in shape:
            size *= dim
        rows = -(-size // PACK_ALIGN) * 8
        out.append(packed[row:row + rows].reshape(-1)[:size].reshape(shape))
        row += rows
    return out


TB_PROJ = 512
TB_MIX = 256
TB_FFN_BWD = 256
TB_TN = 1024
TQ = 128
TK = 128
TB_ADAM = 64


def kernel(x, norm_mix_g, w_in, gmlp_v_g, gmlp_w_s, gmlp_b_s, short_conv_w, conf_conv_w, conf_ln_g, conf_ln_b, mix_out_g, w_out, norm_ffn_g, w_up, w_down, final_norm_g, loss_target, m_norm_mix_g, m_w_in, m_gmlp_v_g, m_gmlp_w_s, m_gmlp_b_s, m_short_conv_w, m_conf_conv_w, m_conf_ln_g, m_conf_ln_b, m_mix_out_g, m_w_out, m_norm_ffn_g, m_w_up, m_w_down, m_final_norm_g, v_norm_mix_g, v_w_in, v_gmlp_v_g, v_gmlp_w_s, v_gmlp_b_s, v_short_conv_w, v_conf_conv_w, v_conf_ln_g, v_conf_ln_b, v_mix_out_g, v_w_out, v_norm_ffn_g, v_w_up, v_w_down, v_final_norm_g):
    me = 4 * lax.axis_index("x") + 2 * lax.axis_index("y") + lax.axis_index("c")
    x0, target = x[0], loss_target[0]
    s = x0.shape[0]
    tb_proj, tb_mix, tb_fb, tb_tn = min(TB_PROJ, s), min(TB_MIX, s), min(TB_FFN_BWD, s), min(TB_TN, s)
    conv_cols = D_GROUP // N_DEV

    def pad_rows(a, rows):
        return jnp.pad(a, ((0, rows - a.shape[0]), (0, 0)))

    local = []
    for l in range(N_LAYERS):
        local += [w_in[l].T.astype(BF16), w_out[l].astype(BF16), w_up[l].astype(BF16), w_down[l].astype(BF16)]
    local.append(jnp.concatenate([pad_rows(short_conv_w[l], 8) for l in range(N_LAYERS)]
                                 + [pad_rows(conf_conv_w[l], HALO) for l in range(N_LAYERS)], axis=0))
    gathered = _exchange(local, scatter=False, name="gather_weights")
    wint = [gathered[4 * l].reshape(D_IN, D_MODEL) for l in range(N_LAYERS)]
    wout = [gathered[4 * l + 1].reshape(D_MODEL, D_MODEL) for l in range(N_LAYERS)]
    wup = [gathered[4 * l + 2] for l in range(N_LAYERS)]
    wdn = [gathered[4 * l + 3] for l in range(N_LAYERS)]
    conv_full = gathered[4 * N_LAYERS].transpose(1, 0, 2).reshape(-1, D_GROUP)
    scw = [conv_full[8 * l:8 * (l + 1)] for l in range(N_LAYERS)]
    ccw = [conv_full[8 * N_LAYERS + HALO * l:8 * N_LAYERS + HALO * (l + 1)] for l in range(N_LAYERS)]

    tril = jnp.tril(jnp.ones((CHUNK, CHUNK), dtype=bool))
    wm = [jnp.where(tril, gmlp_w_s[l], 0.0).astype(BF16) for l in range(N_LAYERS)]
    wmt = [w.transpose(0, 2, 1) for w in wm]
    bexp = [jnp.repeat(gmlp_b_s[l].T, HEAD_DIM, axis=1) for l in range(N_LAYERS)]

    def row(vec):
        return vec.reshape(1, -1)

    saved = []
    xc = x0
    for l in range(N_LAYERS):
        z, qkv = _in_proj_fwd(xc, row(norm_mix_g[l]), wint[l], tb_proj)
        ya, yb, yd = _mix_fwd(z, row(gmlp_v_g[l]), wm[l], bexp[l], scw[l], ccw[l], row(conf_ln_g[l]), row(conf_ln_b[l]), tb_mix)
        yc = _attn_fwd(qkv, TQ, TK)
        ys = (ya, yb, yc, yd)
        x1 = _out_proj_fwd(ys, xc, row(mix_out_g[l]), wout[l], tb_proj)
        saved.append((xc, z, qkv, ys, x1))
        xc = _ffn_fwd(x1, row(norm_ffn_g[l]), wup[l], wdn[l], tb_proj)
    dx, g_final, loss_part = _loss_head(xc, row(final_norm_g), target, tb_proj)
    loss = lax.psum(loss_part[0, 0], MESH_AXES)

    big_grads = [None] * (4 * N_LAYERS)
    small_grads = [None] * N_LAYERS
    for l in reversed(range(N_LAYERS)):
        xin, z, qkv, ys, x1 = saved[l]
        dx1, hb_ffn, dpre, sq, g_ffn = _ffn_bwd(x1, dx, row(norm_ffn_g[l]), wup[l], wdn[l], tb_fb)
        big_grads[4 * l + 2] = _tn_matmul(hb_ffn, dpre, N_DEV, False, True, tb_tn, "grad_w_up")
        big_grads[4 * l + 3] = _tn_matmul(sq, dx, N_DEV, True, False, tb_tn, "grad_w_down")
        dya, dyb, dyc, dyd, yn, g_mixout = _out_proj_bwd(dx1, ys, row(mix_out_g[l]), wout[l], tb_proj)
        big_grads[4 * l + 1] = _tn_matmul(yn, dx1, 1, False, False, tb_tn, "grad_w_out").reshape(N_DEV, D_MODEL // N_DEV, D_MODEL)
        dq, dk, dv = _attn_bwd(qkv, dyc, ys[2], TQ, TK)
        dz, g_vg, g_ws, g_bs, g_scw, g_ccw, g_lng, g_lnb = _mix_bwd(
            z, dya, dyb, dyd, dq, dk, dv, row(gmlp_v_g[l]), wm[l], wmt[l], bexp[l], scw[l], ccw[l],
            row(conf_ln_g[l]), row(conf_ln_b[l]), tb_mix)
        dx, hb_in, g_mix = _in_proj_bwd(xin, dz, dx1, row(norm_mix_g[l]), wint[l], tb_proj)
        big_grads[4 * l] = _tn_matmul(dz, hb_in, 1, False, False, tb_tn, "grad_w_in").reshape(N_DEV, D_IN // N_DEV, D_MODEL)
        small_grads[l] = dict(norm_mix_g=g_mix[0], gmlp_v_g=g_vg[0], gmlp_w_s=g_ws, gmlp_b_s=g_bs[:, :4].T,
                              short_conv_w=g_scw[:K_SHORT], conf_conv_w=g_ccw[:K_CONF], conf_ln_g=g_lng[0],
                              conf_ln_b=g_lnb[0], mix_out_g=g_mixout[0], norm_ffn_g=g_ffn[0])

    parts = _exchange(big_grads, scatter=True, name="scatter_grads")
    small_names = ["norm_mix_g", "gmlp_v_g", "gmlp_w_s", "gmlp_b_s", "short_conv_w", "conf_conv_w", "conf_ln_g",
                   "conf_ln_b", "mix_out_g", "norm_ffn_g"]
    small_list = [jnp.stack([small_grads[l][n] for l in range(N_LAYERS)]) for n in small_names] + [g_final[0]]
    small_shapes = [a.shape for a in small_list]
    small_parts = _exchange([_pack(small_list)], scatter=False, name="gather_small_grads")[0]

    given = dict(norm_mix_g=(norm_mix_g, m_norm_mix_g, v_norm_mix_g), gmlp_v_g=(gmlp_v_g, m_gmlp_v_g, v_gmlp_v_g),
                 gmlp_w_s=(gmlp_w_s, m_gmlp_w_s, v_gmlp_w_s), gmlp_b_s=(gmlp_b_s, m_gmlp_b_s, v_gmlp_b_s),
                 short_conv_w=(short_conv_w, m_short_conv_w, v_short_conv_w),
                 conf_conv_w=(conf_conv_w, m_conf_conv_w, v_conf_conv_w),
                 conf_ln_g=(conf_ln_g, m_conf_ln_g, v_conf_ln_g), conf_ln_b=(conf_ln_b, m_conf_ln_b, v_conf_ln_b),
                 mix_out_g=(mix_out_g, m_mix_out_g, v_mix_out_g), norm_ffn_g=(norm_ffn_g, m_norm_ffn_g, v_norm_ffn_g),
                 final_norm_g=(final_norm_g, m_final_norm_g, v_final_norm_g))
    sharded_small = ("short_conv_w", "conf_conv_w")

    def widen(a):
        full = jnp.zeros(a.shape[:-1] + (D_GROUP,), a.dtype)
        return lax.dynamic_update_slice(full, a, (0, 0, me * conv_cols))

    packed_state = []
    for k in range(3):
        packed_state.append(_pack([widen(given[n][k]) if n in sharded_small else given[n][k]
                                   for n in small_names + ["final_norm_g"]]))
    small_out = _reduce_adamw(small_parts, *packed_state, packed_state[0].shape[0], "adamw_small")
    small_res = {}
    for kind, packed in zip(("grad", "delta", "new_m", "new_v"), small_out):
        for n, val in zip(small_names + ["final_norm_g"], _unpack(packed, small_shapes)):
            if n in sharded_small:
                val = lax.dynamic_slice(val, (0, 0, me * conv_cols), val.shape[:-1] + (conv_cols,))
            small_res[kind, n] = val

    big_names = ["w_in", "w_out", "w_up", "w_down"]
    big_given = dict(w_in=(w_in, m_w_in, v_w_in), w_out=(w_out, m_w_out, v_w_out), w_up=(w_up, m_w_up, v_w_up),
                     w_down=(w_down, m_w_down, v_w_down))
    big_res = {}
    for j, n in enumerate(big_names):
        per_layer = []
        for l in range(N_LAYERS):
            state = [t[l].T if n == "w_in" else t[l] for t in big_given[n]]
            outs = _reduce_adamw(parts[4 * l + j], *state, TB_ADAM, "adamw_" + n)
            per_layer.append([o.T if n == "w_in" else o for o in outs])
        for k, kind in enumerate(("grad", "delta", "new_m", "new_v")):
            big_res[kind, n] = jnp.stack([per_layer[l][k] for l in range(N_LAYERS)])

    order = ["norm_mix_g", "w_in", "gmlp_v_g", "gmlp_w_s", "gmlp_b_s", "short_conv_w", "conf_conv_w", "conf_ln_g",
             "conf_ln_b", "mix_out_g", "w_out", "norm_ffn_g", "w_up", "w_down", "final_norm_g"]
    result = [loss, dx.reshape(x.shape)]
    for kind in ("grad", "delta", "new_m", "new_v"):
        for n in order:
            result.append(big_res[kind, n] if n in big_given else small_res[kind, n])
    return tuple(result)
```

```python
import jax
import jax.numpy as jnp
from jax import lax
from jax.experimental import pallas as pl
from jax.experimental.pallas import tpu as pltpu

F32 = jnp.float32
BF16 = jnp.bfloat16

D_MODEL = 1024
D_GROUP = 256
D_IN = 10 * D_GROUP
D_FF = 4 * D_MODEL
N_DEV = 8
N_LAYERS = 2
HEAD_DIM = 64
HEADS_PER_PAIR = 2
PAIR = HEADS_PER_PAIR * HEAD_DIM
CHUNK = 128
K_SHORT = 3
K_CONF = 31
HALO = 32
EPS = 1e-6
ATT_SCALE = HEAD_DIM ** -0.5
LOG_CUT = -104.0
MIB = 2 ** 20

ADAM_LR = 0.001
ADAM_B1 = 0.9
ADAM_B2 = 0.999
ADAM_EPS = 1e-08
ADAM_WD = 0.01
ADAM_STEP = 10

MESH_AXES = ("x", "y", "c")
GELU_C = 0.7978845608028654
GELU_A = 0.044715


def _mm(a, b):
    return jnp.dot(a, b, preferred_element_type=F32)


def _mm_nt(a, b):
    return lax.dot_general(a, b, (((1,), (1,)), ((), ())), preferred_element_type=F32)


def _mm_tn(a, b):
    return lax.dot_general(a, b, (((0,), (0,)), ((), ())), preferred_element_type=F32)


def _rms(x):
    return lax.rsqrt(jnp.mean(x * x, axis=-1, keepdims=True) + EPS)


def _rms_bwd(dy, xh, r):
    return r * (dy - xh * jnp.mean(dy * xh, axis=-1, keepdims=True))


def _sigmoid(x):
    return 1.0 / (1.0 + jnp.exp(-x))


def _whole(shape):
    return pl.BlockSpec(shape, lambda *_: (0,) * len(shape))


def _rows(tb, width, col=0):
    return pl.BlockSpec((tb, width), lambda i: (i, col))


def _params(semantics, vmem_mib):
    return pltpu.CompilerParams(dimension_semantics=semantics, vmem_limit_bytes=vmem_mib * MIB)


def _sds(shape, dtype=F32):
    return jax.ShapeDtypeStruct(shape, dtype)


def _split_bf16(v):
    hi = v.astype(BF16)
    lo = (v - hi.astype(F32)).astype(BF16)
    return hi, lo


def _exchange(arrays, scatter, name):
    n = len(arrays)

    def body(*refs):
        ins, outs = refs[:n], refs[n:2 * n]
        send_sems, recv_sems, local_sems = refs[2 * n:]
        x, y, c = lax.axis_index("x"), lax.axis_index("y"), lax.axis_index("c")
        me = 4 * x + 2 * y + c
        peers = []
        for k in range(1, N_DEV):
            px = 1 - x if (k >> 2) & 1 else x
            py = 1 - y if (k >> 1) & 1 else y
            pc = 1 - c if k & 1 else c
            peers.append((px, py, pc))

        local = []
        for a in range(n):
            cp = pltpu.make_async_copy(ins[a].at[me] if scatter else ins[a], outs[a].at[me], local_sems.at[a])
            cp.start()
            local.append(cp)
        remote = []
        for k, peer in enumerate(peers):
            pf = 4 * peer[0] + 2 * peer[1] + peer[2]
            for a in range(n):
                cp = pltpu.make_async_remote_copy(
                    src_ref=ins[a].at[pf] if scatter else ins[a], dst_ref=outs[a].at[me],
                    send_sem=send_sems.at[a, k], recv_sem=recv_sems.at[a, k],
                    device_id=peer, device_id_type=pl.DeviceIdType.MESH)
                cp.start()
                remote.append(cp)
        for cp in remote:
            cp.wait_send()
        for k, peer in enumerate(peers):
            pf = 4 * peer[0] + 2 * peer[1] + peer[2]
            for a in range(n):
                pltpu.make_async_remote_copy(
                    src_ref=ins[a].at[pf] if scatter else ins[a], dst_ref=outs[a].at[pf],
                    send_sem=send_sems.at[a, k], recv_sem=recv_sems.at[a, k],
                    device_id=peer, device_id_type=pl.DeviceIdType.MESH).wait_recv()
        for cp in local:
            cp.wait()

    any_spec = pl.BlockSpec(memory_space=pl.ANY)
    out_shape = [_sds(a.shape if scatter else (N_DEV,) + a.shape, a.dtype) for a in arrays]
    return pl.pallas_call(
        body, name=name, out_shape=out_shape,
        in_specs=[any_spec] * n, out_specs=[any_spec] * n,
        scratch_shapes=[pltpu.SemaphoreType.DMA((n, N_DEV - 1)), pltpu.SemaphoreType.DMA((n, N_DEV - 1)),
                        pltpu.SemaphoreType.DMA((n,))],
    )(*arrays)


def _in_proj_fwd(x, g, wint, tb):
    s = x.shape[0]

    def body(x_ref, g_ref, w_ref, z_ref, qkv_ref):
        xv = x_ref[...]
        h = (xv * _rms(xv) * g_ref[...]).astype(BF16)
        z = _mm_nt(h, w_ref[...])
        z_ref[...] = z
        qkv_ref[:, 0:D_GROUP] = (z[:, 5 * D_GROUP:6 * D_GROUP] * ATT_SCALE).astype(BF16)
        qkv_ref[:, D_GROUP:3 * D_GROUP] = z[:, 6 * D_GROUP:8 * D_GROUP].astype(BF16)

    return pl.pallas_call(
        body, name="in_proj_fwd", grid=(s // tb,),
        in_specs=[_rows(tb, D_MODEL), _whole((1, D_MODEL)), _whole((D_IN, D_MODEL))],
        out_specs=[_rows(tb, D_IN), _rows(tb, 3 * D_GROUP)],
        out_shape=[_sds((s, D_IN)), _sds((s, 3 * D_GROUP), BF16)],
        compiler_params=_params(("parallel",), 48),
    )(x, g, wint)


def _gelu(x):
    return 0.5 * x * (1.0 + jnp.tanh(GELU_C * (x + GELU_A * x * x * x)))


def _gelu_grad(x):
    t = jnp.tanh(GELU_C * (x + GELU_A * x * x * x))
    return 0.5 * (1.0 + t) + 0.5 * x * (1.0 - t * t) * GELU_C * (1.0 + 3.0 * GELU_A * x * x)


def _head_lane(width):
    return lax.broadcasted_iota(jnp.int32, (1, width), 1) // HEAD_DIM


def _gating_chunk(wm_ref, bexp_ref, vc, lane_h):
    f = bexp_ref[...]
    for h in range(D_GROUP // HEAD_DIM):
        f = f + _mm(wm_ref[h], jnp.where(lane_h == h, vc, 0))
    return f


def _halo_specs(s, tb, width_blocks):
    per = tb // HALO
    prev = pl.BlockSpec((HALO, width_blocks), lambda i: (jnp.maximum(i * per - 1, 0), 0))
    nxt = pl.BlockSpec((HALO, width_blocks), lambda i: (jnp.minimum((i + 1) * per, s // HALO - 1), 0))
    return prev, nxt


def _mix_fwd(z, vg, wm, bexp, scw, ccw, lng, lnb, tb):
    s = z.shape[0]
    prev_spec, _ = _halo_specs(s, tb, D_IN)

    def body(z_ref, zp_ref, vg_ref, wm_ref, bexp_ref, scw_ref, ccw_ref, lng_ref, lnb_ref,
             ya_ref, yb_ref, yd_ref, pbuf, hbuf):
        keep = (pl.program_id(0) > 0).astype(F32)
        lane_h = _head_lane(D_GROUP)
        ga = _gelu(z_ref[:, 0:2 * D_GROUP])
        u, v = ga[:, :D_GROUP], ga[:, D_GROUP:]
        vn = (v * _rms(v) * vg_ref[...]).astype(BF16)
        for n in range(tb // CHUNK):
            rows = slice(n * CHUNK, (n + 1) * CHUNK)
            ya_ref[rows, :] = u[rows] * _gating_chunk(wm_ref, bexp_ref, vn[rows], lane_h)
        p = z_ref[:, 3 * D_GROUP:4 * D_GROUP] * z_ref[:, 4 * D_GROUP:5 * D_GROUP]
        pbuf[0:HALO, :] = zp_ref[:, 3 * D_GROUP:4 * D_GROUP] * zp_ref[:, 4 * D_GROUP:5 * D_GROUP] * keep
        pbuf[HALO:HALO + tb, :] = p
        cv = scw_ref[K_SHORT - 1:K_SHORT, :] * p
        for k in range(K_SHORT - 1):
            cv = cv + scw_ref[k:k + 1, :] * pbuf[pl.ds(HALO - (K_SHORT - 1) + k, tb), :]
        yb_ref[...] = z_ref[:, 2 * D_GROUP:3 * D_GROUP] * cv
        hbuf[0:HALO, :] = zp_ref[:, 8 * D_GROUP:9 * D_GROUP] * _sigmoid(zp_ref[:, 9 * D_GROUP:10 * D_GROUP]) * keep
        hbuf[HALO:HALO + tb, :] = z_ref[:, 8 * D_GROUP:9 * D_GROUP] * _sigmoid(z_ref[:, 9 * D_GROUP:10 * D_GROUP])
        c = jnp.zeros((tb, D_GROUP), F32)
        for k in range(K_CONF):
            c = c + ccw_ref[k:k + 1, :] * hbuf[pl.ds(HALO - (K_CONF - 1) + k, tb), :]
        xc = c - jnp.mean(c, axis=-1, keepdims=True)
        ln = xc * lax.rsqrt(jnp.mean(xc * xc, axis=-1, keepdims=True) + EPS) * lng_ref[...] + lnb_ref[...]
        yd_ref[...] = ln * _sigmoid(ln)

    grp = _rows(tb, D_GROUP)
    return pl.pallas_call(
        body, name="mix_fwd", grid=(s // tb,),
        in_specs=[_rows(tb, D_IN), prev_spec, _whole((1, D_GROUP)), _whole(wm.shape), _whole(bexp.shape),
                  _whole(scw.shape), _whole(ccw.shape), _whole((1, D_GROUP)), _whole((1, D_GROUP))],
        out_specs=[grp, grp, grp],
        out_shape=[_sds((s, D_GROUP))] * 3,
        scratch_shapes=[pltpu.VMEM((HALO + tb, D_GROUP), F32), pltpu.VMEM((HALO + tb, D_GROUP), F32)],
        compiler_params=_params(("parallel",), 40),
    )(z, z, vg, wm, bexp, scw, ccw, lng, lnb)


def _stick_tile(qh, kt, causal, c, upper):
    x = _mm_nt(qh, kt)
    soft = jnp.log1p(jnp.exp(-jnp.abs(x)))
    lb = jnp.minimum(x, 0.0) - soft
    lom = jnp.where(causal, -jnp.maximum(x, 0.0) - soft, 0.0)
    hi, lo = _split_bf16(lom)
    stick = c + _mm(hi, upper) + _mm(lo, upper)
    w = jnp.where(causal, jnp.exp(lb + stick), 0.0)
    return w, lb, lom


def _causal_tile(qi, tq, k0, tk):
    qpos = qi * tq + lax.broadcasted_iota(jnp.int32, (tq, 1), 0)
    return k0 + lax.broadcasted_iota(jnp.int32, (1, tk), 1) < qpos


def _sticks_alive(cs):
    longest = cs[0]
    for c in cs[1:]:
        longest = jnp.maximum(longest, c)
    return (jnp.max(longest) > LOG_CUT).astype(jnp.int32)


def _walk(body, qi, tq, tk, init):
    start = (((qi + 1) * tq - 1) // tk, jnp.int32(1)) + tuple(init)
    return lax.while_loop(lambda cr: jnp.logical_and(cr[0] >= 0, cr[1] > 0), body, start)[2:]


def _attn_fwd(qkv, tq, tk):
    s = qkv.shape[0]
    n_heads = D_GROUP // HEAD_DIM

    def body(q_ref, k_ref, v_ref, o_ref):
        qi = pl.program_id(0)
        q = q_ref[...]
        lane_h = _head_lane(D_GROUP)
        upper = (lax.broadcasted_iota(jnp.int32, (tk, tk), 0) > lax.broadcasted_iota(jnp.int32, (tk, tk), 1)).astype(BF16)
        qhs = [jnp.where(lane_h == h, q, 0) for h in range(n_heads)]

        def step(carry):
            kb, _, acc = carry[:3]
            cs = list(carry[3:])
            k0 = pl.multiple_of(kb * tk, tk)
            kt = k_ref[pl.ds(k0, tk), :]
            vt = v_ref[pl.ds(k0, tk), :]
            causal = _causal_tile(qi, tq, k0, tk)
            for h in range(n_heads):
                w, _, lom = _stick_tile(qhs[h], kt, causal, cs[h], upper)
                acc = acc + _mm(w.astype(BF16), jnp.where(lane_h == h, vt, 0))
                cs[h] = cs[h] + jnp.sum(lom, axis=1, keepdims=True)
            return (kb - 1, _sticks_alive(cs), acc) + tuple(cs)

        init = [jnp.zeros((tq, D_GROUP), F32)] + [jnp.zeros((tq, 1), F32)] * n_heads
        o_ref[...] = _walk(step, qi, tq, tk, init)[0]

    return pl.pallas_call(
        body, name="attn_fwd", grid=(s // tq,),
        in_specs=[pl.BlockSpec((tq, D_GROUP), lambda qi: (qi, 0)),
                  pl.BlockSpec((s, D_GROUP), lambda qi: (0, 1), pipeline_mode=pl.Buffered(1)),
                  pl.BlockSpec((s, D_GROUP), lambda qi: (0, 2), pipeline_mode=pl.Buffered(1))],
        out_specs=pl.BlockSpec((tq, D_GROUP), lambda qi: (qi, 0)),
        out_shape=_sds((s, D_GROUP)),
        compiler_params=_params(("parallel",), 40),
    )(qkv, qkv, qkv)


def _out_proj_fwd(ys, x, mg, wout, tb):
    s = x.shape[0]

    def body(ya_ref, yb_ref, yc_ref, yd_ref, x_ref, mg_ref, w_ref, o_ref):
        acc = x_ref[...]
        for gi, y_ref in enumerate((ya_ref, yb_ref, yc_ref, yd_ref)):
            cols = slice(gi * D_GROUP, (gi + 1) * D_GROUP)
            y = y_ref[...]
            acc = acc + _mm((y * _rms(y) * mg_ref[:, cols]).astype(BF16), w_ref[cols, :])
        o_ref[...] = acc

    grp = _rows(tb, D_GROUP)
    return pl.pallas_call(
        body, name="out_proj_fwd", grid=(s // tb,),
        in_specs=[grp, grp, grp, grp, _rows(tb, D_MODEL), _whole((1, D_MODEL)), _whole((D_MODEL, D_MODEL))],
        out_specs=_rows(tb, D_MODEL), out_shape=_sds((s, D_MODEL)),
        compiler_params=_params(("parallel",), 32),
    )(*ys, x, mg, wout)


def _ffn_fwd(x, g, wup, wdn, tb):
    s = x.shape[0]
    ff = D_FF // N_DEV

    def body(x_ref, g_ref, wu_ref, wd_ref, o_ref):
        xv = x_ref[...]
        h = (xv * _rms(xv) * g_ref[...]).astype(BF16)
        acc = xv
        for d in range(N_DEV):
            a = jnp.maximum(_mm(h, wu_ref[d]), 0.0)
            acc = acc + _mm((a * a).astype(BF16), wd_ref[d])
        o_ref[...] = acc

    return pl.pallas_call(
        body, name="ffn_fwd", grid=(s // tb,),
        in_specs=[_rows(tb, D_MODEL), _whole((1, D_MODEL)), _whole((N_DEV, D_MODEL, ff)), _whole((N_DEV, ff, D_MODEL))],
        out_specs=_rows(tb, D_MODEL), out_shape=_sds((s, D_MODEL)),
        compiler_params=_params(("parallel",), 56),
    )(x, g, wup, wdn)


def _loss_head(x, g, tgt, tb):
    s = x.shape[0]

    def body(x_ref, g_ref, t_ref, dx_ref, dg_ref, loss_ref):
        @pl.when(pl.program_id(0) == 0)
        def _():
            dg_ref[...] = jnp.zeros_like(dg_ref)
            loss_ref[...] = jnp.zeros_like(loss_ref)

        xv = x_ref[...]
        r = _rms(xv)
        xh = xv * r
        err = xh * g_ref[...] - t_ref[...]
        loss_ref[...] += 0.5 * jnp.sum(jnp.mean(err * err, axis=-1, keepdims=True))
        dy = err * (1.0 / D_MODEL)
        dg_ref[...] += jnp.sum(dy * xh, axis=0, keepdims=True)
        dx_ref[...] = _rms_bwd(dy * g_ref[...], xh, r)

    return pl.pallas_call(
        body, name="loss_head", grid=(s // tb,),
        in_specs=[_rows(tb, D_MODEL), _whole((1, D_MODEL)), _rows(tb, D_MODEL)],
        out_specs=[_rows(tb, D_MODEL), _whole((1, D_MODEL)), _whole((8, 128))],
        out_shape=[_sds((s, D_MODEL)), _sds((1, D_MODEL)), _sds((8, 128))],
        compiler_params=_params(("arbitrary",), 32),
    )(x, g, tgt)


def _ffn_bwd(x1, dx2, g, wup, wdn, tb):
    s = x1.shape[0]
    ff = D_FF // N_DEV

    def body(x_ref, dy_ref, g_ref, wu_ref, wd_ref, dx_ref, hb_ref, dpre_ref, sq_ref, dg_ref):
        @pl.when(pl.program_id(0) == 0)
        def _():
            dg_ref[...] = jnp.zeros_like(dg_ref)

        xv = x_ref[...]
        r = _rms(xv)
        xh = xv * r
        hb = (xh * g_ref[...]).astype(BF16)
        hb_ref[...] = hb
        dyv = dy_ref[...]
        dyb = dyv.astype(BF16)
        dh = jnp.zeros((tb, D_MODEL), F32)
        for d in range(N_DEV):
            cols = slice(d * ff, (d + 1) * ff)
            a = jnp.maximum(_mm(hb, wu_ref[d]), 0.0)
            sq_ref[:, cols] = (a * a).astype(BF16)
            dpre = (_mm_nt(dyb, wd_ref[d]) * (2.0 * a)).astype(BF16)
            dpre_ref[:, cols] = dpre
            dh = dh + _mm_nt(dpre, wu_ref[d])
        dg_ref[...] += jnp.sum(dh * xh, axis=0, keepdims=True)
        dx_ref[...] = dyv + _rms_bwd(dh * g_ref[...], xh, r)

    return pl.pallas_call(
        body, name="ffn_bwd", grid=(s // tb,),
        in_specs=[_rows(tb, D_MODEL), _rows(tb, D_MODEL), _whole((1, D_MODEL)),
                  _whole((N_DEV, D_MODEL, ff)), _whole((N_DEV, ff, D_MODEL))],
        out_specs=[_rows(tb, D_MODEL), _rows(tb, D_MODEL), _rows(tb, D_FF), _rows(tb, D_FF), _whole((1, D_MODEL))],
        out_shape=[_sds((s, D_MODEL)), _sds((s, D_MODEL), BF16), _sds((s, D_FF), BF16), _sds((s, D_FF), BF16),
                   _sds((1, D_MODEL))],
        compiler_params=_params(("arbitrary",), 60),
    )(x1, dx2, g, wup, wdn)


def _tn_matmul(a, b, nd, a_blocked, b_blocked, tb, name):
    s = a.shape[0]
    ka = a.shape[1] // nd if a_blocked else a.shape[1]
    nb = b.shape[1] // nd if b_blocked else b.shape[1]

    def body(a_ref, b_ref, o_ref):
        @pl.when(pl.program_id(1) == 0)
        def _():
            o_ref[...] = jnp.zeros_like(o_ref)

        o_ref[0] += _mm_tn(a_ref[...].astype(BF16), b_ref[...].astype(BF16))

    return pl.pallas_call(
        body, name=name, grid=(nd, s // tb),
        in_specs=[pl.BlockSpec((tb, ka), (lambda d, i: (i, d)) if a_blocked else (lambda d, i: (i, 0))),
                  pl.BlockSpec((tb, nb), (lambda d, i: (i, d)) if b_blocked else (lambda d, i: (i, 0)))],
        out_specs=pl.BlockSpec((1, ka, nb), lambda d, i: (d, 0, 0)),
        out_shape=_sds((nd, ka, nb)),
        compiler_params=_params(("parallel", "arbitrary"), 56),
    )(a, b)


def _out_proj_bwd(dx1, ys, mg, wout, tb):
    s = dx1.shape[0]

    def body(dx_ref, ya_ref, yb_ref, yc_ref, yd_ref, mg_ref, w_ref,
             dya_ref, dyb_ref, dyc_ref, dyd_ref, yn_ref, dmg_ref):
        @pl.when(pl.program_id(0) == 0)
        def _():
            dmg_ref[...] = jnp.zeros_like(dmg_ref)

        dyn = _mm_nt(dx_ref[...].astype(BF16), w_ref[...])
        groups = ((ya_ref, dya_ref), (yb_ref, dyb_ref), (yc_ref, dyc_ref), (yd_ref, dyd_ref))
        for gi, (y_ref, dy_ref) in enumerate(groups):
            cols = slice(gi * D_GROUP, (gi + 1) * D_GROUP)
            y = y_ref[...]
            r = _rms(y)
            n = y * r
            gain = mg_ref[:, cols]
            dn = dyn[:, cols]
            yn_ref[:, cols] = (n * gain).astype(BF16)
            dmg_ref[:, cols] += jnp.sum(dn * n, axis=0, keepdims=True)
            dy_ref[...] = _rms_bwd(dn * gain, n, r)

    grp = _rows(tb, D_GROUP)
    return pl.pallas_call(
        body, name="out_proj_bwd", grid=(s // tb,),
        in_specs=[_rows(tb, D_MODEL), grp, grp, grp, grp, _whole((1, D_MODEL)), _whole((D_MODEL, D_MODEL))],
        out_specs=[grp, grp, grp, grp, _rows(tb, D_MODEL), _whole((1, D_MODEL))],
        out_shape=[_sds((s, D_GROUP))] * 4 + [_sds((s, D_MODEL), BF16), _sds((1, D_MODEL))],
        compiler_params=_params(("arbitrary",), 32),
    )(dx1, *ys, mg, wout)


def _attn_bwd(qkv, do, o, tq, tk):
    s = qkv.shape[0]
    nq = s // tq
    n_pairs = D_GROUP // PAIR

    def body(q_ref, k_ref, v_ref, do_ref, o_ref, dq_ref, dk_hbm, dv_hbm, dk_acc, dv_acc, sems):
        hp, qi = pl.program_id(0), pl.program_id(1)

        @pl.when(qi == 0)
        def _():
            dk_acc[...] = jnp.zeros_like(dk_acc)
            dv_acc[...] = jnp.zeros_like(dv_acc)

        q = q_ref[...]
        dob = do_ref[...].astype(BF16)
        prod = dob.astype(F32) * o_ref[...]
        lane_h = _head_lane(PAIR)
        row = lax.broadcasted_iota(jnp.int32, (tk, tk), 0)
        col = lax.broadcasted_iota(jnp.int32, (tk, tk), 1)
        upper = (row > col).astype(BF16)
        upper_eq = (row >= col).astype(BF16)
        heads = []
        for h in range(HEADS_PER_PAIR):
            in_head = lane_h == h
            total = jnp.sum(jnp.where(in_head, prod, 0.0), axis=1, keepdims=True)
            heads.append((in_head, jnp.where(in_head, q, 0), jnp.where(in_head, dob, 0), total))

        def step(carry):
            kb, _, acc = carry[:3]
            cs = list(carry[3:3 + HEADS_PER_PAIR])
            nears = list(carry[3 + HEADS_PER_PAIR:])
            k0 = pl.multiple_of(kb * tk, tk)
            kt = k_ref[pl.ds(k0, tk), :]
            vt = v_ref[pl.ds(k0, tk), :]
            causal = _causal_tile(qi, tq, k0, tk)
            dk_t = jnp.zeros((tk, PAIR), F32)
            dv_t = jnp.zeros((tk, PAIR), F32)
            for h, (in_head, qh, doh, total) in enumerate(heads):
                w, lb, lom = _stick_tile(qh, kt, causal, cs[h], upper)
                wb = w.astype(BF16)
                gw = _mm_nt(doh, vt) * wb.astype(F32)
                hi, lo = _split_bf16(gw)
                far = total - nears[h] - _mm(hi, upper_eq) - _mm(lo, upper_eq)
                beta = jnp.exp(lb)
                dxb = jnp.where(causal, gw * (1.0 - beta) - far * beta, 0.0).astype(BF16)
                acc = acc + _mm(dxb, jnp.where(in_head, kt, 0))
                dk_t = dk_t + _mm_tn(dxb, qh)
                dv_t = dv_t + _mm_tn(wb, doh)
                cs[h] = cs[h] + jnp.sum(lom, axis=1, keepdims=True)
                nears[h] = nears[h] + jnp.sum(gw, axis=1, keepdims=True)
            dk_acc[pl.ds(k0, tk), :] += dk_t
            dv_acc[pl.ds(k0, tk), :] += dv_t
            return (kb - 1, _sticks_alive(cs), acc) + tuple(cs) + tuple(nears)

        init = [jnp.zeros((tq, PAIR), F32)] + [jnp.zeros((tq, 1), F32)] * (2 * HEADS_PER_PAIR)
        dq_ref[...] = _walk(step, qi, tq, tk, init)[0]

        @pl.when(qi == nq - 1)
        def _():
            ck = pltpu.make_async_copy(dk_acc, dk_hbm.at[hp], sems.at[0])
            cv = pltpu.make_async_copy(dv_acc, dv_hbm.at[hp], sems.at[1])
            ck.start()
            cv.start()
            ck.wait()
            cv.wait()

    blk = pl.BlockSpec((tq, PAIR), lambda hp, qi: (qi, hp))
    any_spec = pl.BlockSpec(memory_space=pl.ANY)
    return pl.pallas_call(
        body, name="attn_bwd", grid=(n_pairs, nq),
        in_specs=[blk, pl.BlockSpec((s, PAIR), lambda hp, qi: (0, 2 + hp)),
                  pl.BlockSpec((s, PAIR), lambda hp, qi: (0, 4 + hp)), blk, blk],
        out_specs=[blk, any_spec, any_spec],
        out_shape=[_sds((s, D_GROUP)), _sds((n_pairs, s, PAIR)), _sds((n_pairs, s, PAIR))],
        scratch_shapes=[pltpu.VMEM((s, PAIR), F32), pltpu.VMEM((s, PAIR), F32), pltpu.SemaphoreType.DMA((2,))],
        compiler_params=_params(("arbitrary", "arbitrary"), 56),
    )(qkv, qkv, qkv, do, o)


def _mix_bwd(z, dya, dyb, dyd, dq, dk, dv, vg, wm, wmt, bexp, scw, ccw, lng, lnb, tb):
    s = z.shape[0]
    n_steps = s // tb
    prev_spec, next_spec = _halo_specs(s, tb, D_IN)
    _, next_grp = _halo_specs(s, tb, D_GROUP)
    ext = tb + HALO

    def body(z_ref, zp_ref, zn_ref, dya_ref, dyb_ref, dybn_ref, dyd_ref, dydn_ref, dq_ref, dk0_ref, dk1_ref,
             dv0_ref, dv1_ref, vg_ref, wm_ref, wmt_ref, bexp_ref, scw_ref, ccw_ref, lng_ref, lnb_ref,
             dz_ref, dvg_ref, dws_ref, dbs_ref, dscw_ref, dccw_ref, dlng_ref, dlnb_ref,
             pbuf, hbuf, gbuf, cbuf, dubuf, dvnbuf):
        i = pl.program_id(0)

        @pl.when(i == 0)
        def _():
            for ref in (dvg_ref, dws_ref, dbs_ref, dscw_ref, dccw_ref, dlng_ref, dlnb_ref):
                ref[...] = jnp.zeros_like(ref)

        keep_prev = (i > 0).astype(F32)
        keep_next = (i < n_steps - 1).astype(F32)
        lane_h = _head_lane(D_GROUP)

        za = z_ref[:, 0:2 * D_GROUP]
        ga = _gelu(za)
        u, v = ga[:, :D_GROUP], ga[:, D_GROUP:]
        r = _rms(v)
        vh = v * r
        vn = (vh * vg_ref[...]).astype(BF16)
        tril = lax.broadcasted_iota(jnp.int32, (CHUNK, CHUNK), 0) >= lax.broadcasted_iota(jnp.int32, (CHUNK, CHUNK), 1)
        dbias = jnp.zeros((CHUNK, D_GROUP), F32)
        for n in range(tb // CHUNK):
            rows = slice(n * CHUNK, (n + 1) * CHUNK)
            vc = vn[rows]
            dy = dya_ref[rows, :]
            dubuf[rows, :] = dy * _gating_chunk(wm_ref, bexp_ref, vc, lane_h)
            df = dy * u[rows]
            dfb = df.astype(BF16)
            dvn = jnp.zeros((CHUNK, D_GROUP), F32)
            for h in range(D_GROUP // HEAD_DIM):
                dfh = jnp.where(lane_h == h, dfb, 0)
                dvn = dvn + _mm(wmt_ref[h], dfh)
                dws_ref[h] += jnp.where(tril, _mm_nt(dfh, vc), 0.0)
            dvnbuf[rows, :] = dvn
            dbias = dbias + df
        for h in range(D_GROUP // HEAD_DIM):
            per_head = jnp.sum(jnp.where(lane_h == h, dbias, 0.0), axis=1, keepdims=True)
            dbs_ref[...] += per_head * (lax.broadcasted_iota(jnp.int32, (1, CHUNK), 1) == h).astype(F32)
        dvn = dvnbuf[...]
        dvg_ref[...] += jnp.sum(dvn * vh, axis=0, keepdims=True)
        dgelu = _gelu_grad(za)
        dz_ref[:, 0:D_GROUP] = (dubuf[...] * dgelu[:, :D_GROUP]).astype(BF16)
        dz_ref[:, D_GROUP:2 * D_GROUP] = (_rms_bwd(dvn * vg_ref[...], vh, r) * dgelu[:, D_GROUP:]).astype(BF16)

        gate_b = z_ref[:, 2 * D_GROUP:3 * D_GROUP]
        gate_c = z_ref[:, 3 * D_GROUP:4 * D_GROUP]
        hh = z_ref[:, 4 * D_GROUP:5 * D_GROUP]
        p = gate_c * hh
        pbuf[0:HALO, :] = zp_ref[:, 3 * D_GROUP:4 * D_GROUP] * zp_ref[:, 4 * D_GROUP:5 * D_GROUP] * keep_prev
        pbuf[HALO:HALO + tb, :] = p
        dyb_v = dyb_ref[...]
        dcv = dyb_v * gate_b
        gbuf[0:tb, :] = dcv
        gbuf[tb:ext, :] = dybn_ref[...] * zn_ref[:, 2 * D_GROUP:3 * D_GROUP] * keep_next
        cv = scw_ref[K_SHORT - 1:K_SHORT, :] * p
        dp = scw_ref[K_SHORT - 1:K_SHORT, :] * dcv
        dscw_ref[K_SHORT - 1:K_SHORT, :] += jnp.sum(dcv * p, axis=0, keepdims=True)
        for k in range(K_SHORT - 1):
            shifted = pbuf[pl.ds(HALO - (K_SHORT - 1) + k, tb), :]
            cv = cv + scw_ref[k:k + 1, :] * shifted
            dp = dp + scw_ref[k:k + 1, :] * gbuf[pl.ds(K_SHORT - 1 - k, tb), :]
            dscw_ref[k:k + 1, :] += jnp.sum(dcv * shifted, axis=0, keepdims=True)
        dz_ref[:, 2 * D_GROUP:3 * D_GROUP] = (dyb_v * cv).astype(BF16)
        dz_ref[:, 3 * D_GROUP:4 * D_GROUP] = (dp * hh).astype(BF16)
        dz_ref[:, 4 * D_GROUP:5 * D_GROUP] = (dp * gate_c).astype(BF16)

        dz_ref[:, 5 * D_GROUP:6 * D_GROUP] = (dq_ref[...] * ATT_SCALE).astype(BF16)
        dz_ref[:, 6 * D_GROUP:6 * D_GROUP + PAIR] = dk0_ref[...].astype(BF16)
        dz_ref[:, 6 * D_GROUP + PAIR:7 * D_GROUP] = dk1_ref[...].astype(BF16)
        dz_ref[:, 7 * D_GROUP:7 * D_GROUP + PAIR] = dv0_ref[...].astype(BF16)
        dz_ref[:, 7 * D_GROUP + PAIR:8 * D_GROUP] = dv1_ref[...].astype(BF16)

        a = z_ref[:, 8 * D_GROUP:9 * D_GROUP]
        sg = _sigmoid(z_ref[:, 9 * D_GROUP:10 * D_GROUP])
        hbuf[0:HALO, :] = zp_ref[:, 8 * D_GROUP:9 * D_GROUP] * _sigmoid(zp_ref[:, 9 * D_GROUP:10 * D_GROUP]) * keep_prev
        hbuf[HALO:HALO + tb, :] = a * sg
        hbuf[HALO + tb:HALO + ext, :] = zn_ref[:, 8 * D_GROUP:9 * D_GROUP] * _sigmoid(zn_ref[:, 9 * D_GROUP:10 * D_GROUP])
        c = jnp.zeros((ext, D_GROUP), F32)
        for k in range(K_CONF):
            c = c + ccw_ref[k:k + 1, :] * hbuf[pl.ds(HALO - (K_CONF - 1) + k, ext), :]
        xc = c - jnp.mean(c, axis=-1, keepdims=True)
        rs = lax.rsqrt(jnp.mean(xc * xc, axis=-1, keepdims=True) + EPS)
        xh = xc * rs
        ln = xh * lng_ref[...] + lnb_ref[...]
        sl = _sigmoid(ln)
        dy_ext = jnp.concatenate([dyd_ref[...], dydn_ref[...] * keep_next], axis=0)
        dln = dy_ext * sl * (1.0 + ln * (1.0 - sl))
        dlng_ref[...] += jnp.sum(dln[:tb] * xh[:tb], axis=0, keepdims=True)
        dlnb_ref[...] += jnp.sum(dln[:tb], axis=0, keepdims=True)
        dxh = dln * lng_ref[...]
        dc = rs * (dxh - jnp.mean(dxh, axis=-1, keepdims=True) - xh * jnp.mean(dxh * xh, axis=-1, keepdims=True))
        cbuf[...] = dc
        dc_blk = dc[:tb]
        dhd = jnp.zeros((tb, D_GROUP), F32)
        for k in range(K_CONF):
            dhd = dhd + ccw_ref[k:k + 1, :] * cbuf[pl.ds(K_CONF - 1 - k, tb), :]
            dccw_ref[k:k + 1, :] += jnp.sum(dc_blk * hbuf[pl.ds(HALO - (K_CONF - 1) + k, tb), :], axis=0, keepdims=True)
        dz_ref[:, 8 * D_GROUP:9 * D_GROUP] = (dhd * sg).astype(BF16)
        dz_ref[:, 9 * D_GROUP:10 * D_GROUP] = (dhd * a * sg * (1.0 - sg)).astype(BF16)

    grp = _rows(tb, D_GROUP)
    pair0 = pl.BlockSpec((None, tb, PAIR), lambda i: (0, i, 0))
    pair1 = pl.BlockSpec((None, tb, PAIR), lambda i: (1, i, 0))
    small = [_sds((1, D_GROUP)), _sds((4, CHUNK, CHUNK)), _sds((CHUNK, CHUNK)), _sds((8, D_GROUP)),
             _sds((HALO, D_GROUP)), _sds((1, D_GROUP)), _sds((1, D_GROUP))]
    return pl.pallas_call(
        body, name="mix_bwd", grid=(n_steps,),
        in_specs=[_rows(tb, D_IN), prev_spec, next_spec, grp, grp, next_grp, grp, next_grp, grp, pair0, pair1, pair0, pair1,
                  _whole((1, D_GROUP)), _whole(wm.shape), _whole(wmt.shape), _whole(bexp.shape), _whole(scw.shape),
                  _whole(ccw.shape), _whole((1, D_GROUP)), _whole((1, D_GROUP))],
        out_specs=[_rows(tb, D_IN)] + [_whole(t.shape) for t in small],
        out_shape=[_sds((s, D_IN), BF16)] + small,
        scratch_shapes=[pltpu.VMEM((HALO + tb, D_GROUP), F32), pltpu.VMEM((HALO + ext, D_GROUP), F32),
                        pltpu.VMEM((ext, D_GROUP), F32), pltpu.VMEM((ext, D_GROUP), F32),
                        pltpu.VMEM((tb, D_GROUP), F32), pltpu.VMEM((tb, D_GROUP), F32)],
        compiler_params=_params(("arbitrary",), 48),
    )(z, z, z, dya, dyb, dyb, dyd, dyd, dq, dk, dk, dv, dv, vg, wm, wmt, bexp, scw, ccw, lng, lnb)


def _in_proj_bwd(x, dz, dres, g, wint, tb):
    s = x.shape[0]

    def body(x_ref, dz_ref, dr_ref, g_ref, w_ref, dx_ref, hb_ref, dg_ref):
        @pl.when(pl.program_id(0) == 0)
        def _():
            dg_ref[...] = jnp.zeros_like(dg_ref)

        xv = x_ref[...]
        r = _rms(xv)
        xh = xv * r
        hb_ref[...] = (xh * g_ref[...]).astype(BF16)
        dh = _mm(dz_ref[...], w_ref[...])
        dg_ref[...] += jnp.sum(dh * xh, axis=0, keepdims=True)
        dx_ref[...] = dr_ref[...] + _rms_bwd(dh * g_ref[...], xh, r)

    return pl.pallas_call(
        body, name="in_proj_bwd", grid=(s // tb,),
        in_specs=[_rows(tb, D_MODEL), _rows(tb, D_IN), _rows(tb, D_MODEL), _whole((1, D_MODEL)), _whole((D_IN, D_MODEL))],
        out_specs=[_rows(tb, D_MODEL), _rows(tb, D_MODEL), _whole((1, D_MODEL))],
        out_shape=[_sds((s, D_MODEL)), _sds((s, D_MODEL), BF16), _sds((1, D_MODEL))],
        compiler_params=_params(("arbitrary",), 48),
    )(x, dz, dres, g, wint)


def _adamw(w, g, m, v):
    m = ADAM_B1 * m + (1.0 - ADAM_B1) * g
    v = ADAM_B2 * v + (1.0 - ADAM_B2) * (g * g)
    m_hat = m / (1.0 - ADAM_B1 ** ADAM_STEP)
    v_hat = v / (1.0 - ADAM_B2 ** ADAM_STEP)
    delta = -ADAM_LR * (m_hat / (jnp.sqrt(v_hat) + ADAM_EPS) + ADAM_WD * w)
    return delta, m, v


def _reduce_adamw(parts, w, m, v, tb, name):
    rows, cols = w.shape

    def body(p_ref, w_ref, m_ref, v_ref, g_ref, d_ref, m2_ref, v2_ref):
        g = p_ref[0]
        for j in range(1, N_DEV):
            g = g + p_ref[j]
        g_ref[...] = g
        d_ref[...], m2_ref[...], v2_ref[...] = _adamw(w_ref[...], g, m_ref[...], v_ref[...])

    blk = _rows(tb, cols)
    return pl.pallas_call(
        body, name=name, grid=(rows // tb,),
        in_specs=[pl.BlockSpec((N_DEV, tb, cols), lambda i: (0, i, 0)), blk, blk, blk],
        out_specs=[blk] * 4, out_shape=[_sds((rows, cols))] * 4,
        compiler_params=_params(("parallel",), 32),
    )(parts, w, m, v)


LANES = 128
PACK_ALIGN = 8 * LANES


def _pack(arrays):
    pieces = []
    for a in arrays:
        flat = a.reshape(-1)
        pieces.append(jnp.pad(flat, (0, -flat.shape[0] % PACK_ALIGN)).reshape(-1, LANES))
    return jnp.concatenate(pieces, axis=0)


def _unpack(packed, shapes):
    out, row = [], 0
    for shape in shapes:
        size = 1
        for dim in shape:
            size *= dim
        rows = -(-size // PACK_ALIGN) * 8
        out.append(packed[row:row + rows].reshape(-1)[:size].reshape(shape))
        row += rows
    return out


TB_PROJ = 512
TB_MIX = 256
TB_FFN_BWD = 256
TB_TN = 1024
TQ = 256
TK = 256
TB_ADAM = 64


def kernel(x, norm_mix_g, w_in, gmlp_v_g, gmlp_w_s, gmlp_b_s, short_conv_w, conf_conv_w, conf_ln_g, conf_ln_b, mix_out_g, w_out, norm_ffn_g, w_up, w_down, final_norm_g, loss_target, m_norm_mix_g, m_w_in, m_gmlp_v_g, m_gmlp_w_s, m_gmlp_b_s, m_short_conv_w, m_conf_conv_w, m_conf_ln_g, m_conf_ln_b, m_mix_out_g, m_w_out, m_norm_ffn_g, m_w_up, m_w_down, m_final_norm_g, v_norm_mix_g, v_w_in, v_gmlp_v_g, v_gmlp_w_s, v_gmlp_b_s, v_short_conv_w, v_conf_conv_w, v_conf_ln_g, v_conf_ln_b, v_mix_out_g, v_w_out, v_norm_ffn_g, v_w_up, v_w_down, v_final_norm_g):
    me = 4 * lax.axis_index("x") + 2 * lax.axis_index("y") + lax.axis_index("c")
    x0, target = x[0], loss_target[0]
    s = x0.shape[0]
    tb_proj, tb_mix, tb_fb, tb_tn = min(TB_PROJ, s), min(TB_MIX, s), min(TB_FFN_BWD, s), min(TB_TN, s)
    conv_cols = D_GROUP // N_DEV

    def pad_rows(a, rows):
        return jnp.pad(a, ((0, rows - a.shape[0]), (0, 0)))

    local = []
    for l in range(N_LAYERS):
        local += [w_in[l].T.astype(BF16), w_out[l].astype(BF16), w_up[l].astype(BF16), w_down[l].astype(BF16)]
    local.append(jnp.concatenate([pad_rows(short_conv_w[l], 8) for l in range(N_LAYERS)]
                                 + [pad_rows(conf_conv_w[l], HALO) for l in range(N_LAYERS)], axis=0))
    gathered = _exchange(local, scatter=False, name="gather_weights")
    wint = [gathered[4 * l].reshape(D_IN, D_MODEL) for l in range(N_LAYERS)]
    wout = [gathered[4 * l + 1].reshape(D_MODEL, D_MODEL) for l in range(N_LAYERS)]
    wup = [gathered[4 * l + 2] for l in range(N_LAYERS)]
    wdn = [gathered[4 * l + 3] for l in range(N_LAYERS)]
    conv_full = gathered[4 * N_LAYERS].transpose(1, 0, 2).reshape(-1, D_GROUP)
    scw = [conv_full[8 * l:8 * (l + 1)] for l in range(N_LAYERS)]
    ccw = [conv_full[8 * N_LAYERS + HALO * l:8 * N_LAYERS + HALO * (l + 1)] for l in range(N_LAYERS)]

    tril = jnp.tril(jnp.ones((CHUNK, CHUNK), dtype=bool))
    wm = [jnp.where(tril, gmlp_w_s[l], 0.0).astype(BF16) for l in range(N_LAYERS)]
    wmt = [w.transpose(0, 2, 1) for w in wm]
    bexp = [jnp.repeat(gmlp_b_s[l].T, HEAD_DIM, axis=1) for l in range(N_LAYERS)]

    def row(vec):
        return vec.reshape(1, -1)

    saved = []
    xc = x0
    for l in range(N_LAYERS):
        z, qkv = _in_proj_fwd(xc, row(norm_mix_g[l]), wint[l], tb_proj)
        ya, yb, yd = _mix_fwd(z, row(gmlp_v_g[l]), wm[l], bexp[l], scw[l], ccw[l], row(conf_ln_g[l]), row(conf_ln_b[l]), tb_mix)
        yc = _attn_fwd(qkv, TQ, TK)
        ys = (ya, yb, yc, yd)
        x1 = _out_proj_fwd(ys, xc, row(mix_out_g[l]), wout[l], tb_proj)
        saved.append((xc, z, qkv, ys, x1))
        xc = _ffn_fwd(x1, row(norm_ffn_g[l]), wup[l], wdn[l], tb_proj)
    dx, g_final, loss_part = _loss_head(xc, row(final_norm_g), target, tb_proj)
    loss = lax.psum(loss_part[0, 0], MESH_AXES)

    big_grads = [None] * (4 * N_LAYERS)
    small_grads = [None] * N_LAYERS
    for l in reversed(range(N_LAYERS)):
        xin, z, qkv, ys, x1 = saved[l]
        dx1, hb_ffn, dpre, sq, g_ffn = _ffn_bwd(x1, dx, row(norm_ffn_g[l]), wup[l], wdn[l], tb_fb)
        big_grads[4 * l + 2] = _tn_matmul(hb_ffn, dpre, N_DEV, False, True, tb_tn, "grad_w_up")
        big_grads[4 * l + 3] = _tn_matmul(sq, dx, N_DEV, True, False, tb_tn, "grad_w_down")
        dya, dyb, dyc, dyd, yn, g_mixout = _out_proj_bwd(dx1, ys, row(mix_out_g[l]), wout[l], tb_proj)
        big_grads[4 * l + 1] = _tn_matmul(yn, dx1, 1, False, False, tb_tn, "grad_w_out").reshape(N_DEV, D_MODEL // N_DEV, D_MODEL)
        dq, dk, dv = _attn_bwd(qkv, dyc, ys[2], TQ, TK)
        dz, g_vg, g_ws, g_bs, g_scw, g_ccw, g_lng, g_lnb = _mix_bwd(
            z, dya, dyb, dyd, dq, dk, dv, row(gmlp_v_g[l]), wm[l], wmt[l], bexp[l], scw[l], ccw[l],
            row(conf_ln_g[l]), row(conf_ln_b[l]), tb_mix)
        dx, hb_in, g_mix = _in_proj_bwd(xin, dz, dx1, row(norm_mix_g[l]), wint[l], tb_proj)
        big_grads[4 * l] = _tn_matmul(dz, hb_in, 1, False, False, tb_tn, "grad_w_in").reshape(N_DEV, D_IN // N_DEV, D_MODEL)
        small_grads[l] = dict(norm_mix_g=g_mix[0], gmlp_v_g=g_vg[0], gmlp_w_s=g_ws, gmlp_b_s=g_bs[:, :4].T,
                              short_conv_w=g_scw[:K_SHORT], conf_conv_w=g_ccw[:K_CONF], conf_ln_g=g_lng[0],
                              conf_ln_b=g_lnb[0], mix_out_g=g_mixout[0], norm_ffn_g=g_ffn[0])

    parts = _exchange(big_grads, scatter=True, name="scatter_grads")
    small_names = ["norm_mix_g", "gmlp_v_g", "gmlp_w_s", "gmlp_b_s", "short_conv_w", "conf_conv_w", "conf_ln_g",
                   "conf_ln_b", "mix_out_g", "norm_ffn_g"]
    small_list = [jnp.stack([small_grads[l][n] for l in range(N_LAYERS)]) for n in small_names] + [g_final[0]]
    small_shapes = [a.shape for a in small_list]
    small_parts = _exchange([_pack(small_list)], scatter=False, name="gather_small_grads")[0]

    given = dict(norm_mix_g=(norm_mix_g, m_norm_mix_g, v_norm_mix_g), gmlp_v_g=(gmlp_v_g, m_gmlp_v_g, v_gmlp_v_g),
                 gmlp_w_s=(gmlp_w_s, m_gmlp_w_s, v_gmlp_w_s), gmlp_b_s=(gmlp_b_s, m_gmlp_b_s, v_gmlp_b_s),
                 short_conv_w=(short_conv_w, m_short_conv_w, v_short_conv_w),
                 conf_conv_w=(conf_conv_w, m_conf_conv_w, v_conf_conv_w),
                 conf_ln_g=(conf_ln_g, m_conf_ln_g, v_conf_ln_g), conf_ln_b=(conf_ln_b, m_conf_ln_b, v_conf_ln_b),
                 mix_out_g=(mix_out_g, m_mix_out_g, v_mix_out_g), norm_ffn_g=(norm_ffn_g, m_norm_ffn_g, v_norm_ffn_g),
                 final_norm_g=(final_norm_g, m_final_norm_g, v_final_norm_g))
    sharded_small = ("short_conv_w", "conf_conv_w")

    def widen(a):
        full = jnp.zeros(a.shape[:-1] + (D_GROUP,), a.dtype)
        return lax.dynamic_update_slice(full, a, (0, 0, me * conv_cols))

    packed_state = []
    for k in range(3):
        packed_state.append(_pack([widen(given[n][k]) if n in sharded_small else given[n][k]
                                   for n in small_names + ["final_norm_g"]]))
    small_out = _reduce_adamw(small_parts, *packed_state, packed_state[0].shape[0], "adamw_small")
    small_res = {}
    for kind, packed in zip(("grad", "delta", "new_m", "new_v"), small_out):
        for n, val in zip(small_names + ["final_norm_g"], _unpack(packed, small_shapes)):
            if n in sharded_small:
                val = lax.dynamic_slice(val, (0, 0, me * conv_cols), val.shape[:-1] + (conv_cols,))
            small_res[kind, n] = val

    big_names = ["w_in", "w_out", "w_up", "w_down"]
    big_given = dict(w_in=(w_in, m_w_in, v_w_in), w_out=(w_out, m_w_out, v_w_out), w_up=(w_up, m_w_up, v_w_up),
                     w_down=(w_down, m_w_down, v_w_down))
    big_res = {}
    for j, n in enumerate(big_names):
        per_layer = []
        for l in range(N_LAYERS):
            state = [t[l].T if n == "w_in" else t[l] for t in big_given[n]]
            outs = _reduce_adamw(parts[4 * l + j], *state, TB_ADAM, "adamw_" + n)
            per_layer.append([o.T if n == "w_in" else o for o in outs])
        for k, kind in enumerate(("grad", "delta", "new_m", "new_v")):
            big_res[kind, n] = jnp.stack([per_layer[l][k] for l in range(N_LAYERS)])

    order = ["norm_mix_g", "w_in", "gmlp_v_g", "gmlp_w_s", "gmlp_b_s", "short_conv_w", "conf_conv_w", "conf_ln_g",
             "conf_ln_b", "mix_out_g", "w_out", "norm_ffn_g", "w_up", "w_down", "final_norm_g"]
    result = [loss, dx.reshape(x.shape)]
    for kind in ("grad", "delta", "new_m", "new_v"):
        for n in order:
            result.append(big_res[kind, n] if n in big_given else small_res[kind, n])
    return tuple(result)
```

```python
import jax
import jax.numpy as jnp
from jax import lax
from jax.experimental import pallas as pl
from jax.experimental.pallas import tpu as pltpu

F32 = jnp.float32
BF16 = jnp.bfloat16

D_MODEL = 1024
D_GROUP = 256
D_IN = 10 * D_GROUP
D_FF = 4 * D_MODEL
N_DEV = 8
N_LAYERS = 2
HEAD_DIM = 64
HEADS_PER_PAIR = 2
PAIR = HEADS_PER_PAIR * HEAD_DIM
CHUNK = 128
K_SHORT = 3
K_CONF = 31
HALO = 32
EPS = 1e-6
ATT_SCALE = HEAD_DIM ** -0.5
LOG_CUT = -104.0
MIB = 2 ** 20

ADAM_LR = 0.001
ADAM_B1 = 0.9
ADAM_B2 = 0.999
ADAM_EPS = 1e-08
ADAM_WD = 0.01
ADAM_STEP = 10

MESH_AXES = ("x", "y", "c")
GELU_C = 0.7978845608028654
GELU_A = 0.044715


def _mm(a, b):
    return jnp.dot(a, b, preferred_element_type=F32)


def _mm_nt(a, b):
    return lax.dot_general(a, b, (((1,), (1,)), ((), ())), preferred_element_type=F32)


def _mm_tn(a, b):
    return lax.dot_general(a, b, (((0,), (0,)), ((), ())), preferred_element_type=F32)


def _rms(x):
    return lax.rsqrt(jnp.mean(x * x, axis=-1, keepdims=True) + EPS)


def _rms_bwd(dy, xh, r):
    return r * (dy - xh * jnp.mean(dy * xh, axis=-1, keepdims=True))


def _sigmoid(x):
    return 1.0 / (1.0 + jnp.exp(-x))


def _whole(shape):
    return pl.BlockSpec(shape, lambda *_: (0,) * len(shape))


def _rows(tb, width, col=0):
    return pl.BlockSpec((tb, width), lambda i: (i, col))


def _params(semantics, vmem_mib):
    return pltpu.CompilerParams(dimension_semantics=semantics, vmem_limit_bytes=vmem_mib * MIB)


def _sds(shape, dtype=F32):
    return jax.ShapeDtypeStruct(shape, dtype)


def _split_bf16(v):
    hi = v.astype(BF16)
    lo = (v - hi.astype(F32)).astype(BF16)
    return hi, lo


GATHER, SCATTER = "gather", "scatter"
ANY_SPEC = pl.BlockSpec(memory_space=pl.ANY)


def _exchange_copies(ins, outs, modes, send_sems, recv_sems, local_sems, with_arrivals=True):
    x, y, c = lax.axis_index("x"), lax.axis_index("y"), lax.axis_index("c")
    me = 4 * x + 2 * y + c
    local, sends, arrivals = [], [], []
    for a, mode in enumerate(modes):
        local.append(pltpu.make_async_copy(ins[a].at[me] if mode == SCATTER else ins[a], outs[a].at[me], local_sems.at[a]))
    for k in range(N_DEV - 1):
        flip = k + 1
        peer = (1 - x if flip & 4 else x, 1 - y if flip & 2 else y, 1 - c if flip & 1 else c)
        pf = 4 * peer[0] + 2 * peer[1] + peer[2]
        for a, mode in enumerate(modes):
            src = ins[a].at[pf] if mode == SCATTER else ins[a]
            for dst, group in ((outs[a].at[me], sends), (outs[a].at[pf], arrivals)):
                if group is sends or with_arrivals:
                    group.append(pltpu.make_async_remote_copy(
                        src_ref=src, dst_ref=dst, send_sem=send_sems.at[a, k], recv_sem=recv_sems.at[a, k],
                        device_id=peer, device_id_type=pl.DeviceIdType.MESH))
    return local, sends, arrivals


def _exchange_start(*refs_and_modes):
    local, sends, _ = _exchange_copies(*refs_and_modes, with_arrivals=False)
    for cp in local + sends:
        cp.start()


def _exchange_wait(*refs_and_modes):
    local, sends, arrivals = _exchange_copies(*refs_and_modes)
    for cp in sends:
        cp.wait_send()
    for cp in arrivals:
        cp.wait_recv()
    for cp in local:
        cp.wait()


def _exchange_shapes(arrays, modes):
    out_shape = [_sds(a.shape if mode == SCATTER else (N_DEV,) + a.shape, a.dtype) for a, mode in zip(arrays, modes)]
    n = len(arrays)
    sems = [pltpu.SemaphoreType.DMA((n, N_DEV - 1)), pltpu.SemaphoreType.DMA((n, N_DEV - 1)), pltpu.SemaphoreType.DMA((n,))]
    return out_shape, sems


def _exchange(arrays, modes, name):
    n = len(arrays)
    out_shape, sems = _exchange_shapes(arrays, modes)

    def body(*refs):
        _exchange_start(refs[:n], refs[n:2 * n], modes, *refs[2 * n:])
        _exchange_wait(refs[:n], refs[n:2 * n], modes, *refs[2 * n:])

    return pl.pallas_call(body, name=name, out_shape=out_shape, in_specs=[ANY_SPEC] * n, out_specs=[ANY_SPEC] * n,
                          scratch_shapes=sems)(*arrays)


def _call(body, args, ride, *, name, grid, in_specs, out_specs, out_shape, scratch_shapes=(), compiler_params):
    if ride is None:
        outs = pl.pallas_call(body, name=name, grid=grid, in_specs=in_specs, out_specs=out_specs, out_shape=out_shape,
                              scratch_shapes=scratch_shapes, compiler_params=compiler_params)(*args)
        return outs, []
    arrays, modes = ride
    n, n_in, n_out, n_scratch = len(arrays), len(in_specs), len(out_specs), len(scratch_shapes)
    moved_shape, sems = _exchange_shapes(arrays, modes)
    n_steps = 1
    for g in grid:
        n_steps *= g

    def riding(*refs):
        ins, refs = refs[:n_in], refs[n_in:]
        r_ins, refs = refs[:n], refs[n:]
        outs, refs = refs[:n_out], refs[n_out:]
        r_outs, refs = refs[:n], refs[n:]
        scratch, r_sems = refs[:n_scratch], refs[n_scratch:]
        step = pl.program_id(0)
        for axis in range(1, len(grid)):
            step = step * grid[axis] + pl.program_id(axis)

        @pl.when(step == 0)
        def _():
            _exchange_start(r_ins, r_outs, modes, *r_sems)

        body(*ins, *outs, *scratch)

        @pl.when(step == n_steps - 1)
        def _():
            _exchange_wait(r_ins, r_outs, modes, *r_sems)

    outs = pl.pallas_call(
        riding, name=name, grid=grid, in_specs=list(in_specs) + [ANY_SPEC] * n,
        out_specs=list(out_specs) + [ANY_SPEC] * n, out_shape=list(out_shape) + moved_shape,
        scratch_shapes=list(scratch_shapes) + sems, compiler_params=compiler_params)(*args, *arrays)
    return outs[:n_out], outs[n_out:]


def _in_proj_fwd(x, g, wint, tb, ride=None):
    s = x.shape[0]

    def body(x_ref, g_ref, w_ref, z_ref, qkv_ref):
        xv = x_ref[...]
        h = (xv * _rms(xv) * g_ref[...]).astype(BF16)
        z = _mm_nt(h, w_ref[...])
        z_ref[...] = z
        qkv_ref[:, 0:D_GROUP] = (z[:, 5 * D_GROUP:6 * D_GROUP] * ATT_SCALE).astype(BF16)
        qkv_ref[:, D_GROUP:3 * D_GROUP] = z[:, 6 * D_GROUP:8 * D_GROUP].astype(BF16)

    return _call(
        body, (x, g, wint), ride, name="in_proj_fwd", grid=(s // tb,),
        in_specs=[_rows(tb, D_MODEL), _whole((1, D_MODEL)), _whole((D_IN, D_MODEL))],
        out_specs=[_rows(tb, D_IN), _rows(tb, 3 * D_GROUP)],
        out_shape=[_sds((s, D_IN)), _sds((s, 3 * D_GROUP), BF16)],
        compiler_params=_params(("arbitrary",), 48))


def _gelu(x):
    return 0.5 * x * (1.0 + jnp.tanh(GELU_C * (x + GELU_A * x * x * x)))


def _gelu_grad(x):
    t = jnp.tanh(GELU_C * (x + GELU_A * x * x * x))
    return 0.5 * (1.0 + t) + 0.5 * x * (1.0 - t * t) * GELU_C * (1.0 + 3.0 * GELU_A * x * x)


def _head_lane(width):
    return lax.broadcasted_iota(jnp.int32, (1, width), 1) // HEAD_DIM


def _gating_chunk(wm_ref, bexp_ref, vc, lane_h):
    f = bexp_ref[...]
    for h in range(D_GROUP // HEAD_DIM):
        f = f + _mm(wm_ref[h], jnp.where(lane_h == h, vc, 0))
    return f


def _halo_specs(s, tb, width_blocks):
    per = tb // HALO
    prev = pl.BlockSpec((HALO, width_blocks), lambda i: (jnp.maximum(i * per - 1, 0), 0))
    nxt = pl.BlockSpec((HALO, width_blocks), lambda i: (jnp.minimum((i + 1) * per, s // HALO - 1), 0))
    return prev, nxt


def _mix_fwd(z, vg, wm, bexp, scw, ccw, lng, lnb, tb, ride=None):
    s = z.shape[0]
    prev_spec, _ = _halo_specs(s, tb, D_IN)

    def body(z_ref, zp_ref, vg_ref, wm_ref, bexp_ref, scw_ref, ccw_ref, lng_ref, lnb_ref,
             ya_ref, yb_ref, yd_ref, pbuf, hbuf):
        keep = (pl.program_id(0) > 0).astype(F32)
        lane_h = _head_lane(D_GROUP)
        ga = _gelu(z_ref[:, 0:2 * D_GROUP])
        u, v = ga[:, :D_GROUP], ga[:, D_GROUP:]
        vn = (v * _rms(v) * vg_ref[...]).astype(BF16)
        for n in range(tb // CHUNK):
            rows = slice(n * CHUNK, (n + 1) * CHUNK)
            ya_ref[rows, :] = u[rows] * _gating_chunk(wm_ref, bexp_ref, vn[rows], lane_h)
        p = z_ref[:, 3 * D_GROUP:4 * D_GROUP] * z_ref[:, 4 * D_GROUP:5 * D_GROUP]
        pbuf[0:HALO, :] = zp_ref[:, 3 * D_GROUP:4 * D_GROUP] * zp_ref[:, 4 * D_GROUP:5 * D_GROUP] * keep
        pbuf[HALO:HALO + tb, :] = p
        cv = scw_ref[K_SHORT - 1:K_SHORT, :] * p
        for k in range(K_SHORT - 1):
            cv = cv + scw_ref[k:k + 1, :] * pbuf[pl.ds(HALO - (K_SHORT - 1) + k, tb), :]
        yb_ref[...] = z_ref[:, 2 * D_GROUP:3 * D_GROUP] * cv
        hbuf[0:HALO, :] = zp_ref[:, 8 * D_GROUP:9 * D_GROUP] * _sigmoid(zp_ref[:, 9 * D_GROUP:10 * D_GROUP]) * keep
        hbuf[HALO:HALO + tb, :] = z_ref[:, 8 * D_GROUP:9 * D_GROUP] * _sigmoid(z_ref[:, 9 * D_GROUP:10 * D_GROUP])
        c = jnp.zeros((tb, D_GROUP), F32)
        for k in range(K_CONF):
            c = c + ccw_ref[k:k + 1, :] * hbuf[pl.ds(HALO - (K_CONF - 1) + k, tb), :]
        xc = c - jnp.mean(c, axis=-1, keepdims=True)
        ln = xc * lax.rsqrt(jnp.mean(xc * xc, axis=-1, keepdims=True) + EPS) * lng_ref[...] + lnb_ref[...]
        yd_ref[...] = ln * _sigmoid(ln)

    grp = _rows(tb, D_GROUP)
    return _call(
        body, (z, z, vg, wm, bexp, scw, ccw, lng, lnb), ride, name="mix_fwd", grid=(s // tb,),
        in_specs=[_rows(tb, D_IN), prev_spec, _whole((1, D_GROUP)), _whole(wm.shape), _whole(bexp.shape),
                  _whole(scw.shape), _whole(ccw.shape), _whole((1, D_GROUP)), _whole((1, D_GROUP))],
        out_specs=[grp, grp, grp],
        out_shape=[_sds((s, D_GROUP))] * 3,
        scratch_shapes=[pltpu.VMEM((HALO + tb, D_GROUP), F32), pltpu.VMEM((HALO + tb, D_GROUP), F32)],
        compiler_params=_params(("arbitrary",), 40))


def _stick_tile(qh, kt, causal, c, upper):
    x = _mm_nt(qh, kt)
    soft = jnp.log1p(jnp.exp(-jnp.abs(x)))
    lb = jnp.minimum(x, 0.0) - soft
    lom = jnp.where(causal, -jnp.maximum(x, 0.0) - soft, 0.0)
    hi, lo = _split_bf16(lom)
    stick = c + _mm(hi, upper) + _mm(lo, upper)
    w = jnp.where(causal, jnp.exp(lb + stick), 0.0)
    return w, lb, lom


def _causal_tile(qi, tq, k0, tk):
    qpos = qi * tq + lax.broadcasted_iota(jnp.int32, (tq, 1), 0)
    return k0 + lax.broadcasted_iota(jnp.int32, (1, tk), 1) < qpos


def _sticks_alive(cs):
    longest = cs[0]
    for c in cs[1:]:
        longest = jnp.maximum(longest, c)
    return (jnp.max(longest) > LOG_CUT).astype(jnp.int32)


def _walk(body, qi, tq, tk, init):
    start = (((qi + 1) * tq - 1) // tk, jnp.int32(1)) + tuple(init)
    return lax.while_loop(lambda cr: jnp.logical_and(cr[0] >= 0, cr[1] > 0), body, start)[2:]


def _attn_fwd(qkv, tq, tk, ride=None):
    s = qkv.shape[0]
    n_heads = D_GROUP // HEAD_DIM

    def body(q_ref, k_ref, v_ref, o_ref):
        qi = pl.program_id(0)
        q = q_ref[...]
        lane_h = _head_lane(D_GROUP)
        upper = (lax.broadcasted_iota(jnp.int32, (tk, tk), 0) > lax.broadcasted_iota(jnp.int32, (tk, tk), 1)).astype(BF16)
        qhs = [jnp.where(lane_h == h, q, 0) for h in range(n_heads)]

        def step(carry):
            kb, _, acc = carry[:3]
            cs = list(carry[3:])
            k0 = pl.multiple_of(kb * tk, tk)
            kt = k_ref[pl.ds(k0, tk), :]
            vt = v_ref[pl.ds(k0, tk), :]
            causal = _causal_tile(qi, tq, k0, tk)
            for h in range(n_heads):
                w, _, lom = _stick_tile(qhs[h], kt, causal, cs[h], upper)
                acc = acc + _mm(w.astype(BF16), jnp.where(lane_h == h, vt, 0))
                cs[h] = cs[h] + jnp.sum(lom, axis=1, keepdims=True)
            return (kb - 1, _sticks_alive(cs), acc) + tuple(cs)

        init = [jnp.zeros((tq, D_GROUP), F32)] + [jnp.zeros((tq, 1), F32)] * n_heads
        o_ref[...] = _walk(step, qi, tq, tk, init)[0]

    return _call(
        body, (qkv, qkv, qkv), ride, name="attn_fwd", grid=(s // tq,),
        in_specs=[pl.BlockSpec((tq, D_GROUP), lambda qi: (qi, 0)),
                  pl.BlockSpec((s, D_GROUP), lambda qi: (0, 1), pipeline_mode=pl.Buffered(1)),
                  pl.BlockSpec((s, D_GROUP), lambda qi: (0, 2), pipeline_mode=pl.Buffered(1))],
        out_specs=[pl.BlockSpec((tq, D_GROUP), lambda qi: (qi, 0))],
        out_shape=[_sds((s, D_GROUP))],
        compiler_params=_params(("arbitrary",), 40))


def _out_proj_fwd(ys, x, mg, wout, tb):
    s = x.shape[0]

    def body(ya_ref, yb_ref, yc_ref, yd_ref, x_ref, mg_ref, w_ref, o_ref):
        acc = x_ref[...]
        for gi, y_ref in enumerate((ya_ref, yb_ref, yc_ref, yd_ref)):
            cols = slice(gi * D_GROUP, (gi + 1) * D_GROUP)
            y = y_ref[...]
            acc = acc + _mm((y * _rms(y) * mg_ref[:, cols]).astype(BF16), w_ref[cols, :])
        o_ref[...] = acc

    grp = _rows(tb, D_GROUP)
    return pl.pallas_call(
        body, name="out_proj_fwd", grid=(s // tb,),
        in_specs=[grp, grp, grp, grp, _rows(tb, D_MODEL), _whole((1, D_MODEL)), _whole((D_MODEL, D_MODEL))],
        out_specs=_rows(tb, D_MODEL), out_shape=_sds((s, D_MODEL)),
        compiler_params=_params(("parallel",), 32),
    )(*ys, x, mg, wout)


def _ffn_fwd(x, g, wup, wdn, tb, ride=None):
    s = x.shape[0]
    ff = D_FF // N_DEV

    def body(x_ref, g_ref, wu_ref, wd_ref, o_ref):
        xv = x_ref[...]
        h = (xv * _rms(xv) * g_ref[...]).astype(BF16)
        acc = xv
        for d in range(N_DEV):
            a = jnp.maximum(_mm(h, wu_ref[d]), 0.0)
            acc = acc + _mm((a * a).astype(BF16), wd_ref[d])
        o_ref[...] = acc

    return _call(
        body, (x, g, wup, wdn), ride, name="ffn_fwd", grid=(s // tb,),
        in_specs=[_rows(tb, D_MODEL), _whole((1, D_MODEL)), _whole((N_DEV, D_MODEL, ff)), _whole((N_DEV, ff, D_MODEL))],
        out_specs=[_rows(tb, D_MODEL)], out_shape=[_sds((s, D_MODEL))],
        compiler_params=_params(("arbitrary",), 56))


def _loss_head(x, g, tgt, tb):
    s = x.shape[0]

    def body(x_ref, g_ref, t_ref, dx_ref, dg_ref, loss_ref):
        @pl.when(pl.program_id(0) == 0)
        def _():
            dg_ref[...] = jnp.zeros_like(dg_ref)
            loss_ref[...] = jnp.zeros_like(loss_ref)

        xv = x_ref[...]
        r = _rms(xv)
        xh = xv * r
        err = xh * g_ref[...] - t_ref[...]
        loss_ref[...] += 0.5 * jnp.sum(jnp.mean(err * err, axis=-1, keepdims=True))
        dy = err * (1.0 / D_MODEL)
        dg_ref[...] += jnp.sum(dy * xh, axis=0, keepdims=True)
        dx_ref[...] = _rms_bwd(dy * g_ref[...], xh, r)

    return pl.pallas_call(
        body, name="loss_head", grid=(s // tb,),
        in_specs=[_rows(tb, D_MODEL), _whole((1, D_MODEL)), _rows(tb, D_MODEL)],
        out_specs=[_rows(tb, D_MODEL), _whole((1, D_MODEL)), _whole((8, 128))],
        out_shape=[_sds((s, D_MODEL)), _sds((1, D_MODEL)), _sds((8, 128))],
        compiler_params=_params(("arbitrary",), 32),
    )(x, g, tgt)


def _ffn_bwd(x1, dx2, g, wup, wdn, tb, ride=None):
    s = x1.shape[0]
    ff = D_FF // N_DEV

    def body(x_ref, dy_ref, g_ref, wu_ref, wd_ref, dx_ref, hb_ref, dpre_ref, sq_ref, dyb_ref, dg_ref):
        @pl.when(pl.program_id(0) == 0)
        def _():
            dg_ref[...] = jnp.zeros_like(dg_ref)

        xv = x_ref[...]
        r = _rms(xv)
        xh = xv * r
        hb = (xh * g_ref[...]).astype(BF16)
        hb_ref[...] = hb
        dyv = dy_ref[...]
        dyb = dyv.astype(BF16)
        dyb_ref[...] = dyb
        dh = jnp.zeros((tb, D_MODEL), F32)
        for d in range(N_DEV):
            cols = slice(d * ff, (d + 1) * ff)
            a = jnp.maximum(_mm(hb, wu_ref[d]), 0.0)
            sq_ref[:, cols] = (a * a).astype(BF16)
            dpre = (_mm_nt(dyb, wd_ref[d]) * (2.0 * a)).astype(BF16)
            dpre_ref[:, cols] = dpre
            dh = dh + _mm_nt(dpre, wu_ref[d])
        dg_ref[...] += jnp.sum(dh * xh, axis=0, keepdims=True)
        dx_ref[...] = dyv + _rms_bwd(dh * g_ref[...], xh, r)

    return _call(
        body, (x1, dx2, g, wup, wdn), ride, name="ffn_bwd", grid=(s // tb,),
        in_specs=[_rows(tb, D_MODEL), _rows(tb, D_MODEL), _whole((1, D_MODEL)),
                  _whole((N_DEV, D_MODEL, ff)), _whole((N_DEV, ff, D_MODEL))],
        out_specs=[_rows(tb, D_MODEL), _rows(tb, D_MODEL), _rows(tb, D_FF), _rows(tb, D_FF), _rows(tb, D_MODEL),
                   _whole((1, D_MODEL))],
        out_shape=[_sds((s, D_MODEL)), _sds((s, D_MODEL), BF16), _sds((s, D_FF), BF16), _sds((s, D_FF), BF16),
                   _sds((s, D_MODEL), BF16), _sds((1, D_MODEL))],
        compiler_params=_params(("arbitrary",), 60))


def _tn_matmul(a, b, nd, a_blocked, b_blocked, tb, name):
    s = a.shape[0]
    ka = a.shape[1] // nd if a_blocked else a.shape[1]
    nb = b.shape[1] // nd if b_blocked else b.shape[1]

    def body(a_ref, b_ref, o_ref):
        @pl.when(pl.program_id(1) == 0)
        def _():
            o_ref[...] = jnp.zeros_like(o_ref)

        o_ref[0] += _mm_tn(a_ref[...].astype(BF16), b_ref[...].astype(BF16))

    return pl.pallas_call(
        body, name=name, grid=(nd, s // tb),
        in_specs=[pl.BlockSpec((tb, ka), (lambda d, i: (i, d)) if a_blocked else (lambda d, i: (i, 0))),
                  pl.BlockSpec((tb, nb), (lambda d, i: (i, d)) if b_blocked else (lambda d, i: (i, 0)))],
        out_specs=pl.BlockSpec((1, ka, nb), lambda d, i: (d, 0, 0)),
        out_shape=_sds((nd, ka, nb)),
        compiler_params=_params(("parallel", "arbitrary"), 56),
    )(a, b)


def _out_proj_bwd(dx1, ys, mg, wout, tb):
    s = dx1.shape[0]

    def body(dx_ref, ya_ref, yb_ref, yc_ref, yd_ref, mg_ref, w_ref,
             dya_ref, dyb_ref, dyc_ref, dyd_ref, yn_ref, dmg_ref):
        @pl.when(pl.program_id(0) == 0)
        def _():
            dmg_ref[...] = jnp.zeros_like(dmg_ref)

        dyn = _mm_nt(dx_ref[...].astype(BF16), w_ref[...])
        groups = ((ya_ref, dya_ref), (yb_ref, dyb_ref), (yc_ref, dyc_ref), (yd_ref, dyd_ref))
        for gi, (y_ref, dy_ref) in enumerate(groups):
            cols = slice(gi * D_GROUP, (gi + 1) * D_GROUP)
            y = y_ref[...]
            r = _rms(y)
            n = y * r
            gain = mg_ref[:, cols]
            dn = dyn[:, cols]
            yn_ref[:, cols] = (n * gain).astype(BF16)
            dmg_ref[:, cols] += jnp.sum(dn * n, axis=0, keepdims=True)
            dy_ref[...] = _rms_bwd(dn * gain, n, r)

    grp = _rows(tb, D_GROUP)
    return pl.pallas_call(
        body, name="out_proj_bwd", grid=(s // tb,),
        in_specs=[_rows(tb, D_MODEL), grp, grp, grp, grp, _whole((1, D_MODEL)), _whole((D_MODEL, D_MODEL))],
        out_specs=[grp, grp, grp, grp, _rows(tb, D_MODEL), _whole((1, D_MODEL))],
        out_shape=[_sds((s, D_GROUP))] * 4 + [_sds((s, D_MODEL), BF16), _sds((1, D_MODEL))],
        compiler_params=_params(("arbitrary",), 32),
    )(dx1, *ys, mg, wout)


def _attn_bwd(qkv, do, o, tq, tk, ride=None):
    s = qkv.shape[0]
    nq = s // tq
    n_pairs = D_GROUP // PAIR

    def body(q_ref, k_ref, v_ref, do_ref, o_ref, dq_ref, dk_hbm, dv_hbm, dk_acc, dv_acc, sems):
        hp, qi = pl.program_id(0), pl.program_id(1)

        @pl.when(qi == 0)
        def _():
            dk_acc[...] = jnp.zeros_like(dk_acc)
            dv_acc[...] = jnp.zeros_like(dv_acc)

        q = q_ref[...]
        dob = do_ref[...].astype(BF16)
        prod = dob.astype(F32) * o_ref[...]
        lane_h = _head_lane(PAIR)
        row = lax.broadcasted_iota(jnp.int32, (tk, tk), 0)
        col = lax.broadcasted_iota(jnp.int32, (tk, tk), 1)
        upper = (row > col).astype(BF16)
        upper_eq = (row >= col).astype(BF16)
        heads = []
        for h in range(HEADS_PER_PAIR):
            in_head = lane_h == h
            total = jnp.sum(jnp.where(in_head, prod, 0.0), axis=1, keepdims=True)
            heads.append((in_head, jnp.where(in_head, q, 0), jnp.where(in_head, dob, 0), total))

        def step(carry):
            kb, _, acc = carry[:3]
            cs = list(carry[3:3 + HEADS_PER_PAIR])
            nears = list(carry[3 + HEADS_PER_PAIR:])
            k0 = pl.multiple_of(kb * tk, tk)
            kt = k_ref[pl.ds(k0, tk), :]
            vt = v_ref[pl.ds(k0, tk), :]
            causal = _causal_tile(qi, tq, k0, tk)
            dk_t = jnp.zeros((tk, PAIR), F32)
            dv_t = jnp.zeros((tk, PAIR), F32)
            for h, (in_head, qh, doh, total) in enumerate(heads):
                w, lb, lom = _stick_tile(qh, kt, causal, cs[h], upper)
                wb = w.astype(BF16)
                gw = _mm_nt(doh, vt) * wb.astype(F32)
                hi, lo = _split_bf16(gw)
                far = total - nears[h] - _mm(hi, upper_eq) - _mm(lo, upper_eq)
                beta = jnp.exp(lb)
                dxb = jnp.where(causal, gw * (1.0 - beta) - far * beta, 0.0).astype(BF16)
                acc = acc + _mm(dxb, jnp.where(in_head, kt, 0))
                dk_t = dk_t + _mm_tn(dxb, qh)
                dv_t = dv_t + _mm_tn(wb, doh)
                cs[h] = cs[h] + jnp.sum(lom, axis=1, keepdims=True)
                nears[h] = nears[h] + jnp.sum(gw, axis=1, keepdims=True)
            dk_acc[pl.ds(k0, tk), :] += dk_t
            dv_acc[pl.ds(k0, tk), :] += dv_t
            return (kb - 1, _sticks_alive(cs), acc) + tuple(cs) + tuple(nears)

        init = [jnp.zeros((tq, PAIR), F32)] + [jnp.zeros((tq, 1), F32)] * (2 * HEADS_PER_PAIR)
        dq_ref[...] = _walk(step, qi, tq, tk, init)[0]

        @pl.when(qi == nq - 1)
        def _():
            ck = pltpu.make_async_copy(dk_acc, dk_hbm.at[hp], sems.at[0])
            cv = pltpu.make_async_copy(dv_acc, dv_hbm.at[hp], sems.at[1])
            ck.start()
            cv.start()
            ck.wait()
            cv.wait()

    blk = pl.BlockSpec((tq, PAIR), lambda hp, qi: (qi, hp))
    return _call(
        body, (qkv, qkv, qkv, do, o), ride, name="attn_bwd", grid=(n_pairs, nq),
        in_specs=[blk, pl.BlockSpec((s, PAIR), lambda hp, qi: (0, 2 + hp)),
                  pl.BlockSpec((s, PAIR), lambda hp, qi: (0, 4 + hp)), blk, blk],
        out_specs=[blk, ANY_SPEC, ANY_SPEC],
        out_shape=[_sds((s, D_GROUP)), _sds((n_pairs, s, PAIR)), _sds((n_pairs, s, PAIR))],
        scratch_shapes=[pltpu.VMEM((s, PAIR), F32), pltpu.VMEM((s, PAIR), F32), pltpu.SemaphoreType.DMA((2,))],
        compiler_params=_params(("arbitrary", "arbitrary"), 56))


def _mix_bwd(z, dya, dyb, dyd, dq, dk, dv, vg, wm, wmt, bexp, scw, ccw, lng, lnb, tb, ride=None):
    s = z.shape[0]
    n_steps = s // tb
    prev_spec, next_spec = _halo_specs(s, tb, D_IN)
    _, next_grp = _halo_specs(s, tb, D_GROUP)
    ext = tb + HALO

    def body(z_ref, zp_ref, zn_ref, dya_ref, dyb_ref, dybn_ref, dyd_ref, dydn_ref, dq_ref, dk0_ref, dk1_ref,
             dv0_ref, dv1_ref, vg_ref, wm_ref, wmt_ref, bexp_ref, scw_ref, ccw_ref, lng_ref, lnb_ref,
             dz_ref, dvg_ref, dws_ref, dbs_ref, dscw_ref, dccw_ref, dlng_ref, dlnb_ref,
             pbuf, hbuf, gbuf, cbuf, dubuf, dvnbuf):
        i = pl.program_id(0)

        @pl.when(i == 0)
        def _():
            for ref in (dvg_ref, dws_ref, dbs_ref, dscw_ref, dccw_ref, dlng_ref, dlnb_ref):
                ref[...] = jnp.zeros_like(ref)

        keep_prev = (i > 0).astype(F32)
        keep_next = (i < n_steps - 1).astype(F32)
        lane_h = _head_lane(D_GROUP)

        za = z_ref[:, 0:2 * D_GROUP]
        ga = _gelu(za)
        u, v = ga[:, :D_GROUP], ga[:, D_GROUP:]
        r = _rms(v)
        vh = v * r
        vn = (vh * vg_ref[...]).astype(BF16)
        tril = lax.broadcasted_iota(jnp.int32, (CHUNK, CHUNK), 0) >= lax.broadcasted_iota(jnp.int32, (CHUNK, CHUNK), 1)
        dbias = jnp.zeros((CHUNK, D_GROUP), F32)
        for n in range(tb // CHUNK):
            rows = slice(n * CHUNK, (n + 1) * CHUNK)
            vc = vn[rows]
            dy = dya_ref[rows, :]
            dubuf[rows, :] = dy * _gating_chunk(wm_ref, bexp_ref, vc, lane_h)
            df = dy * u[rows]
            dfb = df.astype(BF16)
            dvn = jnp.zeros((CHUNK, D_GROUP), F32)
            for h in range(D_GROUP // HEAD_DIM):
                dfh = jnp.where(lane_h == h, dfb, 0)
                dvn = dvn + _mm(wmt_ref[h], dfh)
                dws_ref[h] += jnp.where(tril, _mm_nt(dfh, vc), 0.0)
            dvnbuf[rows, :] = dvn
            dbias = dbias + df
        for h in range(D_GROUP // HEAD_DIM):
            per_head = jnp.sum(jnp.where(lane_h == h, dbias, 0.0), axis=1, keepdims=True)
            dbs_ref[...] += per_head * (lax.broadcasted_iota(jnp.int32, (1, CHUNK), 1) == h).astype(F32)
        dvn = dvnbuf[...]
        dvg_ref[...] += jnp.sum(dvn * vh, axis=0, keepdims=True)
        dgelu = _gelu_grad(za)
        dz_ref[:, 0:D_GROUP] = (dubuf[...] * dgelu[:, :D_GROUP]).astype(BF16)
        dz_ref[:, D_GROUP:2 * D_GROUP] = (_rms_bwd(dvn * vg_ref[...], vh, r) * dgelu[:, D_GROUP:]).astype(BF16)

        gate_b = z_ref[:, 2 * D_GROUP:3 * D_GROUP]
        gate_c = z_ref[:, 3 * D_GROUP:4 * D_GROUP]
        hh = z_ref[:, 4 * D_GROUP:5 * D_GROUP]
        p = gate_c * hh
        pbuf[0:HALO, :] = zp_ref[:, 3 * D_GROUP:4 * D_GROUP] * zp_ref[:, 4 * D_GROUP:5 * D_GROUP] * keep_prev
        pbuf[HALO:HALO + tb, :] = p
        dyb_v = dyb_ref[...]
        dcv = dyb_v * gate_b
        gbuf[0:tb, :] = dcv
        gbuf[tb:ext, :] = dybn_ref[...] * zn_ref[:, 2 * D_GROUP:3 * D_GROUP] * keep_next
        cv = scw_ref[K_SHORT - 1:K_SHORT, :] * p
        dp = scw_ref[K_SHORT - 1:K_SHORT, :] * dcv
        dscw_ref[K_SHORT - 1:K_SHORT, :] += jnp.sum(dcv * p, axis=0, keepdims=True)
        for k in range(K_SHORT - 1):
            shifted = pbuf[pl.ds(HALO - (K_SHORT - 1) + k, tb), :]
            cv = cv + scw_ref[k:k + 1, :] * shifted
            dp = dp + scw_ref[k:k + 1, :] * gbuf[pl.ds(K_SHORT - 1 - k, tb), :]
            dscw_ref[k:k + 1, :] += jnp.sum(dcv * shifted, axis=0, keepdims=True)
        dz_ref[:, 2 * D_GROUP:3 * D_GROUP] = (dyb_v * cv).astype(BF16)
        dz_ref[:, 3 * D_GROUP:4 * D_GROUP] = (dp * hh).astype(BF16)
        dz_ref[:, 4 * D_GROUP:5 * D_GROUP] = (dp * gate_c).astype(BF16)

        dz_ref[:, 5 * D_GROUP:6 * D_GROUP] = (dq_ref[...] * ATT_SCALE).astype(BF16)
        dz_ref[:, 6 * D_GROUP:6 * D_GROUP + PAIR] = dk0_ref[...].astype(BF16)
        dz_ref[:, 6 * D_GROUP + PAIR:7 * D_GROUP] = dk1_ref[...].astype(BF16)
        dz_ref[:, 7 * D_GROUP:7 * D_GROUP + PAIR] = dv0_ref[...].astype(BF16)
        dz_ref[:, 7 * D_GROUP + PAIR:8 * D_GROUP] = dv1_ref[...].astype(BF16)

        a = z_ref[:, 8 * D_GROUP:9 * D_GROUP]
        sg = _sigmoid(z_ref[:, 9 * D_GROUP:10 * D_GROUP])
        hbuf[0:HALO, :] = zp_ref[:, 8 * D_GROUP:9 * D_GROUP] * _sigmoid(zp_ref[:, 9 * D_GROUP:10 * D_GROUP]) * keep_prev
        hbuf[HALO:HALO + tb, :] = a * sg
        hbuf[HALO + tb:HALO + ext, :] = zn_ref[:, 8 * D_GROUP:9 * D_GROUP] * _sigmoid(zn_ref[:, 9 * D_GROUP:10 * D_GROUP])
        c = jnp.zeros((ext, D_GROUP), F32)
        for k in range(K_CONF):
            c = c + ccw_ref[k:k + 1, :] * hbuf[pl.ds(HALO - (K_CONF - 1) + k, ext), :]
        xc = c - jnp.mean(c, axis=-1, keepdims=True)
        rs = lax.rsqrt(jnp.mean(xc * xc, axis=-1, keepdims=True) + EPS)
        xh = xc * rs
        ln = xh * lng_ref[...] + lnb_ref[...]
        sl = _sigmoid(ln)
        dy_ext = jnp.concatenate([dyd_ref[...], dydn_ref[...] * keep_next], axis=0)
        dln = dy_ext * sl * (1.0 + ln * (1.0 - sl))
        dlng_ref[...] += jnp.sum(dln[:tb] * xh[:tb], axis=0, keepdims=True)
        dlnb_ref[...] += jnp.sum(dln[:tb], axis=0, keepdims=True)
        dxh = dln * lng_ref[...]
        dc = rs * (dxh - jnp.mean(dxh, axis=-1, keepdims=True) - xh * jnp.mean(dxh * xh, axis=-1, keepdims=True))
        cbuf[...] = dc
        dc_blk = dc[:tb]
        dhd = jnp.zeros((tb, D_GROUP), F32)
        for k in range(K_CONF):
            dhd = dhd + ccw_ref[k:k + 1, :] * cbuf[pl.ds(K_CONF - 1 - k, tb), :]
            dccw_ref[k:k + 1, :] += jnp.sum(dc_blk * hbuf[pl.ds(HALO - (K_CONF - 1) + k, tb), :], axis=0, keepdims=True)
        dz_ref[:, 8 * D_GROUP:9 * D_GROUP] = (dhd * sg).astype(BF16)
        dz_ref[:, 9 * D_GROUP:10 * D_GROUP] = (dhd * a * sg * (1.0 - sg)).astype(BF16)

    grp = _rows(tb, D_GROUP)
    pair0 = pl.BlockSpec((None, tb, PAIR), lambda i: (0, i, 0))
    pair1 = pl.BlockSpec((None, tb, PAIR), lambda i: (1, i, 0))
    small = [_sds((1, D_GROUP)), _sds((4, CHUNK, CHUNK)), _sds((CHUNK, CHUNK)), _sds((8, D_GROUP)),
             _sds((HALO, D_GROUP)), _sds((1, D_GROUP)), _sds((1, D_GROUP))]
    return _call(
        body, (z, z, z, dya, dyb, dyb, dyd, dyd, dq, dk, dk, dv, dv, vg, wm, wmt, bexp, scw, ccw, lng, lnb), ride,
        name="mix_bwd", grid=(n_steps,),
        in_specs=[_rows(tb, D_IN), prev_spec, next_spec, grp, grp, next_grp, grp, next_grp, grp, pair0, pair1, pair0, pair1,
                  _whole((1, D_GROUP)), _whole(wm.shape), _whole(wmt.shape), _whole(bexp.shape), _whole(scw.shape),
                  _whole(ccw.shape), _whole((1, D_GROUP)), _whole((1, D_GROUP))],
        out_specs=[_rows(tb, D_IN)] + [_whole(t.shape) for t in small],
        out_shape=[_sds((s, D_IN), BF16)] + small,
        scratch_shapes=[pltpu.VMEM((HALO + tb, D_GROUP), F32), pltpu.VMEM((HALO + ext, D_GROUP), F32),
                        pltpu.VMEM((ext, D_GROUP), F32), pltpu.VMEM((ext, D_GROUP), F32),
                        pltpu.VMEM((tb, D_GROUP), F32), pltpu.VMEM((tb, D_GROUP), F32)],
        compiler_params=_params(("arbitrary",), 48))


def _in_proj_bwd(x, dz, dres, g, wint, tb):
    s = x.shape[0]

    def body(x_ref, dz_ref, dr_ref, g_ref, w_ref, dx_ref, hb_ref, dg_ref):
        @pl.when(pl.program_id(0) == 0)
        def _():
            dg_ref[...] = jnp.zeros_like(dg_ref)

        xv = x_ref[...]
        r = _rms(xv)
        xh = xv * r
        hb_ref[...] = (xh * g_ref[...]).astype(BF16)
        dh = _mm(dz_ref[...], w_ref[...])
        dg_ref[...] += jnp.sum(dh * xh, axis=0, keepdims=True)
        dx_ref[...] = dr_ref[...] + _rms_bwd(dh * g_ref[...], xh, r)

    return pl.pallas_call(
        body, name="in_proj_bwd", grid=(s // tb,),
        in_specs=[_rows(tb, D_MODEL), _rows(tb, D_IN), _rows(tb, D_MODEL), _whole((1, D_MODEL)), _whole((D_IN, D_MODEL))],
        out_specs=[_rows(tb, D_MODEL), _rows(tb, D_MODEL), _whole((1, D_MODEL))],
        out_shape=[_sds((s, D_MODEL)), _sds((s, D_MODEL), BF16), _sds((1, D_MODEL))],
        compiler_params=_params(("arbitrary",), 48),
    )(x, dz, dres, g, wint)


def _adamw(w, g, m, v):
    m = ADAM_B1 * m + (1.0 - ADAM_B1) * g
    v = ADAM_B2 * v + (1.0 - ADAM_B2) * (g * g)
    m_hat = m / (1.0 - ADAM_B1 ** ADAM_STEP)
    v_hat = v / (1.0 - ADAM_B2 ** ADAM_STEP)
    delta = -ADAM_LR * (m_hat / (jnp.sqrt(v_hat) + ADAM_EPS) + ADAM_WD * w)
    return delta, m, v


def _reduce_adamw(parts, w, m, v, tb, name):
    rows, cols = w.shape

    def body(p_ref, w_ref, m_ref, v_ref, g_ref, d_ref, m2_ref, v2_ref):
        g = p_ref[0]
        for j in range(1, N_DEV):
            g = g + p_ref[j]
        g_ref[...] = g
        d_ref[...], m2_ref[...], v2_ref[...] = _adamw(w_ref[...], g, m_ref[...], v_ref[...])

    blk = _rows(tb, cols)
    return pl.pallas_call(
        body, name=name, grid=(rows // tb,),
        in_specs=[pl.BlockSpec((N_DEV, tb, cols), lambda i: (0, i, 0)), blk, blk, blk],
        out_specs=[blk] * 4, out_shape=[_sds((rows, cols))] * 4,
        compiler_params=_params(("parallel",), 32),
    )(parts, w, m, v)


LANES = 128
PACK_ALIGN = 8 * LANES


def _pack(arrays):
    pieces = []
    for a in arrays:
        flat = a.reshape(-1)
        pieces.append(jnp.pad(flat, (0, -flat.shape[0] % PACK_ALIGN)).reshape(-1, LANES))
    return jnp.concatenate(pieces, axis=0)


def _unpack(packed, shapes):
    out, row = [], 0
    for shape in shapes:
        size = 1
        for dim in shape:
            size *= dim
        rows = -(-size // PACK_ALIGN) * 8
        out.append(packed[row:row + rows].reshape(-1)[:size].reshape(shape))
        row += rows
    return out


TB_PROJ = 512
TB_MIX = 256
TB_FFN_BWD = 256
TB_TN = 1024
TQ = 256
TK = 256
TB_ADAM = 64


def kernel(x, norm_mix_g, w_in, gmlp_v_g, gmlp_w_s, gmlp_b_s, short_conv_w, conf_conv_w, conf_ln_g, conf_ln_b, mix_out_g, w_out, norm_ffn_g, w_up, w_down, final_norm_g, loss_target, m_norm_mix_g, m_w_in, m_gmlp_v_g, m_gmlp_w_s, m_gmlp_b_s, m_short_conv_w, m_conf_conv_w, m_conf_ln_g, m_conf_ln_b, m_mix_out_g, m_w_out, m_norm_ffn_g, m_w_up, m_w_down, m_final_norm_g, v_norm_mix_g, v_w_in, v_gmlp_v_g, v_gmlp_w_s, v_gmlp_b_s, v_short_conv_w, v_conf_conv_w, v_conf_ln_g, v_conf_ln_b, v_mix_out_g, v_w_out, v_norm_ffn_g, v_w_up, v_w_down, v_final_norm_g):
    me = 4 * lax.axis_index("x") + 2 * lax.axis_index("y") + lax.axis_index("c")
    x0, target = x[0], loss_target[0]
    s = x0.shape[0]
    tb_proj, tb_mix, tb_fb, tb_tn = min(TB_PROJ, s), min(TB_MIX, s), min(TB_FFN_BWD, s), min(TB_TN, s)
    conv_cols = D_GROUP // N_DEV

    def pad_rows(a, rows):
        return jnp.pad(a, ((0, rows - a.shape[0]), (0, 0)))

    wint_loc = [w_in[l].T.astype(BF16) for l in range(N_LAYERS)]
    wout_loc = [w_out[l].astype(BF16) for l in range(N_LAYERS)]
    wup_loc = [w_up[l].astype(BF16) for l in range(N_LAYERS)]
    wdn_loc = [w_down[l].astype(BF16) for l in range(N_LAYERS)]
    conv_loc = jnp.concatenate([pad_rows(short_conv_w[l], 8) for l in range(N_LAYERS)]
                               + [pad_rows(conf_conv_w[l], HALO) for l in range(N_LAYERS)], axis=0)
    wint, wout, wup, wdn = [None] * N_LAYERS, [None] * N_LAYERS, [None] * N_LAYERS, [None] * N_LAYERS
    wint0, conv_all = _exchange([wint_loc[0], conv_loc], [GATHER, GATHER], "gather_first_weights")
    wint[0] = wint0.reshape(D_IN, D_MODEL)
    conv_full = conv_all.transpose(1, 0, 2).reshape(-1, D_GROUP)
    scw = [conv_full[8 * l:8 * (l + 1)] for l in range(N_LAYERS)]
    ccw = [conv_full[8 * N_LAYERS + HALO * l:8 * N_LAYERS + HALO * (l + 1)] for l in range(N_LAYERS)]

    tril = jnp.tril(jnp.ones((CHUNK, CHUNK), dtype=bool))
    wm = [jnp.where(tril, gmlp_w_s[l], 0.0).astype(BF16) for l in range(N_LAYERS)]
    wmt = [w.transpose(0, 2, 1) for w in wm]
    bexp = [jnp.repeat(gmlp_b_s[l].T, HEAD_DIM, axis=1) for l in range(N_LAYERS)]

    def row(vec):
        return vec.reshape(1, -1)

    saved = []
    xc = x0
    for l in range(N_LAYERS):
        first = l == 0
        (z, qkv), moved = _in_proj_fwd(xc, row(norm_mix_g[l]), wint[l], tb_proj,
                                       ride=([wout_loc[0]], [GATHER]) if first else None)
        if first:
            wout[0] = moved[0].reshape(D_MODEL, D_MODEL)
        (ya, yb, yd), moved = _mix_fwd(z, row(gmlp_v_g[l]), wm[l], bexp[l], scw[l], ccw[l], row(conf_ln_g[l]),
                                       row(conf_ln_b[l]), tb_mix, ride=([wup_loc[0]], [GATHER]) if first else None)
        if first:
            wup[0] = moved[0]
        (yc,), moved = _attn_fwd(qkv, TQ, TK, ride=([wdn_loc[0], wint_loc[1]], [GATHER, GATHER]) if first else None)
        if first:
            wdn[0], wint[1] = moved[0], moved[1].reshape(D_IN, D_MODEL)
        ys = (ya, yb, yc, yd)
        x1 = _out_proj_fwd(ys, xc, row(mix_out_g[l]), wout[l], tb_proj)
        saved.append((xc, z, qkv, ys, x1))
        (xc,), moved = _ffn_fwd(x1, row(norm_ffn_g[l]), wup[l], wdn[l], tb_proj,
                                ride=([wout_loc[1], wup_loc[1], wdn_loc[1]], [GATHER] * 3) if first else None)
        if first:
            wout[1], wup[1], wdn[1] = moved[0].reshape(D_MODEL, D_MODEL), moved[1], moved[2]
    dx, g_final, loss_part = _loss_head(xc, row(final_norm_g), target, tb_proj)
    loss = lax.psum(loss_part[0, 0], MESH_AXES)

    parts = [None] * (4 * N_LAYERS)
    small_grads = [None] * N_LAYERS
    waiting_in = None
    for l in reversed(range(N_LAYERS)):
        xin, z, qkv, ys, x1 = saved[l]
        (dx1, hb_ffn, dpre, sq, dyb, g_ffn), moved = _ffn_bwd(
            x1, dx, row(norm_ffn_g[l]), wup[l], wdn[l], tb_fb,
            ride=([waiting_in], [SCATTER]) if waiting_in is not None else None)
        if waiting_in is not None:
            parts[4 * (l + 1)] = moved[0]
        grad_up = _tn_matmul(hb_ffn, dpre, N_DEV, False, True, tb_tn, "grad_w_up")
        grad_dn = _tn_matmul(sq, dyb, N_DEV, True, False, tb_tn, "grad_w_down")
        dya, dyb_mix, dyc, dyd, yn, g_mixout = _out_proj_bwd(dx1, ys, row(mix_out_g[l]), wout[l], tb_proj)
        grad_out = _tn_matmul(yn, dx1, 1, False, False, tb_tn, "grad_w_out").reshape(N_DEV, D_MODEL // N_DEV, D_MODEL)
        (dq, dk, dv), moved = _attn_bwd(qkv, dyc, ys[2], TQ, TK, ride=([grad_up, grad_dn], [SCATTER] * 2))
        parts[4 * l + 2], parts[4 * l + 3] = moved
        (dz, g_vg, g_ws, g_bs, g_scw, g_ccw, g_lng, g_lnb), moved = _mix_bwd(
            z, dya, dyb_mix, dyd, dq, dk, dv, row(gmlp_v_g[l]), wm[l], wmt[l], bexp[l], scw[l], ccw[l],
            row(conf_ln_g[l]), row(conf_ln_b[l]), tb_mix, ride=([grad_out], [SCATTER]))
        parts[4 * l + 1] = moved[0]
        dx, hb_in, g_mix = _in_proj_bwd(xin, dz, dx1, row(norm_mix_g[l]), wint[l], tb_proj)
        waiting_in = _tn_matmul(dz, hb_in, 1, False, False, tb_tn, "grad_w_in").reshape(N_DEV, D_IN // N_DEV, D_MODEL)
        small_grads[l] = dict(norm_mix_g=g_mix[0], gmlp_v_g=g_vg[0], gmlp_w_s=g_ws, gmlp_b_s=g_bs[:, :4].T,
                              short_conv_w=g_scw[:K_SHORT], conf_conv_w=g_ccw[:K_CONF], conf_ln_g=g_lng[0],
                              conf_ln_b=g_lnb[0], mix_out_g=g_mixout[0], norm_ffn_g=g_ffn[0])

    small_names = ["norm_mix_g", "gmlp_v_g", "gmlp_w_s", "gmlp_b_s", "short_conv_w", "conf_conv_w", "conf_ln_g",
                   "conf_ln_b", "mix_out_g", "norm_ffn_g"]
    small_list = [jnp.stack([small_grads[l][n] for l in range(N_LAYERS)]) for n in small_names] + [g_final[0]]
    small_shapes = [a.shape for a in small_list]
    parts[0], small_parts = _exchange([waiting_in, _pack(small_list)], [SCATTER, GATHER], "exchange_last_grads")

    given = dict(norm_mix_g=(norm_mix_g, m_norm_mix_g, v_norm_mix_g), gmlp_v_g=(gmlp_v_g, m_gmlp_v_g, v_gmlp_v_g),
                 gmlp_w_s=(gmlp_w_s, m_gmlp_w_s, v_gmlp_w_s), gmlp_b_s=(gmlp_b_s, m_gmlp_b_s, v_gmlp_b_s),
                 short_conv_w=(short_conv_w, m_short_conv_w, v_short_conv_w),
                 conf_conv_w=(conf_conv_w, m_conf_conv_w, v_conf_conv_w),
                 conf_ln_g=(conf_ln_g, m_conf_ln_g, v_conf_ln_g), conf_ln_b=(conf_ln_b, m_conf_ln_b, v_conf_ln_b),
                 mix_out_g=(mix_out_g, m_mix_out_g, v_mix_out_g), norm_ffn_g=(norm_ffn_g, m_norm_ffn_g, v_norm_ffn_g),
                 final_norm_g=(final_norm_g, m_final_norm_g, v_final_norm_g))
    sharded_small = ("short_conv_w", "conf_conv_w")

    def widen(a):
        full = jnp.zeros(a.shape[:-1] + (D_GROUP,), a.dtype)
        return lax.dynamic_update_slice(full, a, (0, 0, me * conv_cols))

    packed_state = []
    for k in range(3):
        packed_state.append(_pack([widen(given[n][k]) if n in sharded_small else given[n][k]
                                   for n in small_names + ["final_norm_g"]]))
    small_out = _reduce_adamw(small_parts, *packed_state, packed_state[0].shape[0], "adamw_small")
    small_res = {}
    for kind, packed in zip(("grad", "delta", "new_m", "new_v"), small_out):
        for n, val in zip(small_names + ["final_norm_g"], _unpack(packed, small_shapes)):
            if n in sharded_small:
                val = lax.dynamic_slice(val, (0, 0, me * conv_cols), val.shape[:-1] + (conv_cols,))
            small_res[kind, n] = val

    big_names = ["w_in", "w_out", "w_up", "w_down"]
    big_given = dict(w_in=(w_in, m_w_in, v_w_in), w_out=(w_out, m_w_out, v_w_out), w_up=(w_up, m_w_up, v_w_up),
                     w_down=(w_down, m_w_down, v_w_down))
    big_res = {}
    for j, n in enumerate(big_names):
        per_layer = []
        for l in range(N_LAYERS):
            state = [t[l].T if n == "w_in" else t[l] for t in big_given[n]]
            outs = _reduce_adamw(parts[4 * l + j], *state, TB_ADAM, "adamw_" + n)
            per_layer.append([o.T if n == "w_in" else o for o in outs])
        for k, kind in enumerate(("grad", "delta", "new_m", "new_v")):
            big_res[kind, n] = jnp.stack([per_layer[l][k] for l in range(N_LAYERS)])

    order = ["norm_mix_g", "w_in", "gmlp_v_g", "gmlp_w_s", "gmlp_b_s", "short_conv_w", "conf_conv_w", "conf_ln_g",
             "conf_ln_b", "mix_out_g", "w_out", "norm_ffn_g", "w_up", "w_down", "final_norm_g"]
    result = [loss, dx.reshape(x.shape)]
    for kind in ("grad", "delta", "new_m", "new_v"):
        for n in order:
            result.append(big_res[kind, n] if n in big_given else small_res[kind, n])
    return tuple(result)
```

```python
import jax
import jax.numpy as jnp
from jax import lax
from jax.experimental import pallas as pl
from jax.experimental.pallas import tpu as pltpu

F32 = jnp.float32
BF16 = jnp.bfloat16

D_MODEL = 1024
D_GROUP = 256
D_IN = 10 * D_GROUP
D_FF = 4 * D_MODEL
N_DEV = 8
N_LAYERS = 2
HEAD_DIM = 64
HEADS_PER_PAIR = 2
PAIR = HEADS_PER_PAIR * HEAD_DIM
CHUNK = 128
K_SHORT = 3
K_CONF = 31
HALO = 32
EPS = 1e-6
ATT_SCALE = HEAD_DIM ** -0.5
LOG_CUT = -104.0
MIB = 2 ** 20

ADAM_LR = 0.001
ADAM_B1 = 0.9
ADAM_B2 = 0.999
ADAM_EPS = 1e-08
ADAM_WD = 0.01
ADAM_STEP = 10

MESH_AXES = ("x", "y", "c")
GELU_C = 0.7978845608028654
GELU_A = 0.044715


def _mm(a, b):
    return jnp.dot(a, b, preferred_element_type=F32)


def _mm_nt(a, b):
    return lax.dot_general(a, b, (((1,), (1,)), ((), ())), preferred_element_type=F32)


def _mm_tn(a, b):
    return lax.dot_general(a, b, (((0,), (0,)), ((), ())), preferred_element_type=F32)


def _rms(x):
    return lax.rsqrt(jnp.mean(x * x, axis=-1, keepdims=True) + EPS)


def _rms_bwd(dy, xh, r):
    return r * (dy - xh * jnp.mean(dy * xh, axis=-1, keepdims=True))


def _sigmoid(x):
    return 1.0 / (1.0 + jnp.exp(-x))


def _whole(shape):
    return pl.BlockSpec(shape, lambda *_: (0,) * len(shape))


def _resident(shape):
    return pl.BlockSpec(shape, lambda *_: (0,) * len(shape), pipeline_mode=pl.Buffered(1))


def _rows(tb, width, col=0):
    return pl.BlockSpec((tb, width), lambda i: (i, col))


def _params(semantics, vmem_mib):
    return pltpu.CompilerParams(dimension_semantics=semantics, vmem_limit_bytes=vmem_mib * MIB)


def _sds(shape, dtype=F32):
    return jax.ShapeDtypeStruct(shape, dtype)


def _split_bf16(v):
    hi = v.astype(BF16)
    lo = (v - hi.astype(F32)).astype(BF16)
    return hi, lo


GATHER, SCATTER = "gather", "scatter"
ANY_SPEC = pl.BlockSpec(memory_space=pl.ANY)


def _exchange_copies(ins, outs, modes, send_sems, recv_sems, local_sems, with_arrivals=True):
    x, y, c = lax.axis_index("x"), lax.axis_index("y"), lax.axis_index("c")
    me = 4 * x + 2 * y + c
    local, sends, arrivals = [], [], []
    for a, mode in enumerate(modes):
        local.append(pltpu.make_async_copy(ins[a].at[me] if mode == SCATTER else ins[a], outs[a].at[me], local_sems.at[a]))
    for k in range(N_DEV - 1):
        flip = k + 1
        peer = (1 - x if flip & 4 else x, 1 - y if flip & 2 else y, 1 - c if flip & 1 else c)
        pf = 4 * peer[0] + 2 * peer[1] + peer[2]
        for a, mode in enumerate(modes):
            src = ins[a].at[pf] if mode == SCATTER else ins[a]
            for dst, group in ((outs[a].at[me], sends), (outs[a].at[pf], arrivals)):
                if group is sends or with_arrivals:
                    group.append(pltpu.make_async_remote_copy(
                        src_ref=src, dst_ref=dst, send_sem=send_sems.at[a, k], recv_sem=recv_sems.at[a, k],
                        device_id=peer, device_id_type=pl.DeviceIdType.MESH))
    return local, sends, arrivals


def _exchange_start(*refs_and_modes):
    local, sends, _ = _exchange_copies(*refs_and_modes, with_arrivals=False)
    for cp in local + sends:
        cp.start()


def _exchange_wait(*refs_and_modes):
    local, sends, arrivals = _exchange_copies(*refs_and_modes)
    for cp in sends:
        cp.wait_send()
    for cp in arrivals:
        cp.wait_recv()
    for cp in local:
        cp.wait()


def _exchange_shapes(arrays, modes):
    out_shape = [_sds(a.shape if mode == SCATTER else (N_DEV,) + a.shape, a.dtype) for a, mode in zip(arrays, modes)]
    n = len(arrays)
    sems = [pltpu.SemaphoreType.DMA((n, N_DEV - 1)), pltpu.SemaphoreType.DMA((n, N_DEV - 1)), pltpu.SemaphoreType.DMA((n,))]
    return out_shape, sems


def _exchange(arrays, modes, name):
    n = len(arrays)
    out_shape, sems = _exchange_shapes(arrays, modes)

    def body(*refs):
        _exchange_start(refs[:n], refs[n:2 * n], modes, *refs[2 * n:])
        _exchange_wait(refs[:n], refs[n:2 * n], modes, *refs[2 * n:])

    return pl.pallas_call(body, name=name, out_shape=out_shape, in_specs=[ANY_SPEC] * n, out_specs=[ANY_SPEC] * n,
                          scratch_shapes=sems)(*arrays)


def _call(body, args, ride, *, name, grid, in_specs, out_specs, out_shape, scratch_shapes=(), compiler_params):
    if ride is None:
        outs = pl.pallas_call(body, name=name, grid=grid, in_specs=in_specs, out_specs=out_specs, out_shape=out_shape,
                              scratch_shapes=scratch_shapes, compiler_params=compiler_params)(*args)
        return outs, []
    arrays, modes = ride
    n, n_in, n_out, n_scratch = len(arrays), len(in_specs), len(out_specs), len(scratch_shapes)
    moved_shape, sems = _exchange_shapes(arrays, modes)
    n_steps = 1
    for g in grid:
        n_steps *= g

    def riding(*refs):
        ins, refs = refs[:n_in], refs[n_in:]
        r_ins, refs = refs[:n], refs[n:]
        outs, refs = refs[:n_out], refs[n_out:]
        r_outs, refs = refs[:n], refs[n:]
        scratch, r_sems = refs[:n_scratch], refs[n_scratch:]
        step = pl.program_id(0)
        for axis in range(1, len(grid)):
            step = step * grid[axis] + pl.program_id(axis)

        @pl.when(step == 0)
        def _():
            _exchange_start(r_ins, r_outs, modes, *r_sems)

        body(*ins, *outs, *scratch)

        @pl.when(step == n_steps - 1)
        def _():
            _exchange_wait(r_ins, r_outs, modes, *r_sems)

    outs = pl.pallas_call(
        riding, name=name, grid=grid, in_specs=list(in_specs) + [ANY_SPEC] * n,
        out_specs=list(out_specs) + [ANY_SPEC] * n, out_shape=list(out_shape) + moved_shape,
        scratch_shapes=list(scratch_shapes) + sems, compiler_params=compiler_params)(*args, *arrays)
    return outs[:n_out], outs[n_out:]


def _in_proj_fwd(x, g, wint, tb, ride=None):
    s = x.shape[0]

    def body(x_ref, g_ref, w_ref, z_ref, qkv_ref, hb_ref):
        xv = x_ref[...]
        h = (xv * _rms(xv) * g_ref[...]).astype(BF16)
        hb_ref[...] = h
        z = _mm_nt(h, w_ref[...])
        z_ref[...] = z
        qkv_ref[:, 0:D_GROUP] = (z[:, 5 * D_GROUP:6 * D_GROUP] * ATT_SCALE).astype(BF16)
        qkv_ref[:, D_GROUP:3 * D_GROUP] = z[:, 6 * D_GROUP:8 * D_GROUP].astype(BF16)

    return _call(
        body, (x, g, wint), ride, name="in_proj_fwd", grid=(s // tb,),
        in_specs=[_rows(tb, D_MODEL), _whole((1, D_MODEL)), _resident((D_IN, D_MODEL))],
        out_specs=[_rows(tb, D_IN), _rows(tb, 3 * D_GROUP), _rows(tb, D_MODEL)],
        out_shape=[_sds((s, D_IN)), _sds((s, 3 * D_GROUP), BF16), _sds((s, D_MODEL), BF16)],
        compiler_params=_params(("arbitrary",), 48))


def _gelu(x):
    return 0.5 * x * (1.0 + jnp.tanh(GELU_C * (x + GELU_A * x * x * x)))


def _gelu_grad(x):
    t = jnp.tanh(GELU_C * (x + GELU_A * x * x * x))
    return 0.5 * (1.0 + t) + 0.5 * x * (1.0 - t * t) * GELU_C * (1.0 + 3.0 * GELU_A * x * x)


def _head_lane(width):
    return lax.broadcasted_iota(jnp.int32, (1, width), 1) // HEAD_DIM


def _gating_chunk(wm_ref, bexp_ref, vc, lane_h):
    f = bexp_ref[...]
    for h in range(D_GROUP // HEAD_DIM):
        f = f + _mm(wm_ref[h], jnp.where(lane_h == h, vc, 0))
    return f


def _halo_specs(s, tb, width_blocks):
    per = tb // HALO
    prev = pl.BlockSpec((HALO, width_blocks), lambda i: (jnp.maximum(i * per - 1, 0), 0))
    nxt = pl.BlockSpec((HALO, width_blocks), lambda i: (jnp.minimum((i + 1) * per, s // HALO - 1), 0))
    return prev, nxt


SUBLANES = 8
TAP_SLACK = 24


def _taps(buf_ref, shifted_ref, first, n_taps, rows, visit):
    for residue in range(SUBLANES):
        taps = [j for j in range(n_taps) if (first + j) % SUBLANES == residue]
        if not taps:
            continue
        lo = first + taps[0]
        span = first + taps[-1] - lo + rows
        shifted_ref[0:span, :] = buf_ref[pl.ds(lo, span), :]
        for j in taps:
            visit(j, shifted_ref[pl.ds(first + j - lo, rows), :])


def _mix_fwd(z, vg, wm, bexp, scw, ccw, lng, lnb, tb, ride=None):
    s = z.shape[0]
    prev_spec, _ = _halo_specs(s, tb, D_IN)

    def body(z_ref, zp_ref, vg_ref, wm_ref, bexp_ref, scw_ref, ccw_ref, lng_ref, lnb_ref,
             ya_ref, yb_ref, yd_ref, c_ref, pbuf, hbuf, shifted):
        keep = (pl.program_id(0) > 0).astype(F32)
        lane_h = _head_lane(D_GROUP)
        ga = _gelu(z_ref[:, 0:2 * D_GROUP])
        u, v = ga[:, :D_GROUP], ga[:, D_GROUP:]
        vn = (v * _rms(v) * vg_ref[...]).astype(BF16)
        for n in range(tb // CHUNK):
            rows = slice(n * CHUNK, (n + 1) * CHUNK)
            ya_ref[rows, :] = u[rows] * _gating_chunk(wm_ref, bexp_ref, vn[rows], lane_h)
        p = z_ref[:, 3 * D_GROUP:4 * D_GROUP] * z_ref[:, 4 * D_GROUP:5 * D_GROUP]
        pbuf[0:HALO, :] = zp_ref[:, 3 * D_GROUP:4 * D_GROUP] * zp_ref[:, 4 * D_GROUP:5 * D_GROUP] * keep
        pbuf[HALO:HALO + tb, :] = p
        cv = scw_ref[K_SHORT - 1:K_SHORT, :] * p
        for k in range(K_SHORT - 1):
            cv = cv + scw_ref[k:k + 1, :] * pbuf[pl.ds(HALO - (K_SHORT - 1) + k, tb), :]
        yb_ref[...] = z_ref[:, 2 * D_GROUP:3 * D_GROUP] * cv
        hbuf[0:HALO, :] = zp_ref[:, 8 * D_GROUP:9 * D_GROUP] * _sigmoid(zp_ref[:, 9 * D_GROUP:10 * D_GROUP]) * keep
        hbuf[HALO:HALO + tb, :] = z_ref[:, 8 * D_GROUP:9 * D_GROUP] * _sigmoid(z_ref[:, 9 * D_GROUP:10 * D_GROUP])
        conv = [jnp.zeros((tb, D_GROUP), F32)]

        def tap(k, window):
            conv[0] = conv[0] + ccw_ref[k:k + 1, :] * window

        _taps(hbuf, shifted, HALO - (K_CONF - 1), K_CONF, tb, tap)
        c = conv[0]
        c_ref[...] = c
        xc = c - jnp.mean(c, axis=-1, keepdims=True)
        ln = xc * lax.rsqrt(jnp.mean(xc * xc, axis=-1, keepdims=True) + EPS) * lng_ref[...] + lnb_ref[...]
        yd_ref[...] = ln * _sigmoid(ln)

    grp = _rows(tb, D_GROUP)
    return _call(
        body, (z, z, vg, wm, bexp, scw, ccw, lng, lnb), ride, name="mix_fwd", grid=(s // tb,),
        in_specs=[_rows(tb, D_IN), prev_spec, _whole((1, D_GROUP)), _whole(wm.shape), _whole(bexp.shape),
                  _whole(scw.shape), _whole(ccw.shape), _whole((1, D_GROUP)), _whole((1, D_GROUP))],
        out_specs=[grp, grp, grp, grp],
        out_shape=[_sds((s, D_GROUP))] * 4,
        scratch_shapes=[pltpu.VMEM((HALO + tb, D_GROUP), F32), pltpu.VMEM((HALO + tb, D_GROUP), F32),
                        pltpu.VMEM((tb + TAP_SLACK, D_GROUP), F32)],
        compiler_params=_params(("arbitrary",), 40))


def _stick_tile(qh, kt, causal, c, upper):
    x = _mm_nt(qh, kt)
    soft = jnp.log(1.0 + jnp.exp(-jnp.abs(x)))
    lb = jnp.minimum(x, 0.0) - soft
    lom = jnp.where(causal, -jnp.maximum(x, 0.0) - soft, 0.0)
    hi, lo = _split_bf16(lom)
    stick = c + _mm(hi, upper[...]) + _mm(lo, upper[...])
    w = jnp.where(causal, jnp.exp(lb + stick), 0.0)
    return w, lb, lom


def _triangle(n, diagonal):
    return jnp.tri(n, n, diagonal, dtype=BF16)


def _causal_tile(qi, tq, k0, tk):
    qpos = qi * tq + lax.broadcasted_iota(jnp.int32, (tq, 1), 0)
    return k0 + lax.broadcasted_iota(jnp.int32, (1, tk), 1) < qpos


def _sticks_alive(cs):
    longest = cs[0]
    for c in cs[1:]:
        longest = jnp.maximum(longest, c)
    return (jnp.max(longest) > LOG_CUT).astype(jnp.int32)


def _walk(body, qi, tq, tk, init):
    start = (((qi + 1) * tq - 1) // tk, jnp.int32(1)) + tuple(init)
    return lax.while_loop(lambda cr: jnp.logical_and(cr[0] >= 0, cr[1] > 0), body, start)[2:]


def _attn_fwd(qkv, tq, tk, ride=None):
    s = qkv.shape[0]
    n_heads = D_GROUP // HEAD_DIM

    def body(q_ref, k_ref, v_ref, upper, o_ref):
        qi = pl.program_id(0)
        q = q_ref[...]
        lane_h = _head_lane(D_GROUP)
        qhs = [jnp.where(lane_h == h, q, 0) for h in range(n_heads)]

        def step(carry):
            kb, _, acc = carry[:3]
            cs = list(carry[3:])
            k0 = pl.multiple_of(kb * tk, tk)
            kt = k_ref[pl.ds(k0, tk), :]
            vt = v_ref[pl.ds(k0, tk), :]
            causal = _causal_tile(qi, tq, k0, tk)
            for h in range(n_heads):
                w, _, lom = _stick_tile(qhs[h], kt, causal, cs[h], upper)
                acc = acc + _mm(w.astype(BF16), jnp.where(lane_h == h, vt, 0))
                cs[h] = cs[h] + jnp.sum(lom, axis=1, keepdims=True)
            return (kb - 1, _sticks_alive(cs), acc) + tuple(cs)

        init = [jnp.zeros((tq, D_GROUP), F32)] + [jnp.zeros((tq, 1), F32)] * n_heads
        o_ref[...] = _walk(step, qi, tq, tk, init)[0]

    return _call(
        body, (qkv, qkv, qkv, _triangle(tk, -1)), ride, name="attn_fwd", grid=(s // tq,),
        in_specs=[pl.BlockSpec((tq, D_GROUP), lambda qi: (qi, 0)),
                  pl.BlockSpec((s, D_GROUP), lambda qi: (0, 1), pipeline_mode=pl.Buffered(1)),
                  pl.BlockSpec((s, D_GROUP), lambda qi: (0, 2), pipeline_mode=pl.Buffered(1)),
                  _resident((tk, tk))],
        out_specs=[pl.BlockSpec((tq, D_GROUP), lambda qi: (qi, 0))],
        out_shape=[_sds((s, D_GROUP))],
        compiler_params=_params(("arbitrary",), 40))


def _out_proj_fwd(ys, x, mg, wout, tb):
    s = x.shape[0]

    def body(ya_ref, yb_ref, yc_ref, yd_ref, x_ref, mg_ref, w_ref, o_ref):
        acc = x_ref[...]
        for gi, y_ref in enumerate((ya_ref, yb_ref, yc_ref, yd_ref)):
            cols = slice(gi * D_GROUP, (gi + 1) * D_GROUP)
            y = y_ref[...]
            acc = acc + _mm((y * _rms(y) * mg_ref[:, cols]).astype(BF16), w_ref[cols, :])
        o_ref[...] = acc

    grp = _rows(tb, D_GROUP)
    return pl.pallas_call(
        body, name="out_proj_fwd", grid=(s // tb,),
        in_specs=[grp, grp, grp, grp, _rows(tb, D_MODEL), _whole((1, D_MODEL)), _whole((D_MODEL, D_MODEL))],
        out_specs=_rows(tb, D_MODEL), out_shape=_sds((s, D_MODEL)),
        compiler_params=_params(("parallel",), 32),
    )(*ys, x, mg, wout)


def _ffn_fwd(x, g, wup, wdn, tb, ride=None):
    s = x.shape[0]
    ff = D_FF // N_DEV

    def body(x_ref, g_ref, wu_ref, wd_ref, o_ref, a_ref):
        xv = x_ref[...]
        h = (xv * _rms(xv) * g_ref[...]).astype(BF16)
        acc = xv
        for d in range(N_DEV):
            a = jnp.maximum(_mm(h, wu_ref[d]), 0.0)
            a_ref[:, d * ff:(d + 1) * ff] = a
            acc = acc + _mm((a * a).astype(BF16), wd_ref[d])
        o_ref[...] = acc

    return _call(
        body, (x, g, wup, wdn), ride, name="ffn_fwd", grid=(s // tb,),
        in_specs=[_rows(tb, D_MODEL), _whole((1, D_MODEL)), _resident((N_DEV, D_MODEL, ff)), _resident((N_DEV, ff, D_MODEL))],
        out_specs=[_rows(tb, D_MODEL), _rows(tb, D_FF)], out_shape=[_sds((s, D_MODEL)), _sds((s, D_FF))],
        compiler_params=_params(("arbitrary",), 56))


def _loss_head(x, g, tgt, tb):
    s = x.shape[0]

    def body(x_ref, g_ref, t_ref, dx_ref, dg_ref, loss_ref):
        @pl.when(pl.program_id(0) == 0)
        def _():
            dg_ref[...] = jnp.zeros_like(dg_ref)
            loss_ref[...] = jnp.zeros_like(loss_ref)

        xv = x_ref[...]
        r = _rms(xv)
        xh = xv * r
        err = xh * g_ref[...] - t_ref[...]
        loss_ref[...] += 0.5 * jnp.sum(jnp.mean(err * err, axis=-1, keepdims=True))
        dy = err * (1.0 / D_MODEL)
        dg_ref[...] += jnp.sum(dy * xh, axis=0, keepdims=True)
        dx_ref[...] = _rms_bwd(dy * g_ref[...], xh, r)

    return pl.pallas_call(
        body, name="loss_head", grid=(s // tb,),
        in_specs=[_rows(tb, D_MODEL), _whole((1, D_MODEL)), _rows(tb, D_MODEL)],
        out_specs=[_rows(tb, D_MODEL), _whole((1, D_MODEL)), _whole((8, 128))],
        out_shape=[_sds((s, D_MODEL)), _sds((1, D_MODEL)), _sds((8, 128))],
        compiler_params=_params(("arbitrary",), 32),
    )(x, g, tgt)


def _ffn_bwd(x1, act, dx2, g, wup, wdn, tb, ride=None):
    s = x1.shape[0]
    ff = D_FF // N_DEV

    def body(x_ref, a_ref, dy_ref, g_ref, wu_ref, wd_ref, dx_ref, hb_ref, dpre_ref, sq_ref, dyb_ref, dg_ref):
        @pl.when(pl.program_id(0) == 0)
        def _():
            dg_ref[...] = jnp.zeros_like(dg_ref)

        xv = x_ref[...]
        r = _rms(xv)
        xh = xv * r
        hb_ref[...] = (xh * g_ref[...]).astype(BF16)
        dyv = dy_ref[...]
        dyb = dyv.astype(BF16)
        dyb_ref[...] = dyb
        dh = jnp.zeros((tb, D_MODEL), F32)
        for d in range(N_DEV):
            cols = slice(d * ff, (d + 1) * ff)
            a = a_ref[:, cols]
            sq_ref[:, cols] = (a * a).astype(BF16)
            dpre = (_mm_nt(dyb, wd_ref[d]) * (2.0 * a)).astype(BF16)
            dpre_ref[:, cols] = dpre
            dh = dh + _mm_nt(dpre, wu_ref[d])
        dg_ref[...] += jnp.sum(dh * xh, axis=0, keepdims=True)
        dx_ref[...] = dyv + _rms_bwd(dh * g_ref[...], xh, r)

    return _call(
        body, (x1, act, dx2, g, wup, wdn), ride, name="ffn_bwd", grid=(s // tb,),
        in_specs=[_rows(tb, D_MODEL), _rows(tb, D_FF), _rows(tb, D_MODEL), _whole((1, D_MODEL)),
                  _resident((N_DEV, D_MODEL, ff)), _resident((N_DEV, ff, D_MODEL))],
        out_specs=[_rows(tb, D_MODEL), _rows(tb, D_MODEL), _rows(tb, D_FF), _rows(tb, D_FF), _rows(tb, D_MODEL),
                   _whole((1, D_MODEL))],
        out_shape=[_sds((s, D_MODEL)), _sds((s, D_MODEL), BF16), _sds((s, D_FF), BF16), _sds((s, D_FF), BF16),
                   _sds((s, D_MODEL), BF16), _sds((1, D_MODEL))],
        compiler_params=_params(("arbitrary",), 60))


def _tn_matmul(a, b, nd, a_blocked, b_blocked, tb, name):
    s = a.shape[0]
    ka = a.shape[1] // nd if a_blocked else a.shape[1]
    nb = b.shape[1] // nd if b_blocked else b.shape[1]

    def body(a_ref, b_ref, o_ref):
        @pl.when(pl.program_id(1) == 0)
        def _():
            o_ref[...] = jnp.zeros_like(o_ref)

        o_ref[0] += _mm_tn(a_ref[...].astype(BF16), b_ref[...].astype(BF16))

    return pl.pallas_call(
        body, name=name, grid=(nd, s // tb),
        in_specs=[pl.BlockSpec((tb, ka), (lambda d, i: (i, d)) if a_blocked else (lambda d, i: (i, 0))),
                  pl.BlockSpec((tb, nb), (lambda d, i: (i, d)) if b_blocked else (lambda d, i: (i, 0)))],
        out_specs=pl.BlockSpec((1, ka, nb), lambda d, i: (d, 0, 0)),
        out_shape=_sds((nd, ka, nb)),
        compiler_params=_params(("parallel", "arbitrary"), 56),
    )(a, b)


def _out_proj_bwd(dx1, ys, mg, wout, tb):
    s = dx1.shape[0]

    def body(dx_ref, ya_ref, yb_ref, yc_ref, yd_ref, mg_ref, w_ref,
             dya_ref, dyb_ref, dyc_ref, dyd_ref, yn_ref, dmg_ref):
        @pl.when(pl.program_id(0) == 0)
        def _():
            dmg_ref[...] = jnp.zeros_like(dmg_ref)

        dyn = _mm_nt(dx_ref[...].astype(BF16), w_ref[...])
        groups = ((ya_ref, dya_ref), (yb_ref, dyb_ref), (yc_ref, dyc_ref), (yd_ref, dyd_ref))
        for gi, (y_ref, dy_ref) in enumerate(groups):
            cols = slice(gi * D_GROUP, (gi + 1) * D_GROUP)
            y = y_ref[...]
            r = _rms(y)
            n = y * r
            gain = mg_ref[:, cols]
            dn = dyn[:, cols]
            yn_ref[:, cols] = (n * gain).astype(BF16)
            dmg_ref[:, cols] += jnp.sum(dn * n, axis=0, keepdims=True)
            dy_ref[...] = _rms_bwd(dn * gain, n, r)

    grp = _rows(tb, D_GROUP)
    return pl.pallas_call(
        body, name="out_proj_bwd", grid=(s // tb,),
        in_specs=[_rows(tb, D_MODEL), grp, grp, grp, grp, _whole((1, D_MODEL)), _whole((D_MODEL, D_MODEL))],
        out_specs=[grp, grp, grp, grp, _rows(tb, D_MODEL), _whole((1, D_MODEL))],
        out_shape=[_sds((s, D_GROUP))] * 4 + [_sds((s, D_MODEL), BF16), _sds((1, D_MODEL))],
        compiler_params=_params(("arbitrary",), 32),
    )(dx1, *ys, mg, wout)


def _attn_bwd(qkv, do, o, tq, tk, ride=None):
    s = qkv.shape[0]
    nq = s // tq
    n_pairs = D_GROUP // PAIR

    def body(q_ref, k_ref, v_ref, do_ref, o_ref, upper, upper_eq, dq_ref, dk_hbm, dv_hbm, dk_acc, dv_acc, sems):
        hp, qi = pl.program_id(0), pl.program_id(1)

        @pl.when(qi == 0)
        def _():
            dk_acc[...] = jnp.zeros_like(dk_acc)
            dv_acc[...] = jnp.zeros_like(dv_acc)

        q = q_ref[...]
        dob = do_ref[...].astype(BF16)
        prod = dob.astype(F32) * o_ref[...]
        lane_h = _head_lane(PAIR)
        heads = []
        for h in range(HEADS_PER_PAIR):
            in_head = lane_h == h
            total = jnp.sum(jnp.where(in_head, prod, 0.0), axis=1, keepdims=True)
            heads.append((in_head, jnp.where(in_head, q, 0), jnp.where(in_head, dob, 0), total))

        def step(carry):
            kb, _, acc = carry[:3]
            cs = list(carry[3:3 + HEADS_PER_PAIR])
            nears = list(carry[3 + HEADS_PER_PAIR:])
            k0 = pl.multiple_of(kb * tk, tk)
            kt = k_ref[pl.ds(k0, tk), :]
            vt = v_ref[pl.ds(k0, tk), :]
            causal = _causal_tile(qi, tq, k0, tk)
            dk_t = jnp.zeros((tk, PAIR), F32)
            dv_t = jnp.zeros((tk, PAIR), F32)
            for h, (in_head, qh, doh, total) in enumerate(heads):
                w, lb, lom = _stick_tile(qh, kt, causal, cs[h], upper)
                wb = w.astype(BF16)
                gw = _mm_nt(doh, vt) * wb.astype(F32)
                hi, lo = _split_bf16(gw)
                far = total - nears[h] - _mm(hi, upper_eq[...]) - _mm(lo, upper_eq[...])
                beta = jnp.exp(lb)
                dxb = jnp.where(causal, gw * (1.0 - beta) - far * beta, 0.0).astype(BF16)
                acc = acc + _mm(dxb, jnp.where(in_head, kt, 0))
                dk_t = dk_t + _mm_tn(dxb, qh)
                dv_t = dv_t + _mm_tn(wb, doh)
                cs[h] = cs[h] + jnp.sum(lom, axis=1, keepdims=True)
                nears[h] = nears[h] + jnp.sum(gw, axis=1, keepdims=True)
            dk_acc[pl.ds(k0, tk), :] += dk_t
            dv_acc[pl.ds(k0, tk), :] += dv_t
            return (kb - 1, _sticks_alive(cs), acc) + tuple(cs) + tuple(nears)

        init = [jnp.zeros((tq, PAIR), F32)] + [jnp.zeros((tq, 1), F32)] * (2 * HEADS_PER_PAIR)
        dq_ref[...] = _walk(step, qi, tq, tk, init)[0]

        @pl.when(qi == nq - 1)
        def _():
            ck = pltpu.make_async_copy(dk_acc, dk_hbm.at[hp], sems.at[0])
            cv = pltpu.make_async_copy(dv_acc, dv_hbm.at[hp], sems.at[1])
            ck.start()
            cv.start()
            ck.wait()
            cv.wait()

    blk = pl.BlockSpec((tq, PAIR), lambda hp, qi: (qi, hp))
    return _call(
        body, (qkv, qkv, qkv, do, o, _triangle(tk, -1), _triangle(tk, 0)), ride, name="attn_bwd", grid=(n_pairs, nq),
        in_specs=[blk, pl.BlockSpec((s, PAIR), lambda hp, qi: (0, 2 + hp)),
                  pl.BlockSpec((s, PAIR), lambda hp, qi: (0, 4 + hp)), blk, blk, _resident((tk, tk)), _resident((tk, tk))],
        out_specs=[blk, ANY_SPEC, ANY_SPEC],
        out_shape=[_sds((s, D_GROUP)), _sds((n_pairs, s, PAIR)), _sds((n_pairs, s, PAIR))],
        scratch_shapes=[pltpu.VMEM((s, PAIR), F32), pltpu.VMEM((s, PAIR), F32), pltpu.SemaphoreType.DMA((2,))],
        compiler_params=_params(("arbitrary", "arbitrary"), 56))


def _mix_bwd(z, conv, dya, dyb, dyd, dq, dk, dv, vg, wm, wmt, bexp, scw, ccw, lng, lnb, tb, ride=None):
    s = z.shape[0]
    n_steps = s // tb
    prev_spec, next_spec = _halo_specs(s, tb, D_IN)
    _, next_grp = _halo_specs(s, tb, D_GROUP)
    ext = tb + HALO

    def body(z_ref, zp_ref, zn_ref, dya_ref, dyb_ref, dybn_ref, dyd_ref, dydn_ref, c_ref, cn_ref, dq_ref, dk0_ref, dk1_ref,
             dv0_ref, dv1_ref, vg_ref, wm_ref, wmt_ref, bexp_ref, scw_ref, ccw_ref, lng_ref, lnb_ref,
             dz_ref, dvg_ref, dws_ref, dbs_ref, dscw_ref, dccw_ref, dlng_ref, dlnb_ref,
             pbuf, hbuf, gbuf, cbuf, dubuf, dvnbuf, shifted):
        i = pl.program_id(0)

        @pl.when(i == 0)
        def _():
            for ref in (dvg_ref, dws_ref, dbs_ref, dscw_ref, dccw_ref, dlng_ref, dlnb_ref):
                ref[...] = jnp.zeros_like(ref)

        keep_prev = (i > 0).astype(F32)
        keep_next = (i < n_steps - 1).astype(F32)
        lane_h = _head_lane(D_GROUP)

        za = z_ref[:, 0:2 * D_GROUP]
        ga = _gelu(za)
        u, v = ga[:, :D_GROUP], ga[:, D_GROUP:]
        r = _rms(v)
        vh = v * r
        vn = (vh * vg_ref[...]).astype(BF16)
        tril = lax.broadcasted_iota(jnp.int32, (CHUNK, CHUNK), 0) >= lax.broadcasted_iota(jnp.int32, (CHUNK, CHUNK), 1)
        dbias = jnp.zeros((CHUNK, D_GROUP), F32)
        for n in range(tb // CHUNK):
            rows = slice(n * CHUNK, (n + 1) * CHUNK)
            vc = vn[rows]
            dy = dya_ref[rows, :]
            dubuf[rows, :] = dy * _gating_chunk(wm_ref, bexp_ref, vc, lane_h)
            df = dy * u[rows]
            dfb = df.astype(BF16)
            dvn = jnp.zeros((CHUNK, D_GROUP), F32)
            for h in range(D_GROUP // HEAD_DIM):
                dfh = jnp.where(lane_h == h, dfb, 0)
                dvn = dvn + _mm(wmt_ref[h], dfh)
                dws_ref[h] += jnp.where(tril, _mm_nt(dfh, vc), 0.0)
            dvnbuf[rows, :] = dvn
            dbias = dbias + df
        for h in range(D_GROUP // HEAD_DIM):
            per_head = jnp.sum(jnp.where(lane_h == h, dbias, 0.0), axis=1, keepdims=True)
            dbs_ref[...] += per_head * (lax.broadcasted_iota(jnp.int32, (1, CHUNK), 1) == h).astype(F32)
        dvn = dvnbuf[...]
        dvg_ref[...] += jnp.sum(dvn * vh, axis=0, keepdims=True)
        dgelu = _gelu_grad(za)
        dz_ref[:, 0:D_GROUP] = (dubuf[...] * dgelu[:, :D_GROUP]).astype(BF16)
        dz_ref[:, D_GROUP:2 * D_GROUP] = (_rms_bwd(dvn * vg_ref[...], vh, r) * dgelu[:, D_GROUP:]).astype(BF16)

        gate_b = z_ref[:, 2 * D_GROUP:3 * D_GROUP]
        gate_c = z_ref[:, 3 * D_GROUP:4 * D_GROUP]
        hh = z_ref[:, 4 * D_GROUP:5 * D_GROUP]
        p = gate_c * hh
        pbuf[0:HALO, :] = zp_ref[:, 3 * D_GROUP:4 * D_GROUP] * zp_ref[:, 4 * D_GROUP:5 * D_GROUP] * keep_prev
        pbuf[HALO:HALO + tb, :] = p
        dyb_v = dyb_ref[...]
        dcv = dyb_v * gate_b
        gbuf[0:tb, :] = dcv
        gbuf[tb:ext, :] = dybn_ref[...] * zn_ref[:, 2 * D_GROUP:3 * D_GROUP] * keep_next
        cv = scw_ref[K_SHORT - 1:K_SHORT, :] * p
        dp = scw_ref[K_SHORT - 1:K_SHORT, :] * dcv
        dscw_ref[K_SHORT - 1:K_SHORT, :] += jnp.sum(dcv * p, axis=0, keepdims=True)
        for k in range(K_SHORT - 1):
            earlier = pbuf[pl.ds(HALO - (K_SHORT - 1) + k, tb), :]
            cv = cv + scw_ref[k:k + 1, :] * earlier
            dp = dp + scw_ref[k:k + 1, :] * gbuf[pl.ds(K_SHORT - 1 - k, tb), :]
            dscw_ref[k:k + 1, :] += jnp.sum(dcv * earlier, axis=0, keepdims=True)
        dz_ref[:, 2 * D_GROUP:3 * D_GROUP] = (dyb_v * cv).astype(BF16)
        dz_ref[:, 3 * D_GROUP:4 * D_GROUP] = (dp * hh).astype(BF16)
        dz_ref[:, 4 * D_GROUP:5 * D_GROUP] = (dp * gate_c).astype(BF16)

        dz_ref[:, 5 * D_GROUP:6 * D_GROUP] = (dq_ref[...] * ATT_SCALE).astype(BF16)
        dz_ref[:, 6 * D_GROUP:6 * D_GROUP + PAIR] = dk0_ref[...].astype(BF16)
        dz_ref[:, 6 * D_GROUP + PAIR:7 * D_GROUP] = dk1_ref[...].astype(BF16)
        dz_ref[:, 7 * D_GROUP:7 * D_GROUP + PAIR] = dv0_ref[...].astype(BF16)
        dz_ref[:, 7 * D_GROUP + PAIR:8 * D_GROUP] = dv1_ref[...].astype(BF16)

        a = z_ref[:, 8 * D_GROUP:9 * D_GROUP]
        sg = _sigmoid(z_ref[:, 9 * D_GROUP:10 * D_GROUP])
        hbuf[0:HALO, :] = zp_ref[:, 8 * D_GROUP:9 * D_GROUP] * _sigmoid(zp_ref[:, 9 * D_GROUP:10 * D_GROUP]) * keep_prev
        hbuf[HALO:HALO + tb, :] = a * sg
        c = jnp.concatenate([c_ref[...], cn_ref[...]], axis=0)
        xc = c - jnp.mean(c, axis=-1, keepdims=True)
        rs = lax.rsqrt(jnp.mean(xc * xc, axis=-1, keepdims=True) + EPS)
        xh = xc * rs
        ln = xh * lng_ref[...] + lnb_ref[...]
        sl = _sigmoid(ln)
        dy_ext = jnp.concatenate([dyd_ref[...], dydn_ref[...] * keep_next], axis=0)
        dln = dy_ext * sl * (1.0 + ln * (1.0 - sl))
        dlng_ref[...] += jnp.sum(dln[:tb] * xh[:tb], axis=0, keepdims=True)
        dlnb_ref[...] += jnp.sum(dln[:tb], axis=0, keepdims=True)
        dxh = dln * lng_ref[...]
        dc = rs * (dxh - jnp.mean(dxh, axis=-1, keepdims=True) - xh * jnp.mean(dxh * xh, axis=-1, keepdims=True))
        cbuf[...] = dc
        dc_blk = dc[:tb]
        grad_in = [jnp.zeros((tb, D_GROUP), F32)]

        def tap_input(j, window):
            k = K_CONF - 1 - j
            grad_in[0] = grad_in[0] + ccw_ref[k:k + 1, :] * window

        def tap_filter(k, window):
            dccw_ref[k:k + 1, :] += jnp.sum(dc_blk * window, axis=0, keepdims=True)

        _taps(cbuf, shifted, 0, K_CONF, tb, tap_input)
        _taps(hbuf, shifted, HALO - (K_CONF - 1), K_CONF, tb, tap_filter)
        dhd = grad_in[0]
        dz_ref[:, 8 * D_GROUP:9 * D_GROUP] = (dhd * sg).astype(BF16)
        dz_ref[:, 9 * D_GROUP:10 * D_GROUP] = (dhd * a * sg * (1.0 - sg)).astype(BF16)

    grp = _rows(tb, D_GROUP)
    pair0 = pl.BlockSpec((None, tb, PAIR), lambda i: (0, i, 0))
    pair1 = pl.BlockSpec((None, tb, PAIR), lambda i: (1, i, 0))
    small = [_sds((1, D_GROUP)), _sds((4, CHUNK, CHUNK)), _sds((CHUNK, CHUNK)), _sds((8, D_GROUP)),
             _sds((HALO, D_GROUP)), _sds((1, D_GROUP)), _sds((1, D_GROUP))]
    return _call(
        body, (z, z, z, dya, dyb, dyb, dyd, dyd, conv, conv, dq, dk, dk, dv, dv, vg, wm, wmt, bexp, scw, ccw, lng, lnb), ride,
        name="mix_bwd", grid=(n_steps,),
        in_specs=[_rows(tb, D_IN), prev_spec, next_spec, grp, grp, next_grp, grp, next_grp, grp, next_grp, grp,
                  pair0, pair1, pair0, pair1,
                  _whole((1, D_GROUP)), _whole(wm.shape), _whole(wmt.shape), _whole(bexp.shape), _whole(scw.shape),
                  _whole(ccw.shape), _whole((1, D_GROUP)), _whole((1, D_GROUP))],
        out_specs=[_rows(tb, D_IN)] + [_whole(t.shape) for t in small],
        out_shape=[_sds((s, D_IN), BF16)] + small,
        scratch_shapes=[pltpu.VMEM((HALO + tb, D_GROUP), F32), pltpu.VMEM((HALO + tb, D_GROUP), F32),
                        pltpu.VMEM((ext, D_GROUP), F32), pltpu.VMEM((ext, D_GROUP), F32),
                        pltpu.VMEM((tb, D_GROUP), F32), pltpu.VMEM((tb, D_GROUP), F32),
                        pltpu.VMEM((tb + TAP_SLACK, D_GROUP), F32)],
        compiler_params=_params(("arbitrary",), 48))


def _in_proj_bwd(x, dz, dres, g, wint, tb, ride=None):
    s = x.shape[0]

    def body(x_ref, dz_ref, dr_ref, g_ref, w_ref, dx_ref, dg_ref):
        @pl.when(pl.program_id(0) == 0)
        def _():
            dg_ref[...] = jnp.zeros_like(dg_ref)

        xv = x_ref[...]
        r = _rms(xv)
        xh = xv * r
        dh = _mm(dz_ref[...], w_ref[...])
        dg_ref[...] += jnp.sum(dh * xh, axis=0, keepdims=True)
        dx_ref[...] = dr_ref[...] + _rms_bwd(dh * g_ref[...], xh, r)

    return _call(
        body, (x, dz, dres, g, wint), ride, name="in_proj_bwd", grid=(s // tb,),
        in_specs=[_rows(tb, D_MODEL), _rows(tb, D_IN), _rows(tb, D_MODEL), _whole((1, D_MODEL)), _resident((D_IN, D_MODEL))],
        out_specs=[_rows(tb, D_MODEL), _whole((1, D_MODEL))],
        out_shape=[_sds((s, D_MODEL)), _sds((1, D_MODEL))],
        compiler_params=_params(("arbitrary",), 48))


def _adamw(w, g, m, v):
    m = ADAM_B1 * m + (1.0 - ADAM_B1) * g
    v = ADAM_B2 * v + (1.0 - ADAM_B2) * (g * g)
    m_hat = m / (1.0 - ADAM_B1 ** ADAM_STEP)
    v_hat = v / (1.0 - ADAM_B2 ** ADAM_STEP)
    delta = -ADAM_LR * (m_hat / (jnp.sqrt(v_hat) + ADAM_EPS) + ADAM_WD * w)
    return delta, m, v


def _reduce_adamw(parts, w, m, v, tb, name):
    rows, cols = w.shape

    def body(p_ref, w_ref, m_ref, v_ref, g_ref, d_ref, m2_ref, v2_ref):
        g = p_ref[0]
        for j in range(1, N_DEV):
            g = g + p_ref[j]
        g_ref[...] = g
        d_ref[...], m2_ref[...], v2_ref[...] = _adamw(w_ref[...], g, m_ref[...], v_ref[...])

    blk = _rows(tb, cols)
    return pl.pallas_call(
        body, name=name, grid=(rows // tb,),
        in_specs=[pl.BlockSpec((N_DEV, tb, cols), lambda i: (0, i, 0)), blk, blk, blk],
        out_specs=[blk] * 4, out_shape=[_sds((rows, cols))] * 4,
        compiler_params=_params(("parallel",), 32),
    )(parts, w, m, v)


LANES = 128
PACK_ALIGN = 8 * LANES


def _pack(arrays):
    pieces = []
    for a in arrays:
        flat = a.reshape(-1)
        pieces.append(jnp.pad(flat, (0, -flat.shape[0] % PACK_ALIGN)).reshape(-1, LANES))
    return jnp.concatenate(pieces, axis=0)


def _unpack(packed, shapes):
    out, row = [], 0
    for shape in shapes:
        size = 1
        for dim in shape:
            size *= dim
        rows = -(-size // PACK_ALIGN) * 8
        out.append(packed[row:row + rows].reshape(-1)[:size].reshape(shape))
        row += rows
    return out


TB_PROJ = 512
TB_MIX = 256
TB_FFN_BWD = 256
TB_TN = 1024
TQ = 256
TK = 256
TB_ADAM = 64


def kernel(x, norm_mix_g, w_in, gmlp_v_g, gmlp_w_s, gmlp_b_s, short_conv_w, conf_conv_w, conf_ln_g, conf_ln_b, mix_out_g, w_out, norm_ffn_g, w_up, w_down, final_norm_g, loss_target, m_norm_mix_g, m_w_in, m_gmlp_v_g, m_gmlp_w_s, m_gmlp_b_s, m_short_conv_w, m_conf_conv_w, m_conf_ln_g, m_conf_ln_b, m_mix_out_g, m_w_out, m_norm_ffn_g, m_w_up, m_w_down, m_final_norm_g, v_norm_mix_g, v_w_in, v_gmlp_v_g, v_gmlp_w_s, v_gmlp_b_s, v_short_conv_w, v_conf_conv_w, v_conf_ln_g, v_conf_ln_b, v_mix_out_g, v_w_out, v_norm_ffn_g, v_w_up, v_w_down, v_final_norm_g):
    me = 4 * lax.axis_index("x") + 2 * lax.axis_index("y") + lax.axis_index("c")
    x0, target = x[0], loss_target[0]
    s = x0.shape[0]
    tb_proj, tb_mix, tb_fb, tb_tn = min(TB_PROJ, s), min(TB_MIX, s), min(TB_FFN_BWD, s), min(TB_TN, s)
    conv_cols = D_GROUP // N_DEV

    def pad_rows(a, rows):
        return jnp.pad(a, ((0, rows - a.shape[0]), (0, 0)))

    wint_loc = [w_in[l].T.astype(BF16) for l in range(N_LAYERS)]
    wout_loc = [w_out[l].astype(BF16) for l in range(N_LAYERS)]
    wup_loc = [w_up[l].astype(BF16) for l in range(N_LAYERS)]
    wdn_loc = [w_down[l].astype(BF16) for l in range(N_LAYERS)]
    conv_loc = jnp.concatenate([pad_rows(short_conv_w[l], 8) for l in range(N_LAYERS)]
                               + [pad_rows(conf_conv_w[l], HALO) for l in range(N_LAYERS)], axis=0)
    wint, wout, wup, wdn = [None] * N_LAYERS, [None] * N_LAYERS, [None] * N_LAYERS, [None] * N_LAYERS
    wint0, conv_all = _exchange([wint_loc[0], conv_loc], [GATHER, GATHER], "gather_first_weights")
    wint[0] = wint0.reshape(D_IN, D_MODEL)
    conv_full = conv_all.transpose(1, 0, 2).reshape(-1, D_GROUP)
    scw = [conv_full[8 * l:8 * (l + 1)] for l in range(N_LAYERS)]
    ccw = [conv_full[8 * N_LAYERS + HALO * l:8 * N_LAYERS + HALO * (l + 1)] for l in range(N_LAYERS)]

    tril = jnp.tril(jnp.ones((CHUNK, CHUNK), dtype=bool))
    wm = [jnp.where(tril, gmlp_w_s[l], 0.0).astype(BF16) for l in range(N_LAYERS)]
    wmt = [w.transpose(0, 2, 1) for w in wm]
    bexp = [jnp.repeat(gmlp_b_s[l].T, HEAD_DIM, axis=1) for l in range(N_LAYERS)]

    def row(vec):
        return vec.reshape(1, -1)

    saved = []
    xc = x0
    for l in range(N_LAYERS):
        first = l == 0
        (z, qkv, hb_in), moved = _in_proj_fwd(xc, row(norm_mix_g[l]), wint[l], tb_proj,
                                              ride=([wout_loc[0]], [GATHER]) if first else None)
        if first:
            wout[0] = moved[0].reshape(D_MODEL, D_MODEL)
        (ya, yb, yd, conv), moved = _mix_fwd(z, row(gmlp_v_g[l]), wm[l], bexp[l], scw[l], ccw[l], row(conf_ln_g[l]),
                                             row(conf_ln_b[l]), tb_mix, ride=([wup_loc[0]], [GATHER]) if first else None)
        if first:
            wup[0] = moved[0]
        (yc,), moved = _attn_fwd(qkv, TQ, TK, ride=([wdn_loc[0], wint_loc[1]], [GATHER, GATHER]) if first else None)
        if first:
            wdn[0], wint[1] = moved[0], moved[1].reshape(D_IN, D_MODEL)
        ys = (ya, yb, yc, yd)
        x1 = _out_proj_fwd(ys, xc, row(mix_out_g[l]), wout[l], tb_proj)
        (x2, act), moved = _ffn_fwd(x1, row(norm_ffn_g[l]), wup[l], wdn[l], tb_proj,
                                    ride=([wout_loc[1], wup_loc[1], wdn_loc[1]], [GATHER] * 3) if first else None)
        saved.append((xc, z, qkv, ys, x1, act, hb_in, conv))
        xc = x2
        if first:
            wout[1], wup[1], wdn[1] = moved[0].reshape(D_MODEL, D_MODEL), moved[1], moved[2]
    dx, g_final, loss_part = _loss_head(xc, row(final_norm_g), target, tb_proj)
    loss = lax.psum(loss_part[0, 0], MESH_AXES)

    parts = [None] * (4 * N_LAYERS)
    small_grads = [None] * N_LAYERS
    for l in reversed(range(N_LAYERS)):
        xin, z, qkv, ys, x1, act, hb_in, conv = saved[l]
        (dx1, hb_ffn, dpre, sq, dyb, g_ffn), _ = _ffn_bwd(x1, act, dx, row(norm_ffn_g[l]), wup[l], wdn[l], tb_fb)
        grad_up = _tn_matmul(hb_ffn, dpre, N_DEV, False, True, tb_tn, "grad_w_up")
        grad_dn = _tn_matmul(sq, dyb, N_DEV, True, False, tb_tn, "grad_w_down")
        dya, dyb_mix, dyc, dyd, yn, g_mixout = _out_proj_bwd(dx1, ys, row(mix_out_g[l]), wout[l], tb_proj)
        grad_out = _tn_matmul(yn, dx1, 1, False, False, tb_tn, "grad_w_out").reshape(N_DEV, D_MODEL // N_DEV, D_MODEL)
        (dq, dk, dv), moved = _attn_bwd(qkv, dyc, ys[2], TQ, TK, ride=([grad_up, grad_dn], [SCATTER] * 2))
        parts[4 * l + 2], parts[4 * l + 3] = moved
        (dz, g_vg, g_ws, g_bs, g_scw, g_ccw, g_lng, g_lnb), moved = _mix_bwd(
            z, conv, dya, dyb_mix, dyd, dq, dk, dv, row(gmlp_v_g[l]), wm[l], wmt[l], bexp[l], scw[l], ccw[l],
            row(conf_ln_g[l]), row(conf_ln_b[l]), tb_mix, ride=([grad_out], [SCATTER]))
        parts[4 * l + 1] = moved[0]
        grad_in = _tn_matmul(dz, hb_in, 1, False, False, tb_tn, "grad_w_in").reshape(N_DEV, D_IN // N_DEV, D_MODEL)
        (dx, g_mix), moved = _in_proj_bwd(xin, dz, dx1, row(norm_mix_g[l]), wint[l], tb_proj, ride=([grad_in], [SCATTER]))
        parts[4 * l] = moved[0]
        small_grads[l] = dict(norm_mix_g=g_mix[0], gmlp_v_g=g_vg[0], gmlp_w_s=g_ws, gmlp_b_s=g_bs[:, :4].T,
                              short_conv_w=g_scw[:K_SHORT], conf_conv_w=g_ccw[:K_CONF], conf_ln_g=g_lng[0],
                              conf_ln_b=g_lnb[0], mix_out_g=g_mixout[0], norm_ffn_g=g_ffn[0])

    small_names = ["norm_mix_g", "gmlp_v_g", "gmlp_w_s", "gmlp_b_s", "short_conv_w", "conf_conv_w", "conf_ln_g",
                   "conf_ln_b", "mix_out_g", "norm_ffn_g"]
    small_list = [jnp.stack([small_grads[l][n] for l in range(N_LAYERS)]) for n in small_names] + [g_final[0]]
    small_shapes = [a.shape for a in small_list]
    small_parts = _exchange([_pack(small_list)], [GATHER], "gather_small_grads")[0]

    given = dict(norm_mix_g=(norm_mix_g, m_norm_mix_g, v_norm_mix_g), gmlp_v_g=(gmlp_v_g, m_gmlp_v_g, v_gmlp_v_g),
                 gmlp_w_s=(gmlp_w_s, m_gmlp_w_s, v_gmlp_w_s), gmlp_b_s=(gmlp_b_s, m_gmlp_b_s, v_gmlp_b_s),
                 short_conv_w=(short_conv_w, m_short_conv_w, v_short_conv_w),
                 conf_conv_w=(conf_conv_w, m_conf_conv_w, v_conf_conv_w),
                 conf_ln_g=(conf_ln_g, m_conf_ln_g, v_conf_ln_g), conf_ln_b=(conf_ln_b, m_conf_ln_b, v_conf_ln_b),
                 mix_out_g=(mix_out_g, m_mix_out_g, v_mix_out_g), norm_ffn_g=(norm_ffn_g, m_norm_ffn_g, v_norm_ffn_g),
                 final_norm_g=(final_norm_g, m_final_norm_g, v_final_norm_g))
    sharded_small = ("short_conv_w", "conf_conv_w")

    def widen(a):
        full = jnp.zeros(a.shape[:-1] + (D_GROUP,), a.dtype)
        return lax.dynamic_update_slice(full, a, (0, 0, me * conv_cols))

    packed_state = []
    for k in range(3):
        packed_state.append(_pack([widen(given[n][k]) if n in sharded_small else given[n][k]
                                   for n in small_names + ["final_norm_g"]]))
    small_out = _reduce_adamw(small_parts, *packed_state, packed_state[0].shape[0], "adamw_small")
    small_res = {}
    for kind, packed in zip(("grad", "delta", "new_m", "new_v"), small_out):
        for n, val in zip(small_names + ["final_norm_g"], _unpack(packed, small_shapes)):
            if n in sharded_small:
                val = lax.dynamic_slice(val, (0, 0, me * conv_cols), val.shape[:-1] + (conv_cols,))
            small_res[kind, n] = val

    big_names = ["w_in", "w_out", "w_up", "w_down"]
    big_given = dict(w_in=(w_in, m_w_in, v_w_in), w_out=(w_out, m_w_out, v_w_out), w_up=(w_up, m_w_up, v_w_up),
                     w_down=(w_down, m_w_down, v_w_down))
    big_res = {}
    for j, n in enumerate(big_names):
        per_layer = []
        for l in range(N_LAYERS):
            state = [t[l].T if n == "w_in" else t[l] for t in big_given[n]]
            outs = _reduce_adamw(parts[4 * l + j], *state, TB_ADAM, "adamw_" + n)
            per_layer.append([o.T if n == "w_in" else o for o in outs])
        for k, kind in enumerate(("grad", "delta", "new_m", "new_v")):
            big_res[kind, n] = jnp.stack([per_layer[l][k] for l in range(N_LAYERS)])

    order = ["norm_mix_g", "w_in", "gmlp_v_g", "gmlp_w_s", "gmlp_b_s", "short_conv_w", "conf_conv_w", "conf_ln_g",
             "conf_ln_b", "mix_out_g", "w_out", "norm_ffn_g", "w_up", "w_down", "final_norm_g"]
    result = [loss, dx.reshape(x.shape)]
    for kind in ("grad", "delta", "new_m", "new_v"):
        for n in order:
            result.append(big_res[kind, n] if n in big_given else small_res[kind, n])
    return tuple(result)
```

```python
import jax
import jax.numpy as jnp
from jax import lax
from jax.experimental import pallas as pl
from jax.experimental.pallas import tpu as pltpu

F32 = jnp.float32
BF16 = jnp.bfloat16

D_MODEL = 1024
D_GROUP = 256
D_IN = 10 * D_GROUP
D_FF = 4 * D_MODEL
N_DEV = 8
N_LAYERS = 2
HEAD_DIM = 64
HEADS_PER_PAIR = 2
PAIR = HEADS_PER_PAIR * HEAD_DIM
CHUNK = 128
K_SHORT = 3
K_CONF = 31
HALO = 32
EPS = 1e-6
ATT_SCALE = HEAD_DIM ** -0.5
LOG_CUT = -104.0
MIB = 2 ** 20

ADAM_LR = 0.001
ADAM_B1 = 0.9
ADAM_B2 = 0.999
ADAM_EPS = 1e-08
ADAM_WD = 0.01
ADAM_STEP = 10

MESH_AXES = ("x", "y", "c")
GELU_C = 0.7978845608028654
GELU_A = 0.044715


def _mm(a, b):
    return jnp.dot(a, b, preferred_element_type=F32)


def _mm_nt(a, b):
    return lax.dot_general(a, b, (((1,), (1,)), ((), ())), preferred_element_type=F32)


def _mm_tn(a, b):
    return lax.dot_general(a, b, (((0,), (0,)), ((), ())), preferred_element_type=F32)


def _rms(x):
    return lax.rsqrt(jnp.mean(x * x, axis=-1, keepdims=True) + EPS)


def _rms_bwd(dy, xh, r):
    return r * (dy - xh * jnp.mean(dy * xh, axis=-1, keepdims=True))


def _sigmoid(x):
    return 1.0 / (1.0 + jnp.exp(-x))


def _whole(shape):
    return pl.BlockSpec(shape, lambda *_: (0,) * len(shape))


def _resident(shape):
    return pl.BlockSpec(shape, lambda *_: (0,) * len(shape), pipeline_mode=pl.Buffered(1))


def _rows(tb, width, col=0):
    return pl.BlockSpec((tb, width), lambda i: (i, col))


def _params(semantics, vmem_mib):
    return pltpu.CompilerParams(dimension_semantics=semantics, vmem_limit_bytes=vmem_mib * MIB)


def _sds(shape, dtype=F32):
    return jax.ShapeDtypeStruct(shape, dtype)


def _split_bf16(v):
    hi = v.astype(BF16)
    lo = (v - hi.astype(F32)).astype(BF16)
    return hi, lo


GATHER, SCATTER = "gather", "scatter"
ANY_SPEC = pl.BlockSpec(memory_space=pl.ANY)


def _exchange_copies(ins, outs, modes, send_sems, recv_sems, local_sems, with_arrivals=True):
    x, y, c = lax.axis_index("x"), lax.axis_index("y"), lax.axis_index("c")
    me = 4 * x + 2 * y + c
    local, sends, arrivals = [], [], []
    for a, mode in enumerate(modes):
        local.append(pltpu.make_async_copy(ins[a].at[me] if mode == SCATTER else ins[a], outs[a].at[me], local_sems.at[a]))
    for k in range(N_DEV - 1):
        flip = k + 1
        peer = (1 - x if flip & 4 else x, 1 - y if flip & 2 else y, 1 - c if flip & 1 else c)
        pf = 4 * peer[0] + 2 * peer[1] + peer[2]
        for a, mode in enumerate(modes):
            src = ins[a].at[pf] if mode == SCATTER else ins[a]
            for dst, group in ((outs[a].at[me], sends), (outs[a].at[pf], arrivals)):
                if group is sends or with_arrivals:
                    group.append(pltpu.make_async_remote_copy(
                        src_ref=src, dst_ref=dst, send_sem=send_sems.at[a, k], recv_sem=recv_sems.at[a, k],
                        device_id=peer, device_id_type=pl.DeviceIdType.MESH))
    return local, sends, arrivals


def _exchange_start(*refs_and_modes):
    local, sends, _ = _exchange_copies(*refs_and_modes, with_arrivals=False)
    for cp in local + sends:
        cp.start()


def _exchange_wait(*refs_and_modes):
    local, sends, arrivals = _exchange_copies(*refs_and_modes)
    for cp in sends:
        cp.wait_send()
    for cp in arrivals:
        cp.wait_recv()
    for cp in local:
        cp.wait()


def _exchange_shapes(arrays, modes):
    out_shape = [_sds(a.shape if mode == SCATTER else (N_DEV,) + a.shape, a.dtype) for a, mode in zip(arrays, modes)]
    n = len(arrays)
    sems = [pltpu.SemaphoreType.DMA((n, N_DEV - 1)), pltpu.SemaphoreType.DMA((n, N_DEV - 1)), pltpu.SemaphoreType.DMA((n,))]
    return out_shape, sems


def _exchange(arrays, modes, name):
    n = len(arrays)
    out_shape, sems = _exchange_shapes(arrays, modes)

    def body(*refs):
        _exchange_start(refs[:n], refs[n:2 * n], modes, *refs[2 * n:])
        _exchange_wait(refs[:n], refs[n:2 * n], modes, *refs[2 * n:])

    return pl.pallas_call(body, name=name, out_shape=out_shape, in_specs=[ANY_SPEC] * n, out_specs=[ANY_SPEC] * n,
                          scratch_shapes=sems)(*arrays)


def _call(body, args, ride, *, name, grid, in_specs, out_specs, out_shape, scratch_shapes=(), compiler_params):
    if ride is None:
        outs = pl.pallas_call(body, name=name, grid=grid, in_specs=in_specs, out_specs=out_specs, out_shape=out_shape,
                              scratch_shapes=scratch_shapes, compiler_params=compiler_params)(*args)
        return outs, []
    arrays, modes = ride
    n, n_in, n_out, n_scratch = len(arrays), len(in_specs), len(out_specs), len(scratch_shapes)
    moved_shape, sems = _exchange_shapes(arrays, modes)
    n_steps = 1
    for g in grid:
        n_steps *= g

    def riding(*refs):
        ins, refs = refs[:n_in], refs[n_in:]
        r_ins, refs = refs[:n], refs[n:]
        outs, refs = refs[:n_out], refs[n_out:]
        r_outs, refs = refs[:n], refs[n:]
        scratch, r_sems = refs[:n_scratch], refs[n_scratch:]
        step = pl.program_id(0)
        for axis in range(1, len(grid)):
            step = step * grid[axis] + pl.program_id(axis)

        @pl.when(step == 0)
        def _():
            _exchange_start(r_ins, r_outs, modes, *r_sems)

        body(*ins, *outs, *scratch)

        @pl.when(step == n_steps - 1)
        def _():
            _exchange_wait(r_ins, r_outs, modes, *r_sems)

    outs = pl.pallas_call(
        riding, name=name, grid=grid, in_specs=list(in_specs) + [ANY_SPEC] * n,
        out_specs=list(out_specs) + [ANY_SPEC] * n, out_shape=list(out_shape) + moved_shape,
        scratch_shapes=list(scratch_shapes) + sems, compiler_params=compiler_params)(*args, *arrays)
    return outs[:n_out], outs[n_out:]


def _in_proj_fwd(x, g, wint, tb, ride=None):
    s = x.shape[0]

    def body(x_ref, g_ref, w_ref, z_ref, qkv_ref, hb_ref):
        xv = x_ref[...]
        h = (xv * _rms(xv) * g_ref[...]).astype(BF16)
        hb_ref[...] = h
        z = _mm_nt(h, w_ref[...])
        z_ref[...] = z
        qkv_ref[:, 0:D_GROUP] = (z[:, 5 * D_GROUP:6 * D_GROUP] * ATT_SCALE).astype(BF16)
        qkv_ref[:, D_GROUP:3 * D_GROUP] = z[:, 6 * D_GROUP:8 * D_GROUP].astype(BF16)

    return _call(
        body, (x, g, wint), ride, name="in_proj_fwd", grid=(s // tb,),
        in_specs=[_rows(tb, D_MODEL), _whole((1, D_MODEL)), _resident((D_IN, D_MODEL))],
        out_specs=[_rows(tb, D_IN), _rows(tb, 3 * D_GROUP), _rows(tb, D_MODEL)],
        out_shape=[_sds((s, D_IN)), _sds((s, 3 * D_GROUP), BF16), _sds((s, D_MODEL), BF16)],
        compiler_params=_params(("arbitrary",), 48))


def _gelu(x):
    return 0.5 * x * (1.0 + jnp.tanh(GELU_C * (x + GELU_A * x * x * x)))


def _gelu_grad(x):
    t = jnp.tanh(GELU_C * (x + GELU_A * x * x * x))
    return 0.5 * (1.0 + t) + 0.5 * x * (1.0 - t * t) * GELU_C * (1.0 + 3.0 * GELU_A * x * x)


def _head_lane(width):
    return lax.broadcasted_iota(jnp.int32, (1, width), 1) // HEAD_DIM


def _gating_chunk(wm_ref, bexp_ref, vc, lane_h):
    f = bexp_ref[...]
    for h in range(D_GROUP // HEAD_DIM):
        f = f + _mm(wm_ref[h], jnp.where(lane_h == h, vc, 0))
    return f


def _halo_specs(s, tb, width_blocks):
    per = tb // HALO
    prev = pl.BlockSpec((HALO, width_blocks), lambda i: (jnp.maximum(i * per - 1, 0), 0))
    nxt = pl.BlockSpec((HALO, width_blocks), lambda i: (jnp.minimum((i + 1) * per, s // HALO - 1), 0))
    return prev, nxt


SUBLANES = 8
TAP_SLACK = 24


def _taps(buf_ref, shifted_ref, first, n_taps, rows, visit):
    for residue in range(SUBLANES):
        taps = [j for j in range(n_taps) if (first + j) % SUBLANES == residue]
        if not taps:
            continue
        lo = first + taps[0]
        span = first + taps[-1] - lo + rows
        shifted_ref[0:span, :] = buf_ref[pl.ds(lo, span), :]
        for j in taps:
            visit(j, shifted_ref[pl.ds(first + j - lo, rows), :])


def _mix_fwd(z, vg, wm, bexp, scw, ccw, lng, lnb, tb, ride=None):
    s = z.shape[0]
    prev_spec, _ = _halo_specs(s, tb, D_IN)

    def body(z_ref, zp_ref, vg_ref, wm_ref, bexp_ref, scw_ref, ccw_ref, lng_ref, lnb_ref,
             ya_ref, yb_ref, yd_ref, c_ref, pbuf, hbuf, shifted):
        keep = (pl.program_id(0) > 0).astype(F32)
        lane_h = _head_lane(D_GROUP)
        ga = _gelu(z_ref[:, 0:2 * D_GROUP])
        u, v = ga[:, :D_GROUP], ga[:, D_GROUP:]
        vn = (v * _rms(v) * vg_ref[...]).astype(BF16)
        for n in range(tb // CHUNK):
            rows = slice(n * CHUNK, (n + 1) * CHUNK)
            ya_ref[rows, :] = u[rows] * _gating_chunk(wm_ref, bexp_ref, vn[rows], lane_h)
        p = z_ref[:, 3 * D_GROUP:4 * D_GROUP] * z_ref[:, 4 * D_GROUP:5 * D_GROUP]
        pbuf[0:HALO, :] = zp_ref[:, 3 * D_GROUP:4 * D_GROUP] * zp_ref[:, 4 * D_GROUP:5 * D_GROUP] * keep
        pbuf[HALO:HALO + tb, :] = p
        cv = scw_ref[K_SHORT - 1:K_SHORT, :] * p
        for k in range(K_SHORT - 1):
            cv = cv + scw_ref[k:k + 1, :] * pbuf[pl.ds(HALO - (K_SHORT - 1) + k, tb), :]
        yb_ref[...] = z_ref[:, 2 * D_GROUP:3 * D_GROUP] * cv
        hbuf[0:HALO, :] = zp_ref[:, 8 * D_GROUP:9 * D_GROUP] * _sigmoid(zp_ref[:, 9 * D_GROUP:10 * D_GROUP]) * keep
        hbuf[HALO:HALO + tb, :] = z_ref[:, 8 * D_GROUP:9 * D_GROUP] * _sigmoid(z_ref[:, 9 * D_GROUP:10 * D_GROUP])
        conv = [jnp.zeros((tb, D_GROUP), F32)]

        def tap(k, window):
            conv[0] = conv[0] + ccw_ref[k:k + 1, :] * window

        _taps(hbuf, shifted, HALO - (K_CONF - 1), K_CONF, tb, tap)
        c = conv[0]
        c_ref[...] = c
        xc = c - jnp.mean(c, axis=-1, keepdims=True)
        ln = xc * lax.rsqrt(jnp.mean(xc * xc, axis=-1, keepdims=True) + EPS) * lng_ref[...] + lnb_ref[...]
        yd_ref[...] = ln * _sigmoid(ln)

    grp = _rows(tb, D_GROUP)
    return _call(
        body, (z, z, vg, wm, bexp, scw, ccw, lng, lnb), ride, name="mix_fwd", grid=(s // tb,),
        in_specs=[_rows(tb, D_IN), prev_spec, _whole((1, D_GROUP)), _whole(wm.shape), _whole(bexp.shape),
                  _whole(scw.shape), _whole(ccw.shape), _whole((1, D_GROUP)), _whole((1, D_GROUP))],
        out_specs=[grp, grp, grp, grp],
        out_shape=[_sds((s, D_GROUP))] * 4,
        scratch_shapes=[pltpu.VMEM((HALO + tb, D_GROUP), F32), pltpu.VMEM((HALO + tb, D_GROUP), F32),
                        pltpu.VMEM((tb + TAP_SLACK, D_GROUP), F32)],
        compiler_params=_params(("arbitrary",), 40))


def _stick_tile(qh, kt, causal, c, upper):
    x = _mm_nt(qh, kt)
    soft = jnp.log(1.0 + jnp.exp(-jnp.abs(x)))
    lb = jnp.minimum(x, 0.0) - soft
    lom = jnp.where(causal, -jnp.maximum(x, 0.0) - soft, 0.0)
    hi, lo = _split_bf16(lom)
    stick = c + _mm(hi, upper[...]) + _mm(lo, upper[...])
    w = jnp.where(causal, jnp.exp(lb + stick), 0.0)
    return w, lb, lom


def _triangle(n, diagonal):
    return jnp.tri(n, n, diagonal, dtype=BF16)


def _causal_tile(qi, tq, k0, tk):
    qpos = qi * tq + lax.broadcasted_iota(jnp.int32, (tq, 1), 0)
    return k0 + lax.broadcasted_iota(jnp.int32, (1, tk), 1) < qpos


def _sticks_alive(cs):
    longest = cs[0]
    for c in cs[1:]:
        longest = jnp.maximum(longest, c)
    return (jnp.max(longest) > LOG_CUT).astype(jnp.int32)


def _walk(body, qi, tq, tk, init):
    start = (((qi + 1) * tq - 1) // tk, jnp.int32(1)) + tuple(init)
    return lax.while_loop(lambda cr: jnp.logical_and(cr[0] >= 0, cr[1] > 0), body, start)[2:]


def _attn_fwd(qkv, tq, tk, ride=None):
    s = qkv.shape[0]
    n_heads = D_GROUP // HEAD_DIM

    def body(q_ref, k_ref, v_ref, upper, o_ref):
        qi = pl.program_id(0)
        q = q_ref[...]
        lane_h = _head_lane(D_GROUP)
        qhs = [jnp.where(lane_h == h, q, 0) for h in range(n_heads)]

        def step(carry):
            kb, _, acc = carry[:3]
            cs = list(carry[3:])
            k0 = pl.multiple_of(kb * tk, tk)
            kt = k_ref[pl.ds(k0, tk), :]
            vt = v_ref[pl.ds(k0, tk), :]
            causal = _causal_tile(qi, tq, k0, tk)
            for h in range(n_heads):
                w, _, lom = _stick_tile(qhs[h], kt, causal, cs[h], upper)
                acc = acc + _mm(w.astype(BF16), jnp.where(lane_h == h, vt, 0))
                cs[h] = cs[h] + jnp.sum(lom, axis=1, keepdims=True)
            return (kb - 1, _sticks_alive(cs), acc) + tuple(cs)

        init = [jnp.zeros((tq, D_GROUP), F32)] + [jnp.zeros((tq, 1), F32)] * n_heads
        o_ref[...] = _walk(step, qi, tq, tk, init)[0]

    return _call(
        body, (qkv, qkv, qkv, _triangle(tk, -1)), ride, name="attn_fwd", grid=(s // tq,),
        in_specs=[pl.BlockSpec((tq, D_GROUP), lambda qi: (qi, 0)),
                  pl.BlockSpec((s, D_GROUP), lambda qi: (0, 1), pipeline_mode=pl.Buffered(1)),
                  pl.BlockSpec((s, D_GROUP), lambda qi: (0, 2), pipeline_mode=pl.Buffered(1)),
                  _resident((tk, tk))],
        out_specs=[pl.BlockSpec((tq, D_GROUP), lambda qi: (qi, 0))],
        out_shape=[_sds((s, D_GROUP))],
        compiler_params=_params(("arbitrary",), 40))


def _out_proj_fwd(ys, x, mg, wout, tb):
    s = x.shape[0]

    def body(ya_ref, yb_ref, yc_ref, yd_ref, x_ref, mg_ref, w_ref, o_ref):
        acc = x_ref[...]
        for gi, y_ref in enumerate((ya_ref, yb_ref, yc_ref, yd_ref)):
            cols = slice(gi * D_GROUP, (gi + 1) * D_GROUP)
            y = y_ref[...]
            acc = acc + _mm((y * _rms(y) * mg_ref[:, cols]).astype(BF16), w_ref[cols, :])
        o_ref[...] = acc

    grp = _rows(tb, D_GROUP)
    return pl.pallas_call(
        body, name="out_proj_fwd", grid=(s // tb,),
        in_specs=[grp, grp, grp, grp, _rows(tb, D_MODEL), _whole((1, D_MODEL)), _whole((D_MODEL, D_MODEL))],
        out_specs=_rows(tb, D_MODEL), out_shape=_sds((s, D_MODEL)),
        compiler_params=_params(("parallel",), 32),
    )(*ys, x, mg, wout)


def _ffn_fwd(x, g, wup, wdn, tb, ride=None):
    s = x.shape[0]
    ff = D_FF // N_DEV

    def body(x_ref, g_ref, wu_ref, wd_ref, o_ref, a_ref):
        xv = x_ref[...]
        h = (xv * _rms(xv) * g_ref[...]).astype(BF16)
        acc = xv
        for d in range(N_DEV):
            a = jnp.maximum(_mm(h, wu_ref[d]), 0.0)
            a_ref[:, d * ff:(d + 1) * ff] = a
            acc = acc + _mm((a * a).astype(BF16), wd_ref[d])
        o_ref[...] = acc

    return _call(
        body, (x, g, wup, wdn), ride, name="ffn_fwd", grid=(s // tb,),
        in_specs=[_rows(tb, D_MODEL), _whole((1, D_MODEL)), _resident((N_DEV, D_MODEL, ff)), _resident((N_DEV, ff, D_MODEL))],
        out_specs=[_rows(tb, D_MODEL), _rows(tb, D_FF)], out_shape=[_sds((s, D_MODEL)), _sds((s, D_FF))],
        compiler_params=_params(("arbitrary",), 56))


def _loss_head(x, g, tgt, tb):
    s = x.shape[0]

    def body(x_ref, g_ref, t_ref, dx_ref, dg_ref, loss_ref):
        @pl.when(pl.program_id(0) == 0)
        def _():
            dg_ref[...] = jnp.zeros_like(dg_ref)
            loss_ref[...] = jnp.zeros_like(loss_ref)

        xv = x_ref[...]
        r = _rms(xv)
        xh = xv * r
        err = xh * g_ref[...] - t_ref[...]
        loss_ref[...] += 0.5 * jnp.sum(jnp.mean(err * err, axis=-1, keepdims=True))
        dy = err * (1.0 / D_MODEL)
        dg_ref[...] += jnp.sum(dy * xh, axis=0, keepdims=True)
        dx_ref[...] = _rms_bwd(dy * g_ref[...], xh, r)

    return pl.pallas_call(
        body, name="loss_head", grid=(s // tb,),
        in_specs=[_rows(tb, D_MODEL), _whole((1, D_MODEL)), _rows(tb, D_MODEL)],
        out_specs=[_rows(tb, D_MODEL), _whole((1, D_MODEL)), _whole((8, 128))],
        out_shape=[_sds((s, D_MODEL)), _sds((1, D_MODEL)), _sds((8, 128))],
        compiler_params=_params(("arbitrary",), 32),
    )(x, g, tgt)


def _ffn_bwd(x1, act, dx2, g, wup, wdn, tb, ride=None):
    s = x1.shape[0]
    ff = D_FF // N_DEV

    def body(x_ref, a_ref, dy_ref, g_ref, wu_ref, wd_ref, dx_ref, hb_ref, dpre_ref, sq_ref, dyb_ref, dg_ref):
        @pl.when(pl.program_id(0) == 0)
        def _():
            dg_ref[...] = jnp.zeros_like(dg_ref)

        xv = x_ref[...]
        r = _rms(xv)
        xh = xv * r
        hb_ref[...] = (xh * g_ref[...]).astype(BF16)
        dyv = dy_ref[...]
        dyb = dyv.astype(BF16)
        dyb_ref[...] = dyb
        dh = jnp.zeros((tb, D_MODEL), F32)
        for d in range(N_DEV):
            cols = slice(d * ff, (d + 1) * ff)
            a = a_ref[:, cols]
            sq_ref[:, cols] = (a * a).astype(BF16)
            dpre = (_mm_nt(dyb, wd_ref[d]) * (2.0 * a)).astype(BF16)
            dpre_ref[:, cols] = dpre
            dh = dh + _mm_nt(dpre, wu_ref[d])
        dg_ref[...] += jnp.sum(dh * xh, axis=0, keepdims=True)
        dx_ref[...] = dyv + _rms_bwd(dh * g_ref[...], xh, r)

    return _call(
        body, (x1, act, dx2, g, wup, wdn), ride, name="ffn_bwd", grid=(s // tb,),
        in_specs=[_rows(tb, D_MODEL), _rows(tb, D_FF), _rows(tb, D_MODEL), _whole((1, D_MODEL)),
                  _resident((N_DEV, D_MODEL, ff)), _resident((N_DEV, ff, D_MODEL))],
        out_specs=[_rows(tb, D_MODEL), _rows(tb, D_MODEL), _rows(tb, D_FF), _rows(tb, D_FF), _rows(tb, D_MODEL),
                   _whole((1, D_MODEL))],
        out_shape=[_sds((s, D_MODEL)), _sds((s, D_MODEL), BF16), _sds((s, D_FF), BF16), _sds((s, D_FF), BF16),
                   _sds((s, D_MODEL), BF16), _sds((1, D_MODEL))],
        compiler_params=_params(("arbitrary",), 60))


def _tn_matmul(a, b, tb, name):
    s, ka = a.shape
    nb = b.shape[1]

    def body(a_ref, b_ref, o_ref):
        @pl.when(pl.program_id(0) == 0)
        def _():
            o_ref[...] = jnp.zeros_like(o_ref)

        o_ref[...] += _mm_tn(a_ref[...].astype(BF16), b_ref[...].astype(BF16))

    return pl.pallas_call(
        body, name=name, grid=(s // tb,),
        in_specs=[_rows(tb, ka), _rows(tb, nb)], out_specs=_whole((ka, nb)), out_shape=_sds((ka, nb)),
        compiler_params=_params(("arbitrary",), 56),
    )(a, b)


def _tn_slabs(a, b, tb, transpose_slabs, name):
    s, m = a.shape
    width = b.shape[1] // N_DEV
    n_steps = s // tb
    slab = (width, m) if transpose_slabs else (m, width)

    def body(a_ref, b_ref, o_hbm, acc, stage, sem):
        step = pl.program_id(0)

        @pl.when(step == 0)
        def _():
            acc[...] = jnp.zeros_like(acc)

        acc[...] += _mm_tn(a_ref[...].astype(BF16), b_ref[...].astype(BF16))

        @pl.when(step == n_steps - 1)
        def _():
            for d in range(N_DEV):
                cols = acc.at[:, pl.ds(d * width, width)]
                if transpose_slabs:
                    stage[...] = cols[...].T
                cp = pltpu.make_async_copy(stage if transpose_slabs else cols, o_hbm.at[d], sem.at[0])
                cp.start()
                cp.wait()

    return pl.pallas_call(
        body, name=name, grid=(n_steps,),
        in_specs=[_rows(tb, m), _rows(tb, b.shape[1])], out_specs=ANY_SPEC, out_shape=_sds((N_DEV,) + slab),
        scratch_shapes=[pltpu.VMEM((m, b.shape[1]), F32), pltpu.VMEM(slab, F32), pltpu.SemaphoreType.DMA((1,))],
        compiler_params=_params(("arbitrary",), 56),
    )(a, b)


def _out_proj_bwd(dx1, ys, mg, wout, tb):
    s = dx1.shape[0]

    def body(dx_ref, ya_ref, yb_ref, yc_ref, yd_ref, mg_ref, w_ref,
             dya_ref, dyb_ref, dyc_ref, dyd_ref, yn_ref, dmg_ref):
        @pl.when(pl.program_id(0) == 0)
        def _():
            dmg_ref[...] = jnp.zeros_like(dmg_ref)

        dyn = _mm_nt(dx_ref[...].astype(BF16), w_ref[...])
        groups = ((ya_ref, dya_ref), (yb_ref, dyb_ref), (yc_ref, dyc_ref), (yd_ref, dyd_ref))
        for gi, (y_ref, dy_ref) in enumerate(groups):
            cols = slice(gi * D_GROUP, (gi + 1) * D_GROUP)
            y = y_ref[...]
            r = _rms(y)
            n = y * r
            gain = mg_ref[:, cols]
            dn = dyn[:, cols]
            yn_ref[:, cols] = (n * gain).astype(BF16)
            dmg_ref[:, cols] += jnp.sum(dn * n, axis=0, keepdims=True)
            dy_ref[...] = _rms_bwd(dn * gain, n, r)

    grp = _rows(tb, D_GROUP)
    return pl.pallas_call(
        body, name="out_proj_bwd", grid=(s // tb,),
        in_specs=[_rows(tb, D_MODEL), grp, grp, grp, grp, _whole((1, D_MODEL)), _whole((D_MODEL, D_MODEL))],
        out_specs=[grp, grp, grp, grp, _rows(tb, D_MODEL), _whole((1, D_MODEL))],
        out_shape=[_sds((s, D_GROUP))] * 4 + [_sds((s, D_MODEL), BF16), _sds((1, D_MODEL))],
        compiler_params=_params(("arbitrary",), 32),
    )(dx1, *ys, mg, wout)


def _attn_bwd(qkv, do, o, tq, tk, ride=None):
    s = qkv.shape[0]
    nq = s // tq
    n_pairs = D_GROUP // PAIR

    def body(q_ref, k_ref, v_ref, do_ref, o_ref, upper, upper_eq, dq_ref, dk_hbm, dv_hbm, dk_acc, dv_acc, sems):
        hp, qi = pl.program_id(0), pl.program_id(1)

        @pl.when(qi == 0)
        def _():
            dk_acc[...] = jnp.zeros_like(dk_acc)
            dv_acc[...] = jnp.zeros_like(dv_acc)

        q = q_ref[...]
        dob = do_ref[...].astype(BF16)
        prod = dob.astype(F32) * o_ref[...]
        lane_h = _head_lane(PAIR)
        heads = []
        for h in range(HEADS_PER_PAIR):
            in_head = lane_h == h
            total = jnp.sum(jnp.where(in_head, prod, 0.0), axis=1, keepdims=True)
            heads.append((in_head, jnp.where(in_head, q, 0), jnp.where(in_head, dob, 0), total))

        def step(carry):
            kb, _, acc = carry[:3]
            cs = list(carry[3:3 + HEADS_PER_PAIR])
            nears = list(carry[3 + HEADS_PER_PAIR:])
            k0 = pl.multiple_of(kb * tk, tk)
            kt = k_ref[pl.ds(k0, tk), :]
            vt = v_ref[pl.ds(k0, tk), :]
            causal = _causal_tile(qi, tq, k0, tk)
            dk_t = jnp.zeros((tk, PAIR), F32)
            dv_t = jnp.zeros((tk, PAIR), F32)
            for h, (in_head, qh, doh, total) in enumerate(heads):
                w, lb, lom = _stick_tile(qh, kt, causal, cs[h], upper)
                wb = w.astype(BF16)
                gw = _mm_nt(doh, vt) * wb.astype(F32)
                hi, lo = _split_bf16(gw)
                far = total - nears[h] - _mm(hi, upper_eq[...]) - _mm(lo, upper_eq[...])
                beta = jnp.exp(lb)
                dxb = jnp.where(causal, gw - beta * (gw + far), 0.0).astype(BF16)
                acc = acc + _mm(dxb, jnp.where(in_head, kt, 0))
                dk_t = dk_t + _mm_tn(dxb, qh)
                dv_t = dv_t + _mm_tn(wb, doh)
                cs[h] = cs[h] + jnp.sum(lom, axis=1, keepdims=True)
                nears[h] = nears[h] + jnp.sum(gw, axis=1, keepdims=True)
            dk_acc[pl.ds(k0, tk), :] += dk_t
            dv_acc[pl.ds(k0, tk), :] += dv_t
            return (kb - 1, _sticks_alive(cs), acc) + tuple(cs) + tuple(nears)

        init = [jnp.zeros((tq, PAIR), F32)] + [jnp.zeros((tq, 1), F32)] * (2 * HEADS_PER_PAIR)
        dq_ref[...] = _walk(step, qi, tq, tk, init)[0]

        @pl.when(qi == nq - 1)
        def _():
            ck = pltpu.make_async_copy(dk_acc, dk_hbm.at[hp], sems.at[0])
            cv = pltpu.make_async_copy(dv_acc, dv_hbm.at[hp], sems.at[1])
            ck.start()
            cv.start()
            ck.wait()
            cv.wait()

    blk = pl.BlockSpec((tq, PAIR), lambda hp, qi: (qi, hp))
    return _call(
        body, (qkv, qkv, qkv, do, o, _triangle(tk, -1), _triangle(tk, 0)), ride, name="attn_bwd", grid=(n_pairs, nq),
        in_specs=[blk, pl.BlockSpec((s, PAIR), lambda hp, qi: (0, 2 + hp)),
                  pl.BlockSpec((s, PAIR), lambda hp, qi: (0, 4 + hp)), blk, blk, _resident((tk, tk)), _resident((tk, tk))],
        out_specs=[blk, ANY_SPEC, ANY_SPEC],
        out_shape=[_sds((s, D_GROUP)), _sds((n_pairs, s, PAIR)), _sds((n_pairs, s, PAIR))],
        scratch_shapes=[pltpu.VMEM((s, PAIR), F32), pltpu.VMEM((s, PAIR), F32), pltpu.SemaphoreType.DMA((2,))],
        compiler_params=_params(("arbitrary", "arbitrary"), 56))


def _mix_bwd(z, conv, dya, dyb, dyd, dq, dk, dv, vg, wm, wmt, bexp, scw, ccw, lng, lnb, tb, ride=None):
    s = z.shape[0]
    n_steps = s // tb
    prev_spec, next_spec = _halo_specs(s, tb, D_IN)
    _, next_grp = _halo_specs(s, tb, D_GROUP)
    ext = tb + HALO

    def body(z_ref, zp_ref, zn_ref, dya_ref, dyb_ref, dybn_ref, dyd_ref, dydn_ref, c_ref, cn_ref, dq_ref, dk0_ref, dk1_ref,
             dv0_ref, dv1_ref, vg_ref, wm_ref, wmt_ref, bexp_ref, scw_ref, ccw_ref, lng_ref, lnb_ref,
             dz_ref, dvg_ref, dws_ref, dbs_ref, dscw_ref, dccw_ref, dlng_ref, dlnb_ref,
             pbuf, hbuf, gbuf, cbuf, dubuf, dvnbuf, shifted):
        i = pl.program_id(0)

        @pl.when(i == 0)
        def _():
            for ref in (dvg_ref, dws_ref, dbs_ref, dscw_ref, dccw_ref, dlng_ref, dlnb_ref):
                ref[...] = jnp.zeros_like(ref)

        keep_prev = (i > 0).astype(F32)
        keep_next = (i < n_steps - 1).astype(F32)
        lane_h = _head_lane(D_GROUP)

        za = z_ref[:, 0:2 * D_GROUP]
        ga = _gelu(za)
        u, v = ga[:, :D_GROUP], ga[:, D_GROUP:]
        r = _rms(v)
        vh = v * r
        vn = (vh * vg_ref[...]).astype(BF16)
        tril = lax.broadcasted_iota(jnp.int32, (CHUNK, CHUNK), 0) >= lax.broadcasted_iota(jnp.int32, (CHUNK, CHUNK), 1)
        dbias = jnp.zeros((CHUNK, D_GROUP), F32)
        for n in range(tb // CHUNK):
            rows = slice(n * CHUNK, (n + 1) * CHUNK)
            vc = vn[rows]
            dy = dya_ref[rows, :]
            dubuf[rows, :] = dy * _gating_chunk(wm_ref, bexp_ref, vc, lane_h)
            df = dy * u[rows]
            dfb = df.astype(BF16)
            dvn = jnp.zeros((CHUNK, D_GROUP), F32)
            for h in range(D_GROUP // HEAD_DIM):
                dfh = jnp.where(lane_h == h, dfb, 0)
                dvn = dvn + _mm(wmt_ref[h], dfh)
                dws_ref[h] += jnp.where(tril, _mm_nt(dfh, vc), 0.0)
            dvnbuf[rows, :] = dvn
            dbias = dbias + df
        for h in range(D_GROUP // HEAD_DIM):
            per_head = jnp.sum(jnp.where(lane_h == h, dbias, 0.0), axis=1, keepdims=True)
            dbs_ref[...] += per_head * (lax.broadcasted_iota(jnp.int32, (1, CHUNK), 1) == h).astype(F32)
        dvn = dvnbuf[...]
        dvg_ref[...] += jnp.sum(dvn * vh, axis=0, keepdims=True)
        dgelu = _gelu_grad(za)
        dz_ref[:, 0:D_GROUP] = (dubuf[...] * dgelu[:, :D_GROUP]).astype(BF16)
        dz_ref[:, D_GROUP:2 * D_GROUP] = (_rms_bwd(dvn * vg_ref[...], vh, r) * dgelu[:, D_GROUP:]).astype(BF16)

        gate_b = z_ref[:, 2 * D_GROUP:3 * D_GROUP]
        gate_c = z_ref[:, 3 * D_GROUP:4 * D_GROUP]
        hh = z_ref[:, 4 * D_GROUP:5 * D_GROUP]
        p = gate_c * hh
        pbuf[0:HALO, :] = zp_ref[:, 3 * D_GROUP:4 * D_GROUP] * zp_ref[:, 4 * D_GROUP:5 * D_GROUP] * keep_prev
        pbuf[HALO:HALO + tb, :] = p
        dyb_v = dyb_ref[...]
        dcv = dyb_v * gate_b
        gbuf[0:tb, :] = dcv
        gbuf[tb:ext, :] = dybn_ref[...] * zn_ref[:, 2 * D_GROUP:3 * D_GROUP] * keep_next
        cv = scw_ref[K_SHORT - 1:K_SHORT, :] * p
        dp = scw_ref[K_SHORT - 1:K_SHORT, :] * dcv
        dscw_ref[K_SHORT - 1:K_SHORT, :] += jnp.sum(dcv * p, axis=0, keepdims=True)
        for k in range(K_SHORT - 1):
            earlier = pbuf[pl.ds(HALO - (K_SHORT - 1) + k, tb), :]
            cv = cv + scw_ref[k:k + 1, :] * earlier
            dp = dp + scw_ref[k:k + 1, :] * gbuf[pl.ds(K_SHORT - 1 - k, tb), :]
            dscw_ref[k:k + 1, :] += jnp.sum(dcv * earlier, axis=0, keepdims=True)
        dz_ref[:, 2 * D_GROUP:3 * D_GROUP] = (dyb_v * cv).astype(BF16)
        dz_ref[:, 3 * D_GROUP:4 * D_GROUP] = (dp * hh).astype(BF16)
        dz_ref[:, 4 * D_GROUP:5 * D_GROUP] = (dp * gate_c).astype(BF16)

        dz_ref[:, 5 * D_GROUP:6 * D_GROUP] = (dq_ref[...] * ATT_SCALE).astype(BF16)
        dz_ref[:, 6 * D_GROUP:6 * D_GROUP + PAIR] = dk0_ref[...].astype(BF16)
        dz_ref[:, 6 * D_GROUP + PAIR:7 * D_GROUP] = dk1_ref[...].astype(BF16)
        dz_ref[:, 7 * D_GROUP:7 * D_GROUP + PAIR] = dv0_ref[...].astype(BF16)
        dz_ref[:, 7 * D_GROUP + PAIR:8 * D_GROUP] = dv1_ref[...].astype(BF16)

        a = z_ref[:, 8 * D_GROUP:9 * D_GROUP]
        sg = _sigmoid(z_ref[:, 9 * D_GROUP:10 * D_GROUP])
        hbuf[0:HALO, :] = zp_ref[:, 8 * D_GROUP:9 * D_GROUP] * _sigmoid(zp_ref[:, 9 * D_GROUP:10 * D_GROUP]) * keep_prev
        hbuf[HALO:HALO + tb, :] = a * sg
        c = jnp.concatenate([c_ref[...], cn_ref[...]], axis=0)
        xc = c - jnp.mean(c, axis=-1, keepdims=True)
        rs = lax.rsqrt(jnp.mean(xc * xc, axis=-1, keepdims=True) + EPS)
        xh = xc * rs
        ln = xh * lng_ref[...] + lnb_ref[...]
        sl = _sigmoid(ln)
        dy_ext = jnp.concatenate([dyd_ref[...], dydn_ref[...] * keep_next], axis=0)
        dln = dy_ext * sl * (1.0 + ln * (1.0 - sl))
        dlng_ref[...] += jnp.sum(dln[:tb] * xh[:tb], axis=0, keepdims=True)
        dlnb_ref[...] += jnp.sum(dln[:tb], axis=0, keepdims=True)
        dxh = dln * lng_ref[...]
        dc = rs * (dxh - jnp.mean(dxh, axis=-1, keepdims=True) - xh * jnp.mean(dxh * xh, axis=-1, keepdims=True))
        cbuf[...] = dc
        dc_blk = dc[:tb]
        grad_in = [jnp.zeros((tb, D_GROUP), F32)]

        def tap_input(j, window):
            k = K_CONF - 1 - j
            grad_in[0] = grad_in[0] + ccw_ref[k:k + 1, :] * window

        def tap_filter(k, window):
            dccw_ref[k:k + 1, :] += jnp.sum(dc_blk * window, axis=0, keepdims=True)

        _taps(cbuf, shifted, 0, K_CONF, tb, tap_input)
        _taps(hbuf, shifted, HALO - (K_CONF - 1), K_CONF, tb, tap_filter)
        dhd = grad_in[0]
        dz_ref[:, 8 * D_GROUP:9 * D_GROUP] = (dhd * sg).astype(BF16)
        dz_ref[:, 9 * D_GROUP:10 * D_GROUP] = (dhd * a * sg * (1.0 - sg)).astype(BF16)

    grp = _rows(tb, D_GROUP)
    pair0 = pl.BlockSpec((None, tb, PAIR), lambda i: (0, i, 0))
    pair1 = pl.BlockSpec((None, tb, PAIR), lambda i: (1, i, 0))
    small = [_sds((1, D_GROUP)), _sds((4, CHUNK, CHUNK)), _sds((CHUNK, CHUNK)), _sds((8, D_GROUP)),
             _sds((HALO, D_GROUP)), _sds((1, D_GROUP)), _sds((1, D_GROUP))]
    return _call(
        body, (z, z, z, dya, dyb, dyb, dyd, dyd, conv, conv, dq, dk, dk, dv, dv, vg, wm, wmt, bexp, scw, ccw, lng, lnb), ride,
        name="mix_bwd", grid=(n_steps,),
        in_specs=[_rows(tb, D_IN), prev_spec, next_spec, grp, grp, next_grp, grp, next_grp, grp, next_grp, grp,
                  pair0, pair1, pair0, pair1,
                  _whole((1, D_GROUP)), _whole(wm.shape), _whole(wmt.shape), _whole(bexp.shape), _whole(scw.shape),
                  _whole(ccw.shape), _whole((1, D_GROUP)), _whole((1, D_GROUP))],
        out_specs=[_rows(tb, D_IN)] + [_whole(t.shape) for t in small],
        out_shape=[_sds((s, D_IN), BF16)] + small,
        scratch_shapes=[pltpu.VMEM((HALO + tb, D_GROUP), F32), pltpu.VMEM((HALO + tb, D_GROUP), F32),
                        pltpu.VMEM((ext, D_GROUP), F32), pltpu.VMEM((ext, D_GROUP), F32),
                        pltpu.VMEM((tb, D_GROUP), F32), pltpu.VMEM((tb, D_GROUP), F32),
                        pltpu.VMEM((tb + TAP_SLACK, D_GROUP), F32)],
        compiler_params=_params(("arbitrary",), 48))


def _in_proj_bwd(x, dz, dres, g, wint, tb, ride=None):
    s = x.shape[0]

    def body(x_ref, dz_ref, dr_ref, g_ref, w_ref, dx_ref, dg_ref):
        @pl.when(pl.program_id(0) == 0)
        def _():
            dg_ref[...] = jnp.zeros_like(dg_ref)

        xv = x_ref[...]
        r = _rms(xv)
        xh = xv * r
        dh = _mm(dz_ref[...], w_ref[...])
        dg_ref[...] += jnp.sum(dh * xh, axis=0, keepdims=True)
        dx_ref[...] = dr_ref[...] + _rms_bwd(dh * g_ref[...], xh, r)

    return _call(
        body, (x, dz, dres, g, wint), ride, name="in_proj_bwd", grid=(s // tb,),
        in_specs=[_rows(tb, D_MODEL), _rows(tb, D_IN), _rows(tb, D_MODEL), _whole((1, D_MODEL)), _resident((D_IN, D_MODEL))],
        out_specs=[_rows(tb, D_MODEL), _whole((1, D_MODEL))],
        out_shape=[_sds((s, D_MODEL)), _sds((1, D_MODEL))],
        compiler_params=_params(("arbitrary",), 48))


def _adamw(w, g, m, v):
    m = ADAM_B1 * m + (1.0 - ADAM_B1) * g
    v = ADAM_B2 * v + (1.0 - ADAM_B2) * (g * g)
    m_hat = m / (1.0 - ADAM_B1 ** ADAM_STEP)
    v_hat = v / (1.0 - ADAM_B2 ** ADAM_STEP)
    delta = -ADAM_LR * (m_hat / (jnp.sqrt(v_hat) + ADAM_EPS) + ADAM_WD * w)
    return delta, m, v


def _reduce_adamw(parts, w, m, v, tb, name):
    rows, cols = w.shape

    def body(p_ref, w_ref, m_ref, v_ref, g_ref, d_ref, m2_ref, v2_ref):
        g = p_ref[0]
        for j in range(1, N_DEV):
            g = g + p_ref[j]
        g_ref[...] = g
        d_ref[...], m2_ref[...], v2_ref[...] = _adamw(w_ref[...], g, m_ref[...], v_ref[...])

    blk = _rows(tb, cols)
    return pl.pallas_call(
        body, name=name, grid=(rows // tb,),
        in_specs=[pl.BlockSpec((N_DEV, tb, cols), lambda i: (0, i, 0)), blk, blk, blk],
        out_specs=[blk] * 4, out_shape=[_sds((rows, cols))] * 4,
        compiler_params=_params(("parallel",), 32),
    )(parts, w, m, v)


LANES = 128
PACK_ALIGN = 8 * LANES


def _pack(arrays):
    pieces = []
    for a in arrays:
        flat = a.reshape(-1)
        pieces.append(jnp.pad(flat, (0, -flat.shape[0] % PACK_ALIGN)).reshape(-1, LANES))
    return jnp.concatenate(pieces, axis=0)


def _unpack(packed, shapes):
    out, row = [], 0
    for shape in shapes:
        size = 1
        for dim in shape:
            size *= dim
        rows = -(-size // PACK_ALIGN) * 8
        out.append(packed[row:row + rows].reshape(-1)[:size].reshape(shape))
        row += rows
    return out


TB_PROJ = 512
TB_MIX = 256
TB_FFN_BWD = 256
TB_TN = 1024
TQ = 256
TK = 256
TB_ADAM = 64


def kernel(x, norm_mix_g, w_in, gmlp_v_g, gmlp_w_s, gmlp_b_s, short_conv_w, conf_conv_w, conf_ln_g, conf_ln_b, mix_out_g, w_out, norm_ffn_g, w_up, w_down, final_norm_g, loss_target, m_norm_mix_g, m_w_in, m_gmlp_v_g, m_gmlp_w_s, m_gmlp_b_s, m_short_conv_w, m_conf_conv_w, m_conf_ln_g, m_conf_ln_b, m_mix_out_g, m_w_out, m_norm_ffn_g, m_w_up, m_w_down, m_final_norm_g, v_norm_mix_g, v_w_in, v_gmlp_v_g, v_gmlp_w_s, v_gmlp_b_s, v_short_conv_w, v_conf_conv_w, v_conf_ln_g, v_conf_ln_b, v_mix_out_g, v_w_out, v_norm_ffn_g, v_w_up, v_w_down, v_final_norm_g):
    me = 4 * lax.axis_index("x") + 2 * lax.axis_index("y") + lax.axis_index("c")
    x0, target = x[0], loss_target[0]
    s = x0.shape[0]
    tb_proj, tb_mix, tb_fb, tb_tn = min(TB_PROJ, s), min(TB_MIX, s), min(TB_FFN_BWD, s), min(TB_TN, s)
    conv_cols = D_GROUP // N_DEV

    def pad_rows(a, rows):
        return jnp.pad(a, ((0, rows - a.shape[0]), (0, 0)))

    wint_loc = [w_in[l].T.astype(BF16) for l in range(N_LAYERS)]
    wout_loc = [w_out[l].astype(BF16) for l in range(N_LAYERS)]
    wup_loc = [w_up[l].astype(BF16) for l in range(N_LAYERS)]
    wdn_loc = [w_down[l].astype(BF16) for l in range(N_LAYERS)]
    conv_loc = jnp.concatenate([pad_rows(short_conv_w[l], 8) for l in range(N_LAYERS)]
                               + [pad_rows(conf_conv_w[l], HALO) for l in range(N_LAYERS)], axis=0)
    wint, wout, wup, wdn = [None] * N_LAYERS, [None] * N_LAYERS, [None] * N_LAYERS, [None] * N_LAYERS
    wint0, conv_all = _exchange([wint_loc[0], conv_loc], [GATHER, GATHER], "gather_first_weights")
    wint[0] = wint0.reshape(D_IN, D_MODEL)
    conv_full = conv_all.transpose(1, 0, 2).reshape(-1, D_GROUP)
    scw = [conv_full[8 * l:8 * (l + 1)] for l in range(N_LAYERS)]
    ccw = [conv_full[8 * N_LAYERS + HALO * l:8 * N_LAYERS + HALO * (l + 1)] for l in range(N_LAYERS)]

    tril = jnp.tril(jnp.ones((CHUNK, CHUNK), dtype=bool))
    wm = [jnp.where(tril, gmlp_w_s[l], 0.0).astype(BF16) for l in range(N_LAYERS)]
    wmt = [w.transpose(0, 2, 1) for w in wm]
    bexp = [jnp.repeat(gmlp_b_s[l].T, HEAD_DIM, axis=1) for l in range(N_LAYERS)]

    def row(vec):
        return vec.reshape(1, -1)

    saved = []
    xc = x0
    for l in range(N_LAYERS):
        first = l == 0
        (z, qkv, hb_in), moved = _in_proj_fwd(xc, row(norm_mix_g[l]), wint[l], tb_proj,
                                              ride=([wout_loc[0]], [GATHER]) if first else None)
        if first:
            wout[0] = moved[0].reshape(D_MODEL, D_MODEL)
        (ya, yb, yd, conv), moved = _mix_fwd(z, row(gmlp_v_g[l]), wm[l], bexp[l], scw[l], ccw[l], row(conf_ln_g[l]),
                                             row(conf_ln_b[l]), tb_mix, ride=([wup_loc[0]], [GATHER]) if first else None)
        if first:
            wup[0] = moved[0]
        (yc,), moved = _attn_fwd(qkv, TQ, TK, ride=([wdn_loc[0], wint_loc[1]], [GATHER, GATHER]) if first else None)
        if first:
            wdn[0], wint[1] = moved[0], moved[1].reshape(D_IN, D_MODEL)
        ys = (ya, yb, yc, yd)
        x1 = _out_proj_fwd(ys, xc, row(mix_out_g[l]), wout[l], tb_proj)
        (x2, act), moved = _ffn_fwd(x1, row(norm_ffn_g[l]), wup[l], wdn[l], tb_proj,
                                    ride=([wout_loc[1], wup_loc[1], wdn_loc[1]], [GATHER] * 3) if first else None)
        saved.append((xc, z, qkv, ys, x1, act, hb_in, conv))
        xc = x2
        if first:
            wout[1], wup[1], wdn[1] = moved[0].reshape(D_MODEL, D_MODEL), moved[1], moved[2]
    dx, g_final, loss_part = _loss_head(xc, row(final_norm_g), target, tb_proj)
    loss = lax.psum(loss_part[0, 0], MESH_AXES)

    parts = [None] * (4 * N_LAYERS)
    small_grads = [None] * N_LAYERS
    for l in reversed(range(N_LAYERS)):
        xin, z, qkv, ys, x1, act, hb_in, conv = saved[l]
        (dx1, hb_ffn, dpre, sq, dyb, g_ffn), _ = _ffn_bwd(x1, act, dx, row(norm_ffn_g[l]), wup[l], wdn[l], tb_fb)
        grad_up = _tn_slabs(hb_ffn, dpre, tb_tn, False, "grad_w_up")
        grad_dn = _tn_slabs(dyb, sq, tb_tn, True, "grad_w_down")
        dya, dyb_mix, dyc, dyd, yn, g_mixout = _out_proj_bwd(dx1, ys, row(mix_out_g[l]), wout[l], tb_proj)
        grad_out = _tn_matmul(yn, dx1, tb_tn, "grad_w_out").reshape(N_DEV, D_MODEL // N_DEV, D_MODEL)
        (dq, dk, dv), moved = _attn_bwd(qkv, dyc, ys[2], TQ, TK, ride=([grad_up, grad_dn], [SCATTER] * 2))
        parts[4 * l + 2], parts[4 * l + 3] = moved
        (dz, g_vg, g_ws, g_bs, g_scw, g_ccw, g_lng, g_lnb), moved = _mix_bwd(
            z, conv, dya, dyb_mix, dyd, dq, dk, dv, row(gmlp_v_g[l]), wm[l], wmt[l], bexp[l], scw[l], ccw[l],
            row(conf_ln_g[l]), row(conf_ln_b[l]), tb_mix, ride=([grad_out], [SCATTER]))
        parts[4 * l + 1] = moved[0]
        grad_in = _tn_matmul(dz, hb_in, tb_tn, "grad_w_in").reshape(N_DEV, D_IN // N_DEV, D_MODEL)
        (dx, g_mix), moved = _in_proj_bwd(xin, dz, dx1, row(norm_mix_g[l]), wint[l], tb_proj, ride=([grad_in], [SCATTER]))
        parts[4 * l] = moved[0]
        small_grads[l] = dict(norm_mix_g=g_mix[0], gmlp_v_g=g_vg[0], gmlp_w_s=g_ws, gmlp_b_s=g_bs[:, :4].T,
                              short_conv_w=g_scw[:K_SHORT], conf_conv_w=g_ccw[:K_CONF], conf_ln_g=g_lng[0],
                              conf_ln_b=g_lnb[0], mix_out_g=g_mixout[0], norm_ffn_g=g_ffn[0])

    small_names = ["norm_mix_g", "gmlp_v_g", "gmlp_w_s", "gmlp_b_s", "short_conv_w", "conf_conv_w", "conf_ln_g",
                   "conf_ln_b", "mix_out_g", "norm_ffn_g"]
    small_list = [jnp.stack([small_grads[l][n] for l in range(N_LAYERS)]) for n in small_names] + [g_final[0]]
    small_shapes = [a.shape for a in small_list]
    small_parts = _exchange([_pack(small_list)], [GATHER], "gather_small_grads")[0]

    given = dict(norm_mix_g=(norm_mix_g, m_norm_mix_g, v_norm_mix_g), gmlp_v_g=(gmlp_v_g, m_gmlp_v_g, v_gmlp_v_g),
                 gmlp_w_s=(gmlp_w_s, m_gmlp_w_s, v_gmlp_w_s), gmlp_b_s=(gmlp_b_s, m_gmlp_b_s, v_gmlp_b_s),
                 short_conv_w=(short_conv_w, m_short_conv_w, v_short_conv_w),
                 conf_conv_w=(conf_conv_w, m_conf_conv_w, v_conf_conv_w),
                 conf_ln_g=(conf_ln_g, m_conf_ln_g, v_conf_ln_g), conf_ln_b=(conf_ln_b, m_conf_ln_b, v_conf_ln_b),
                 mix_out_g=(mix_out_g, m_mix_out_g, v_mix_out_g), norm_ffn_g=(norm_ffn_g, m_norm_ffn_g, v_norm_ffn_g),
                 final_norm_g=(final_norm_g, m_final_norm_g, v_final_norm_g))
    sharded_small = ("short_conv_w", "conf_conv_w")

    def widen(a):
        full = jnp.zeros(a.shape[:-1] + (D_GROUP,), a.dtype)
        return lax.dynamic_update_slice(full, a, (0, 0, me * conv_cols))

    packed_state = []
    for k in range(3):
        packed_state.append(_pack([widen(given[n][k]) if n in sharded_small else given[n][k]
                                   for n in small_names + ["final_norm_g"]]))
    small_out = _reduce_adamw(small_parts, *packed_state, packed_state[0].shape[0], "adamw_small")
    small_res = {}
    for kind, packed in zip(("grad", "delta", "new_m", "new_v"), small_out):
        for n, val in zip(small_names + ["final_norm_g"], _unpack(packed, small_shapes)):
            if n in sharded_small:
                val = lax.dynamic_slice(val, (0, 0, me * conv_cols), val.shape[:-1] + (conv_cols,))
            small_res[kind, n] = val

    big_names = ["w_in", "w_out", "w_up", "w_down"]
    big_given = dict(w_in=(w_in, m_w_in, v_w_in), w_out=(w_out, m_w_out, v_w_out), w_up=(w_up, m_w_up, v_w_up),
                     w_down=(w_down, m_w_down, v_w_down))
    big_res = {}
    for j, n in enumerate(big_names):
        per_layer = []
        for l in range(N_LAYERS):
            state = [t[l].T if n == "w_in" else t[l] for t in big_given[n]]
            outs = _reduce_adamw(parts[4 * l + j], *state, TB_ADAM, "adamw_" + n)
            per_layer.append([o.T if n == "w_in" else o for o in outs])
        for k, kind in enumerate(("grad", "delta", "new_m", "new_v")):
            big_res[kind, n] = jnp.stack([per_layer[l][k] for l in range(N_LAYERS)])

    order = ["norm_mix_g", "w_in", "gmlp_v_g", "gmlp_w_s", "gmlp_b_s", "short_conv_w", "conf_conv_w", "conf_ln_g",
             "conf_ln_b", "mix_out_g", "w_out", "norm_ffn_g", "w_up", "w_down", "final_norm_g"]
    result = [loss, dx.reshape(x.shape)]
    for kind in ("grad", "delta", "new_m", "new_v"):
        for n in order:
            result.append(big_res[kind, n] if n in big_given else small_res[kind, n])
    return tuple(result)
```

```python
import jax
import jax.numpy as jnp
from jax import lax
from jax.experimental import pallas as pl
from jax.experimental.pallas import tpu as pltpu

F32 = jnp.float32
BF16 = jnp.bfloat16

D_MODEL = 1024
D_GROUP = 256
D_IN = 10 * D_GROUP
D_FF = 4 * D_MODEL
N_DEV = 8
N_LAYERS = 2
HEAD_DIM = 64
HEADS_PER_PAIR = 2
PAIR = HEADS_PER_PAIR * HEAD_DIM
CHUNK = 128
K_SHORT = 3
K_CONF = 31
HALO = 32
EPS = 1e-6
ATT_SCALE = HEAD_DIM ** -0.5
LOG_CUT = -104.0
MIB = 2 ** 20

ADAM_LR = 0.001
ADAM_B1 = 0.9
ADAM_B2 = 0.999
ADAM_EPS = 1e-08
ADAM_WD = 0.01
ADAM_STEP = 10

MESH_AXES = ("x", "y", "c")
GELU_C = 0.7978845608028654
GELU_A = 0.044715


def _mm(a, b):
    return jnp.dot(a, b, preferred_element_type=F32)


def _mm_nt(a, b):
    return lax.dot_general(a, b, (((1,), (1,)), ((), ())), preferred_element_type=F32)


def _mm_tn(a, b):
    return lax.dot_general(a, b, (((0,), (0,)), ((), ())), preferred_element_type=F32)


def _rms(x):
    return lax.rsqrt(jnp.mean(x * x, axis=-1, keepdims=True) + EPS)


def _rms_bwd(dy, xh, r):
    return r * (dy - xh * jnp.mean(dy * xh, axis=-1, keepdims=True))


def _sigmoid(x):
    return 1.0 / (1.0 + jnp.exp(-x))


def _whole(shape):
    return pl.BlockSpec(shape, lambda *_: (0,) * len(shape))


def _resident(shape):
    return pl.BlockSpec(shape, lambda *_: (0,) * len(shape), pipeline_mode=pl.Buffered(1))


def _rows(tb, width, col=0):
    return pl.BlockSpec((tb, width), lambda i: (i, col))


def _params(semantics, vmem_mib):
    return pltpu.CompilerParams(dimension_semantics=semantics, vmem_limit_bytes=vmem_mib * MIB)


def _sds(shape, dtype=F32):
    return jax.ShapeDtypeStruct(shape, dtype)


def _split_bf16(v):
    hi = v.astype(BF16)
    lo = (v - hi.astype(F32)).astype(BF16)
    return hi, lo


GATHER, SCATTER = "gather", "scatter"
ANY_SPEC = pl.BlockSpec(memory_space=pl.ANY)


def _exchange_copies(ins, outs, modes, send_sems, recv_sems, local_sems, with_arrivals=True):
    x, y, c = lax.axis_index("x"), lax.axis_index("y"), lax.axis_index("c")
    me = 4 * x + 2 * y + c
    local, sends, arrivals = [], [], []
    for a, mode in enumerate(modes):
        local.append(pltpu.make_async_copy(ins[a].at[me] if mode == SCATTER else ins[a], outs[a].at[me], local_sems.at[a]))
    for k in range(N_DEV - 1):
        flip = k + 1
        peer = (1 - x if flip & 4 else x, 1 - y if flip & 2 else y, 1 - c if flip & 1 else c)
        pf = 4 * peer[0] + 2 * peer[1] + peer[2]
        for a, mode in enumerate(modes):
            src = ins[a].at[pf] if mode == SCATTER else ins[a]
            for dst, group in ((outs[a].at[me], sends), (outs[a].at[pf], arrivals)):
                if group is sends or with_arrivals:
                    group.append(pltpu.make_async_remote_copy(
                        src_ref=src, dst_ref=dst, send_sem=send_sems.at[a, k], recv_sem=recv_sems.at[a, k],
                        device_id=peer, device_id_type=pl.DeviceIdType.MESH))
    return local, sends, arrivals


def _exchange_start(*refs_and_modes):
    local, sends, _ = _exchange_copies(*refs_and_modes, with_arrivals=False)
    for cp in local + sends:
        cp.start()


def _exchange_wait(*refs_and_modes):
    local, sends, arrivals = _exchange_copies(*refs_and_modes)
    for cp in sends:
        cp.wait_send()
    for cp in arrivals:
        cp.wait_recv()
    for cp in local:
        cp.wait()


def _exchange_shapes(arrays, modes):
    out_shape = [_sds(a.shape if mode == SCATTER else (N_DEV,) + a.shape, a.dtype) for a, mode in zip(arrays, modes)]
    n = len(arrays)
    sems = [pltpu.SemaphoreType.DMA((n, N_DEV - 1)), pltpu.SemaphoreType.DMA((n, N_DEV - 1)), pltpu.SemaphoreType.DMA((n,))]
    return out_shape, sems


def _exchange(arrays, modes, name):
    n = len(arrays)
    out_shape, sems = _exchange_shapes(arrays, modes)

    def body(*refs):
        _exchange_start(refs[:n], refs[n:2 * n], modes, *refs[2 * n:])
        _exchange_wait(refs[:n], refs[n:2 * n], modes, *refs[2 * n:])

    return pl.pallas_call(body, name=name, out_shape=out_shape, in_specs=[ANY_SPEC] * n, out_specs=[ANY_SPEC] * n,
                          scratch_shapes=sems)(*arrays)


def _call(body, args, ride, *, name, grid, in_specs, out_specs, out_shape, scratch_shapes=(), compiler_params):
    if ride is None:
        outs = pl.pallas_call(body, name=name, grid=grid, in_specs=in_specs, out_specs=out_specs, out_shape=out_shape,
                              scratch_shapes=scratch_shapes, compiler_params=compiler_params)(*args)
        return outs, []
    arrays, modes = ride
    n, n_in, n_out, n_scratch = len(arrays), len(in_specs), len(out_specs), len(scratch_shapes)
    moved_shape, sems = _exchange_shapes(arrays, modes)
    n_steps = 1
    for g in grid:
        n_steps *= g

    def riding(*refs):
        ins, refs = refs[:n_in], refs[n_in:]
        r_ins, refs = refs[:n], refs[n:]
        outs, refs = refs[:n_out], refs[n_out:]
        r_outs, refs = refs[:n], refs[n:]
        scratch, r_sems = refs[:n_scratch], refs[n_scratch:]
        step = pl.program_id(0)
        for axis in range(1, len(grid)):
            step = step * grid[axis] + pl.program_id(axis)

        @pl.when(step == 0)
        def _():
            _exchange_start(r_ins, r_outs, modes, *r_sems)

        body(*ins, *outs, *scratch)

        @pl.when(step == n_steps - 1)
        def _():
            _exchange_wait(r_ins, r_outs, modes, *r_sems)

    outs = pl.pallas_call(
        riding, name=name, grid=grid, in_specs=list(in_specs) + [ANY_SPEC] * n,
        out_specs=list(out_specs) + [ANY_SPEC] * n, out_shape=list(out_shape) + moved_shape,
        scratch_shapes=list(scratch_shapes) + sems, compiler_params=compiler_params)(*args, *arrays)
    return outs[:n_out], outs[n_out:]


def _in_proj_fwd(x, g, wint, tb, ride=None):
    s = x.shape[0]

    def body(x_ref, g_ref, w_ref, z_ref, qkv_ref, hb_ref):
        xv = x_ref[...]
        h = (xv * _rms(xv) * g_ref[...]).astype(BF16)
        hb_ref[...] = h
        z = _mm_nt(h, w_ref[...])
        z_ref[...] = z
        qkv_ref[:, 0:D_GROUP] = (z[:, 5 * D_GROUP:6 * D_GROUP] * ATT_SCALE).astype(BF16)
        qkv_ref[:, D_GROUP:3 * D_GROUP] = z[:, 6 * D_GROUP:8 * D_GROUP].astype(BF16)

    return _call(
        body, (x, g, wint), ride, name="in_proj_fwd", grid=(s // tb,),
        in_specs=[_rows(tb, D_MODEL), _whole((1, D_MODEL)), _resident((D_IN, D_MODEL))],
        out_specs=[_rows(tb, D_IN), _rows(tb, 3 * D_GROUP), _rows(tb, D_MODEL)],
        out_shape=[_sds((s, D_IN)), _sds((s, 3 * D_GROUP), BF16), _sds((s, D_MODEL), BF16)],
        compiler_params=_params(("arbitrary",), 48))


def _gelu(x):
    return 0.5 * x * (1.0 + jnp.tanh(GELU_C * (x + GELU_A * x * x * x)))


def _gelu_grad(x):
    t = jnp.tanh(GELU_C * (x + GELU_A * x * x * x))
    return 0.5 * (1.0 + t) + 0.5 * x * (1.0 - t * t) * GELU_C * (1.0 + 3.0 * GELU_A * x * x)


def _head_lane(width):
    return lax.broadcasted_iota(jnp.int32, (1, width), 1) // HEAD_DIM


def _gating_chunk(wm_ref, bexp_ref, vc, lane_h):
    f = bexp_ref[...]
    for h in range(D_GROUP // HEAD_DIM):
        f = f + _mm(wm_ref[h], jnp.where(lane_h == h, vc, 0))
    return f


def _halo_specs(s, tb, width_blocks):
    per = tb // HALO
    prev = pl.BlockSpec((HALO, width_blocks), lambda i: (jnp.maximum(i * per - 1, 0), 0))
    nxt = pl.BlockSpec((HALO, width_blocks), lambda i: (jnp.minimum((i + 1) * per, s // HALO - 1), 0))
    return prev, nxt


SUBLANES = 8
TAP_SLACK = 24


def _taps(buf_ref, shifted_ref, first, n_taps, rows, visit):
    for residue in range(SUBLANES):
        taps = [j for j in range(n_taps) if (first + j) % SUBLANES == residue]
        if not taps:
            continue
        lo = first + taps[0]
        span = first + taps[-1] - lo + rows
        shifted_ref[0:span, :] = buf_ref[pl.ds(lo, span), :]
        for j in taps:
            visit(j, shifted_ref[pl.ds(first + j - lo, rows), :])


def _mix_fwd(z, vg, wm, bexp, scw, ccw, lng, lnb, tb, ride=None):
    s = z.shape[0]
    prev_spec, _ = _halo_specs(s, tb, D_IN)

    def body(z_ref, zp_ref, vg_ref, wm_ref, bexp_ref, scw_ref, ccw_ref, lng_ref, lnb_ref,
             ya_ref, yb_ref, yd_ref, c_ref, pbuf, hbuf, shifted):
        keep = (pl.program_id(0) > 0).astype(F32)
        lane_h = _head_lane(D_GROUP)
        ga = _gelu(z_ref[:, 0:2 * D_GROUP])
        u, v = ga[:, :D_GROUP], ga[:, D_GROUP:]
        vn = (v * _rms(v) * vg_ref[...]).astype(BF16)
        for n in range(tb // CHUNK):
            rows = slice(n * CHUNK, (n + 1) * CHUNK)
            ya_ref[rows, :] = u[rows] * _gating_chunk(wm_ref, bexp_ref, vn[rows], lane_h)
        p = z_ref[:, 3 * D_GROUP:4 * D_GROUP] * z_ref[:, 4 * D_GROUP:5 * D_GROUP]
        pbuf[0:HALO, :] = zp_ref[:, 3 * D_GROUP:4 * D_GROUP] * zp_ref[:, 4 * D_GROUP:5 * D_GROUP] * keep
        pbuf[HALO:HALO + tb, :] = p
        cv = scw_ref[K_SHORT - 1:K_SHORT, :] * p
        for k in range(K_SHORT - 1):
            cv = cv + scw_ref[k:k + 1, :] * pbuf[pl.ds(HALO - (K_SHORT - 1) + k, tb), :]
        yb_ref[...] = z_ref[:, 2 * D_GROUP:3 * D_GROUP] * cv
        hbuf[0:HALO, :] = zp_ref[:, 8 * D_GROUP:9 * D_GROUP] * _sigmoid(zp_ref[:, 9 * D_GROUP:10 * D_GROUP]) * keep
        hbuf[HALO:HALO + tb, :] = z_ref[:, 8 * D_GROUP:9 * D_GROUP] * _sigmoid(z_ref[:, 9 * D_GROUP:10 * D_GROUP])
        conv = [jnp.zeros((tb, D_GROUP), F32)]

        def tap(k, window):
            conv[0] = conv[0] + ccw_ref[k:k + 1, :] * window

        _taps(hbuf, shifted, HALO - (K_CONF - 1), K_CONF, tb, tap)
        c = conv[0]
        c_ref[...] = c
        xc = c - jnp.mean(c, axis=-1, keepdims=True)
        ln = xc * lax.rsqrt(jnp.mean(xc * xc, axis=-1, keepdims=True) + EPS) * lng_ref[...] + lnb_ref[...]
        yd_ref[...] = ln * _sigmoid(ln)

    grp = _rows(tb, D_GROUP)
    return _call(
        body, (z, z, vg, wm, bexp, scw, ccw, lng, lnb), ride, name="mix_fwd", grid=(s // tb,),
        in_specs=[_rows(tb, D_IN), prev_spec, _whole((1, D_GROUP)), _whole(wm.shape), _whole(bexp.shape),
                  _whole(scw.shape), _whole(ccw.shape), _whole((1, D_GROUP)), _whole((1, D_GROUP))],
        out_specs=[grp, grp, grp, grp],
        out_shape=[_sds((s, D_GROUP))] * 4,
        scratch_shapes=[pltpu.VMEM((HALO + tb, D_GROUP), F32), pltpu.VMEM((HALO + tb, D_GROUP), F32),
                        pltpu.VMEM((tb + TAP_SLACK, D_GROUP), F32)],
        compiler_params=_params(("arbitrary",), 40))


def _stick_tile(qh, kt, causal, c, upper):
    x = _mm_nt(qh, kt)
    soft = jnp.log(1.0 + jnp.exp(-jnp.abs(x)))
    lb = jnp.minimum(x, 0.0) - soft
    lom = jnp.where(causal, -jnp.maximum(x, 0.0) - soft, 0.0)
    hi, lo = _split_bf16(lom)
    stick = c + _mm(hi, upper[...]) + _mm(lo, upper[...])
    w = jnp.where(causal, jnp.exp(lb + stick), 0.0)
    return w, lb, lom


def _triangle(n, diagonal):
    return jnp.tri(n, n, diagonal, dtype=BF16)


def _causal_tile(qi, tq, k0, tk):
    qpos = qi * tq + lax.broadcasted_iota(jnp.int32, (tq, 1), 0)
    return k0 + lax.broadcasted_iota(jnp.int32, (1, tk), 1) < qpos


def _sticks_alive(cs):
    longest = cs[0]
    for c in cs[1:]:
        longest = jnp.maximum(longest, c)
    return (jnp.max(longest) > LOG_CUT).astype(jnp.int32)


def _walk(body, qi, tq, tk, init):
    start = (((qi + 1) * tq - 1) // tk, jnp.int32(1)) + tuple(init)
    return lax.while_loop(lambda cr: jnp.logical_and(cr[0] >= 0, cr[1] > 0), body, start)[2:]


def _attn_fwd(qkv, tq, tk, ride=None):
    s = qkv.shape[0]
    n_heads = D_GROUP // HEAD_DIM

    def body(q_ref, k_ref, v_ref, upper, o_ref):
        qi = pl.program_id(0)
        q = q_ref[...]
        lane_h = _head_lane(D_GROUP)
        qhs = [jnp.where(lane_h == h, q, 0) for h in range(n_heads)]

        def step(carry):
            kb, _, acc = carry[:3]
            cs = list(carry[3:])
            k0 = pl.multiple_of(kb * tk, tk)
            kt = k_ref[pl.ds(k0, tk), :]
            vt = v_ref[pl.ds(k0, tk), :]
            causal = _causal_tile(qi, tq, k0, tk)
            for h in range(n_heads):
                w, _, lom = _stick_tile(qhs[h], kt, causal, cs[h], upper)
                acc = acc + _mm(w.astype(BF16), jnp.where(lane_h == h, vt, 0))
                cs[h] = cs[h] + jnp.sum(lom, axis=1, keepdims=True)
            return (kb - 1, _sticks_alive(cs), acc) + tuple(cs)

        init = [jnp.zeros((tq, D_GROUP), F32)] + [jnp.zeros((tq, 1), F32)] * n_heads
        o_ref[...] = _walk(step, qi, tq, tk, init)[0]

    return _call(
        body, (qkv, qkv, qkv, _triangle(tk, -1)), ride, name="attn_fwd", grid=(s // tq,),
        in_specs=[pl.BlockSpec((tq, D_GROUP), lambda qi: (qi, 0)),
                  pl.BlockSpec((s, D_GROUP), lambda qi: (0, 1), pipeline_mode=pl.Buffered(1)),
                  pl.BlockSpec((s, D_GROUP), lambda qi: (0, 2), pipeline_mode=pl.Buffered(1)),
                  _resident((tk, tk))],
        out_specs=[pl.BlockSpec((tq, D_GROUP), lambda qi: (qi, 0))],
        out_shape=[_sds((s, D_GROUP))],
        compiler_params=_params(("arbitrary",), 40))


def _out_proj_fwd(ys, x, mg, wout, tb):
    s = x.shape[0]

    def body(ya_ref, yb_ref, yc_ref, yd_ref, x_ref, mg_ref, w_ref, o_ref):
        acc = x_ref[...]
        for gi, y_ref in enumerate((ya_ref, yb_ref, yc_ref, yd_ref)):
            cols = slice(gi * D_GROUP, (gi + 1) * D_GROUP)
            y = y_ref[...]
            acc = acc + _mm((y * _rms(y) * mg_ref[:, cols]).astype(BF16), w_ref[cols, :])
        o_ref[...] = acc

    grp = _rows(tb, D_GROUP)
    return pl.pallas_call(
        body, name="out_proj_fwd", grid=(s // tb,),
        in_specs=[grp, grp, grp, grp, _rows(tb, D_MODEL), _whole((1, D_MODEL)), _whole((D_MODEL, D_MODEL))],
        out_specs=_rows(tb, D_MODEL), out_shape=_sds((s, D_MODEL)),
        compiler_params=_params(("parallel",), 32),
    )(*ys, x, mg, wout)


def _ffn_fwd(x, g, wup, wdn, tb, ride=None):
    s = x.shape[0]
    ff = D_FF // N_DEV

    def body(x_ref, g_ref, wu_ref, wd_ref, o_ref, a_ref):
        xv = x_ref[...]
        h = (xv * _rms(xv) * g_ref[...]).astype(BF16)
        acc = xv
        for d in range(N_DEV):
            a = jnp.maximum(_mm(h, wu_ref[d]), 0.0)
            a_ref[:, d * ff:(d + 1) * ff] = a.astype(BF16)
            acc = acc + _mm((a * a).astype(BF16), wd_ref[d])
        o_ref[...] = acc

    return _call(
        body, (x, g, wup, wdn), ride, name="ffn_fwd", grid=(s // tb,),
        in_specs=[_rows(tb, D_MODEL), _whole((1, D_MODEL)), _resident((N_DEV, D_MODEL, ff)), _resident((N_DEV, ff, D_MODEL))],
        out_specs=[_rows(tb, D_MODEL), _rows(tb, D_FF)], out_shape=[_sds((s, D_MODEL)), _sds((s, D_FF), BF16)],
        compiler_params=_params(("arbitrary",), 56))


def _loss_head(x, g, tgt, tb):
    s = x.shape[0]

    def body(x_ref, g_ref, t_ref, dx_ref, dg_ref, loss_ref):
        @pl.when(pl.program_id(0) == 0)
        def _():
            dg_ref[...] = jnp.zeros_like(dg_ref)
            loss_ref[...] = jnp.zeros_like(loss_ref)

        xv = x_ref[...]
        r = _rms(xv)
        xh = xv * r
        err = xh * g_ref[...] - t_ref[...]
        loss_ref[...] += 0.5 * jnp.sum(jnp.mean(err * err, axis=-1, keepdims=True))
        dy = err * (1.0 / D_MODEL)
        dg_ref[...] += jnp.sum(dy * xh, axis=0, keepdims=True)
        dx_ref[...] = _rms_bwd(dy * g_ref[...], xh, r)

    return pl.pallas_call(
        body, name="loss_head", grid=(s // tb,),
        in_specs=[_rows(tb, D_MODEL), _whole((1, D_MODEL)), _rows(tb, D_MODEL)],
        out_specs=[_rows(tb, D_MODEL), _whole((1, D_MODEL)), _whole((8, 128))],
        out_shape=[_sds((s, D_MODEL)), _sds((1, D_MODEL)), _sds((8, 128))],
        compiler_params=_params(("arbitrary",), 32),
    )(x, g, tgt)


def _ffn_bwd(x1, act, dx2, g, wup, wdn, tb, ride=None):
    s = x1.shape[0]
    ff = D_FF // N_DEV

    def body(x_ref, a_ref, dy_ref, g_ref, wu_ref, wd_ref, dx_ref, hb_ref, dpre_ref, dg_ref):
        @pl.when(pl.program_id(0) == 0)
        def _():
            dg_ref[...] = jnp.zeros_like(dg_ref)

        xv = x_ref[...]
        r = _rms(xv)
        xh = xv * r
        hb_ref[...] = (xh * g_ref[...]).astype(BF16)
        dyv = dy_ref[...]
        dyb = dyv.astype(BF16)
        dh = jnp.zeros((tb, D_MODEL), F32)
        for d in range(N_DEV):
            cols = slice(d * ff, (d + 1) * ff)
            a = a_ref[:, cols].astype(F32)
            dpre = (_mm_nt(dyb, wd_ref[d]) * (2.0 * a)).astype(BF16)
            dpre_ref[:, cols] = dpre
            dh = dh + _mm_nt(dpre, wu_ref[d])
        dg_ref[...] += jnp.sum(dh * xh, axis=0, keepdims=True)
        dx_ref[...] = dyv + _rms_bwd(dh * g_ref[...], xh, r)

    return _call(
        body, (x1, act, dx2, g, wup, wdn), ride, name="ffn_bwd", grid=(s // tb,),
        in_specs=[_rows(tb, D_MODEL), _rows(tb, D_FF), _rows(tb, D_MODEL), _whole((1, D_MODEL)),
                  _resident((N_DEV, D_MODEL, ff)), _resident((N_DEV, ff, D_MODEL))],
        out_specs=[_rows(tb, D_MODEL), _rows(tb, D_MODEL), _rows(tb, D_FF), _whole((1, D_MODEL))],
        out_shape=[_sds((s, D_MODEL)), _sds((s, D_MODEL), BF16), _sds((s, D_FF), BF16), _sds((1, D_MODEL))],
        compiler_params=_params(("arbitrary",), 58))


def _tn_matmul(a, b, tb, name):
    s, ka = a.shape
    nb = b.shape[1]

    def body(a_ref, b_ref, o_ref):
        @pl.when(pl.program_id(0) == 0)
        def _():
            o_ref[...] = jnp.zeros_like(o_ref)

        o_ref[...] += _mm_tn(a_ref[...].astype(BF16), b_ref[...].astype(BF16))

    return pl.pallas_call(
        body, name=name, grid=(s // tb,),
        in_specs=[_rows(tb, ka), _rows(tb, nb)], out_specs=_whole((ka, nb)), out_shape=_sds((ka, nb)),
        compiler_params=_params(("arbitrary",), 56),
    )(a, b)


def _tn_slabs(a, b, tb, transpose_slabs, name, square_b=False):
    s, m = a.shape
    width = b.shape[1] // N_DEV
    n_steps = s // tb
    slab = (width, m) if transpose_slabs else (m, width)

    def body(a_ref, b_ref, o_hbm, acc, stage, sem):
        step = pl.program_id(0)

        @pl.when(step == 0)
        def _():
            acc[...] = jnp.zeros_like(acc)

        bv = b_ref[...]
        if square_b:
            bv = bv.astype(F32)
            bv = bv * bv
        acc[...] += _mm_tn(a_ref[...].astype(BF16), bv.astype(BF16))

        @pl.when(step == n_steps - 1)
        def _():
            for d in range(N_DEV):
                cols = acc.at[:, pl.ds(d * width, width)]
                if transpose_slabs:
                    stage[...] = cols[...].T
                cp = pltpu.make_async_copy(stage if transpose_slabs else cols, o_hbm.at[d], sem.at[0])
                cp.start()
                cp.wait()

    return pl.pallas_call(
        body, name=name, grid=(n_steps,),
        in_specs=[_rows(tb, m), _rows(tb, b.shape[1])], out_specs=ANY_SPEC, out_shape=_sds((N_DEV,) + slab),
        scratch_shapes=[pltpu.VMEM((m, b.shape[1]), F32), pltpu.VMEM(slab, F32), pltpu.SemaphoreType.DMA((1,))],
        compiler_params=_params(("arbitrary",), 56),
    )(a, b)


def _out_proj_bwd(dx1, ys, mg, wout, tb):
    s = dx1.shape[0]

    def body(dx_ref, ya_ref, yb_ref, yc_ref, yd_ref, mg_ref, w_ref,
             dya_ref, dyb_ref, dyc_ref, dyd_ref, yn_ref, dmg_ref):
        @pl.when(pl.program_id(0) == 0)
        def _():
            dmg_ref[...] = jnp.zeros_like(dmg_ref)

        dyn = _mm_nt(dx_ref[...].astype(BF16), w_ref[...])
        groups = ((ya_ref, dya_ref), (yb_ref, dyb_ref), (yc_ref, dyc_ref), (yd_ref, dyd_ref))
        for gi, (y_ref, dy_ref) in enumerate(groups):
            cols = slice(gi * D_GROUP, (gi + 1) * D_GROUP)
            y = y_ref[...]
            r = _rms(y)
            n = y * r
            gain = mg_ref[:, cols]
            dn = dyn[:, cols]
            yn_ref[:, cols] = (n * gain).astype(BF16)
            dmg_ref[:, cols] += jnp.sum(dn * n, axis=0, keepdims=True)
            dy_ref[...] = _rms_bwd(dn * gain, n, r)

    grp = _rows(tb, D_GROUP)
    return pl.pallas_call(
        body, name="out_proj_bwd", grid=(s // tb,),
        in_specs=[_rows(tb, D_MODEL), grp, grp, grp, grp, _whole((1, D_MODEL)), _whole((D_MODEL, D_MODEL))],
        out_specs=[grp, grp, grp, grp, _rows(tb, D_MODEL), _whole((1, D_MODEL))],
        out_shape=[_sds((s, D_GROUP))] * 4 + [_sds((s, D_MODEL), BF16), _sds((1, D_MODEL))],
        compiler_params=_params(("arbitrary",), 32),
    )(dx1, *ys, mg, wout)


def _attn_bwd(qkv, do, o, tq, tk, ride=None):
    s = qkv.shape[0]
    nq = s // tq
    n_pairs = D_GROUP // PAIR

    def body(q_ref, k_ref, v_ref, do_ref, o_ref, upper, upper_eq, dq_ref, dk_hbm, dv_hbm, dk_acc, dv_acc, sems):
        hp, qi = pl.program_id(0), pl.program_id(1)

        @pl.when(qi == 0)
        def _():
            dk_acc[...] = jnp.zeros_like(dk_acc)
            dv_acc[...] = jnp.zeros_like(dv_acc)

        q = q_ref[...]
        dob = do_ref[...].astype(BF16)
        prod = dob.astype(F32) * o_ref[...]
        lane_h = _head_lane(PAIR)
        heads = []
        for h in range(HEADS_PER_PAIR):
            in_head = lane_h == h
            total = jnp.sum(jnp.where(in_head, prod, 0.0), axis=1, keepdims=True)
            heads.append((in_head, jnp.where(in_head, q, 0), jnp.where(in_head, dob, 0), total))

        def step(carry):
            kb, _, acc = carry[:3]
            cs = list(carry[3:3 + HEADS_PER_PAIR])
            nears = list(carry[3 + HEADS_PER_PAIR:])
            k0 = pl.multiple_of(kb * tk, tk)
            kt = k_ref[pl.ds(k0, tk), :]
            vt = v_ref[pl.ds(k0, tk), :]
            causal = _causal_tile(qi, tq, k0, tk)
            dk_t = jnp.zeros((tk, PAIR), F32)
            dv_t = jnp.zeros((tk, PAIR), F32)
            for h, (in_head, qh, doh, total) in enumerate(heads):
                w, lb, lom = _stick_tile(qh, kt, causal, cs[h], upper)
                wb = w.astype(BF16)
                gw = _mm_nt(doh, vt) * wb.astype(F32)
                hi, lo = _split_bf16(gw)
                far = total - nears[h] - _mm(hi, upper_eq[...]) - _mm(lo, upper_eq[...])
                beta = jnp.exp(lb)
                dxb = jnp.where(causal, gw - beta * (gw + far), 0.0).astype(BF16)
                acc = acc + _mm(dxb, jnp.where(in_head, kt, 0))
                dk_t = dk_t + _mm_tn(dxb, qh)
                dv_t = dv_t + _mm_tn(wb, doh)
                cs[h] = cs[h] + jnp.sum(lom, axis=1, keepdims=True)
                nears[h] = nears[h] + jnp.sum(gw, axis=1, keepdims=True)
            dk_acc[pl.ds(k0, tk), :] += dk_t
            dv_acc[pl.ds(k0, tk), :] += dv_t
            return (kb - 1, _sticks_alive(cs), acc) + tuple(cs) + tuple(nears)

        init = [jnp.zeros((tq, PAIR), F32)] + [jnp.zeros((tq, 1), F32)] * (2 * HEADS_PER_PAIR)
        dq_ref[...] = _walk(step, qi, tq, tk, init)[0]

        @pl.when(qi == nq - 1)
        def _():
            ck = pltpu.make_async_copy(dk_acc, dk_hbm.at[hp], sems.at[0])
            cv = pltpu.make_async_copy(dv_acc, dv_hbm.at[hp], sems.at[1])
            ck.start()
            cv.start()
            ck.wait()
            cv.wait()

    blk = pl.BlockSpec((tq, PAIR), lambda hp, qi: (qi, hp))
    return _call(
        body, (qkv, qkv, qkv, do, o, _triangle(tk, -1), _triangle(tk, 0)), ride, name="attn_bwd", grid=(n_pairs, nq),
        in_specs=[blk, pl.BlockSpec((s, PAIR), lambda hp, qi: (0, 2 + hp)),
                  pl.BlockSpec((s, PAIR), lambda hp, qi: (0, 4 + hp)), blk, blk, _resident((tk, tk)), _resident((tk, tk))],
        out_specs=[blk, ANY_SPEC, ANY_SPEC],
        out_shape=[_sds((s, D_GROUP)), _sds((n_pairs, s, PAIR)), _sds((n_pairs, s, PAIR))],
        scratch_shapes=[pltpu.VMEM((s, PAIR), F32), pltpu.VMEM((s, PAIR), F32), pltpu.SemaphoreType.DMA((2,))],
        compiler_params=_params(("arbitrary", "arbitrary"), 56))


def _mix_bwd(z, conv, dya, dyb, dyd, dq, dk, dv, vg, wm, wmt, bexp, scw, ccw, lng, lnb, tb, ride=None):
    s = z.shape[0]
    n_steps = s // tb
    prev_spec, next_spec = _halo_specs(s, tb, D_IN)
    _, next_grp = _halo_specs(s, tb, D_GROUP)
    ext = tb + HALO

    def body(z_ref, zp_ref, zn_ref, dya_ref, dyb_ref, dybn_ref, dyd_ref, dydn_ref, c_ref, cn_ref, dq_ref, dk0_ref, dk1_ref,
             dv0_ref, dv1_ref, vg_ref, wm_ref, wmt_ref, bexp_ref, scw_ref, ccw_ref, lng_ref, lnb_ref,
             dz_ref, dvg_ref, dws_ref, dbs_ref, dscw_ref, dccw_ref, dlng_ref, dlnb_ref,
             pbuf, hbuf, gbuf, cbuf, dubuf, dvnbuf, shifted):
        i = pl.program_id(0)

        @pl.when(i == 0)
        def _():
            for ref in (dvg_ref, dws_ref, dbs_ref, dscw_ref, dccw_ref, dlng_ref, dlnb_ref):
                ref[...] = jnp.zeros_like(ref)

        keep_prev = (i > 0).astype(F32)
        keep_next = (i < n_steps - 1).astype(F32)
        lane_h = _head_lane(D_GROUP)

        za = z_ref[:, 0:2 * D_GROUP]
        ga = _gelu(za)
        u, v = ga[:, :D_GROUP], ga[:, D_GROUP:]
        r = _rms(v)
        vh = v * r
        vn = (vh * vg_ref[...]).astype(BF16)
        tril = lax.broadcasted_iota(jnp.int32, (CHUNK, CHUNK), 0) >= lax.broadcasted_iota(jnp.int32, (CHUNK, CHUNK), 1)
        dbias = jnp.zeros((CHUNK, D_GROUP), F32)
        for n in range(tb // CHUNK):
            rows = slice(n * CHUNK, (n + 1) * CHUNK)
            vc = vn[rows]
            dy = dya_ref[rows, :]
            dubuf[rows, :] = dy * _gating_chunk(wm_ref, bexp_ref, vc, lane_h)
            df = dy * u[rows]
            dfb = df.astype(BF16)
            dvn = jnp.zeros((CHUNK, D_GROUP), F32)
            for h in range(D_GROUP // HEAD_DIM):
                dfh = jnp.where(lane_h == h, dfb, 0)
                dvn = dvn + _mm(wmt_ref[h], dfh)
                dws_ref[h] += jnp.where(tril, _mm_nt(dfh, vc), 0.0)
            dvnbuf[rows, :] = dvn
            dbias = dbias + df
        for h in range(D_GROUP // HEAD_DIM):
            per_head = jnp.sum(jnp.where(lane_h == h, dbias, 0.0), axis=1, keepdims=True)
            dbs_ref[...] += per_head * (lax.broadcasted_iota(jnp.int32, (1, CHUNK), 1) == h).astype(F32)
        dvn = dvnbuf[...]
        dvg_ref[...] += jnp.sum(dvn * vh, axis=0, keepdims=True)
        dgelu = _gelu_grad(za)
        dz_ref[:, 0:D_GROUP] = (dubuf[...] * dgelu[:, :D_GROUP]).astype(BF16)
        dz_ref[:, D_GROUP:2 * D_GROUP] = (_rms_bwd(dvn * vg_ref[...], vh, r) * dgelu[:, D_GROUP:]).astype(BF16)

        gate_b = z_ref[:, 2 * D_GROUP:3 * D_GROUP]
        gate_c = z_ref[:, 3 * D_GROUP:4 * D_GROUP]
        hh = z_ref[:, 4 * D_GROUP:5 * D_GROUP]
        p = gate_c * hh
        pbuf[0:HALO, :] = zp_ref[:, 3 * D_GROUP:4 * D_GROUP] * zp_ref[:, 4 * D_GROUP:5 * D_GROUP] * keep_prev
        pbuf[HALO:HALO + tb, :] = p
        dyb_v = dyb_ref[...]
        dcv = dyb_v * gate_b
        gbuf[0:tb, :] = dcv
        gbuf[tb:ext, :] = dybn_ref[...] * zn_ref[:, 2 * D_GROUP:3 * D_GROUP] * keep_next
        cv = scw_ref[K_SHORT - 1:K_SHORT, :] * p
        dp = scw_ref[K_SHORT - 1:K_SHORT, :] * dcv
        dscw_ref[K_SHORT - 1:K_SHORT, :] += jnp.sum(dcv * p, axis=0, keepdims=True)
        for k in range(K_SHORT - 1):
            earlier = pbuf[pl.ds(HALO - (K_SHORT - 1) + k, tb), :]
            cv = cv + scw_ref[k:k + 1, :] * earlier
            dp = dp + scw_ref[k:k + 1, :] * gbuf[pl.ds(K_SHORT - 1 - k, tb), :]
            dscw_ref[k:k + 1, :] += jnp.sum(dcv * earlier, axis=0, keepdims=True)
        dz_ref[:, 2 * D_GROUP:3 * D_GROUP] = (dyb_v * cv).astype(BF16)
        dz_ref[:, 3 * D_GROUP:4 * D_GROUP] = (dp * hh).astype(BF16)
        dz_ref[:, 4 * D_GROUP:5 * D_GROUP] = (dp * gate_c).astype(BF16)

        dz_ref[:, 5 * D_GROUP:6 * D_GROUP] = (dq_ref[...] * ATT_SCALE).astype(BF16)
        dz_ref[:, 6 * D_GROUP:6 * D_GROUP + PAIR] = dk0_ref[...].astype(BF16)
        dz_ref[:, 6 * D_GROUP + PAIR:7 * D_GROUP] = dk1_ref[...].astype(BF16)
        dz_ref[:, 7 * D_GROUP:7 * D_GROUP + PAIR] = dv0_ref[...].astype(BF16)
        dz_ref[:, 7 * D_GROUP + PAIR:8 * D_GROUP] = dv1_ref[...].astype(BF16)

        a = z_ref[:, 8 * D_GROUP:9 * D_GROUP]
        sg = _sigmoid(z_ref[:, 9 * D_GROUP:10 * D_GROUP])
        hbuf[0:HALO, :] = zp_ref[:, 8 * D_GROUP:9 * D_GROUP] * _sigmoid(zp_ref[:, 9 * D_GROUP:10 * D_GROUP]) * keep_prev
        hbuf[HALO:HALO + tb, :] = a * sg
        c = jnp.concatenate([c_ref[...], cn_ref[...]], axis=0)
        xc = c - jnp.mean(c, axis=-1, keepdims=True)
        rs = lax.rsqrt(jnp.mean(xc * xc, axis=-1, keepdims=True) + EPS)
        xh = xc * rs
        ln = xh * lng_ref[...] + lnb_ref[...]
        sl = _sigmoid(ln)
        dy_ext = jnp.concatenate([dyd_ref[...], dydn_ref[...] * keep_next], axis=0)
        dln = dy_ext * sl * (1.0 + ln * (1.0 - sl))
        dlng_ref[...] += jnp.sum(dln[:tb] * xh[:tb], axis=0, keepdims=True)
        dlnb_ref[...] += jnp.sum(dln[:tb], axis=0, keepdims=True)
        dxh = dln * lng_ref[...]
        dc = rs * (dxh - jnp.mean(dxh, axis=-1, keepdims=True) - xh * jnp.mean(dxh * xh, axis=-1, keepdims=True))
        cbuf[...] = dc
        dc_blk = dc[:tb]
        grad_in = [jnp.zeros((tb, D_GROUP), F32)]

        def tap_input(j, window):
            k = K_CONF - 1 - j
            grad_in[0] = grad_in[0] + ccw_ref[k:k + 1, :] * window

        def tap_filter(k, window):
            dccw_ref[k:k + 1, :] += jnp.sum(dc_blk * window, axis=0, keepdims=True)

        _taps(cbuf, shifted, 0, K_CONF, tb, tap_input)
        _taps(hbuf, shifted, HALO - (K_CONF - 1), K_CONF, tb, tap_filter)
        dhd = grad_in[0]
        dz_ref[:, 8 * D_GROUP:9 * D_GROUP] = (dhd * sg).astype(BF16)
        dz_ref[:, 9 * D_GROUP:10 * D_GROUP] = (dhd * a * sg * (1.0 - sg)).astype(BF16)

    grp = _rows(tb, D_GROUP)
    pair0 = pl.BlockSpec((None, tb, PAIR), lambda i: (0, i, 0))
    pair1 = pl.BlockSpec((None, tb, PAIR), lambda i: (1, i, 0))
    small = [_sds((1, D_GROUP)), _sds((4, CHUNK, CHUNK)), _sds((CHUNK, CHUNK)), _sds((8, D_GROUP)),
             _sds((HALO, D_GROUP)), _sds((1, D_GROUP)), _sds((1, D_GROUP))]
    return _call(
        body, (z, z, z, dya, dyb, dyb, dyd, dyd, conv, conv, dq, dk, dk, dv, dv, vg, wm, wmt, bexp, scw, ccw, lng, lnb), ride,
        name="mix_bwd", grid=(n_steps,),
        in_specs=[_rows(tb, D_IN), prev_spec, next_spec, grp, grp, next_grp, grp, next_grp, grp, next_grp, grp,
                  pair0, pair1, pair0, pair1,
                  _whole((1, D_GROUP)), _whole(wm.shape), _whole(wmt.shape), _whole(bexp.shape), _whole(scw.shape),
                  _whole(ccw.shape), _whole((1, D_GROUP)), _whole((1, D_GROUP))],
        out_specs=[_rows(tb, D_IN)] + [_whole(t.shape) for t in small],
        out_shape=[_sds((s, D_IN), BF16)] + small,
        scratch_shapes=[pltpu.VMEM((HALO + tb, D_GROUP), F32), pltpu.VMEM((HALO + tb, D_GROUP), F32),
                        pltpu.VMEM((ext, D_GROUP), F32), pltpu.VMEM((ext, D_GROUP), F32),
                        pltpu.VMEM((tb, D_GROUP), F32), pltpu.VMEM((tb, D_GROUP), F32),
                        pltpu.VMEM((tb + TAP_SLACK, D_GROUP), F32)],
        compiler_params=_params(("arbitrary",), 48))


def _in_proj_bwd(x, dz, dres, g, wint, tb, ride=None):
    s = x.shape[0]

    def body(x_ref, dz_ref, dr_ref, g_ref, w_ref, dx_ref, dg_ref):
        @pl.when(pl.program_id(0) == 0)
        def _():
            dg_ref[...] = jnp.zeros_like(dg_ref)

        xv = x_ref[...]
        r = _rms(xv)
        xh = xv * r
        dh = _mm(dz_ref[...], w_ref[...])
        dg_ref[...] += jnp.sum(dh * xh, axis=0, keepdims=True)
        dx_ref[...] = dr_ref[...] + _rms_bwd(dh * g_ref[...], xh, r)

    return _call(
        body, (x, dz, dres, g, wint), ride, name="in_proj_bwd", grid=(s // tb,),
        in_specs=[_rows(tb, D_MODEL), _rows(tb, D_IN), _rows(tb, D_MODEL), _whole((1, D_MODEL)), _resident((D_IN, D_MODEL))],
        out_specs=[_rows(tb, D_MODEL), _whole((1, D_MODEL))],
        out_shape=[_sds((s, D_MODEL)), _sds((1, D_MODEL))],
        compiler_params=_params(("arbitrary",), 48))


def _adamw(w, g, m, v):
    m = ADAM_B1 * m + (1.0 - ADAM_B1) * g
    v = ADAM_B2 * v + (1.0 - ADAM_B2) * (g * g)
    m_hat = m / (1.0 - ADAM_B1 ** ADAM_STEP)
    v_hat = v / (1.0 - ADAM_B2 ** ADAM_STEP)
    delta = -ADAM_LR * (m_hat / (jnp.sqrt(v_hat) + ADAM_EPS) + ADAM_WD * w)
    return delta, m, v


def _reduce_adamw(parts, w, m, v, tb, name):
    rows, cols = w.shape

    def body(p_ref, w_ref, m_ref, v_ref, g_ref, d_ref, m2_ref, v2_ref):
        g = p_ref[0]
        for j in range(1, N_DEV):
            g = g + p_ref[j]
        g_ref[...] = g
        d_ref[...], m2_ref[...], v2_ref[...] = _adamw(w_ref[...], g, m_ref[...], v_ref[...])

    blk = _rows(tb, cols)
    return pl.pallas_call(
        body, name=name, grid=(rows // tb,),
        in_specs=[pl.BlockSpec((N_DEV, tb, cols), lambda i: (0, i, 0)), blk, blk, blk],
        out_specs=[blk] * 4, out_shape=[_sds((rows, cols))] * 4,
        compiler_params=_params(("parallel",), 32),
    )(parts, w, m, v)


def _reduce_adamw_layer(parts, w, m, v, layer, taken_over, tb, name):
    _, rows, cols = w.shape
    n_given = 4 if taken_over is None else 8

    def body(*refs):
        p_ref, w_ref, m_ref, v_ref = refs[:4]
        g_ref, d_ref, m2_ref, v2_ref = refs[n_given:]
        g = p_ref[0]
        for j in range(1, N_DEV):
            g = g + p_ref[j]
        g_ref[...] = g
        d_ref[...], m2_ref[...], v2_ref[...] = _adamw(w_ref[...], g, m_ref[...], v_ref[...])

    blk = pl.BlockSpec((None, tb, cols), lambda i: (layer, i, 0))
    return pl.pallas_call(
        body, name=name, grid=(rows // tb,),
        in_specs=[pl.BlockSpec((N_DEV, tb, cols), lambda i: (0, i, 0)), blk, blk, blk] + [ANY_SPEC] * (n_given - 4),
        out_specs=[blk] * 4, out_shape=[_sds(w.shape)] * 4,
        input_output_aliases={} if taken_over is None else {4 + k: k for k in range(4)},
        compiler_params=_params(("arbitrary",), 32),
    )(parts, w, m, v, *(taken_over or ()))


LANES = 128
PACK_ALIGN = 8 * LANES


def _pack(arrays):
    pieces = []
    for a in arrays:
        flat = a.reshape(-1)
        pieces.append(jnp.pad(flat, (0, -flat.shape[0] % PACK_ALIGN)).reshape(-1, LANES))
    return jnp.concatenate(pieces, axis=0)


def _unpack(packed, shapes):
    out, row = [], 0
    for shape in shapes:
        size = 1
        for dim in shape:
            size *= dim
        rows = -(-size // PACK_ALIGN) * 8
        out.append(packed[row:row + rows].reshape(-1)[:size].reshape(shape))
        row += rows
    return out


TB_PROJ = 512
TB_MIX = 256
TB_TN = 1024
TQ = 256
TK = 256
TB_ADAM = 64


def kernel(x, norm_mix_g, w_in, gmlp_v_g, gmlp_w_s, gmlp_b_s, short_conv_w, conf_conv_w, conf_ln_g, conf_ln_b, mix_out_g, w_out, norm_ffn_g, w_up, w_down, final_norm_g, loss_target, m_norm_mix_g, m_w_in, m_gmlp_v_g, m_gmlp_w_s, m_gmlp_b_s, m_short_conv_w, m_conf_conv_w, m_conf_ln_g, m_conf_ln_b, m_mix_out_g, m_w_out, m_norm_ffn_g, m_w_up, m_w_down, m_final_norm_g, v_norm_mix_g, v_w_in, v_gmlp_v_g, v_gmlp_w_s, v_gmlp_b_s, v_short_conv_w, v_conf_conv_w, v_conf_ln_g, v_conf_ln_b, v_mix_out_g, v_w_out, v_norm_ffn_g, v_w_up, v_w_down, v_final_norm_g):
    me = 4 * lax.axis_index("x") + 2 * lax.axis_index("y") + lax.axis_index("c")
    x0, target = x[0], loss_target[0]
    s = x0.shape[0]
    tb_proj, tb_mix, tb_tn = min(TB_PROJ, s), min(TB_MIX, s), min(TB_TN, s)
    conv_cols = D_GROUP // N_DEV

    def pad_rows(a, rows):
        return jnp.pad(a, ((0, rows - a.shape[0]), (0, 0)))

    wint_loc = [w_in[l].T.astype(BF16) for l in range(N_LAYERS)]
    wout_loc = [w_out[l].astype(BF16) for l in range(N_LAYERS)]
    wup_loc = [w_up[l].astype(BF16) for l in range(N_LAYERS)]
    wdn_loc = [w_down[l].astype(BF16) for l in range(N_LAYERS)]
    conv_loc = jnp.concatenate([pad_rows(short_conv_w[l], 8) for l in range(N_LAYERS)]
                               + [pad_rows(conf_conv_w[l], HALO) for l in range(N_LAYERS)], axis=0)
    wint, wout, wup, wdn = [None] * N_LAYERS, [None] * N_LAYERS, [None] * N_LAYERS, [None] * N_LAYERS
    wint0, conv_all = _exchange([wint_loc[0], conv_loc], [GATHER, GATHER], "gather_first_weights")
    wint[0] = wint0.reshape(D_IN, D_MODEL)
    conv_full = conv_all.transpose(1, 0, 2).reshape(-1, D_GROUP)
    scw = [conv_full[8 * l:8 * (l + 1)] for l in range(N_LAYERS)]
    ccw = [conv_full[8 * N_LAYERS + HALO * l:8 * N_LAYERS + HALO * (l + 1)] for l in range(N_LAYERS)]

    tril = jnp.tril(jnp.ones((CHUNK, CHUNK), dtype=bool))
    wm = [jnp.where(tril, gmlp_w_s[l], 0.0).astype(BF16) for l in range(N_LAYERS)]
    wmt = [w.transpose(0, 2, 1) for w in wm]
    bexp = [jnp.repeat(gmlp_b_s[l].T, HEAD_DIM, axis=1) for l in range(N_LAYERS)]

    def row(vec):
        return vec.reshape(1, -1)

    saved = []
    xc = x0
    for l in range(N_LAYERS):
        first = l == 0
        (z, qkv, hb_in), moved = _in_proj_fwd(xc, row(norm_mix_g[l]), wint[l], tb_proj,
                                              ride=([wout_loc[0]], [GATHER]) if first else None)
        if first:
            wout[0] = moved[0].reshape(D_MODEL, D_MODEL)
        (ya, yb, yd, conv), moved = _mix_fwd(z, row(gmlp_v_g[l]), wm[l], bexp[l], scw[l], ccw[l], row(conf_ln_g[l]),
                                             row(conf_ln_b[l]), tb_mix, ride=([wup_loc[0]], [GATHER]) if first else None)
        if first:
            wup[0] = moved[0]
        (yc,), moved = _attn_fwd(qkv, TQ, TK, ride=([wdn_loc[0], wint_loc[1]], [GATHER, GATHER]) if first else None)
        if first:
            wdn[0], wint[1] = moved[0], moved[1].reshape(D_IN, D_MODEL)
        ys = (ya, yb, yc, yd)
        x1 = _out_proj_fwd(ys, xc, row(mix_out_g[l]), wout[l], tb_proj)
        (x2, act), moved = _ffn_fwd(x1, row(norm_ffn_g[l]), wup[l], wdn[l], tb_proj,
                                    ride=([wout_loc[1], wup_loc[1], wdn_loc[1]], [GATHER] * 3) if first else None)
        saved.append((xc, z, qkv, ys, x1, act, hb_in, conv))
        xc = x2
        if first:
            wout[1], wup[1], wdn[1] = moved[0].reshape(D_MODEL, D_MODEL), moved[1], moved[2]
    dx, g_final, loss_part = _loss_head(xc, row(final_norm_g), target, tb_proj)
    loss = lax.psum(loss_part[0, 0], MESH_AXES)

    parts = [None] * (4 * N_LAYERS)
    small_grads = [None] * N_LAYERS
    early_names = ["gmlp_v_g", "gmlp_w_s", "gmlp_b_s", "short_conv_w", "conf_conv_w", "conf_ln_g", "conf_ln_b",
                   "mix_out_g", "norm_ffn_g"]
    for l in reversed(range(N_LAYERS)):
        xin, z, qkv, ys, x1, act, hb_in, conv = saved[l]
        (dx1, hb_ffn, dpre, g_ffn), _ = _ffn_bwd(x1, act, dx, row(norm_ffn_g[l]), wup[l], wdn[l], tb_proj)
        grad_up = _tn_slabs(hb_ffn, dpre, tb_tn, False, "grad_w_up")
        grad_dn = _tn_slabs(dx, act, tb_proj, True, "grad_w_down", square_b=True)
        dya, dyb_mix, dyc, dyd, yn, g_mixout = _out_proj_bwd(dx1, ys, row(mix_out_g[l]), wout[l], tb_proj)
        grad_out = _tn_matmul(yn, dx1, tb_tn, "grad_w_out").reshape(N_DEV, D_MODEL // N_DEV, D_MODEL)
        (dq, dk, dv), moved = _attn_bwd(qkv, dyc, ys[2], TQ, TK, ride=([grad_up, grad_dn], [SCATTER] * 2))
        parts[4 * l + 2], parts[4 * l + 3] = moved
        (dz, g_vg, g_ws, g_bs, g_scw, g_ccw, g_lng, g_lnb), moved = _mix_bwd(
            z, conv, dya, dyb_mix, dyd, dq, dk, dv, row(gmlp_v_g[l]), wm[l], wmt[l], bexp[l], scw[l], ccw[l],
            row(conf_ln_g[l]), row(conf_ln_b[l]), tb_mix, ride=([grad_out], [SCATTER]))
        parts[4 * l + 1] = moved[0]
        grad_in = _tn_matmul(dz, hb_in, tb_tn, "grad_w_in").reshape(N_DEV, D_IN // N_DEV, D_MODEL)
        small_grads[l] = dict(gmlp_v_g=g_vg[0], gmlp_w_s=g_ws, gmlp_b_s=g_bs[:, :4].T, short_conv_w=g_scw[:K_SHORT],
                              conf_conv_w=g_ccw[:K_CONF], conf_ln_g=g_lng[0], conf_ln_b=g_lnb[0],
                              mix_out_g=g_mixout[0], norm_ffn_g=g_ffn[0])
        riders, modes = [grad_in], [SCATTER]
        if l == 0:
            early_list = [jnp.stack([small_grads[k][n] for k in range(N_LAYERS)]) for n in early_names] + [g_final[0]]
            riders, modes = riders + [_pack(early_list)], modes + [GATHER]
        (dx, g_mix), moved = _in_proj_bwd(xin, dz, dx1, row(norm_mix_g[l]), wint[l], tb_proj, ride=(riders, modes))
        parts[4 * l] = moved[0]
        small_grads[l]["norm_mix_g"] = g_mix[0]

    late_list = [jnp.stack([small_grads[l]["norm_mix_g"] for l in range(N_LAYERS)])]
    late_parts = _exchange([_pack(late_list)], [GATHER], "gather_last_grad")[0]
    small_groups = [(early_names + ["final_norm_g"], early_list, moved[1]), (["norm_mix_g"], late_list, late_parts)]

    given = dict(norm_mix_g=(norm_mix_g, m_norm_mix_g, v_norm_mix_g), gmlp_v_g=(gmlp_v_g, m_gmlp_v_g, v_gmlp_v_g),
                 gmlp_w_s=(gmlp_w_s, m_gmlp_w_s, v_gmlp_w_s), gmlp_b_s=(gmlp_b_s, m_gmlp_b_s, v_gmlp_b_s),
                 short_conv_w=(short_conv_w, m_short_conv_w, v_short_conv_w),
                 conf_conv_w=(conf_conv_w, m_conf_conv_w, v_conf_conv_w),
                 conf_ln_g=(conf_ln_g, m_conf_ln_g, v_conf_ln_g), conf_ln_b=(conf_ln_b, m_conf_ln_b, v_conf_ln_b),
                 mix_out_g=(mix_out_g, m_mix_out_g, v_mix_out_g), norm_ffn_g=(norm_ffn_g, m_norm_ffn_g, v_norm_ffn_g),
                 final_norm_g=(final_norm_g, m_final_norm_g, v_final_norm_g))
    sharded_small = ("short_conv_w", "conf_conv_w")

    def widen(a):
        full = jnp.zeros(a.shape[:-1] + (D_GROUP,), a.dtype)
        return lax.dynamic_update_slice(full, a, (0, 0, me * conv_cols))

    small_res = {}
    for names, grads, gathered in small_groups:
        state = [_pack([widen(given[n][k]) if n in sharded_small else given[n][k] for n in names]) for k in range(3)]
        outs = _reduce_adamw(gathered, *state, state[0].shape[0], "adamw_small")
        for kind, packed in zip(("grad", "delta", "new_m", "new_v"), outs):
            for n, val in zip(names, _unpack(packed, [a.shape for a in grads])):
                if n in sharded_small:
                    val = lax.dynamic_slice(val, (0, 0, me * conv_cols), val.shape[:-1] + (conv_cols,))
                small_res[kind, n] = val

    big_names = ["w_in", "w_out", "w_up", "w_down"]
    big_given = dict(w_in=[t.transpose(0, 2, 1) for t in (w_in, m_w_in, v_w_in)], w_out=(w_out, m_w_out, v_w_out),
                     w_up=(w_up, m_w_up, v_w_up), w_down=(w_down, m_w_down, v_w_down))
    big_res = {}
    for j, n in enumerate(big_names):
        outs = None
        for l in range(N_LAYERS):
            outs = _reduce_adamw_layer(parts[4 * l + j], *big_given[n], l, outs, TB_ADAM, "adamw_" + n)
        for kind, out in zip(("grad", "delta", "new_m", "new_v"), outs):
            big_res[kind, n] = out.transpose(0, 2, 1) if n == "w_in" else out

    order = ["norm_mix_g", "w_in", "gmlp_v_g", "gmlp_w_s", "gmlp_b_s", "short_conv_w", "conf_conv_w", "conf_ln_g",
             "conf_ln_b", "mix_out_g", "w_out", "norm_ffn_g", "w_up", "w_down", "final_norm_g"]
    result = [loss, dx.reshape(x.shape)]
    for kind in ("grad", "delta", "new_m", "new_v"):
        for n in order:
            result.append(big_res[kind, n] if n in big_given else small_res[kind, n])
    return tuple(result)
```

```python
import jax
import jax.numpy as jnp
from jax import lax
from jax.experimental import pallas as pl
from jax.experimental.pallas import tpu as pltpu

F32 = jnp.float32
BF16 = jnp.bfloat16

D_MODEL = 1024
D_GROUP = 256
D_IN = 10 * D_GROUP
D_FF = 4 * D_MODEL
N_DEV = 8
N_LAYERS = 2
HEAD_DIM = 64
HEADS_PER_PAIR = 2
PAIR = HEADS_PER_PAIR * HEAD_DIM
CHUNK = 128
K_SHORT = 3
K_CONF = 31
HALO = 32
EPS = 1e-6
ATT_SCALE = HEAD_DIM ** -0.5
LOG_CUT = -104.0
MIB = 2 ** 20

ADAM_LR = 0.001
ADAM_B1 = 0.9
ADAM_B2 = 0.999
ADAM_EPS = 1e-08
ADAM_WD = 0.01
ADAM_STEP = 10

MESH_AXES = ("x", "y", "c")
GELU_C = 0.7978845608028654
GELU_A = 0.044715


def _mm(a, b):
    return jnp.dot(a, b, preferred_element_type=F32)


def _mm_nt(a, b):
    return lax.dot_general(a, b, (((1,), (1,)), ((), ())), preferred_element_type=F32)


def _mm_tn(a, b):
    return lax.dot_general(a, b, (((0,), (0,)), ((), ())), preferred_element_type=F32)


def _rms(x):
    return lax.rsqrt(jnp.mean(x * x, axis=-1, keepdims=True) + EPS)


def _rms_bwd(dy, xh, r):
    return r * (dy - xh * jnp.mean(dy * xh, axis=-1, keepdims=True))


def _sigmoid(x):
    return 1.0 / (1.0 + jnp.exp(-x))


def _whole(shape):
    return pl.BlockSpec(shape, lambda *_: (0,) * len(shape))


def _resident(shape):
    return pl.BlockSpec(shape, lambda *_: (0,) * len(shape), pipeline_mode=pl.Buffered(1))


def _rows(tb, width, col=0):
    return pl.BlockSpec((tb, width), lambda i: (i, col))


def _params(semantics, vmem_mib):
    return pltpu.CompilerParams(dimension_semantics=semantics, vmem_limit_bytes=vmem_mib * MIB)


def _sds(shape, dtype=F32):
    return jax.ShapeDtypeStruct(shape, dtype)


def _split_bf16(v):
    hi = v.astype(BF16)
    lo = (v - hi.astype(F32)).astype(BF16)
    return hi, lo


GATHER, SCATTER = "gather", "scatter"
ANY_SPEC = pl.BlockSpec(memory_space=pl.ANY)


def _exchange_copies(ins, outs, modes, send_sems, recv_sems, local_sems, with_arrivals=True):
    x, y, c = lax.axis_index("x"), lax.axis_index("y"), lax.axis_index("c")
    me = 4 * x + 2 * y + c
    local, sends, arrivals = [], [], []
    for a, mode in enumerate(modes):
        local.append(pltpu.make_async_copy(ins[a].at[me] if mode == SCATTER else ins[a], outs[a].at[me], local_sems.at[a]))
    for k in range(N_DEV - 1):
        flip = k + 1
        peer = (1 - x if flip & 4 else x, 1 - y if flip & 2 else y, 1 - c if flip & 1 else c)
        pf = 4 * peer[0] + 2 * peer[1] + peer[2]
        for a, mode in enumerate(modes):
            src = ins[a].at[pf] if mode == SCATTER else ins[a]
            for dst, group in ((outs[a].at[me], sends), (outs[a].at[pf], arrivals)):
                if group is sends or with_arrivals:
                    group.append(pltpu.make_async_remote_copy(
                        src_ref=src, dst_ref=dst, send_sem=send_sems.at[a, k], recv_sem=recv_sems.at[a, k],
                        device_id=peer, device_id_type=pl.DeviceIdType.MESH))
    return local, sends, arrivals


def _exchange_start(*refs_and_modes):
    local, sends, _ = _exchange_copies(*refs_and_modes, with_arrivals=False)
    for cp in local + sends:
        cp.start()


def _exchange_wait(*refs_and_modes):
    local, sends, arrivals = _exchange_copies(*refs_and_modes)
    for cp in sends:
        cp.wait_send()
    for cp in arrivals:
        cp.wait_recv()
    for cp in local:
        cp.wait()


def _exchange_shapes(arrays, modes):
    out_shape = [_sds(a.shape if mode == SCATTER else (N_DEV,) + a.shape, a.dtype) for a, mode in zip(arrays, modes)]
    n = len(arrays)
    sems = [pltpu.SemaphoreType.DMA((n, N_DEV - 1)), pltpu.SemaphoreType.DMA((n, N_DEV - 1)), pltpu.SemaphoreType.DMA((n,))]
    return out_shape, sems


def _exchange(arrays, modes, name):
    n = len(arrays)
    out_shape, sems = _exchange_shapes(arrays, modes)

    def body(*refs):
        _exchange_start(refs[:n], refs[n:2 * n], modes, *refs[2 * n:])
        _exchange_wait(refs[:n], refs[n:2 * n], modes, *refs[2 * n:])

    return pl.pallas_call(body, name=name, out_shape=out_shape, in_specs=[ANY_SPEC] * n, out_specs=[ANY_SPEC] * n,
                          scratch_shapes=sems)(*arrays)


def _call(body, args, ride, *, name, grid, in_specs, out_specs, out_shape, scratch_shapes=(), compiler_params):
    if ride is None:
        outs = pl.pallas_call(body, name=name, grid=grid, in_specs=in_specs, out_specs=out_specs, out_shape=out_shape,
                              scratch_shapes=scratch_shapes, compiler_params=compiler_params)(*args)
        return outs, []
    arrays, modes = ride
    n, n_in, n_out, n_scratch = len(arrays), len(in_specs), len(out_specs), len(scratch_shapes)
    moved_shape, sems = _exchange_shapes(arrays, modes)
    n_steps = 1
    for g in grid:
        n_steps *= g

    def riding(*refs):
        ins, refs = refs[:n_in], refs[n_in:]
        r_ins, refs = refs[:n], refs[n:]
        outs, refs = refs[:n_out], refs[n_out:]
        r_outs, refs = refs[:n], refs[n:]
        scratch, r_sems = refs[:n_scratch], refs[n_scratch:]
        step = pl.program_id(0)
        for axis in range(1, len(grid)):
            step = step * grid[axis] + pl.program_id(axis)

        @pl.when(step == 0)
        def _():
            _exchange_start(r_ins, r_outs, modes, *r_sems)

        body(*ins, *outs, *scratch)

        @pl.when(step == n_steps - 1)
        def _():
            _exchange_wait(r_ins, r_outs, modes, *r_sems)

    outs = pl.pallas_call(
        riding, name=name, grid=grid, in_specs=list(in_specs) + [ANY_SPEC] * n,
        out_specs=list(out_specs) + [ANY_SPEC] * n, out_shape=list(out_shape) + moved_shape,
        scratch_shapes=list(scratch_shapes) + sems, compiler_params=compiler_params)(*args, *arrays)
    return outs[:n_out], outs[n_out:]


def _in_proj_fwd(x, g, wint, tb, ride=None):
    s = x.shape[0]

    def body(x_ref, g_ref, w_ref, z_ref, qkv_ref, hb_ref):
        xv = x_ref[...]
        h = (xv * _rms(xv) * g_ref[...]).astype(BF16)
        hb_ref[...] = h
        z = _mm_nt(h, w_ref[...])
        z_ref[...] = z
        qkv_ref[:, 0:D_GROUP] = (z[:, 5 * D_GROUP:6 * D_GROUP] * ATT_SCALE).astype(BF16)
        qkv_ref[:, D_GROUP:3 * D_GROUP] = z[:, 6 * D_GROUP:8 * D_GROUP].astype(BF16)

    return _call(
        body, (x, g, wint), ride, name="in_proj_fwd", grid=(s // tb,),
        in_specs=[_rows(tb, D_MODEL), _whole((1, D_MODEL)), _resident((D_IN, D_MODEL))],
        out_specs=[_rows(tb, D_IN), _rows(tb, 3 * D_GROUP), _rows(tb, D_MODEL)],
        out_shape=[_sds((s, D_IN)), _sds((s, 3 * D_GROUP), BF16), _sds((s, D_MODEL), BF16)],
        compiler_params=_params(("arbitrary",), 48))


def _gelu(x):
    return 0.5 * x * (1.0 + jnp.tanh(GELU_C * (x + GELU_A * x * x * x)))


def _gelu_grad(x):
    t = jnp.tanh(GELU_C * (x + GELU_A * x * x * x))
    return 0.5 * (1.0 + t) + 0.5 * x * (1.0 - t * t) * GELU_C * (1.0 + 3.0 * GELU_A * x * x)


def _head_lane(width):
    return lax.broadcasted_iota(jnp.int32, (1, width), 1) // HEAD_DIM


def _gating_chunk(wm_ref, bexp_ref, vc, lane_h):
    f = bexp_ref[...]
    for h in range(D_GROUP // HEAD_DIM):
        f = f + _mm(wm_ref[h], jnp.where(lane_h == h, vc, 0))
    return f


def _halo_specs(s, tb, width_blocks):
    per = tb // HALO
    prev = pl.BlockSpec((HALO, width_blocks), lambda i: (jnp.maximum(i * per - 1, 0), 0))
    nxt = pl.BlockSpec((HALO, width_blocks), lambda i: (jnp.minimum((i + 1) * per, s // HALO - 1), 0))
    return prev, nxt


SUBLANES = 8
TAP_SLACK = 24


def _taps(buf_ref, shifted_ref, first, n_taps, rows, visit):
    for residue in range(SUBLANES):
        taps = [j for j in range(n_taps) if (first + j) % SUBLANES == residue]
        if not taps:
            continue
        lo = first + taps[0]
        span = first + taps[-1] - lo + rows
        shifted_ref[0:span, :] = buf_ref[pl.ds(lo, span), :]
        for j in taps:
            visit(j, shifted_ref[pl.ds(first + j - lo, rows), :])


def _mix_fwd(z, vg, wm, bexp, scw, ccw, lng, lnb, tb, ride=None):
    s = z.shape[0]
    prev_spec, _ = _halo_specs(s, tb, D_IN)

    def body(z_ref, zp_ref, vg_ref, wm_ref, bexp_ref, scw_ref, ccw_ref, lng_ref, lnb_ref,
             ya_ref, yb_ref, yd_ref, c_ref, pbuf, hbuf, shifted):
        keep = (pl.program_id(0) > 0).astype(F32)
        lane_h = _head_lane(D_GROUP)
        ga = _gelu(z_ref[:, 0:2 * D_GROUP])
        u, v = ga[:, :D_GROUP], ga[:, D_GROUP:]
        vn = (v * _rms(v) * vg_ref[...]).astype(BF16)
        for n in range(tb // CHUNK):
            rows = slice(n * CHUNK, (n + 1) * CHUNK)
            ya_ref[rows, :] = u[rows] * _gating_chunk(wm_ref, bexp_ref, vn[rows], lane_h)
        p = z_ref[:, 3 * D_GROUP:4 * D_GROUP] * z_ref[:, 4 * D_GROUP:5 * D_GROUP]
        pbuf[0:HALO, :] = zp_ref[:, 3 * D_GROUP:4 * D_GROUP] * zp_ref[:, 4 * D_GROUP:5 * D_GROUP] * keep
        pbuf[HALO:HALO + tb, :] = p
        cv = scw_ref[K_SHORT - 1:K_SHORT, :] * p
        for k in range(K_SHORT - 1):
            cv = cv + scw_ref[k:k + 1, :] * pbuf[pl.ds(HALO - (K_SHORT - 1) + k, tb), :]
        yb_ref[...] = z_ref[:, 2 * D_GROUP:3 * D_GROUP] * cv
        hbuf[0:HALO, :] = zp_ref[:, 8 * D_GROUP:9 * D_GROUP] * _sigmoid(zp_ref[:, 9 * D_GROUP:10 * D_GROUP]) * keep
        hbuf[HALO:HALO + tb, :] = z_ref[:, 8 * D_GROUP:9 * D_GROUP] * _sigmoid(z_ref[:, 9 * D_GROUP:10 * D_GROUP])
        conv = [jnp.zeros((tb, D_GROUP), F32)]

        def tap(k, window):
            conv[0] = conv[0] + ccw_ref[k:k + 1, :] * window

        _taps(hbuf, shifted, HALO - (K_CONF - 1), K_CONF, tb, tap)
        c = conv[0]
        c_ref[...] = c
        xc = c - jnp.mean(c, axis=-1, keepdims=True)
        ln = xc * lax.rsqrt(jnp.mean(xc * xc, axis=-1, keepdims=True) + EPS) * lng_ref[...] + lnb_ref[...]
        yd_ref[...] = ln * _sigmoid(ln)

    grp = _rows(tb, D_GROUP)
    return _call(
        body, (z, z, vg, wm, bexp, scw, ccw, lng, lnb), ride, name="mix_fwd", grid=(s // tb,),
        in_specs=[_rows(tb, D_IN), prev_spec, _whole((1, D_GROUP)), _whole(wm.shape), _whole(bexp.shape),
                  _whole(scw.shape), _whole(ccw.shape), _whole((1, D_GROUP)), _whole((1, D_GROUP))],
        out_specs=[grp, grp, grp, grp],
        out_shape=[_sds((s, D_GROUP))] * 4,
        scratch_shapes=[pltpu.VMEM((HALO + tb, D_GROUP), F32), pltpu.VMEM((HALO + tb, D_GROUP), F32),
                        pltpu.VMEM((tb + TAP_SLACK, D_GROUP), F32)],
        compiler_params=_params(("arbitrary",), 40))


def _stick_tile(qh, kt, causal, c, upper):
    x = _mm_nt(qh, kt)
    soft = jnp.log(1.0 + jnp.exp(-jnp.abs(x)))
    lb = jnp.minimum(x, 0.0) - soft
    lom = jnp.where(causal, -jnp.maximum(x, 0.0) - soft, 0.0)
    hi, lo = _split_bf16(lom)
    stick = c + _mm(hi, upper[...]) + _mm(lo, upper[...])
    w = jnp.where(causal, jnp.exp(lb + stick), 0.0)
    return w, lb, lom


def _triangle(n, diagonal):
    return jnp.tri(n, n, diagonal, dtype=BF16)


def _causal_tile(qi, tq, k0, tk):
    qpos = qi * tq + lax.broadcasted_iota(jnp.int32, (tq, 1), 0)
    return k0 + lax.broadcasted_iota(jnp.int32, (1, tk), 1) < qpos


def _sticks_alive(cs):
    longest = cs[0]
    for c in cs[1:]:
        longest = jnp.maximum(longest, c)
    return (jnp.max(longest) > LOG_CUT).astype(jnp.int32)


def _walk(body, qi, tq, tk, init):
    start = (((qi + 1) * tq - 1) // tk, jnp.int32(1)) + tuple(init)
    return lax.while_loop(lambda cr: jnp.logical_and(cr[0] >= 0, cr[1] > 0), body, start)[2:]


def _attn_fwd(qkv, tq, tk, ride=None):
    s = qkv.shape[0]
    n_heads = D_GROUP // HEAD_DIM

    def body(q_ref, k_ref, v_ref, upper, o_ref):
        qi = pl.program_id(0)
        q = q_ref[...]
        lane_h = _head_lane(D_GROUP)
        qhs = [jnp.where(lane_h == h, q, 0) for h in range(n_heads)]

        def step(carry):
            kb, _, acc = carry[:3]
            cs = list(carry[3:])
            k0 = pl.multiple_of(kb * tk, tk)
            kt = k_ref[pl.ds(k0, tk), :]
            vt = v_ref[pl.ds(k0, tk), :]
            causal = _causal_tile(qi, tq, k0, tk)
            for h in range(n_heads):
                w, _, lom = _stick_tile(qhs[h], kt, causal, cs[h], upper)
                acc = acc + _mm(w.astype(BF16), jnp.where(lane_h == h, vt, 0))
                cs[h] = cs[h] + jnp.sum(lom, axis=1, keepdims=True)
            return (kb - 1, _sticks_alive(cs), acc) + tuple(cs)

        init = [jnp.zeros((tq, D_GROUP), F32)] + [jnp.zeros((tq, 1), F32)] * n_heads
        o_ref[...] = _walk(step, qi, tq, tk, init)[0]

    return _call(
        body, (qkv, qkv, qkv, _triangle(tk, -1)), ride, name="attn_fwd", grid=(s // tq,),
        in_specs=[pl.BlockSpec((tq, D_GROUP), lambda qi: (qi, 0)),
                  pl.BlockSpec((s, D_GROUP), lambda qi: (0, 1), pipeline_mode=pl.Buffered(1)),
                  pl.BlockSpec((s, D_GROUP), lambda qi: (0, 2), pipeline_mode=pl.Buffered(1)),
                  _resident((tk, tk))],
        out_specs=[pl.BlockSpec((tq, D_GROUP), lambda qi: (qi, 0))],
        out_shape=[_sds((s, D_GROUP))],
        compiler_params=_params(("arbitrary",), 40))


def _out_proj_fwd(ys, x, mg, wout, tb):
    s = x.shape[0]

    def body(ya_ref, yb_ref, yc_ref, yd_ref, x_ref, mg_ref, w_ref, o_ref):
        acc = x_ref[...]
        for gi, y_ref in enumerate((ya_ref, yb_ref, yc_ref, yd_ref)):
            cols = slice(gi * D_GROUP, (gi + 1) * D_GROUP)
            y = y_ref[...]
            acc = acc + _mm((y * _rms(y) * mg_ref[:, cols]).astype(BF16), w_ref[cols, :])
        o_ref[...] = acc

    grp = _rows(tb, D_GROUP)
    return pl.pallas_call(
        body, name="out_proj_fwd", grid=(s // tb,),
        in_specs=[grp, grp, grp, grp, _rows(tb, D_MODEL), _whole((1, D_MODEL)), _whole((D_MODEL, D_MODEL))],
        out_specs=_rows(tb, D_MODEL), out_shape=_sds((s, D_MODEL)),
        compiler_params=_params(("parallel",), 32),
    )(*ys, x, mg, wout)


def _ffn_fwd(x, g, wup, wdn, tb, ride=None):
    s = x.shape[0]
    ff = D_FF // N_DEV

    def body(x_ref, g_ref, wu_ref, wd_ref, o_ref, a_ref):
        xv = x_ref[...]
        h = (xv * _rms(xv) * g_ref[...]).astype(BF16)
        acc = xv
        for d in range(N_DEV):
            a = jnp.maximum(_mm(h, wu_ref[d]), 0.0)
            a_ref[:, d * ff:(d + 1) * ff] = a.astype(BF16)
            acc = acc + _mm((a * a).astype(BF16), wd_ref[d])
        o_ref[...] = acc

    return _call(
        body, (x, g, wup, wdn), ride, name="ffn_fwd", grid=(s // tb,),
        in_specs=[_rows(tb, D_MODEL), _whole((1, D_MODEL)), _resident((N_DEV, D_MODEL, ff)), _resident((N_DEV, ff, D_MODEL))],
        out_specs=[_rows(tb, D_MODEL), _rows(tb, D_FF)], out_shape=[_sds((s, D_MODEL)), _sds((s, D_FF), BF16)],
        compiler_params=_params(("arbitrary",), 56))


def _loss_head(x, g, tgt, tb):
    s = x.shape[0]

    def body(x_ref, g_ref, t_ref, dx_ref, dg_ref, loss_ref):
        @pl.when(pl.program_id(0) == 0)
        def _():
            dg_ref[...] = jnp.zeros_like(dg_ref)
            loss_ref[...] = jnp.zeros_like(loss_ref)

        xv = x_ref[...]
        r = _rms(xv)
        xh = xv * r
        err = xh * g_ref[...] - t_ref[...]
        loss_ref[...] += 0.5 * jnp.sum(jnp.mean(err * err, axis=-1, keepdims=True))
        dy = err * (1.0 / D_MODEL)
        dg_ref[...] += jnp.sum(dy * xh, axis=0, keepdims=True)
        dx_ref[...] = _rms_bwd(dy * g_ref[...], xh, r)

    return pl.pallas_call(
        body, name="loss_head", grid=(s // tb,),
        in_specs=[_rows(tb, D_MODEL), _whole((1, D_MODEL)), _rows(tb, D_MODEL)],
        out_specs=[_rows(tb, D_MODEL), _whole((1, D_MODEL)), _whole((8, 128))],
        out_shape=[_sds((s, D_MODEL)), _sds((1, D_MODEL)), _sds((8, 128))],
        compiler_params=_params(("arbitrary",), 32),
    )(x, g, tgt)


def _ffn_bwd(x1, act, dx2, g, wup, wdn, tb, ride=None):
    s = x1.shape[0]
    ff = D_FF // N_DEV

    def body(x_ref, a_ref, dy_ref, g_ref, wu_ref, wd_ref, dx_ref, hb_ref, dpre_ref, dg_ref):
        @pl.when(pl.program_id(0) == 0)
        def _():
            dg_ref[...] = jnp.zeros_like(dg_ref)

        xv = x_ref[...]
        r = _rms(xv)
        xh = xv * r
        hb_ref[...] = (xh * g_ref[...]).astype(BF16)
        dyv = dy_ref[...]
        dyb = dyv.astype(BF16)
        dh = jnp.zeros((tb, D_MODEL), F32)
        for d in range(N_DEV):
            cols = slice(d * ff, (d + 1) * ff)
            a = a_ref[:, cols].astype(F32)
            dpre = (_mm_nt(dyb, wd_ref[d]) * (2.0 * a)).astype(BF16)
            dpre_ref[:, cols] = dpre
            dh = dh + _mm_nt(dpre, wu_ref[d])
        dg_ref[...] += jnp.sum(dh * xh, axis=0, keepdims=True)
        dx_ref[...] = dyv + _rms_bwd(dh * g_ref[...], xh, r)

    return _call(
        body, (x1, act, dx2, g, wup, wdn), ride, name="ffn_bwd", grid=(s // tb,),
        in_specs=[_rows(tb, D_MODEL), _rows(tb, D_FF), _rows(tb, D_MODEL), _whole((1, D_MODEL)),
                  _resident((N_DEV, D_MODEL, ff)), _resident((N_DEV, ff, D_MODEL))],
        out_specs=[_rows(tb, D_MODEL), _rows(tb, D_MODEL), _rows(tb, D_FF), _whole((1, D_MODEL))],
        out_shape=[_sds((s, D_MODEL)), _sds((s, D_MODEL), BF16), _sds((s, D_FF), BF16), _sds((1, D_MODEL))],
        compiler_params=_params(("arbitrary",), 58))


def _tn_matmul(a, b, tb, name):
    s, ka = a.shape
    nb = b.shape[1]

    def body(a_ref, b_ref, o_ref):
        @pl.when(pl.program_id(0) == 0)
        def _():
            o_ref[...] = jnp.zeros_like(o_ref)

        o_ref[...] += _mm_tn(a_ref[...].astype(BF16), b_ref[...].astype(BF16))

    return pl.pallas_call(
        body, name=name, grid=(s // tb,),
        in_specs=[_rows(tb, ka), _rows(tb, nb)], out_specs=_whole((ka, nb)), out_shape=_sds((ka, nb)),
        compiler_params=_params(("arbitrary",), 56),
    )(a, b)


def _tn_slabs(a, b, tb, transpose_slabs, name, square_b=False):
    s, m = a.shape
    width = b.shape[1] // N_DEV
    n_steps = s // tb
    slab = (width, m) if transpose_slabs else (m, width)

    def body(a_ref, b_ref, o_hbm, acc, stage, sem):
        step = pl.program_id(0)

        @pl.when(step == 0)
        def _():
            acc[...] = jnp.zeros_like(acc)

        bv = b_ref[...]
        if square_b:
            bv = bv.astype(F32)
            bv = bv * bv
        acc[...] += _mm_tn(a_ref[...].astype(BF16), bv.astype(BF16))

        @pl.when(step == n_steps - 1)
        def _():
            for d in range(N_DEV):
                cols = acc.at[:, pl.ds(d * width, width)]
                if transpose_slabs:
                    stage[...] = cols[...].T
                cp = pltpu.make_async_copy(stage if transpose_slabs else cols, o_hbm.at[d], sem.at[0])
                cp.start()
                cp.wait()

    return pl.pallas_call(
        body, name=name, grid=(n_steps,),
        in_specs=[_rows(tb, m), _rows(tb, b.shape[1])], out_specs=ANY_SPEC, out_shape=_sds((N_DEV,) + slab),
        scratch_shapes=[pltpu.VMEM((m, b.shape[1]), F32), pltpu.VMEM(slab, F32), pltpu.SemaphoreType.DMA((1,))],
        compiler_params=_params(("arbitrary",), 56),
    )(a, b)


def _out_proj_bwd(dx1, ys, mg, wout, tb):
    s = dx1.shape[0]

    def body(dx_ref, ya_ref, yb_ref, yc_ref, yd_ref, mg_ref, w_ref,
             dya_ref, dyb_ref, dyc_ref, dyd_ref, yn_ref, dmg_ref):
        @pl.when(pl.program_id(0) == 0)
        def _():
            dmg_ref[...] = jnp.zeros_like(dmg_ref)

        dyn = _mm_nt(dx_ref[...].astype(BF16), w_ref[...])
        groups = ((ya_ref, dya_ref), (yb_ref, dyb_ref), (yc_ref, dyc_ref), (yd_ref, dyd_ref))
        for gi, (y_ref, dy_ref) in enumerate(groups):
            cols = slice(gi * D_GROUP, (gi + 1) * D_GROUP)
            y = y_ref[...]
            r = _rms(y)
            n = y * r
            gain = mg_ref[:, cols]
            dn = dyn[:, cols]
            yn_ref[:, cols] = (n * gain).astype(BF16)
            dmg_ref[:, cols] += jnp.sum(dn * n, axis=0, keepdims=True)
            dy_ref[...] = _rms_bwd(dn * gain, n, r)

    grp = _rows(tb, D_GROUP)
    return pl.pallas_call(
        body, name="out_proj_bwd", grid=(s // tb,),
        in_specs=[_rows(tb, D_MODEL), grp, grp, grp, grp, _whole((1, D_MODEL)), _whole((D_MODEL, D_MODEL))],
        out_specs=[grp, grp, grp, grp, _rows(tb, D_MODEL), _whole((1, D_MODEL))],
        out_shape=[_sds((s, D_GROUP))] * 4 + [_sds((s, D_MODEL), BF16), _sds((1, D_MODEL))],
        compiler_params=_params(("arbitrary",), 32),
    )(dx1, *ys, mg, wout)


def _attn_bwd(qkv, do, o, tq, tk, ride=None):
    s = qkv.shape[0]
    nq = s // tq
    n_pairs = D_GROUP // PAIR

    def body(q_ref, k_ref, v_ref, do_ref, o_ref, upper, upper_eq, dq_ref, dk_hbm, dv_hbm, dk_acc, dv_acc, sems):
        hp, qi = pl.program_id(0), pl.program_id(1)

        @pl.when(qi == 0)
        def _():
            dk_acc[...] = jnp.zeros_like(dk_acc)
            dv_acc[...] = jnp.zeros_like(dv_acc)

        q = q_ref[...]
        dob = do_ref[...].astype(BF16)
        prod = dob.astype(F32) * o_ref[...]
        lane_h = _head_lane(PAIR)
        heads = []
        for h in range(HEADS_PER_PAIR):
            in_head = lane_h == h
            total = jnp.sum(jnp.where(in_head, prod, 0.0), axis=1, keepdims=True)
            heads.append((in_head, jnp.where(in_head, q, 0), jnp.where(in_head, dob, 0), total))

        def step(carry):
            kb, _, acc = carry[:3]
            cs = list(carry[3:3 + HEADS_PER_PAIR])
            nears = list(carry[3 + HEADS_PER_PAIR:])
            k0 = pl.multiple_of(kb * tk, tk)
            kt = k_ref[pl.ds(k0, tk), :]
            vt = v_ref[pl.ds(k0, tk), :]
            causal = _causal_tile(qi, tq, k0, tk)
            dk_t = jnp.zeros((tk, PAIR), F32)
            dv_t = jnp.zeros((tk, PAIR), F32)
            for h, (in_head, qh, doh, total) in enumerate(heads):
                w, lb, lom = _stick_tile(qh, kt, causal, cs[h], upper)
                wb = w.astype(BF16)
                gw = _mm_nt(doh, vt) * wb.astype(F32)
                hi, lo = _split_bf16(gw)
                far = total - nears[h] - _mm(hi, upper_eq[...]) - _mm(lo, upper_eq[...])
                beta = jnp.exp(lb)
                dxb = jnp.where(causal, gw - beta * (gw + far), 0.0).astype(BF16)
                acc = acc + _mm(dxb, jnp.where(in_head, kt, 0))
                dk_t = dk_t + _mm_tn(dxb, qh)
                dv_t = dv_t + _mm_tn(wb, doh)
                cs[h] = cs[h] + jnp.sum(lom, axis=1, keepdims=True)
                nears[h] = nears[h] + jnp.sum(gw, axis=1, keepdims=True)
            dk_acc[pl.ds(k0, tk), :] += dk_t
            dv_acc[pl.ds(k0, tk), :] += dv_t
            return (kb - 1, _sticks_alive(cs), acc) + tuple(cs) + tuple(nears)

        init = [jnp.zeros((tq, PAIR), F32)] + [jnp.zeros((tq, 1), F32)] * (2 * HEADS_PER_PAIR)
        dq_ref[...] = _walk(step, qi, tq, tk, init)[0]

        @pl.when(qi == nq - 1)
        def _():
            ck = pltpu.make_async_copy(dk_acc, dk_hbm.at[hp], sems.at[0])
            cv = pltpu.make_async_copy(dv_acc, dv_hbm.at[hp], sems.at[1])
            ck.start()
            cv.start()
            ck.wait()
            cv.wait()

    blk = pl.BlockSpec((tq, PAIR), lambda hp, qi: (qi, hp))
    return _call(
        body, (qkv, qkv, qkv, do, o, _triangle(tk, -1), _triangle(tk, 0)), ride, name="attn_bwd", grid=(n_pairs, nq),
        in_specs=[blk, pl.BlockSpec((s, PAIR), lambda hp, qi: (0, 2 + hp)),
                  pl.BlockSpec((s, PAIR), lambda hp, qi: (0, 4 + hp)), blk, blk, _resident((tk, tk)), _resident((tk, tk))],
        out_specs=[blk, ANY_SPEC, ANY_SPEC],
        out_shape=[_sds((s, D_GROUP)), _sds((n_pairs, s, PAIR)), _sds((n_pairs, s, PAIR))],
        scratch_shapes=[pltpu.VMEM((s, PAIR), F32), pltpu.VMEM((s, PAIR), F32), pltpu.SemaphoreType.DMA((2,))],
        compiler_params=_params(("arbitrary", "arbitrary"), 56))


def _mix_bwd(z, conv, dya, dyb, dyd, dq, dk, dv, hb, vg, wm, wmt, bexp, scw, ccw, lng, lnb, tb, ride=None):
    s = z.shape[0]
    n_steps = s // tb
    prev_spec, next_spec = _halo_specs(s, tb, D_IN)
    _, next_grp = _halo_specs(s, tb, D_GROUP)
    ext = tb + HALO

    def body(z_ref, zp_ref, zn_ref, dya_ref, dyb_ref, dybn_ref, dyd_ref, dydn_ref, c_ref, cn_ref, dq_ref, dk0_ref, dk1_ref,
             dv0_ref, dv1_ref, hbp_ref, hb_ref, vg_ref, wm_ref, wmt_ref, bexp_ref, scw_ref, ccw_ref, lng_ref, lnb_ref,
             dz_ref, dvg_ref, dws_ref, dbs_ref, dscw_ref, dccw_ref, dlng_ref, dlnb_ref, dwin_hbm,
             pbuf, hbuf, gbuf, cbuf, dubuf, dvnbuf, shifted, win_acc, dz_prev, win_sem):
        i = pl.program_id(0)

        @pl.when(i == 0)
        def _():
            for ref in (dvg_ref, dws_ref, dbs_ref, dscw_ref, dccw_ref, dlng_ref, dlnb_ref, win_acc, dz_prev):
                ref[...] = jnp.zeros_like(ref)

        win_acc[...] += _mm_tn(dz_prev[...], hbp_ref[...])

        keep_prev = (i > 0).astype(F32)
        keep_next = (i < n_steps - 1).astype(F32)
        lane_h = _head_lane(D_GROUP)

        za = z_ref[:, 0:2 * D_GROUP]
        ga = _gelu(za)
        u, v = ga[:, :D_GROUP], ga[:, D_GROUP:]
        r = _rms(v)
        vh = v * r
        vn = (vh * vg_ref[...]).astype(BF16)
        tril = lax.broadcasted_iota(jnp.int32, (CHUNK, CHUNK), 0) >= lax.broadcasted_iota(jnp.int32, (CHUNK, CHUNK), 1)
        dbias = jnp.zeros((CHUNK, D_GROUP), F32)
        for n in range(tb // CHUNK):
            rows = slice(n * CHUNK, (n + 1) * CHUNK)
            vc = vn[rows]
            dy = dya_ref[rows, :]
            dubuf[rows, :] = dy * _gating_chunk(wm_ref, bexp_ref, vc, lane_h)
            df = dy * u[rows]
            dfb = df.astype(BF16)
            dvn = jnp.zeros((CHUNK, D_GROUP), F32)
            for h in range(D_GROUP // HEAD_DIM):
                dfh = jnp.where(lane_h == h, dfb, 0)
                dvn = dvn + _mm(wmt_ref[h], dfh)
                dws_ref[h] += jnp.where(tril, _mm_nt(dfh, vc), 0.0)
            dvnbuf[rows, :] = dvn
            dbias = dbias + df
        for h in range(D_GROUP // HEAD_DIM):
            per_head = jnp.sum(jnp.where(lane_h == h, dbias, 0.0), axis=1, keepdims=True)
            dbs_ref[...] += per_head * (lax.broadcasted_iota(jnp.int32, (1, CHUNK), 1) == h).astype(F32)
        dvn = dvnbuf[...]
        dvg_ref[...] += jnp.sum(dvn * vh, axis=0, keepdims=True)
        dgelu = _gelu_grad(za)
        dz_ref[:, 0:D_GROUP] = (dubuf[...] * dgelu[:, :D_GROUP]).astype(BF16)
        dz_ref[:, D_GROUP:2 * D_GROUP] = (_rms_bwd(dvn * vg_ref[...], vh, r) * dgelu[:, D_GROUP:]).astype(BF16)

        gate_b = z_ref[:, 2 * D_GROUP:3 * D_GROUP]
        gate_c = z_ref[:, 3 * D_GROUP:4 * D_GROUP]
        hh = z_ref[:, 4 * D_GROUP:5 * D_GROUP]
        p = gate_c * hh
        pbuf[0:HALO, :] = zp_ref[:, 3 * D_GROUP:4 * D_GROUP] * zp_ref[:, 4 * D_GROUP:5 * D_GROUP] * keep_prev
        pbuf[HALO:HALO + tb, :] = p
        dyb_v = dyb_ref[...]
        dcv = dyb_v * gate_b
        gbuf[0:tb, :] = dcv
        gbuf[tb:ext, :] = dybn_ref[...] * zn_ref[:, 2 * D_GROUP:3 * D_GROUP] * keep_next
        cv = scw_ref[K_SHORT - 1:K_SHORT, :] * p
        dp = scw_ref[K_SHORT - 1:K_SHORT, :] * dcv
        dscw_ref[K_SHORT - 1:K_SHORT, :] += jnp.sum(dcv * p, axis=0, keepdims=True)
        for k in range(K_SHORT - 1):
            earlier = pbuf[pl.ds(HALO - (K_SHORT - 1) + k, tb), :]
            cv = cv + scw_ref[k:k + 1, :] * earlier
            dp = dp + scw_ref[k:k + 1, :] * gbuf[pl.ds(K_SHORT - 1 - k, tb), :]
            dscw_ref[k:k + 1, :] += jnp.sum(dcv * earlier, axis=0, keepdims=True)
        dz_ref[:, 2 * D_GROUP:3 * D_GROUP] = (dyb_v * cv).astype(BF16)
        dz_ref[:, 3 * D_GROUP:4 * D_GROUP] = (dp * hh).astype(BF16)
        dz_ref[:, 4 * D_GROUP:5 * D_GROUP] = (dp * gate_c).astype(BF16)

        dz_ref[:, 5 * D_GROUP:6 * D_GROUP] = (dq_ref[...] * ATT_SCALE).astype(BF16)
        dz_ref[:, 6 * D_GROUP:6 * D_GROUP + PAIR] = dk0_ref[...].astype(BF16)
        dz_ref[:, 6 * D_GROUP + PAIR:7 * D_GROUP] = dk1_ref[...].astype(BF16)
        dz_ref[:, 7 * D_GROUP:7 * D_GROUP + PAIR] = dv0_ref[...].astype(BF16)
        dz_ref[:, 7 * D_GROUP + PAIR:8 * D_GROUP] = dv1_ref[...].astype(BF16)

        a = z_ref[:, 8 * D_GROUP:9 * D_GROUP]
        sg = _sigmoid(z_ref[:, 9 * D_GROUP:10 * D_GROUP])
        hbuf[0:HALO, :] = zp_ref[:, 8 * D_GROUP:9 * D_GROUP] * _sigmoid(zp_ref[:, 9 * D_GROUP:10 * D_GROUP]) * keep_prev
        hbuf[HALO:HALO + tb, :] = a * sg
        c = jnp.concatenate([c_ref[...], cn_ref[...]], axis=0)
        xc = c - jnp.mean(c, axis=-1, keepdims=True)
        rs = lax.rsqrt(jnp.mean(xc * xc, axis=-1, keepdims=True) + EPS)
        xh = xc * rs
        ln = xh * lng_ref[...] + lnb_ref[...]
        sl = _sigmoid(ln)
        dy_ext = jnp.concatenate([dyd_ref[...], dydn_ref[...] * keep_next], axis=0)
        dln = dy_ext * sl * (1.0 + ln * (1.0 - sl))
        dlng_ref[...] += jnp.sum(dln[:tb] * xh[:tb], axis=0, keepdims=True)
        dlnb_ref[...] += jnp.sum(dln[:tb], axis=0, keepdims=True)
        dxh = dln * lng_ref[...]
        dc = rs * (dxh - jnp.mean(dxh, axis=-1, keepdims=True) - xh * jnp.mean(dxh * xh, axis=-1, keepdims=True))
        cbuf[...] = dc
        dc_blk = dc[:tb]
        grad_in = [jnp.zeros((tb, D_GROUP), F32)]

        def tap_input(j, window):
            k = K_CONF - 1 - j
            grad_in[0] = grad_in[0] + ccw_ref[k:k + 1, :] * window

        def tap_filter(k, window):
            dccw_ref[k:k + 1, :] += jnp.sum(dc_blk * window, axis=0, keepdims=True)

        _taps(cbuf, shifted, 0, K_CONF, tb, tap_input)
        _taps(hbuf, shifted, HALO - (K_CONF - 1), K_CONF, tb, tap_filter)
        dhd = grad_in[0]
        dz_ref[:, 8 * D_GROUP:9 * D_GROUP] = (dhd * sg).astype(BF16)
        dz_ref[:, 9 * D_GROUP:10 * D_GROUP] = (dhd * a * sg * (1.0 - sg)).astype(BF16)

        dz_prev[...] = dz_ref[...]

        @pl.when(i == n_steps - 1)
        def _():
            win_acc[...] += _mm_tn(dz_ref[...], hb_ref[...])
            cp = pltpu.make_async_copy(win_acc, dwin_hbm, win_sem.at[0])
            cp.start()
            cp.wait()

    grp = _rows(tb, D_GROUP)
    pair0 = pl.BlockSpec((None, tb, PAIR), lambda i: (0, i, 0))
    pair1 = pl.BlockSpec((None, tb, PAIR), lambda i: (1, i, 0))
    small = [_sds((1, D_GROUP)), _sds((4, CHUNK, CHUNK)), _sds((CHUNK, CHUNK)), _sds((8, D_GROUP)),
             _sds((HALO, D_GROUP)), _sds((1, D_GROUP)), _sds((1, D_GROUP))]
    return _call(
        body, (z, z, z, dya, dyb, dyb, dyd, dyd, conv, conv, dq, dk, dk, dv, dv, hb, hb, vg, wm, wmt, bexp, scw, ccw, lng, lnb),
        ride, name="mix_bwd", grid=(n_steps,),
        in_specs=[_rows(tb, D_IN), prev_spec, next_spec, grp, grp, next_grp, grp, next_grp, grp, next_grp, grp,
                  pair0, pair1, pair0, pair1,
                  pl.BlockSpec((tb, D_MODEL), lambda i: (jnp.maximum(i - 1, 0), 0)), _rows(tb, D_MODEL),
                  _whole((1, D_GROUP)), _whole(wm.shape), _whole(wmt.shape), _whole(bexp.shape), _whole(scw.shape),
                  _whole(ccw.shape), _whole((1, D_GROUP)), _whole((1, D_GROUP))],
        out_specs=[_rows(tb, D_IN)] + [_whole(t.shape) for t in small] + [ANY_SPEC],
        out_shape=[_sds((s, D_IN), BF16)] + small + [_sds((D_IN, D_MODEL))],
        scratch_shapes=[pltpu.VMEM((HALO + tb, D_GROUP), F32), pltpu.VMEM((HALO + tb, D_GROUP), F32),
                        pltpu.VMEM((ext, D_GROUP), F32), pltpu.VMEM((ext, D_GROUP), F32),
                        pltpu.VMEM((tb, D_GROUP), F32), pltpu.VMEM((tb, D_GROUP), F32),
                        pltpu.VMEM((tb + TAP_SLACK, D_GROUP), F32), pltpu.VMEM((D_IN, D_MODEL), F32),
                        pltpu.VMEM((tb, D_IN), BF16), pltpu.SemaphoreType.DMA((1,))],
        compiler_params=_params(("arbitrary",), 56))


def _in_proj_bwd(x, dz, dres, g, wint, tb, ride=None):
    s = x.shape[0]

    def body(x_ref, dz_ref, dr_ref, g_ref, w_ref, dx_ref, dg_ref):
        @pl.when(pl.program_id(0) == 0)
        def _():
            dg_ref[...] = jnp.zeros_like(dg_ref)

        xv = x_ref[...]
        r = _rms(xv)
        xh = xv * r
        dh = _mm(dz_ref[...], w_ref[...])
        dg_ref[...] += jnp.sum(dh * xh, axis=0, keepdims=True)
        dx_ref[...] = dr_ref[...] + _rms_bwd(dh * g_ref[...], xh, r)

    return _call(
        body, (x, dz, dres, g, wint), ride, name="in_proj_bwd", grid=(s // tb,),
        in_specs=[_rows(tb, D_MODEL), _rows(tb, D_IN), _rows(tb, D_MODEL), _whole((1, D_MODEL)), _resident((D_IN, D_MODEL))],
        out_specs=[_rows(tb, D_MODEL), _whole((1, D_MODEL))],
        out_shape=[_sds((s, D_MODEL)), _sds((1, D_MODEL))],
        compiler_params=_params(("arbitrary",), 48))


def _adamw(w, g, m, v):
    m = ADAM_B1 * m + (1.0 - ADAM_B1) * g
    v = ADAM_B2 * v + (1.0 - ADAM_B2) * (g * g)
    m_hat = m / (1.0 - ADAM_B1 ** ADAM_STEP)
    v_hat = v / (1.0 - ADAM_B2 ** ADAM_STEP)
    delta = -ADAM_LR * (m_hat / (jnp.sqrt(v_hat) + ADAM_EPS) + ADAM_WD * w)
    return delta, m, v


def _reduce_adamw(parts, w, m, v, tb, name):
    rows, cols = w.shape

    def body(p_ref, w_ref, m_ref, v_ref, g_ref, d_ref, m2_ref, v2_ref):
        g = p_ref[0].astype(F32)
        for j in range(1, N_DEV):
            g = g + p_ref[j].astype(F32)
        g_ref[...] = g
        d_ref[...], m2_ref[...], v2_ref[...] = _adamw(w_ref[...], g, m_ref[...], v_ref[...])

    blk = _rows(tb, cols)
    return pl.pallas_call(
        body, name=name, grid=(rows // tb,),
        in_specs=[pl.BlockSpec((N_DEV, tb, cols), lambda i: (0, i, 0)), blk, blk, blk],
        out_specs=[blk] * 4, out_shape=[_sds((rows, cols))] * 4,
        compiler_params=_params(("parallel",), 32),
    )(parts, w, m, v)


def _reduce_adamw_layer(parts, w, m, v, layer, taken_over, tb, name):
    _, rows, cols = w.shape
    n_given = 4 if taken_over is None else 8

    def body(*refs):
        p_ref, w_ref, m_ref, v_ref = refs[:4]
        g_ref, d_ref, m2_ref, v2_ref = refs[n_given:]
        g = p_ref[0].astype(F32)
        for j in range(1, N_DEV):
            g = g + p_ref[j].astype(F32)
        g_ref[...] = g
        d_ref[...], m2_ref[...], v2_ref[...] = _adamw(w_ref[...], g, m_ref[...], v_ref[...])

    blk = pl.BlockSpec((None, tb, cols), lambda i: (layer, i, 0))
    return pl.pallas_call(
        body, name=name, grid=(rows // tb,),
        in_specs=[pl.BlockSpec((N_DEV, tb, cols), lambda i: (0, i, 0)), blk, blk, blk] + [ANY_SPEC] * (n_given - 4),
        out_specs=[blk] * 4, out_shape=[_sds(w.shape)] * 4,
        input_output_aliases={} if taken_over is None else {4 + k: k for k in range(4)},
        compiler_params=_params(("arbitrary",), 32),
    )(parts, w, m, v, *(taken_over or ()))


LANES = 128
PACK_ALIGN = 8 * LANES


def _pack(arrays):
    pieces = []
    for a in arrays:
        flat = a.reshape(-1)
        pieces.append(jnp.pad(flat, (0, -flat.shape[0] % PACK_ALIGN)).reshape(-1, LANES))
    return jnp.concatenate(pieces, axis=0)


def _unpack(packed, shapes):
    out, row = [], 0
    for shape in shapes:
        size = 1
        for dim in shape:
            size *= dim
        rows = -(-size // PACK_ALIGN) * 8
        out.append(packed[row:row + rows].reshape(-1)[:size].reshape(shape))
        row += rows
    return out


TB_PROJ = 512
TB_MIX = 256
TB_TN = 1024
TQ = 256
TK = 256
TB_ADAM = 64


def kernel(x, norm_mix_g, w_in, gmlp_v_g, gmlp_w_s, gmlp_b_s, short_conv_w, conf_conv_w, conf_ln_g, conf_ln_b, mix_out_g, w_out, norm_ffn_g, w_up, w_down, final_norm_g, loss_target, m_norm_mix_g, m_w_in, m_gmlp_v_g, m_gmlp_w_s, m_gmlp_b_s, m_short_conv_w, m_conf_conv_w, m_conf_ln_g, m_conf_ln_b, m_mix_out_g, m_w_out, m_norm_ffn_g, m_w_up, m_w_down, m_final_norm_g, v_norm_mix_g, v_w_in, v_gmlp_v_g, v_gmlp_w_s, v_gmlp_b_s, v_short_conv_w, v_conf_conv_w, v_conf_ln_g, v_conf_ln_b, v_mix_out_g, v_w_out, v_norm_ffn_g, v_w_up, v_w_down, v_final_norm_g):
    me = 4 * lax.axis_index("x") + 2 * lax.axis_index("y") + lax.axis_index("c")
    x0, target = x[0], loss_target[0]
    s = x0.shape[0]
    tb_proj, tb_mix, tb_tn = min(TB_PROJ, s), min(TB_MIX, s), min(TB_TN, s)
    conv_cols = D_GROUP // N_DEV

    def pad_rows(a, rows):
        return jnp.pad(a, ((0, rows - a.shape[0]), (0, 0)))

    wint_loc = [w_in[l].T.astype(BF16) for l in range(N_LAYERS)]
    wout_loc = [w_out[l].astype(BF16) for l in range(N_LAYERS)]
    wup_loc = [w_up[l].astype(BF16) for l in range(N_LAYERS)]
    wdn_loc = [w_down[l].astype(BF16) for l in range(N_LAYERS)]
    conv_loc = jnp.concatenate([pad_rows(short_conv_w[l], 8) for l in range(N_LAYERS)]
                               + [pad_rows(conf_conv_w[l], HALO) for l in range(N_LAYERS)], axis=0)
    wint, wout, wup, wdn = [None] * N_LAYERS, [None] * N_LAYERS, [None] * N_LAYERS, [None] * N_LAYERS
    wint0, conv_all = _exchange([wint_loc[0], conv_loc], [GATHER, GATHER], "gather_first_weights")
    wint[0] = wint0.reshape(D_IN, D_MODEL)
    conv_full = conv_all.transpose(1, 0, 2).reshape(-1, D_GROUP)
    scw = [conv_full[8 * l:8 * (l + 1)] for l in range(N_LAYERS)]
    ccw = [conv_full[8 * N_LAYERS + HALO * l:8 * N_LAYERS + HALO * (l + 1)] for l in range(N_LAYERS)]

    tril = jnp.tril(jnp.ones((CHUNK, CHUNK), dtype=bool))
    wm = [jnp.where(tril, gmlp_w_s[l], 0.0).astype(BF16) for l in range(N_LAYERS)]
    wmt = [w.transpose(0, 2, 1) for w in wm]
    bexp = [jnp.repeat(gmlp_b_s[l].T, HEAD_DIM, axis=1) for l in range(N_LAYERS)]

    def row(vec):
        return vec.reshape(1, -1)

    saved = []
    xc = x0
    for l in range(N_LAYERS):
        first = l == 0
        (z, qkv, hb_in), moved = _in_proj_fwd(xc, row(norm_mix_g[l]), wint[l], tb_proj,
                                              ride=([wout_loc[0]], [GATHER]) if first else None)
        if first:
            wout[0] = moved[0].reshape(D_MODEL, D_MODEL)
        (ya, yb, yd, conv), moved = _mix_fwd(z, row(gmlp_v_g[l]), wm[l], bexp[l], scw[l], ccw[l], row(conf_ln_g[l]),
                                             row(conf_ln_b[l]), tb_mix, ride=([wup_loc[0]], [GATHER]) if first else None)
        if first:
            wup[0] = moved[0]
        (yc,), moved = _attn_fwd(qkv, TQ, TK, ride=([wdn_loc[0], wint_loc[1]], [GATHER, GATHER]) if first else None)
        if first:
            wdn[0], wint[1] = moved[0], moved[1].reshape(D_IN, D_MODEL)
        ys = (ya, yb, yc, yd)
        x1 = _out_proj_fwd(ys, xc, row(mix_out_g[l]), wout[l], tb_proj)
        (x2, act), moved = _ffn_fwd(x1, row(norm_ffn_g[l]), wup[l], wdn[l], tb_proj,
                                    ride=([wout_loc[1], wup_loc[1], wdn_loc[1]], [GATHER] * 3) if first else None)
        saved.append((xc, z, qkv, ys, x1, act, hb_in, conv))
        xc = x2
        if first:
            wout[1], wup[1], wdn[1] = moved[0].reshape(D_MODEL, D_MODEL), moved[1], moved[2]
    dx, g_final, loss_part = _loss_head(xc, row(final_norm_g), target, tb_proj)
    loss = lax.psum(loss_part[0, 0], MESH_AXES)

    parts = [None] * (4 * N_LAYERS)
    small_grads = [None] * N_LAYERS
    early_names = ["gmlp_v_g", "gmlp_w_s", "gmlp_b_s", "short_conv_w", "conf_conv_w", "conf_ln_g", "conf_ln_b",
                   "mix_out_g", "norm_ffn_g"]
    for l in reversed(range(N_LAYERS)):
        xin, z, qkv, ys, x1, act, hb_in, conv = saved[l]
        (dx1, hb_ffn, dpre, g_ffn), _ = _ffn_bwd(x1, act, dx, row(norm_ffn_g[l]), wup[l], wdn[l], tb_proj)
        grad_up = _tn_slabs(hb_ffn, dpre, tb_tn, False, "grad_w_up")
        grad_dn = _tn_slabs(dx, act, tb_proj, True, "grad_w_down", square_b=True)
        dya, dyb_mix, dyc, dyd, yn, g_mixout = _out_proj_bwd(dx1, ys, row(mix_out_g[l]), wout[l], tb_proj)
        grad_out = _tn_matmul(yn, dx1, tb_tn, "grad_w_out").reshape(N_DEV, D_MODEL // N_DEV, D_MODEL)
        (dq, dk, dv), moved = _attn_bwd(qkv, dyc, ys[2], TQ, TK, ride=([grad_up, grad_dn], [SCATTER] * 2))
        parts[4 * l + 2], parts[4 * l + 3] = moved
        (dz, g_vg, g_ws, g_bs, g_scw, g_ccw, g_lng, g_lnb, grad_in), moved = _mix_bwd(
            z, conv, dya, dyb_mix, dyd, dq, dk, dv, hb_in, row(gmlp_v_g[l]), wm[l], wmt[l], bexp[l], scw[l], ccw[l],
            row(conf_ln_g[l]), row(conf_ln_b[l]), tb_mix, ride=([grad_out], [SCATTER]))
        parts[4 * l + 1] = moved[0]
        small_grads[l] = dict(gmlp_v_g=g_vg[0], gmlp_w_s=g_ws, gmlp_b_s=g_bs[:, :4].T, short_conv_w=g_scw[:K_SHORT],
                              conf_conv_w=g_ccw[:K_CONF], conf_ln_g=g_lng[0], conf_ln_b=g_lnb[0],
                              mix_out_g=g_mixout[0], norm_ffn_g=g_ffn[0])
        riders, modes = [grad_in.reshape(N_DEV, D_IN // N_DEV, D_MODEL)], [SCATTER]
        if l == 0:
            early_list = [jnp.stack([small_grads[k][n] for k in range(N_LAYERS)]) for n in early_names] + [g_final[0]]
            riders, modes = [riders[0].astype(BF16), _pack(early_list).astype(BF16)], modes + [GATHER]
        (dx, g_mix), moved = _in_proj_bwd(xin, dz, dx1, row(norm_mix_g[l]), wint[l], tb_proj, ride=(riders, modes))
        parts[4 * l] = moved[0]
        small_grads[l]["norm_mix_g"] = g_mix[0]

    late_list = [jnp.stack([small_grads[l]["norm_mix_g"] for l in range(N_LAYERS)])]
    late_parts = _exchange([_pack(late_list)], [GATHER], "gather_last_grad")[0]
    small_groups = [(early_names + ["final_norm_g"], early_list, moved[1]), (["norm_mix_g"], late_list, late_parts)]

    given = dict(norm_mix_g=(norm_mix_g, m_norm_mix_g, v_norm_mix_g), gmlp_v_g=(gmlp_v_g, m_gmlp_v_g, v_gmlp_v_g),
                 gmlp_w_s=(gmlp_w_s, m_gmlp_w_s, v_gmlp_w_s), gmlp_b_s=(gmlp_b_s, m_gmlp_b_s, v_gmlp_b_s),
                 short_conv_w=(short_conv_w, m_short_conv_w, v_short_conv_w),
                 conf_conv_w=(conf_conv_w, m_conf_conv_w, v_conf_conv_w),
                 conf_ln_g=(conf_ln_g, m_conf_ln_g, v_conf_ln_g), conf_ln_b=(conf_ln_b, m_conf_ln_b, v_conf_ln_b),
                 mix_out_g=(mix_out_g, m_mix_out_g, v_mix_out_g), norm_ffn_g=(norm_ffn_g, m_norm_ffn_g, v_norm_ffn_g),
                 final_norm_g=(final_norm_g, m_final_norm_g, v_final_norm_g))
    sharded_small = ("short_conv_w", "conf_conv_w")

    def widen(a):
        full = jnp.zeros(a.shape[:-1] + (D_GROUP,), a.dtype)
        return lax.dynamic_update_slice(full, a, (0, 0, me * conv_cols))

    small_res = {}
    for names, grads, gathered in small_groups:
        state = [_pack([widen(given[n][k]) if n in sharded_small else given[n][k] for n in names]) for k in range(3)]
        outs = _reduce_adamw(gathered, *state, state[0].shape[0], "adamw_small")
        for kind, packed in zip(("grad", "delta", "new_m", "new_v"), outs):
            for n, val in zip(names, _unpack(packed, [a.shape for a in grads])):
                if n in sharded_small:
                    val = lax.dynamic_slice(val, (0, 0, me * conv_cols), val.shape[:-1] + (conv_cols,))
                small_res[kind, n] = val

    big_names = ["w_in", "w_out", "w_up", "w_down"]
    big_given = dict(w_in=[t.transpose(0, 2, 1) for t in (w_in, m_w_in, v_w_in)], w_out=(w_out, m_w_out, v_w_out),
                     w_up=(w_up, m_w_up, v_w_up), w_down=(w_down, m_w_down, v_w_down))
    big_res = {}
    for j, n in enumerate(big_names):
        outs = None
        for l in range(N_LAYERS):
            outs = _reduce_adamw_layer(parts[4 * l + j], *big_given[n], l, outs, TB_ADAM, "adamw_" + n)
        for kind, out in zip(("grad", "delta", "new_m", "new_v"), outs):
            big_res[kind, n] = out.transpose(0, 2, 1) if n == "w_in" else out

    order = ["norm_mix_g", "w_in", "gmlp_v_g", "gmlp_w_s", "gmlp_b_s", "short_conv_w", "conf_conv_w", "conf_ln_g",
             "conf_ln_b", "mix_out_g", "w_out", "norm_ffn_g", "w_up", "w_down", "final_norm_g"]
    result = [loss, dx.reshape(x.shape)]
    for kind in ("grad", "delta", "new_m", "new_v"):
        for n in order:
            result.append(big_res[kind, n] if n in big_given else small_res[kind, n])
    return tuple(result)
```

```python
import jax
import jax.numpy as jnp
from jax import lax
from jax.experimental import pallas as pl
from jax.experimental.pallas import tpu as pltpu

F32 = jnp.float32
BF16 = jnp.bfloat16

D_MODEL = 1024
D_GROUP = 256
D_IN = 10 * D_GROUP
D_FF = 4 * D_MODEL
N_DEV = 8
N_LAYERS = 2
HEAD_DIM = 64
HEADS_PER_PAIR = 2
PAIR = HEADS_PER_PAIR * HEAD_DIM
CHUNK = 128
K_SHORT = 3
K_CONF = 31
HALO = 32
EPS = 1e-6
ATT_SCALE = HEAD_DIM ** -0.5
LOG_CUT = -104.0
MIB = 2 ** 20

ADAM_LR = 0.001
ADAM_B1 = 0.9
ADAM_B2 = 0.999
ADAM_EPS = 1e-08
ADAM_WD = 0.01
ADAM_STEP = 10

GELU_C = 0.7978845608028654
GELU_A = 0.044715


def _mm(a, b):
    return jnp.dot(a, b, preferred_element_type=F32)


def _mm_nt(a, b):
    return lax.dot_general(a, b, (((1,), (1,)), ((), ())), preferred_element_type=F32)


def _mm_tn(a, b):
    return lax.dot_general(a, b, (((0,), (0,)), ((), ())), preferred_element_type=F32)


def _rms(x):
    return lax.rsqrt(jnp.mean(x * x, axis=-1, keepdims=True) + EPS)


def _rms_bwd(dy, xh, r):
    return r * (dy - xh * jnp.mean(dy * xh, axis=-1, keepdims=True))


def _sigmoid(x):
    return 1.0 / (1.0 + jnp.exp(-x))


def _whole(shape):
    return pl.BlockSpec(shape, lambda *_: (0,) * len(shape))


def _resident(shape):
    return pl.BlockSpec(shape, lambda *_: (0,) * len(shape), pipeline_mode=pl.Buffered(1))


def _rows(tb, width, col=0):
    return pl.BlockSpec((tb, width), lambda i: (i, col))


def _params(semantics, vmem_mib):
    return pltpu.CompilerParams(dimension_semantics=semantics, vmem_limit_bytes=vmem_mib * MIB)


def _sds(shape, dtype=F32):
    return jax.ShapeDtypeStruct(shape, dtype)


def _split_bf16(v):
    hi = v.astype(BF16)
    lo = (v - hi.astype(F32)).astype(BF16)
    return hi, lo


GATHER, SCATTER = "gather", "scatter"
ANY_SPEC = pl.BlockSpec(memory_space=pl.ANY)


def _exchange_copies(ins, outs, modes, send_sems, recv_sems, local_sems, with_arrivals=True):
    x, y, c = lax.axis_index("x"), lax.axis_index("y"), lax.axis_index("c")
    me = 4 * x + 2 * y + c
    local, sends, arrivals = [], [], []
    for a, mode in enumerate(modes):
        local.append(pltpu.make_async_copy(ins[a].at[me] if mode == SCATTER else ins[a], outs[a].at[me], local_sems.at[a]))
    for k in range(N_DEV - 1):
        flip = k + 1
        peer = (1 - x if flip & 4 else x, 1 - y if flip & 2 else y, 1 - c if flip & 1 else c)
        pf = 4 * peer[0] + 2 * peer[1] + peer[2]
        for a, mode in enumerate(modes):
            src = ins[a].at[pf] if mode == SCATTER else ins[a]
            for dst, group in ((outs[a].at[me], sends), (outs[a].at[pf], arrivals)):
                if group is sends or with_arrivals:
                    group.append(pltpu.make_async_remote_copy(
                        src_ref=src, dst_ref=dst, send_sem=send_sems.at[a, k], recv_sem=recv_sems.at[a, k],
                        device_id=peer, device_id_type=pl.DeviceIdType.MESH))
    return local, sends, arrivals


def _exchange_start(*refs_and_modes):
    local, sends, _ = _exchange_copies(*refs_and_modes, with_arrivals=False)
    for cp in local + sends:
        cp.start()


def _exchange_wait(*refs_and_modes):
    local, sends, arrivals = _exchange_copies(*refs_and_modes)
    for cp in sends:
        cp.wait_send()
    for cp in arrivals:
        cp.wait_recv()
    for cp in local:
        cp.wait()


def _exchange_shapes(arrays, modes):
    out_shape = [_sds(a.shape if mode == SCATTER else (N_DEV,) + a.shape, a.dtype) for a, mode in zip(arrays, modes)]
    n = len(arrays)
    sems = [pltpu.SemaphoreType.DMA((n, N_DEV - 1)), pltpu.SemaphoreType.DMA((n, N_DEV - 1)), pltpu.SemaphoreType.DMA((n,))]
    return out_shape, sems


def _exchange(arrays, modes, name):
    n = len(arrays)
    out_shape, sems = _exchange_shapes(arrays, modes)

    def body(*refs):
        _exchange_start(refs[:n], refs[n:2 * n], modes, *refs[2 * n:])
        _exchange_wait(refs[:n], refs[n:2 * n], modes, *refs[2 * n:])

    return pl.pallas_call(body, name=name, out_shape=out_shape, in_specs=[ANY_SPEC] * n, out_specs=[ANY_SPEC] * n,
                          scratch_shapes=sems)(*arrays)


def _call(body, args, ride, *, name, grid, in_specs, out_specs, out_shape, scratch_shapes=(), compiler_params):
    if ride is None:
        outs = pl.pallas_call(body, name=name, grid=grid, in_specs=in_specs, out_specs=out_specs, out_shape=out_shape,
                              scratch_shapes=scratch_shapes, compiler_params=compiler_params)(*args)
        return outs, []
    arrays, modes = ride
    n, n_in, n_out, n_scratch = len(arrays), len(in_specs), len(out_specs), len(scratch_shapes)
    moved_shape, sems = _exchange_shapes(arrays, modes)
    n_steps = 1
    for g in grid:
        n_steps *= g

    def riding(*refs):
        ins, refs = refs[:n_in], refs[n_in:]
        r_ins, refs = refs[:n], refs[n:]
        outs, refs = refs[:n_out], refs[n_out:]
        r_outs, refs = refs[:n], refs[n:]
        scratch, r_sems = refs[:n_scratch], refs[n_scratch:]
        step = pl.program_id(0)
        for axis in range(1, len(grid)):
            step = step * grid[axis] + pl.program_id(axis)

        @pl.when(step == 0)
        def _():
            _exchange_start(r_ins, r_outs, modes, *r_sems)

        body(*ins, *outs, *scratch)

        @pl.when(step == n_steps - 1)
        def _():
            _exchange_wait(r_ins, r_outs, modes, *r_sems)

    outs = pl.pallas_call(
        riding, name=name, grid=grid, in_specs=list(in_specs) + [ANY_SPEC] * n,
        out_specs=list(out_specs) + [ANY_SPEC] * n, out_shape=list(out_shape) + moved_shape,
        scratch_shapes=list(scratch_shapes) + sems, compiler_params=compiler_params)(*args, *arrays)
    return outs[:n_out], outs[n_out:]


def _in_proj_fwd(x, g, wint, tb, ride=None):
    s = x.shape[0]

    def body(x_ref, g_ref, w_ref, z_ref, qkv_ref, hb_ref):
        xv = x_ref[...]
        h = (xv * _rms(xv) * g_ref[...]).astype(BF16)
        hb_ref[...] = h
        z = _mm_nt(h, w_ref[...])
        z_ref[...] = z
        qkv_ref[:, 0:D_GROUP] = (z[:, 5 * D_GROUP:6 * D_GROUP] * ATT_SCALE).astype(BF16)
        qkv_ref[:, D_GROUP:3 * D_GROUP] = z[:, 6 * D_GROUP:8 * D_GROUP].astype(BF16)

    return _call(
        body, (x, g, wint), ride, name="in_proj_fwd", grid=(s // tb,),
        in_specs=[_rows(tb, D_MODEL), _whole((1, D_MODEL)), _resident((D_IN, D_MODEL))],
        out_specs=[_rows(tb, D_IN), _rows(tb, 3 * D_GROUP), _rows(tb, D_MODEL)],
        out_shape=[_sds((s, D_IN)), _sds((s, 3 * D_GROUP), BF16), _sds((s, D_MODEL), BF16)],
        compiler_params=_params(("arbitrary",), 48))


def _gelu(x):
    return 0.5 * x * (1.0 + jnp.tanh(GELU_C * (x + GELU_A * x * x * x)))


def _gelu_grad(x):
    t = jnp.tanh(GELU_C * (x + GELU_A * x * x * x))
    return 0.5 * (1.0 + t) + 0.5 * x * (1.0 - t * t) * GELU_C * (1.0 + 3.0 * GELU_A * x * x)


def _head_lane(width):
    return lax.broadcasted_iota(jnp.int32, (1, width), 1) // HEAD_DIM


def _gating_chunk(wm_ref, bexp_ref, vc, lane_h):
    f = bexp_ref[...]
    for h in range(D_GROUP // HEAD_DIM):
        f = f + _mm(wm_ref[h], jnp.where(lane_h == h, vc, 0))
    return f


def _halo_specs(s, tb, width_blocks):
    per = tb // HALO
    prev = pl.BlockSpec((HALO, width_blocks), lambda i: (jnp.maximum(i * per - 1, 0), 0))
    nxt = pl.BlockSpec((HALO, width_blocks), lambda i: (jnp.minimum((i + 1) * per, s // HALO - 1), 0))
    return prev, nxt


SUBLANES = 8
TAP_SLACK = 24


def _taps(buf_ref, shifted_ref, first, n_taps, rows, visit):
    for residue in range(SUBLANES):
        taps = [j for j in range(n_taps) if (first + j) % SUBLANES == residue]
        if not taps:
            continue
        lo = first + taps[0]
        span = first + taps[-1] - lo + rows
        shifted_ref[0:span, :] = buf_ref[pl.ds(lo, span), :]
        for j in taps:
            visit(j, shifted_ref[pl.ds(first + j - lo, rows), :])


def _mix_fwd(z, vg, wm, bexp, scw, ccw, lng, lnb, tb, ride=None):
    s = z.shape[0]
    prev_spec, _ = _halo_specs(s, tb, D_IN)

    def body(z_ref, zp_ref, vg_ref, wm_ref, bexp_ref, scw_ref, ccw_ref, lng_ref, lnb_ref,
             ya_ref, yb_ref, yd_ref, c_ref, pbuf, hbuf, shifted):
        keep = (pl.program_id(0) > 0).astype(F32)
        lane_h = _head_lane(D_GROUP)
        ga = _gelu(z_ref[:, 0:2 * D_GROUP])
        u, v = ga[:, :D_GROUP], ga[:, D_GROUP:]
        vn = (v * _rms(v) * vg_ref[...]).astype(BF16)
        for n in range(tb // CHUNK):
            rows = slice(n * CHUNK, (n + 1) * CHUNK)
            ya_ref[rows, :] = u[rows] * _gating_chunk(wm_ref, bexp_ref, vn[rows], lane_h)
        p = z_ref[:, 3 * D_GROUP:4 * D_GROUP] * z_ref[:, 4 * D_GROUP:5 * D_GROUP]
        pbuf[0:HALO, :] = zp_ref[:, 3 * D_GROUP:4 * D_GROUP] * zp_ref[:, 4 * D_GROUP:5 * D_GROUP] * keep
        pbuf[HALO:HALO + tb, :] = p
        cv = scw_ref[K_SHORT - 1:K_SHORT, :] * p
        for k in range(K_SHORT - 1):
            cv = cv + scw_ref[k:k + 1, :] * pbuf[pl.ds(HALO - (K_SHORT - 1) + k, tb), :]
        yb_ref[...] = z_ref[:, 2 * D_GROUP:3 * D_GROUP] * cv
        hbuf[0:HALO, :] = zp_ref[:, 8 * D_GROUP:9 * D_GROUP] * _sigmoid(zp_ref[:, 9 * D_GROUP:10 * D_GROUP]) * keep
        hbuf[HALO:HALO + tb, :] = z_ref[:, 8 * D_GROUP:9 * D_GROUP] * _sigmoid(z_ref[:, 9 * D_GROUP:10 * D_GROUP])
        conv = [jnp.zeros((tb, D_GROUP), F32)]

        def tap(k, window):
            conv[0] = conv[0] + ccw_ref[k:k + 1, :] * window

        _taps(hbuf, shifted, HALO - (K_CONF - 1), K_CONF, tb, tap)
        c = conv[0]
        c_ref[...] = c
        xc = c - jnp.mean(c, axis=-1, keepdims=True)
        ln = xc * lax.rsqrt(jnp.mean(xc * xc, axis=-1, keepdims=True) + EPS) * lng_ref[...] + lnb_ref[...]
        yd_ref[...] = ln * _sigmoid(ln)

    grp = _rows(tb, D_GROUP)
    return _call(
        body, (z, z, vg, wm, bexp, scw, ccw, lng, lnb), ride, name="mix_fwd", grid=(s // tb,),
        in_specs=[_rows(tb, D_IN), prev_spec, _whole((1, D_GROUP)), _whole(wm.shape), _whole(bexp.shape),
                  _whole(scw.shape), _whole(ccw.shape), _whole((1, D_GROUP)), _whole((1, D_GROUP))],
        out_specs=[grp, grp, grp, grp],
        out_shape=[_sds((s, D_GROUP))] * 4,
        scratch_shapes=[pltpu.VMEM((HALO + tb, D_GROUP), F32), pltpu.VMEM((HALO + tb, D_GROUP), F32),
                        pltpu.VMEM((tb + TAP_SLACK, D_GROUP), F32)],
        compiler_params=_params(("arbitrary",), 40))


def _stick_tile(qh, kt, causal, c, upper):
    x = _mm_nt(qh, kt)
    soft = jnp.log(1.0 + jnp.exp(-jnp.abs(x)))
    lb = jnp.minimum(x, 0.0) - soft
    lom = jnp.where(causal, -jnp.maximum(x, 0.0) - soft, 0.0)
    hi, lo = _split_bf16(lom)
    stick = c + _mm(hi, upper[...]) + _mm(lo, upper[...])
    w = jnp.where(causal, jnp.exp(lb + stick), 0.0)
    return w, lb, lom


def _triangle(n, diagonal):
    return jnp.tri(n, n, diagonal, dtype=BF16)


def _causal_tile(qi, tq, k0, tk):
    qpos = qi * tq + lax.broadcasted_iota(jnp.int32, (tq, 1), 0)
    return k0 + lax.broadcasted_iota(jnp.int32, (1, tk), 1) < qpos


def _sticks_alive(cs):
    longest = cs[0]
    for c in cs[1:]:
        longest = jnp.maximum(longest, c)
    return (jnp.max(longest) > LOG_CUT).astype(jnp.int32)


def _walk(body, qi, tq, tk, init):
    start = (((qi + 1) * tq - 1) // tk, jnp.int32(1)) + tuple(init)
    return lax.while_loop(lambda cr: jnp.logical_and(cr[0] >= 0, cr[1] > 0), body, start)[2:]


def _attn_fwd(qkv, tq, tk, ride=None):
    s = qkv.shape[0]
    n_heads = D_GROUP // HEAD_DIM

    def body(q_ref, k_ref, v_ref, upper, o_ref):
        qi = pl.program_id(0)
        q = q_ref[...]
        lane_h = _head_lane(D_GROUP)
        qhs = [jnp.where(lane_h == h, q, 0) for h in range(n_heads)]

        def step(carry):
            kb, _, acc = carry[:3]
            cs = list(carry[3:])
            k0 = pl.multiple_of(kb * tk, tk)
            kt = k_ref[pl.ds(k0, tk), :]
            vt = v_ref[pl.ds(k0, tk), :]
            causal = _causal_tile(qi, tq, k0, tk)
            for h in range(n_heads):
                w, _, lom = _stick_tile(qhs[h], kt, causal, cs[h], upper)
                acc = acc + _mm(w.astype(BF16), jnp.where(lane_h == h, vt, 0))
                cs[h] = cs[h] + jnp.sum(lom, axis=1, keepdims=True)
            return (kb - 1, _sticks_alive(cs), acc) + tuple(cs)

        init = [jnp.zeros((tq, D_GROUP), F32)] + [jnp.zeros((tq, 1), F32)] * n_heads
        o_ref[...] = _walk(step, qi, tq, tk, init)[0]

    return _call(
        body, (qkv, qkv, qkv, _triangle(tk, -1)), ride, name="attn_fwd", grid=(s // tq,),
        in_specs=[pl.BlockSpec((tq, D_GROUP), lambda qi: (qi, 0)),
                  pl.BlockSpec((s, D_GROUP), lambda qi: (0, 1), pipeline_mode=pl.Buffered(1)),
                  pl.BlockSpec((s, D_GROUP), lambda qi: (0, 2), pipeline_mode=pl.Buffered(1)),
                  _resident((tk, tk))],
        out_specs=[pl.BlockSpec((tq, D_GROUP), lambda qi: (qi, 0))],
        out_shape=[_sds((s, D_GROUP))],
        compiler_params=_params(("arbitrary",), 40))


def _out_proj_fwd(ys, x, mg, wout, tb):
    s = x.shape[0]

    def body(ya_ref, yb_ref, yc_ref, yd_ref, x_ref, mg_ref, w_ref, o_ref):
        acc = x_ref[...]
        for gi, y_ref in enumerate((ya_ref, yb_ref, yc_ref, yd_ref)):
            cols = slice(gi * D_GROUP, (gi + 1) * D_GROUP)
            y = y_ref[...]
            acc = acc + _mm((y * _rms(y) * mg_ref[:, cols]).astype(BF16), w_ref[cols, :])
        o_ref[...] = acc

    grp = _rows(tb, D_GROUP)
    return pl.pallas_call(
        body, name="out_proj_fwd", grid=(s // tb,),
        in_specs=[grp, grp, grp, grp, _rows(tb, D_MODEL), _whole((1, D_MODEL)), _whole((D_MODEL, D_MODEL))],
        out_specs=_rows(tb, D_MODEL), out_shape=_sds((s, D_MODEL)),
        compiler_params=_params(("parallel",), 32),
    )(*ys, x, mg, wout)


def _ffn_fwd(x, g, wup, wdn, tb, ride=None):
    s = x.shape[0]
    ff = D_FF // N_DEV

    def body(x_ref, g_ref, wu_ref, wd_ref, o_ref, a_ref):
        xv = x_ref[...]
        h = (xv * _rms(xv) * g_ref[...]).astype(BF16)
        acc = xv
        for d in range(N_DEV):
            a = jnp.maximum(_mm(h, wu_ref[d]), 0.0)
            a_ref[:, d * ff:(d + 1) * ff] = a.astype(BF16)
            acc = acc + _mm((a * a).astype(BF16), wd_ref[d])
        o_ref[...] = acc

    return _call(
        body, (x, g, wup, wdn), ride, name="ffn_fwd", grid=(s // tb,),
        in_specs=[_rows(tb, D_MODEL), _whole((1, D_MODEL)), _resident((N_DEV, D_MODEL, ff)), _resident((N_DEV, ff, D_MODEL))],
        out_specs=[_rows(tb, D_MODEL), _rows(tb, D_FF)], out_shape=[_sds((s, D_MODEL)), _sds((s, D_FF), BF16)],
        compiler_params=_params(("arbitrary",), 56))


def _loss_head(x, g, tgt, tb):
    s = x.shape[0]

    def body(x_ref, g_ref, t_ref, dx_ref, dg_ref, loss_ref):
        @pl.when(pl.program_id(0) == 0)
        def _():
            dg_ref[...] = jnp.zeros_like(dg_ref)
            loss_ref[...] = jnp.zeros_like(loss_ref)

        xv = x_ref[...]
        r = _rms(xv)
        xh = xv * r
        err = xh * g_ref[...] - t_ref[...]
        loss_ref[...] += 0.5 * jnp.sum(jnp.mean(err * err, axis=-1, keepdims=True))
        dy = err * (1.0 / D_MODEL)
        dg_ref[...] += jnp.sum(dy * xh, axis=0, keepdims=True)
        dx_ref[...] = _rms_bwd(dy * g_ref[...], xh, r)

    return pl.pallas_call(
        body, name="loss_head", grid=(s // tb,),
        in_specs=[_rows(tb, D_MODEL), _whole((1, D_MODEL)), _rows(tb, D_MODEL)],
        out_specs=[_rows(tb, D_MODEL), _whole((1, D_MODEL)), _whole((8, 128))],
        out_shape=[_sds((s, D_MODEL)), _sds((1, D_MODEL)), _sds((8, 128))],
        compiler_params=_params(("arbitrary",), 32),
    )(x, g, tgt)


def _ffn_bwd(x1, act, dx2, g, wup, wdn, tb, ride=None):
    s = x1.shape[0]
    ff = D_FF // N_DEV

    def body(x_ref, a_ref, dy_ref, g_ref, wu_ref, wd_ref, dx_ref, hb_ref, dpre_ref, dg_ref):
        @pl.when(pl.program_id(0) == 0)
        def _():
            dg_ref[...] = jnp.zeros_like(dg_ref)

        xv = x_ref[...]
        r = _rms(xv)
        xh = xv * r
        hb_ref[...] = (xh * g_ref[...]).astype(BF16)
        dyv = dy_ref[...]
        dyb = dyv.astype(BF16)
        dh = jnp.zeros((tb, D_MODEL), F32)
        for d in range(N_DEV):
            cols = slice(d * ff, (d + 1) * ff)
            a = a_ref[:, cols].astype(F32)
            dpre = (_mm_nt(dyb, wd_ref[d]) * (2.0 * a)).astype(BF16)
            dpre_ref[:, cols] = dpre
            dh = dh + _mm_nt(dpre, wu_ref[d])
        dg_ref[...] += jnp.sum(dh * xh, axis=0, keepdims=True)
        dx_ref[...] = dyv + _rms_bwd(dh * g_ref[...], xh, r)

    return _call(
        body, (x1, act, dx2, g, wup, wdn), ride, name="ffn_bwd", grid=(s // tb,),
        in_specs=[_rows(tb, D_MODEL), _rows(tb, D_FF), _rows(tb, D_MODEL), _whole((1, D_MODEL)),
                  _resident((N_DEV, D_MODEL, ff)), _resident((N_DEV, ff, D_MODEL))],
        out_specs=[_rows(tb, D_MODEL), _rows(tb, D_MODEL), _rows(tb, D_FF), _whole((1, D_MODEL))],
        out_shape=[_sds((s, D_MODEL)), _sds((s, D_MODEL), BF16), _sds((s, D_FF), BF16), _sds((1, D_MODEL))],
        compiler_params=_params(("arbitrary",), 58))


def _tn_slabs(a, b, tb, transpose_slabs, name, square_b=False):
    s, m = a.shape
    width = b.shape[1] // N_DEV
    n_steps = s // tb
    slab = (width, m) if transpose_slabs else (m, width)

    def body(a_ref, b_ref, o_hbm, acc, stage, sem):
        step = pl.program_id(0)

        @pl.when(step == 0)
        def _():
            acc[...] = jnp.zeros_like(acc)

        bv = b_ref[...]
        if square_b:
            bv = bv.astype(F32)
            bv = bv * bv
        acc[...] += _mm_tn(a_ref[...].astype(BF16), bv.astype(BF16))

        @pl.when(step == n_steps - 1)
        def _():
            for d in range(N_DEV):
                cols = acc.at[:, pl.ds(d * width, width)]
                if transpose_slabs:
                    stage[...] = cols[...].T
                cp = pltpu.make_async_copy(stage if transpose_slabs else cols, o_hbm.at[d], sem.at[0])
                cp.start()
                cp.wait()

    return pl.pallas_call(
        body, name=name, grid=(n_steps,),
        in_specs=[_rows(tb, m), _rows(tb, b.shape[1])], out_specs=ANY_SPEC, out_shape=_sds((N_DEV,) + slab),
        scratch_shapes=[pltpu.VMEM((m, b.shape[1]), F32), pltpu.VMEM(slab, F32), pltpu.SemaphoreType.DMA((1,))],
        compiler_params=_params(("arbitrary",), 56),
    )(a, b)


def _out_proj_bwd(dx1, ys, mg, wout, tb):
    s = dx1.shape[0]

    def body(dx_ref, ya_ref, yb_ref, yc_ref, yd_ref, mg_ref, w_ref,
             dya_ref, dyb_ref, dyc_ref, dyd_ref, dmg_ref, dw_ref, yn_ref):
        @pl.when(pl.program_id(0) == 0)
        def _():
            dmg_ref[...] = jnp.zeros_like(dmg_ref)
            dw_ref[...] = jnp.zeros_like(dw_ref)

        dxb = dx_ref[...].astype(BF16)
        dyn = _mm_nt(dxb, w_ref[...])
        groups = ((ya_ref, dya_ref), (yb_ref, dyb_ref), (yc_ref, dyc_ref), (yd_ref, dyd_ref))
        for gi, (y_ref, dy_ref) in enumerate(groups):
            cols = slice(gi * D_GROUP, (gi + 1) * D_GROUP)
            y = y_ref[...]
            r = _rms(y)
            n = y * r
            gain = mg_ref[:, cols]
            dn = dyn[:, cols]
            yn_ref[:, cols] = (n * gain).astype(BF16)
            dmg_ref[:, cols] += jnp.sum(dn * n, axis=0, keepdims=True)
            dy_ref[...] = _rms_bwd(dn * gain, n, r)
        dw_ref[...] += _mm_tn(yn_ref[...], dxb)

    grp = _rows(tb, D_GROUP)
    return pl.pallas_call(
        body, name="out_proj_bwd", grid=(s // tb,),
        in_specs=[_rows(tb, D_MODEL), grp, grp, grp, grp, _whole((1, D_MODEL)), _resident((D_MODEL, D_MODEL))],
        out_specs=[grp, grp, grp, grp, _whole((1, D_MODEL)), _whole((D_MODEL, D_MODEL))],
        out_shape=[_sds((s, D_GROUP))] * 4 + [_sds((1, D_MODEL)), _sds((D_MODEL, D_MODEL))],
        scratch_shapes=[pltpu.VMEM((tb, D_MODEL), BF16)],
        compiler_params=_params(("arbitrary",), 40),
    )(dx1, *ys, mg, wout)


def _attn_bwd(qkv, do, o, tq, tk, ride=None):
    s = qkv.shape[0]
    nq = s // tq
    n_pairs = D_GROUP // PAIR

    def body(q_ref, k_ref, v_ref, do_ref, o_ref, upper, upper_eq, dq_ref, dk_hbm, dv_hbm, dk_acc, dv_acc, sems):
        hp, qi = pl.program_id(0), pl.program_id(1)

        @pl.when(qi == 0)
        def _():
            dk_acc[...] = jnp.zeros_like(dk_acc)
            dv_acc[...] = jnp.zeros_like(dv_acc)

        q = q_ref[...]
        dob = do_ref[...].astype(BF16)
        prod = dob.astype(F32) * o_ref[...]
        lane_h = _head_lane(PAIR)
        heads = []
        for h in range(HEADS_PER_PAIR):
            in_head = lane_h == h
            total = jnp.sum(jnp.where(in_head, prod, 0.0), axis=1, keepdims=True)
            heads.append((in_head, jnp.where(in_head, q, 0), jnp.where(in_head, dob, 0), total))

        def step(carry):
            kb, _, acc = carry[:3]
            cs = list(carry[3:3 + HEADS_PER_PAIR])
            nears = list(carry[3 + HEADS_PER_PAIR:])
            k0 = pl.multiple_of(kb * tk, tk)
            kt = k_ref[pl.ds(k0, tk), :]
            vt = v_ref[pl.ds(k0, tk), :]
            causal = _causal_tile(qi, tq, k0, tk)
            dk_t = jnp.zeros((tk, PAIR), F32)
            dv_t = jnp.zeros((tk, PAIR), F32)
            for h, (in_head, qh, doh, total) in enumerate(heads):
                w, lb, lom = _stick_tile(qh, kt, causal, cs[h], upper)
                wb = w.astype(BF16)
                gw = _mm_nt(doh, vt) * wb.astype(F32)
                hi, lo = _split_bf16(gw)
                far = total - nears[h] - _mm(hi, upper_eq[...]) - _mm(lo, upper_eq[...])
                beta = jnp.exp(lb)
                dxb = jnp.where(causal, gw - beta * (gw + far), 0.0).astype(BF16)
                acc = acc + _mm(dxb, jnp.where(in_head, kt, 0))
                dk_t = dk_t + _mm_tn(dxb, qh)
                dv_t = dv_t + _mm_tn(wb, doh)
                cs[h] = cs[h] + jnp.sum(lom, axis=1, keepdims=True)
                nears[h] = nears[h] + jnp.sum(gw, axis=1, keepdims=True)
            dk_acc[pl.ds(k0, tk), :] += dk_t
            dv_acc[pl.ds(k0, tk), :] += dv_t
            return (kb - 1, _sticks_alive(cs), acc) + tuple(cs) + tuple(nears)

        init = [jnp.zeros((tq, PAIR), F32)] + [jnp.zeros((tq, 1), F32)] * (2 * HEADS_PER_PAIR)
        dq_ref[...] = _walk(step, qi, tq, tk, init)[0]

        @pl.when(qi == nq - 1)
        def _():
            ck = pltpu.make_async_copy(dk_acc, dk_hbm.at[hp], sems.at[0])
            cv = pltpu.make_async_copy(dv_acc, dv_hbm.at[hp], sems.at[1])
            ck.start()
            cv.start()
            ck.wait()
            cv.wait()

    blk = pl.BlockSpec((tq, PAIR), lambda hp, qi: (qi, hp))
    return _call(
        body, (qkv, qkv, qkv, do, o, _triangle(tk, -1), _triangle(tk, 0)), ride, name="attn_bwd", grid=(n_pairs, nq),
        in_specs=[blk, pl.BlockSpec((s, PAIR), lambda hp, qi: (0, 2 + hp)),
                  pl.BlockSpec((s, PAIR), lambda hp, qi: (0, 4 + hp)), blk, blk, _resident((tk, tk)), _resident((tk, tk))],
        out_specs=[blk, ANY_SPEC, ANY_SPEC],
        out_shape=[_sds((s, D_GROUP)), _sds((n_pairs, s, PAIR)), _sds((n_pairs, s, PAIR))],
        scratch_shapes=[pltpu.VMEM((s, PAIR), F32), pltpu.VMEM((s, PAIR), F32), pltpu.SemaphoreType.DMA((2,))],
        compiler_params=_params(("arbitrary", "arbitrary"), 56))


def _mix_bwd(z, conv, dya, dyb, dyd, dq, dk, dv, hb, vg, wm, wmt, bexp, scw, ccw, lng, lnb, tb, ride=None):
    s = z.shape[0]
    n_steps = s // tb
    prev_spec, next_spec = _halo_specs(s, tb, D_IN)
    _, next_grp = _halo_specs(s, tb, D_GROUP)
    ext = tb + HALO

    def body(z_ref, zp_ref, zn_ref, dya_ref, dyb_ref, dybn_ref, dyd_ref, dydn_ref, c_ref, cn_ref, dq_ref, dk0_ref, dk1_ref,
             dv0_ref, dv1_ref, hbp_ref, hb_ref, vg_ref, wm_ref, wmt_ref, bexp_ref, scw_ref, ccw_ref, lng_ref, lnb_ref,
             dz_ref, dvg_ref, dws_ref, dbs_ref, dscw_ref, dccw_ref, dlng_ref, dlnb_ref, dwin_hbm,
             pbuf, hbuf, gbuf, cbuf, dubuf, dvnbuf, shifted, win_acc, dz_prev, win_sem):
        i = pl.program_id(0)

        @pl.when(i == 0)
        def _():
            for ref in (dvg_ref, dws_ref, dbs_ref, dscw_ref, dccw_ref, dlng_ref, dlnb_ref, win_acc, dz_prev):
                ref[...] = jnp.zeros_like(ref)

        win_acc[...] += _mm_tn(dz_prev[...], hbp_ref[...])

        keep_prev = (i > 0).astype(F32)
        keep_next = (i < n_steps - 1).astype(F32)
        lane_h = _head_lane(D_GROUP)

        za = z_ref[:, 0:2 * D_GROUP]
        ga = _gelu(za)
        u, v = ga[:, :D_GROUP], ga[:, D_GROUP:]
        r = _rms(v)
        vh = v * r
        vn = (vh * vg_ref[...]).astype(BF16)
        tril = lax.broadcasted_iota(jnp.int32, (CHUNK, CHUNK), 0) >= lax.broadcasted_iota(jnp.int32, (CHUNK, CHUNK), 1)
        dbias = jnp.zeros((CHUNK, D_GROUP), F32)
        for n in range(tb // CHUNK):
            rows = slice(n * CHUNK, (n + 1) * CHUNK)
            vc = vn[rows]
            dy = dya_ref[rows, :]
            dubuf[rows, :] = dy * _gating_chunk(wm_ref, bexp_ref, vc, lane_h)
            df = dy * u[rows]
            dfb = df.astype(BF16)
            dvn = jnp.zeros((CHUNK, D_GROUP), F32)
            for h in range(D_GROUP // HEAD_DIM):
                dfh = jnp.where(lane_h == h, dfb, 0)
                dvn = dvn + _mm(wmt_ref[h], dfh)
                dws_ref[h] += jnp.where(tril, _mm_nt(dfh, vc), 0.0)
            dvnbuf[rows, :] = dvn
            dbias = dbias + df
        for h in range(D_GROUP // HEAD_DIM):
            per_head = jnp.sum(jnp.where(lane_h == h, dbias, 0.0), axis=1, keepdims=True)
            dbs_ref[...] += per_head * (lax.broadcasted_iota(jnp.int32, (1, CHUNK), 1) == h).astype(F32)
        dvn = dvnbuf[...]
        dvg_ref[...] += jnp.sum(dvn * vh, axis=0, keepdims=True)
        dgelu = _gelu_grad(za)
        dz_ref[:, 0:D_GROUP] = (dubuf[...] * dgelu[:, :D_GROUP]).astype(BF16)
        dz_ref[:, D_GROUP:2 * D_GROUP] = (_rms_bwd(dvn * vg_ref[...], vh, r) * dgelu[:, D_GROUP:]).astype(BF16)

        gate_b = z_ref[:, 2 * D_GROUP:3 * D_GROUP]
        gate_c = z_ref[:, 3 * D_GROUP:4 * D_GROUP]
        hh = z_ref[:, 4 * D_GROUP:5 * D_GROUP]
        p = gate_c * hh
        pbuf[0:HALO, :] = zp_ref[:, 3 * D_GROUP:4 * D_GROUP] * zp_ref[:, 4 * D_GROUP:5 * D_GROUP] * keep_prev
        pbuf[HALO:HALO + tb, :] = p
        dyb_v = dyb_ref[...]
        dcv = dyb_v * gate_b
        gbuf[0:tb, :] = dcv
        gbuf[tb:ext, :] = dybn_ref[...] * zn_ref[:, 2 * D_GROUP:3 * D_GROUP] * keep_next
        cv = scw_ref[K_SHORT - 1:K_SHORT, :] * p
        dp = scw_ref[K_SHORT - 1:K_SHORT, :] * dcv
        dscw_ref[K_SHORT - 1:K_SHORT, :] += jnp.sum(dcv * p, axis=0, keepdims=True)
        for k in range(K_SHORT - 1):
            earlier = pbuf[pl.ds(HALO - (K_SHORT - 1) + k, tb), :]
            cv = cv + scw_ref[k:k + 1, :] * earlier
            dp = dp + scw_ref[k:k + 1, :] * gbuf[pl.ds(K_SHORT - 1 - k, tb), :]
            dscw_ref[k:k + 1, :] += jnp.sum(dcv * earlier, axis=0, keepdims=True)
        dz_ref[:, 2 * D_GROUP:3 * D_GROUP] = (dyb_v * cv).astype(BF16)
        dz_ref[:, 3 * D_GROUP:4 * D_GROUP] = (dp * hh).astype(BF16)
        dz_ref[:, 4 * D_GROUP:5 * D_GROUP] = (dp * gate_c).astype(BF16)

        dz_ref[:, 5 * D_GROUP:6 * D_GROUP] = (dq_ref[...] * ATT_SCALE).astype(BF16)
        dz_ref[:, 6 * D_GROUP:6 * D_GROUP + PAIR] = dk0_ref[...].astype(BF16)
        dz_ref[:, 6 * D_GROUP + PAIR:7 * D_GROUP] = dk1_ref[...].astype(BF16)
        dz_ref[:, 7 * D_GROUP:7 * D_GROUP + PAIR] = dv0_ref[...].astype(BF16)
        dz_ref[:, 7 * D_GROUP + PAIR:8 * D_GROUP] = dv1_ref[...].astype(BF16)

        a = z_ref[:, 8 * D_GROUP:9 * D_GROUP]
        sg = _sigmoid(z_ref[:, 9 * D_GROUP:10 * D_GROUP])
        hbuf[0:HALO, :] = zp_ref[:, 8 * D_GROUP:9 * D_GROUP] * _sigmoid(zp_ref[:, 9 * D_GROUP:10 * D_GROUP]) * keep_prev
        hbuf[HALO:HALO + tb, :] = a * sg
        c = jnp.concatenate([c_ref[...], cn_ref[...]], axis=0)
        xc = c - jnp.mean(c, axis=-1, keepdims=True)
        rs = lax.rsqrt(jnp.mean(xc * xc, axis=-1, keepdims=True) + EPS)
        xh = xc * rs
        ln = xh * lng_ref[...] + lnb_ref[...]
        sl = _sigmoid(ln)
        dy_ext = jnp.concatenate([dyd_ref[...], dydn_ref[...] * keep_next], axis=0)
        dln = dy_ext * sl * (1.0 + ln * (1.0 - sl))
        dlng_ref[...] += jnp.sum(dln[:tb] * xh[:tb], axis=0, keepdims=True)
        dlnb_ref[...] += jnp.sum(dln[:tb], axis=0, keepdims=True)
        dxh = dln * lng_ref[...]
        dc = rs * (dxh - jnp.mean(dxh, axis=-1, keepdims=True) - xh * jnp.mean(dxh * xh, axis=-1, keepdims=True))
        cbuf[...] = dc
        dc_blk = dc[:tb]
        grad_in = [jnp.zeros((tb, D_GROUP), F32)]

        def tap_input(j, window):
            k = K_CONF - 1 - j
            grad_in[0] = grad_in[0] + ccw_ref[k:k + 1, :] * window

        def tap_filter(k, window):
            dccw_ref[k:k + 1, :] += jnp.sum(dc_blk * window, axis=0, keepdims=True)

        _taps(cbuf, shifted, 0, K_CONF, tb, tap_input)
        _taps(hbuf, shifted, HALO - (K_CONF - 1), K_CONF, tb, tap_filter)
        dhd = grad_in[0]
        dz_ref[:, 8 * D_GROUP:9 * D_GROUP] = (dhd * sg).astype(BF16)
        dz_ref[:, 9 * D_GROUP:10 * D_GROUP] = (dhd * a * sg * (1.0 - sg)).astype(BF16)

        dz_prev[...] = dz_ref[...]

        @pl.when(i == n_steps - 1)
        def _():
            win_acc[...] += _mm_tn(dz_ref[...], hb_ref[...])
            cp = pltpu.make_async_copy(win_acc, dwin_hbm, win_sem.at[0])
            cp.start()
            cp.wait()

    grp = _rows(tb, D_GROUP)
    pair0 = pl.BlockSpec((None, tb, PAIR), lambda i: (0, i, 0))
    pair1 = pl.BlockSpec((None, tb, PAIR), lambda i: (1, i, 0))
    small = [_sds((1, D_GROUP)), _sds((4, CHUNK, CHUNK)), _sds((CHUNK, CHUNK)), _sds((8, D_GROUP)),
             _sds((HALO, D_GROUP)), _sds((1, D_GROUP)), _sds((1, D_GROUP))]
    return _call(
        body, (z, z, z, dya, dyb, dyb, dyd, dyd, conv, conv, dq, dk, dk, dv, dv, hb, hb, vg, wm, wmt, bexp, scw, ccw, lng, lnb),
        ride, name="mix_bwd", grid=(n_steps,),
        in_specs=[_rows(tb, D_IN), prev_spec, next_spec, grp, grp, next_grp, grp, next_grp, grp, next_grp, grp,
                  pair0, pair1, pair0, pair1,
                  pl.BlockSpec((tb, D_MODEL), lambda i: (jnp.maximum(i - 1, 0), 0)), _rows(tb, D_MODEL),
                  _whole((1, D_GROUP)), _whole(wm.shape), _whole(wmt.shape), _whole(bexp.shape), _whole(scw.shape),
                  _whole(ccw.shape), _whole((1, D_GROUP)), _whole((1, D_GROUP))],
        out_specs=[_rows(tb, D_IN)] + [_whole(t.shape) for t in small] + [ANY_SPEC],
        out_shape=[_sds((s, D_IN), BF16)] + small + [_sds((D_IN, D_MODEL))],
        scratch_shapes=[pltpu.VMEM((HALO + tb, D_GROUP), F32), pltpu.VMEM((HALO + tb, D_GROUP), F32),
                        pltpu.VMEM((ext, D_GROUP), F32), pltpu.VMEM((ext, D_GROUP), F32),
                        pltpu.VMEM((tb, D_GROUP), F32), pltpu.VMEM((tb, D_GROUP), F32),
                        pltpu.VMEM((tb + TAP_SLACK, D_GROUP), F32), pltpu.VMEM((D_IN, D_MODEL), F32),
                        pltpu.VMEM((tb, D_IN), BF16), pltpu.SemaphoreType.DMA((1,))],
        compiler_params=_params(("arbitrary",), 56))


def _in_proj_bwd(x, dz, dres, g, wint, tb, ride=None):
    s = x.shape[0]

    def body(x_ref, dz_ref, dr_ref, g_ref, w_ref, dx_ref, dg_ref):
        @pl.when(pl.program_id(0) == 0)
        def _():
            dg_ref[...] = jnp.zeros_like(dg_ref)

        xv = x_ref[...]
        r = _rms(xv)
        xh = xv * r
        dh = _mm(dz_ref[...], w_ref[...])
        dg_ref[...] += jnp.sum(dh * xh, axis=0, keepdims=True)
        dx_ref[...] = dr_ref[...] + _rms_bwd(dh * g_ref[...], xh, r)

    return _call(
        body, (x, dz, dres, g, wint), ride, name="in_proj_bwd", grid=(s // tb,),
        in_specs=[_rows(tb, D_MODEL), _rows(tb, D_IN), _rows(tb, D_MODEL), _whole((1, D_MODEL)), _resident((D_IN, D_MODEL))],
        out_specs=[_rows(tb, D_MODEL), _whole((1, D_MODEL))],
        out_shape=[_sds((s, D_MODEL)), _sds((1, D_MODEL))],
        compiler_params=_params(("arbitrary",), 48))


def _adamw(w, g, m, v):
    m = ADAM_B1 * m + (1.0 - ADAM_B1) * g
    v = ADAM_B2 * v + (1.0 - ADAM_B2) * (g * g)
    m_hat = m / (1.0 - ADAM_B1 ** ADAM_STEP)
    v_hat = v / (1.0 - ADAM_B2 ** ADAM_STEP)
    delta = -ADAM_LR * (m_hat / (jnp.sqrt(v_hat) + ADAM_EPS) + ADAM_WD * w)
    return delta, m, v


def _reduce_adamw(parts, w, m, v, tb, name):
    rows, cols = w.shape

    def body(p_ref, w_ref, m_ref, v_ref, g_ref, d_ref, m2_ref, v2_ref):
        g = p_ref[0].astype(F32)
        for j in range(1, N_DEV):
            g = g + p_ref[j].astype(F32)
        g_ref[...] = g
        d_ref[...], m2_ref[...], v2_ref[...] = _adamw(w_ref[...], g, m_ref[...], v_ref[...])

    blk = _rows(tb, cols)
    return pl.pallas_call(
        body, name=name, grid=(rows // tb,),
        in_specs=[pl.BlockSpec((N_DEV, tb, cols), lambda i: (0, i, 0)), blk, blk, blk],
        out_specs=[blk] * 4, out_shape=[_sds((rows, cols))] * 4,
        compiler_params=_params(("parallel",), 32),
    )(parts, w, m, v)


def _reduce_adamw_layer(parts, w, m, v, layer, taken_over, tb, name):
    _, rows, cols = w.shape
    n_given = 4 if taken_over is None else 8

    def body(*refs):
        p_ref, w_ref, m_ref, v_ref = refs[:4]
        g_ref, d_ref, m2_ref, v2_ref = refs[n_given:]
        g = p_ref[0].astype(F32)
        for j in range(1, N_DEV):
            g = g + p_ref[j].astype(F32)
        g_ref[...] = g
        d_ref[...], m2_ref[...], v2_ref[...] = _adamw(w_ref[...], g, m_ref[...], v_ref[...])

    blk = pl.BlockSpec((None, tb, cols), lambda i: (layer, i, 0))
    return pl.pallas_call(
        body, name=name, grid=(rows // tb,),
        in_specs=[pl.BlockSpec((N_DEV, tb, cols), lambda i: (0, i, 0)), blk, blk, blk] + [ANY_SPEC] * (n_given - 4),
        out_specs=[blk] * 4, out_shape=[_sds(w.shape)] * 4,
        input_output_aliases={} if taken_over is None else {4 + k: k for k in range(4)},
        compiler_params=_params(("arbitrary",), 32),
    )(parts, w, m, v, *(taken_over or ()))


LANES = 128
PACK_ALIGN = 8 * LANES


def _pack(arrays):
    pieces = []
    for a in arrays:
        flat = a.reshape(-1)
        pieces.append(jnp.pad(flat, (0, -flat.shape[0] % PACK_ALIGN)).reshape(-1, LANES))
    return jnp.concatenate(pieces, axis=0)


def _unpack(packed, shapes):
    out, row = [], 0
    for shape in shapes:
        size = 1
        for dim in shape:
            size *= dim
        rows = -(-size // PACK_ALIGN) * 8
        out.append(packed[row:row + rows].reshape(-1)[:size].reshape(shape))
        row += rows
    return out


TB_PROJ = 512
TB_MIX = 256
TB_TN = 1024
TQ = 256
TK = 256
TB_ADAM = 64


def kernel(x, norm_mix_g, w_in, gmlp_v_g, gmlp_w_s, gmlp_b_s, short_conv_w, conf_conv_w, conf_ln_g, conf_ln_b, mix_out_g, w_out, norm_ffn_g, w_up, w_down, final_norm_g, loss_target, m_norm_mix_g, m_w_in, m_gmlp_v_g, m_gmlp_w_s, m_gmlp_b_s, m_short_conv_w, m_conf_conv_w, m_conf_ln_g, m_conf_ln_b, m_mix_out_g, m_w_out, m_norm_ffn_g, m_w_up, m_w_down, m_final_norm_g, v_norm_mix_g, v_w_in, v_gmlp_v_g, v_gmlp_w_s, v_gmlp_b_s, v_short_conv_w, v_conf_conv_w, v_conf_ln_g, v_conf_ln_b, v_mix_out_g, v_w_out, v_norm_ffn_g, v_w_up, v_w_down, v_final_norm_g):
    me = 4 * lax.axis_index("x") + 2 * lax.axis_index("y") + lax.axis_index("c")
    x0, target = x[0], loss_target[0]
    s = x0.shape[0]
    tb_proj, tb_mix, tb_tn = min(TB_PROJ, s), min(TB_MIX, s), min(TB_TN, s)
    conv_cols = D_GROUP // N_DEV

    def pad_rows(a, rows):
        return jnp.pad(a, ((0, rows - a.shape[0]), (0, 0)))

    wint_loc = [w_in[l].T.astype(BF16) for l in range(N_LAYERS)]
    wout_loc = [w_out[l].astype(BF16) for l in range(N_LAYERS)]
    wup_loc = [w_up[l].astype(BF16) for l in range(N_LAYERS)]
    wdn_loc = [w_down[l].astype(BF16) for l in range(N_LAYERS)]
    conv_loc = jnp.concatenate([pad_rows(short_conv_w[l], 8) for l in range(N_LAYERS)]
                               + [pad_rows(conf_conv_w[l], HALO) for l in range(N_LAYERS)], axis=0)
    wint, wout, wup, wdn = [None] * N_LAYERS, [None] * N_LAYERS, [None] * N_LAYERS, [None] * N_LAYERS
    wint0, conv_all = _exchange([wint_loc[0], conv_loc], [GATHER, GATHER], "gather_first_weights")
    wint[0] = wint0.reshape(D_IN, D_MODEL)
    conv_full = conv_all.transpose(1, 0, 2).reshape(-1, D_GROUP)
    scw = [conv_full[8 * l:8 * (l + 1)] for l in range(N_LAYERS)]
    ccw = [conv_full[8 * N_LAYERS + HALO * l:8 * N_LAYERS + HALO * (l + 1)] for l in range(N_LAYERS)]

    tril = jnp.tril(jnp.ones((CHUNK, CHUNK), dtype=bool))
    wm = [jnp.where(tril, gmlp_w_s[l], 0.0).astype(BF16) for l in range(N_LAYERS)]
    wmt = [w.transpose(0, 2, 1) for w in wm]
    bexp = [jnp.repeat(gmlp_b_s[l].T, HEAD_DIM, axis=1) for l in range(N_LAYERS)]

    def row(vec):
        return vec.reshape(1, -1)

    saved = []
    xc = x0
    for l in range(N_LAYERS):
        first = l == 0
        (z, qkv, hb_in), moved = _in_proj_fwd(xc, row(norm_mix_g[l]), wint[l], tb_proj,
                                              ride=([wout_loc[0]], [GATHER]) if first else None)
        if first:
            wout[0] = moved[0].reshape(D_MODEL, D_MODEL)
        (ya, yb, yd, conv), moved = _mix_fwd(z, row(gmlp_v_g[l]), wm[l], bexp[l], scw[l], ccw[l], row(conf_ln_g[l]),
                                             row(conf_ln_b[l]), tb_mix, ride=([wup_loc[0]], [GATHER]) if first else None)
        if first:
            wup[0] = moved[0]
        (yc,), moved = _attn_fwd(qkv, TQ, TK, ride=([wdn_loc[0], wint_loc[1]], [GATHER, GATHER]) if first else None)
        if first:
            wdn[0], wint[1] = moved[0], moved[1].reshape(D_IN, D_MODEL)
        ys = (ya, yb, yc, yd)
        x1 = _out_proj_fwd(ys, xc, row(mix_out_g[l]), wout[l], tb_proj)
        (x2, act), moved = _ffn_fwd(x1, row(norm_ffn_g[l]), wup[l], wdn[l], tb_proj,
                                    ride=([wout_loc[1], wup_loc[1], wdn_loc[1]], [GATHER] * 3) if first else None)
        saved.append((xc, z, qkv, ys, x1, act, hb_in, conv))
        xc = x2
        if first:
            wout[1], wup[1], wdn[1] = moved[0].reshape(D_MODEL, D_MODEL), moved[1], moved[2]
    dx, g_final, loss_part = _loss_head(xc, row(final_norm_g), target, tb_proj)
    loss = lax.psum(loss_part[0, 0], ("x", "y", "c"))

    parts = [None] * (4 * N_LAYERS)
    small_grads = [None] * N_LAYERS
    early_names = ["gmlp_v_g", "gmlp_w_s", "gmlp_b_s", "short_conv_w", "conf_conv_w", "conf_ln_g", "conf_ln_b",
                   "mix_out_g", "norm_ffn_g"]
    for l in reversed(range(N_LAYERS)):
        xin, z, qkv, ys, x1, act, hb_in, conv = saved[l]
        (dx1, hb_ffn, dpre, g_ffn), _ = _ffn_bwd(x1, act, dx, row(norm_ffn_g[l]), wup[l], wdn[l], tb_proj)
        grad_up = _tn_slabs(hb_ffn, dpre, tb_tn, False, "grad_w_up")
        grad_dn = _tn_slabs(dx, act, tb_proj, True, "grad_w_down", square_b=True)
        dya, dyb_mix, dyc, dyd, g_mixout, grad_out = _out_proj_bwd(dx1, ys, row(mix_out_g[l]), wout[l], tb_proj)
        grad_out = grad_out.reshape(N_DEV, D_MODEL // N_DEV, D_MODEL)
        (dq, dk, dv), moved = _attn_bwd(qkv, dyc, ys[2], TQ, TK, ride=([grad_up, grad_dn], [SCATTER] * 2))
        parts[4 * l + 2], parts[4 * l + 3] = moved
        (dz, g_vg, g_ws, g_bs, g_scw, g_ccw, g_lng, g_lnb, grad_in), moved = _mix_bwd(
            z, conv, dya, dyb_mix, dyd, dq, dk, dv, hb_in, row(gmlp_v_g[l]), wm[l], wmt[l], bexp[l], scw[l], ccw[l],
            row(conf_ln_g[l]), row(conf_ln_b[l]), tb_mix, ride=([grad_out], [SCATTER]))
        parts[4 * l + 1] = moved[0]
        small_grads[l] = dict(gmlp_v_g=g_vg[0], gmlp_w_s=g_ws, gmlp_b_s=g_bs[:, :4].T, short_conv_w=g_scw[:K_SHORT],
                              conf_conv_w=g_ccw[:K_CONF], conf_ln_g=g_lng[0], conf_ln_b=g_lnb[0],
                              mix_out_g=g_mixout[0], norm_ffn_g=g_ffn[0])
        riders, modes = [grad_in.reshape(N_DEV, D_IN // N_DEV, D_MODEL)], [SCATTER]
        if l == 0:
            early_list = [jnp.stack([small_grads[k][n] for k in range(N_LAYERS)]) for n in early_names] + [g_final[0]]
            riders, modes = [riders[0].astype(BF16), _pack(early_list).astype(BF16)], modes + [GATHER]
        (dx, g_mix), moved = _in_proj_bwd(xin, dz, dx1, row(norm_mix_g[l]), wint[l], tb_proj, ride=(riders, modes))
        parts[4 * l] = moved[0]
        small_grads[l]["norm_mix_g"] = g_mix[0]

    late_list = [jnp.stack([small_grads[l]["norm_mix_g"] for l in range(N_LAYERS)])]
    late_parts = _exchange([_pack(late_list)], [GATHER], "gather_last_grad")[0]
    small_groups = [(early_names + ["final_norm_g"], early_list, moved[1]), (["norm_mix_g"], late_list, late_parts)]

    given = dict(norm_mix_g=(norm_mix_g, m_norm_mix_g, v_norm_mix_g), gmlp_v_g=(gmlp_v_g, m_gmlp_v_g, v_gmlp_v_g),
                 gmlp_w_s=(gmlp_w_s, m_gmlp_w_s, v_gmlp_w_s), gmlp_b_s=(gmlp_b_s, m_gmlp_b_s, v_gmlp_b_s),
                 short_conv_w=(short_conv_w, m_short_conv_w, v_short_conv_w),
                 conf_conv_w=(conf_conv_w, m_conf_conv_w, v_conf_conv_w),
                 conf_ln_g=(conf_ln_g, m_conf_ln_g, v_conf_ln_g), conf_ln_b=(conf_ln_b, m_conf_ln_b, v_conf_ln_b),
                 mix_out_g=(mix_out_g, m_mix_out_g, v_mix_out_g), norm_ffn_g=(norm_ffn_g, m_norm_ffn_g, v_norm_ffn_g),
                 final_norm_g=(final_norm_g, m_final_norm_g, v_final_norm_g))
    sharded_small = ("short_conv_w", "conf_conv_w")

    def widen(a):
        full = jnp.zeros(a.shape[:-1] + (D_GROUP,), a.dtype)
        return lax.dynamic_update_slice(full, a, (0, 0, me * conv_cols))

    small_res = {}
    for names, grads, gathered in small_groups:
        state = [_pack([widen(given[n][k]) if n in sharded_small else given[n][k] for n in names]) for k in range(3)]
        outs = _reduce_adamw(gathered, *state, state[0].shape[0], "adamw_small")
        for kind, packed in zip(("grad", "delta", "new_m", "new_v"), outs):
            for n, val in zip(names, _unpack(packed, [a.shape for a in grads])):
                if n in sharded_small:
                    val = lax.dynamic_slice(val, (0, 0, me * conv_cols), val.shape[:-1] + (conv_cols,))
                small_res[kind, n] = val

    big_names = ["w_in", "w_out", "w_up", "w_down"]
    big_given = dict(w_in=[t.transpose(0, 2, 1) for t in (w_in, m_w_in, v_w_in)], w_out=(w_out, m_w_out, v_w_out),
                     w_up=(w_up, m_w_up, v_w_up), w_down=(w_down, m_w_down, v_w_down))
    big_res = {}
    for j, n in enumerate(big_names):
        outs = None
        for l in range(N_LAYERS):
            outs = _reduce_adamw_layer(parts[4 * l + j], *big_given[n], l, outs, TB_ADAM, "adamw_" + n)
        for kind, out in zip(("grad", "delta", "new_m", "new_v"), outs):
            big_res[kind, n] = out.transpose(0, 2, 1) if n == "w_in" else out

    order = ["norm_mix_g", "w_in", "gmlp_v_g", "gmlp_w_s", "gmlp_b_s", "short_conv_w", "conf_conv_w", "conf_ln_g",
             "conf_ln_b", "mix_out_g", "w_out", "norm_ffn_g", "w_up", "w_down", "final_norm_g"]
    result = [loss, dx.reshape(x.shape)]
    for kind in ("grad", "delta", "new_m", "new_v"):
        for n in order:
            result.append(big_res[kind, n] if n in big_given else small_res[kind, n])
    return tuple(result)
```

```python
import jax
import jax.numpy as jnp
from jax import lax
from jax.experimental import pallas as pl
from jax.experimental.pallas import tpu as pltpu

F32 = jnp.float32
BF16 = jnp.bfloat16

D_MODEL = 1024
D_GROUP = 256
D_IN = 10 * D_GROUP
D_FF = 4 * D_MODEL
N_DEV = 8
N_LAYERS = 2
HEAD_DIM = 64
HEADS_PER_PAIR = 2
PAIR = HEADS_PER_PAIR * HEAD_DIM
CHUNK = 128
K_SHORT = 3
K_CONF = 31
HALO = 32
EPS = 1e-6
ATT_SCALE = HEAD_DIM ** -0.5
LOG_CUT = -104.0
MIB = 2 ** 20

ADAM_LR = 0.001
ADAM_B1 = 0.9
ADAM_B2 = 0.999
ADAM_EPS = 1e-08
ADAM_WD = 0.01
ADAM_STEP = 10

GELU_C = 0.7978845608028654
GELU_A = 0.044715


def _mm(a, b):
    return jnp.dot(a, b, preferred_element_type=F32)


def _mm_nt(a, b):
    return lax.dot_general(a, b, (((1,), (1,)), ((), ())), preferred_element_type=F32)


def _mm_tn(a, b):
    return lax.dot_general(a, b, (((0,), (0,)), ((), ())), preferred_element_type=F32)


def _rms(x):
    return lax.rsqrt(jnp.mean(x * x, axis=-1, keepdims=True) + EPS)


def _rms_bwd(dy, xh, r):
    return r * (dy - xh * jnp.mean(dy * xh, axis=-1, keepdims=True))


def _sigmoid(x):
    return 1.0 / (1.0 + jnp.exp(-x))


def _whole(shape):
    return pl.BlockSpec(shape, lambda *_: (0,) * len(shape))


def _resident(shape):
    return pl.BlockSpec(shape, lambda *_: (0,) * len(shape), pipeline_mode=pl.Buffered(1))


def _rows(tb, width, col=0):
    return pl.BlockSpec((tb, width), lambda i: (i, col))


def _params(semantics, vmem_mib):
    return pltpu.CompilerParams(dimension_semantics=semantics, vmem_limit_bytes=vmem_mib * MIB)


def _sds(shape, dtype=F32):
    return jax.ShapeDtypeStruct(shape, dtype)


def _split_bf16(v):
    hi = v.astype(BF16)
    lo = (v - hi.astype(F32)).astype(BF16)
    return hi, lo


GATHER, SCATTER = "gather", "scatter"
ANY_SPEC = pl.BlockSpec(memory_space=pl.ANY)


def _exchange_copies(ins, outs, modes, send_sems, recv_sems, local_sems, with_arrivals=True):
    x, y, c = lax.axis_index("x"), lax.axis_index("y"), lax.axis_index("c")
    me = 4 * x + 2 * y + c
    local, sends, arrivals = [], [], []
    for a, mode in enumerate(modes):
        local.append(pltpu.make_async_copy(ins[a].at[me] if mode == SCATTER else ins[a], outs[a].at[me], local_sems.at[a]))
    for k in range(N_DEV - 1):
        flip = k + 1
        peer = (1 - x if flip & 4 else x, 1 - y if flip & 2 else y, 1 - c if flip & 1 else c)
        pf = 4 * peer[0] + 2 * peer[1] + peer[2]
        for a, mode in enumerate(modes):
            src = ins[a].at[pf] if mode == SCATTER else ins[a]
            for dst, group in ((outs[a].at[me], sends), (outs[a].at[pf], arrivals)):
                if group is sends or with_arrivals:
                    group.append(pltpu.make_async_remote_copy(
                        src_ref=src, dst_ref=dst, send_sem=send_sems.at[a, k], recv_sem=recv_sems.at[a, k],
                        device_id=peer, device_id_type=pl.DeviceIdType.MESH))
    return local, sends, arrivals


def _exchange_start(*refs_and_modes):
    local, sends, _ = _exchange_copies(*refs_and_modes, with_arrivals=False)
    for cp in local + sends:
        cp.start()


def _exchange_wait(*refs_and_modes):
    local, sends, arrivals = _exchange_copies(*refs_and_modes)
    for cp in sends:
        cp.wait_send()
    for cp in arrivals:
        cp.wait_recv()
    for cp in local:
        cp.wait()


def _exchange_shapes(arrays, modes):
    out_shape = [_sds(a.shape if mode == SCATTER else (N_DEV,) + a.shape, a.dtype) for a, mode in zip(arrays, modes)]
    n = len(arrays)
    sems = [pltpu.SemaphoreType.DMA((n, N_DEV - 1)), pltpu.SemaphoreType.DMA((n, N_DEV - 1)), pltpu.SemaphoreType.DMA((n,))]
    return out_shape, sems


def _exchange(arrays, modes, name):
    n = len(arrays)
    out_shape, sems = _exchange_shapes(arrays, modes)

    def body(*refs):
        _exchange_start(refs[:n], refs[n:2 * n], modes, *refs[2 * n:])
        _exchange_wait(refs[:n], refs[n:2 * n], modes, *refs[2 * n:])

    return pl.pallas_call(body, name=name, out_shape=out_shape, in_specs=[ANY_SPEC] * n, out_specs=[ANY_SPEC] * n,
                          scratch_shapes=sems)(*arrays)


def _call(body, args, ride, *, name, grid, in_specs, out_specs, out_shape, scratch_shapes=(), compiler_params):
    if ride is None:
        outs = pl.pallas_call(body, name=name, grid=grid, in_specs=in_specs, out_specs=out_specs, out_shape=out_shape,
                              scratch_shapes=scratch_shapes, compiler_params=compiler_params)(*args)
        return outs, []
    arrays, modes = ride
    n, n_in, n_out, n_scratch = len(arrays), len(in_specs), len(out_specs), len(scratch_shapes)
    moved_shape, sems = _exchange_shapes(arrays, modes)
    n_steps = 1
    for g in grid:
        n_steps *= g

    def riding(*refs):
        ins, refs = refs[:n_in], refs[n_in:]
        r_ins, refs = refs[:n], refs[n:]
        outs, refs = refs[:n_out], refs[n_out:]
        r_outs, refs = refs[:n], refs[n:]
        scratch, r_sems = refs[:n_scratch], refs[n_scratch:]
        step = pl.program_id(0)
        for axis in range(1, len(grid)):
            step = step * grid[axis] + pl.program_id(axis)

        @pl.when(step == 0)
        def _():
            _exchange_start(r_ins, r_outs, modes, *r_sems)

        body(*ins, *outs, *scratch)

        @pl.when(step == n_steps - 1)
        def _():
            _exchange_wait(r_ins, r_outs, modes, *r_sems)

    outs = pl.pallas_call(
        riding, name=name, grid=grid, in_specs=list(in_specs) + [ANY_SPEC] * n,
        out_specs=list(out_specs) + [ANY_SPEC] * n, out_shape=list(out_shape) + moved_shape,
        scratch_shapes=list(scratch_shapes) + sems, compiler_params=compiler_params)(*args, *arrays)
    return outs[:n_out], outs[n_out:]


def _in_proj_fwd(x, g, wint, tb, ride=None):
    s = x.shape[0]

    def body(x_ref, g_ref, w_ref, z_ref, qkv_ref, hb_ref):
        xv = x_ref[...]
        h = (xv * _rms(xv) * g_ref[...]).astype(BF16)
        hb_ref[...] = h
        z = _mm_nt(h, w_ref[...])
        z_ref[...] = z
        qkv_ref[:, 0:D_GROUP] = (z[:, 5 * D_GROUP:6 * D_GROUP] * ATT_SCALE).astype(BF16)
        qkv_ref[:, D_GROUP:3 * D_GROUP] = z[:, 6 * D_GROUP:8 * D_GROUP].astype(BF16)

    return _call(
        body, (x, g, wint), ride, name="in_proj_fwd", grid=(s // tb,),
        in_specs=[_rows(tb, D_MODEL), _whole((1, D_MODEL)), _resident((D_IN, D_MODEL))],
        out_specs=[_rows(tb, D_IN), _rows(tb, 3 * D_GROUP), _rows(tb, D_MODEL)],
        out_shape=[_sds((s, D_IN)), _sds((s, 3 * D_GROUP), BF16), _sds((s, D_MODEL), BF16)],
        compiler_params=_params(("arbitrary",), 48))


def _gelu(x):
    return 0.5 * x * (1.0 + jnp.tanh(GELU_C * (x + GELU_A * x * x * x)))


def _gelu_grad(x):
    t = jnp.tanh(GELU_C * (x + GELU_A * x * x * x))
    return 0.5 * (1.0 + t) + 0.5 * x * (1.0 - t * t) * GELU_C * (1.0 + 3.0 * GELU_A * x * x)


def _head_lane(width):
    return lax.broadcasted_iota(jnp.int32, (1, width), 1) // HEAD_DIM


def _gating_chunk(wm_ref, bexp_ref, vc, lane_h):
    f = bexp_ref[...]
    for h in range(D_GROUP // HEAD_DIM):
        f = f + _mm(wm_ref[h], jnp.where(lane_h == h, vc, 0))
    return f


def _halo_specs(s, tb, width_blocks):
    per = tb // HALO
    prev = pl.BlockSpec((HALO, width_blocks), lambda i: (jnp.maximum(i * per - 1, 0), 0))
    nxt = pl.BlockSpec((HALO, width_blocks), lambda i: (jnp.minimum((i + 1) * per, s // HALO - 1), 0))
    return prev, nxt


SUBLANES = 8
TAP_SLACK = 24


def _taps(buf_ref, shifted_ref, first, n_taps, rows, visit):
    for residue in range(SUBLANES):
        taps = [j for j in range(n_taps) if (first + j) % SUBLANES == residue]
        if not taps:
            continue
        lo = first + taps[0]
        span = first + taps[-1] - lo + rows
        shifted_ref[0:span, :] = buf_ref[pl.ds(lo, span), :]
        for j in taps:
            visit(j, shifted_ref[pl.ds(first + j - lo, rows), :])


def _mix_fwd(z, vg, wm, bexp, scw, ccw, lng, lnb, tb, ride=None):
    s = z.shape[0]
    prev_spec, _ = _halo_specs(s, tb, D_IN)

    def body(z_ref, zp_ref, vg_ref, wm_ref, bexp_ref, scw_ref, ccw_ref, lng_ref, lnb_ref,
             ya_ref, yb_ref, yd_ref, c_ref, pbuf, hbuf, shifted):
        keep = (pl.program_id(0) > 0).astype(F32)
        lane_h = _head_lane(D_GROUP)
        ga = _gelu(z_ref[:, 0:2 * D_GROUP])
        u, v = ga[:, :D_GROUP], ga[:, D_GROUP:]
        vn = (v * _rms(v) * vg_ref[...]).astype(BF16)
        for n in range(tb // CHUNK):
            rows = slice(n * CHUNK, (n + 1) * CHUNK)
            ya_ref[rows, :] = u[rows] * _gating_chunk(wm_ref, bexp_ref, vn[rows], lane_h)
        p = z_ref[:, 3 * D_GROUP:4 * D_GROUP] * z_ref[:, 4 * D_GROUP:5 * D_GROUP]
        pbuf[0:HALO, :] = zp_ref[:, 3 * D_GROUP:4 * D_GROUP] * zp_ref[:, 4 * D_GROUP:5 * D_GROUP] * keep
        pbuf[HALO:HALO + tb, :] = p
        cv = scw_ref[K_SHORT - 1:K_SHORT, :] * p
        for k in range(K_SHORT - 1):
            cv = cv + scw_ref[k:k + 1, :] * pbuf[pl.ds(HALO - (K_SHORT - 1) + k, tb), :]
        yb_ref[...] = z_ref[:, 2 * D_GROUP:3 * D_GROUP] * cv
        hbuf[0:HALO, :] = zp_ref[:, 8 * D_GROUP:9 * D_GROUP] * _sigmoid(zp_ref[:, 9 * D_GROUP:10 * D_GROUP]) * keep
        hbuf[HALO:HALO + tb, :] = z_ref[:, 8 * D_GROUP:9 * D_GROUP] * _sigmoid(z_ref[:, 9 * D_GROUP:10 * D_GROUP])
        conv = [jnp.zeros((tb, D_GROUP), F32)]

        def tap(k, window):
            conv[0] = conv[0] + ccw_ref[k:k + 1, :] * window

        _taps(hbuf, shifted, HALO - (K_CONF - 1), K_CONF, tb, tap)
        c = conv[0]
        c_ref[...] = c
        xc = c - jnp.mean(c, axis=-1, keepdims=True)
        ln = xc * lax.rsqrt(jnp.mean(xc * xc, axis=-1, keepdims=True) + EPS) * lng_ref[...] + lnb_ref[...]
        yd_ref[...] = ln * _sigmoid(ln)

    grp = _rows(tb, D_GROUP)
    return _call(
        body, (z, z, vg, wm, bexp, scw, ccw, lng, lnb), ride, name="mix_fwd", grid=(s // tb,),
        in_specs=[_rows(tb, D_IN), prev_spec, _whole((1, D_GROUP)), _whole(wm.shape), _whole(bexp.shape),
                  _whole(scw.shape), _whole(ccw.shape), _whole((1, D_GROUP)), _whole((1, D_GROUP))],
        out_specs=[grp, grp, grp, grp],
        out_shape=[_sds((s, D_GROUP))] * 4,
        scratch_shapes=[pltpu.VMEM((HALO + tb, D_GROUP), F32), pltpu.VMEM((HALO + tb, D_GROUP), F32),
                        pltpu.VMEM((tb + TAP_SLACK, D_GROUP), F32)],
        compiler_params=_params(("arbitrary",), 40))


def _stick_tile(qh, kt, causal, c, upper):
    x = _mm_nt(qh, kt)
    soft = jnp.log(1.0 + jnp.exp(-jnp.abs(x)))
    lb = jnp.minimum(x, 0.0) - soft
    lom = jnp.where(causal, -jnp.maximum(x, 0.0) - soft, 0.0)
    hi, lo = _split_bf16(lom)
    stick = c + _mm(hi, upper[...]) + _mm(lo, upper[...])
    w = jnp.where(causal, jnp.exp(lb + stick), 0.0)
    return w, lb, lom


def _triangle(n, diagonal):
    return jnp.tri(n, n, diagonal, dtype=BF16)


def _causal_tile(qi, tq, k0, tk):
    qpos = qi * tq + lax.broadcasted_iota(jnp.int32, (tq, 1), 0)
    return k0 + lax.broadcasted_iota(jnp.int32, (1, tk), 1) < qpos


def _sticks_alive(cs):
    longest = cs[0]
    for c in cs[1:]:
        longest = jnp.maximum(longest, c)
    return (jnp.max(longest) > LOG_CUT).astype(jnp.int32)


def _walk(body, qi, tq, tk, init):
    start = (((qi + 1) * tq - 1) // tk, jnp.int32(1)) + tuple(init)
    return lax.while_loop(lambda cr: jnp.logical_and(cr[0] >= 0, cr[1] > 0), body, start)[2:]


def _attn_fwd(qkv, tq, tk, ride=None):
    s = qkv.shape[0]
    n_heads = D_GROUP // HEAD_DIM

    def body(q_ref, k_ref, v_ref, upper, o_ref):
        qi = pl.program_id(0)
        q = q_ref[...]
        lane_h = _head_lane(D_GROUP)
        qhs = [jnp.where(lane_h == h, q, 0) for h in range(n_heads)]

        def step(carry):
            kb, _, acc = carry[:3]
            cs = list(carry[3:])
            k0 = pl.multiple_of(kb * tk, tk)
            kt = k_ref[pl.ds(k0, tk), :]
            vt = v_ref[pl.ds(k0, tk), :]
            causal = _causal_tile(qi, tq, k0, tk)
            for h in range(n_heads):
                w, _, lom = _stick_tile(qhs[h], kt, causal, cs[h], upper)
                acc = acc + _mm(w.astype(BF16), jnp.where(lane_h == h, vt, 0))
                cs[h] = cs[h] + jnp.sum(lom, axis=1, keepdims=True)
            return (kb - 1, _sticks_alive(cs), acc) + tuple(cs)

        init = [jnp.zeros((tq, D_GROUP), F32)] + [jnp.zeros((tq, 1), F32)] * n_heads
        o_ref[...] = _walk(step, qi, tq, tk, init)[0]

    return _call(
        body, (qkv, qkv, qkv, _triangle(tk, -1)), ride, name="attn_fwd", grid=(s // tq,),
        in_specs=[pl.BlockSpec((tq, D_GROUP), lambda qi: (qi, 0)),
                  pl.BlockSpec((s, D_GROUP), lambda qi: (0, 1), pipeline_mode=pl.Buffered(1)),
                  pl.BlockSpec((s, D_GROUP), lambda qi: (0, 2), pipeline_mode=pl.Buffered(1)),
                  _resident((tk, tk))],
        out_specs=[pl.BlockSpec((tq, D_GROUP), lambda qi: (qi, 0))],
        out_shape=[_sds((s, D_GROUP))],
        compiler_params=_params(("arbitrary",), 40))


def _out_proj_fwd(ys, x, mg, wout, tb):
    s = x.shape[0]

    def body(ya_ref, yb_ref, yc_ref, yd_ref, x_ref, mg_ref, w_ref, o_ref):
        acc = x_ref[...]
        for gi, y_ref in enumerate((ya_ref, yb_ref, yc_ref, yd_ref)):
            cols = slice(gi * D_GROUP, (gi + 1) * D_GROUP)
            y = y_ref[...]
            acc = acc + _mm((y * _rms(y) * mg_ref[:, cols]).astype(BF16), w_ref[cols, :])
        o_ref[...] = acc

    grp = _rows(tb, D_GROUP)
    return pl.pallas_call(
        body, name="out_proj_fwd", grid=(s // tb,),
        in_specs=[grp, grp, grp, grp, _rows(tb, D_MODEL), _whole((1, D_MODEL)), _whole((D_MODEL, D_MODEL))],
        out_specs=_rows(tb, D_MODEL), out_shape=_sds((s, D_MODEL)),
        compiler_params=_params(("parallel",), 32),
    )(*ys, x, mg, wout)


def _ffn_fwd(x, g, wup, wdn, tb, ride=None):
    s = x.shape[0]
    ff = D_FF // N_DEV

    def body(x_ref, g_ref, wu_ref, wd_ref, o_ref, a_ref):
        xv = x_ref[...]
        h = (xv * _rms(xv) * g_ref[...]).astype(BF16)
        acc = xv
        for d in range(N_DEV):
            a = jnp.maximum(_mm(h, wu_ref[d]), 0.0)
            a_ref[:, d * ff:(d + 1) * ff] = a.astype(BF16)
            acc = acc + _mm((a * a).astype(BF16), wd_ref[d])
        o_ref[...] = acc

    return _call(
        body, (x, g, wup, wdn), ride, name="ffn_fwd", grid=(s // tb,),
        in_specs=[_rows(tb, D_MODEL), _whole((1, D_MODEL)), _resident((N_DEV, D_MODEL, ff)), _resident((N_DEV, ff, D_MODEL))],
        out_specs=[_rows(tb, D_MODEL), _rows(tb, D_FF)], out_shape=[_sds((s, D_MODEL)), _sds((s, D_FF), BF16)],
        compiler_params=_params(("arbitrary",), 56))


def _loss_head(x, g, tgt, tb):
    s = x.shape[0]

    def body(x_ref, g_ref, t_ref, dx_ref, dg_ref, loss_ref):
        @pl.when(pl.program_id(0) == 0)
        def _():
            dg_ref[...] = jnp.zeros_like(dg_ref)
            loss_ref[...] = jnp.zeros_like(loss_ref)

        xv = x_ref[...]
        r = _rms(xv)
        xh = xv * r
        err = xh * g_ref[...] - t_ref[...]
        loss_ref[...] += 0.5 * jnp.sum(jnp.mean(err * err, axis=-1, keepdims=True))
        dy = err * (1.0 / D_MODEL)
        dg_ref[...] += jnp.sum(dy * xh, axis=0, keepdims=True)
        dx_ref[...] = _rms_bwd(dy * g_ref[...], xh, r)

    return pl.pallas_call(
        body, name="loss_head", grid=(s // tb,),
        in_specs=[_rows(tb, D_MODEL), _whole((1, D_MODEL)), _rows(tb, D_MODEL)],
        out_specs=[_rows(tb, D_MODEL), _whole((1, D_MODEL)), _whole((8, 128))],
        out_shape=[_sds((s, D_MODEL)), _sds((1, D_MODEL)), _sds((8, 128))],
        compiler_params=_params(("arbitrary",), 32),
    )(x, g, tgt)


def _ffn_bwd(x1, act, dx2, g, wup, wdn, tb, ride=None):
    s = x1.shape[0]
    ff = D_FF // N_DEV

    def body(x_ref, a_ref, dy_ref, g_ref, wu_ref, wd_ref, dx_ref, hb_ref, dpre_ref, dg_ref):
        @pl.when(pl.program_id(0) == 0)
        def _():
            dg_ref[...] = jnp.zeros_like(dg_ref)

        xv = x_ref[...]
        r = _rms(xv)
        xh = xv * r
        hb_ref[...] = (xh * g_ref[...]).astype(BF16)
        dyv = dy_ref[...]
        dyb = dyv.astype(BF16)
        dh = jnp.zeros((tb, D_MODEL), F32)
        for d in range(N_DEV):
            cols = slice(d * ff, (d + 1) * ff)
            a = a_ref[:, cols].astype(F32)
            dpre = (_mm_nt(dyb, wd_ref[d]) * (2.0 * a)).astype(BF16)
            dpre_ref[:, cols] = dpre
            dh = dh + _mm_nt(dpre, wu_ref[d])
        dg_ref[...] += jnp.sum(dh * xh, axis=0, keepdims=True)
        dx_ref[...] = dyv + _rms_bwd(dh * g_ref[...], xh, r)

    return _call(
        body, (x1, act, dx2, g, wup, wdn), ride, name="ffn_bwd", grid=(s // tb,),
        in_specs=[_rows(tb, D_MODEL), _rows(tb, D_FF), _rows(tb, D_MODEL), _whole((1, D_MODEL)),
                  _resident((N_DEV, D_MODEL, ff)), _resident((N_DEV, ff, D_MODEL))],
        out_specs=[_rows(tb, D_MODEL), _rows(tb, D_MODEL), _rows(tb, D_FF), _whole((1, D_MODEL))],
        out_shape=[_sds((s, D_MODEL)), _sds((s, D_MODEL), BF16), _sds((s, D_FF), BF16), _sds((1, D_MODEL))],
        compiler_params=_params(("arbitrary",), 58))


def _tn_slabs(a, b, tb, transpose_slabs, name, square_b=False):
    s, m = a.shape
    width = b.shape[1] // N_DEV
    n_steps = s // tb
    slab = (width, m) if transpose_slabs else (m, width)

    def body(a_ref, b_ref, o_hbm, acc, stage, sem):
        step = pl.program_id(0)

        @pl.when(step == 0)
        def _():
            acc[...] = jnp.zeros_like(acc)

        bv = b_ref[...]
        if square_b:
            bv = bv.astype(F32)
            bv = bv * bv
        acc[...] += _mm_tn(a_ref[...].astype(BF16), bv.astype(BF16))

        @pl.when(step == n_steps - 1)
        def _():
            for d in range(N_DEV):
                cols = acc.at[:, pl.ds(d * width, width)]
                if transpose_slabs:
                    stage[...] = cols[...].T
                cp = pltpu.make_async_copy(stage if transpose_slabs else cols, o_hbm.at[d], sem.at[0])
                cp.start()
                cp.wait()

    return pl.pallas_call(
        body, name=name, grid=(n_steps,),
        in_specs=[_rows(tb, m), _rows(tb, b.shape[1])], out_specs=ANY_SPEC, out_shape=_sds((N_DEV,) + slab),
        scratch_shapes=[pltpu.VMEM((m, b.shape[1]), F32), pltpu.VMEM(slab, F32), pltpu.SemaphoreType.DMA((1,))],
        compiler_params=_params(("arbitrary",), 56),
    )(a, b)


def _out_proj_bwd(dx1, ys, mg, wout, tb):
    s = dx1.shape[0]

    def body(dx_ref, ya_ref, yb_ref, yc_ref, yd_ref, mg_ref, w_ref,
             dya_ref, dyb_ref, dyc_ref, dyd_ref, dmg_ref, dw_ref, yn_ref):
        @pl.when(pl.program_id(0) == 0)
        def _():
            dmg_ref[...] = jnp.zeros_like(dmg_ref)
            dw_ref[...] = jnp.zeros_like(dw_ref)

        dxb = dx_ref[...].astype(BF16)
        dyn = _mm_nt(dxb, w_ref[...])
        groups = ((ya_ref, dya_ref), (yb_ref, dyb_ref), (yc_ref, dyc_ref), (yd_ref, dyd_ref))
        for gi, (y_ref, dy_ref) in enumerate(groups):
            cols = slice(gi * D_GROUP, (gi + 1) * D_GROUP)
            y = y_ref[...]
            r = _rms(y)
            n = y * r
            gain = mg_ref[:, cols]
            dn = dyn[:, cols]
            yn_ref[:, cols] = (n * gain).astype(BF16)
            dmg_ref[:, cols] += jnp.sum(dn * n, axis=0, keepdims=True)
            dy_ref[...] = _rms_bwd(dn * gain, n, r)
        dw_ref[...] += _mm_tn(yn_ref[...], dxb)

    grp = _rows(tb, D_GROUP)
    return pl.pallas_call(
        body, name="out_proj_bwd", grid=(s // tb,),
        in_specs=[_rows(tb, D_MODEL), grp, grp, grp, grp, _whole((1, D_MODEL)), _resident((D_MODEL, D_MODEL))],
        out_specs=[grp, grp, grp, grp, _whole((1, D_MODEL)), _whole((D_MODEL, D_MODEL))],
        out_shape=[_sds((s, D_GROUP))] * 4 + [_sds((1, D_MODEL)), _sds((D_MODEL, D_MODEL))],
        scratch_shapes=[pltpu.VMEM((tb, D_MODEL), BF16)],
        compiler_params=_params(("arbitrary",), 40),
    )(dx1, *ys, mg, wout)


def _attn_bwd(qkv, do, o, tq, tk, ride=None):
    s = qkv.shape[0]
    nq = s // tq
    n_pairs = D_GROUP // PAIR

    def body(q_ref, k_ref, v_ref, do_ref, o_ref, upper, upper_eq, dq_ref, dk_hbm, dv_hbm, dk_acc, dv_acc, sems):
        hp, qi = pl.program_id(0), pl.program_id(1)

        @pl.when(qi == 0)
        def _():
            dk_acc[...] = jnp.zeros_like(dk_acc)
            dv_acc[...] = jnp.zeros_like(dv_acc)

        q = q_ref[...]
        dob = do_ref[...].astype(BF16)
        prod = dob.astype(F32) * o_ref[...]
        lane_h = _head_lane(PAIR)
        heads = []
        for h in range(HEADS_PER_PAIR):
            in_head = lane_h == h
            total = jnp.sum(jnp.where(in_head, prod, 0.0), axis=1, keepdims=True)
            heads.append((in_head, jnp.where(in_head, q, 0), jnp.where(in_head, dob, 0), total))

        def step(carry):
            kb, _, acc = carry[:3]
            cs = list(carry[3:3 + HEADS_PER_PAIR])
            nears = list(carry[3 + HEADS_PER_PAIR:])
            k0 = pl.multiple_of(kb * tk, tk)
            kt = k_ref[pl.ds(k0, tk), :]
            vt = v_ref[pl.ds(k0, tk), :]
            causal = _causal_tile(qi, tq, k0, tk)
            dk_t = jnp.zeros((tk, PAIR), F32)
            dv_t = jnp.zeros((tk, PAIR), F32)
            for h, (in_head, qh, doh, total) in enumerate(heads):
                w, lb, lom = _stick_tile(qh, kt, causal, cs[h], upper)
                wb = w.astype(BF16)
                gw = _mm_nt(doh, vt) * wb.astype(F32)
                hi, lo = _split_bf16(gw)
                far = total - nears[h] - _mm(hi, upper_eq[...]) - _mm(lo, upper_eq[...])
                beta = jnp.exp(lb)
                dxb = jnp.where(causal, gw - beta * (gw + far), 0.0).astype(BF16)
                acc = acc + _mm(dxb, jnp.where(in_head, kt, 0))
                dk_t = dk_t + _mm_tn(dxb, qh)
                dv_t = dv_t + _mm_tn(wb, doh)
                cs[h] = cs[h] + jnp.sum(lom, axis=1, keepdims=True)
                nears[h] = nears[h] + jnp.sum(gw, axis=1, keepdims=True)
            dk_acc[pl.ds(k0, tk), :] += dk_t
            dv_acc[pl.ds(k0, tk), :] += dv_t
            return (kb - 1, _sticks_alive(cs), acc) + tuple(cs) + tuple(nears)

        init = [jnp.zeros((tq, PAIR), F32)] + [jnp.zeros((tq, 1), F32)] * (2 * HEADS_PER_PAIR)
        dq_ref[...] = _walk(step, qi, tq, tk, init)[0]

        @pl.when(qi == nq - 1)
        def _():
            ck = pltpu.make_async_copy(dk_acc, dk_hbm.at[hp], sems.at[0])
            cv = pltpu.make_async_copy(dv_acc, dv_hbm.at[hp], sems.at[1])
            ck.start()
            cv.start()
            ck.wait()
            cv.wait()

    blk = pl.BlockSpec((tq, PAIR), lambda hp, qi: (qi, hp))
    return _call(
        body, (qkv, qkv, qkv, do, o, _triangle(tk, -1), _triangle(tk, 0)), ride, name="attn_bwd", grid=(n_pairs, nq),
        in_specs=[blk, pl.BlockSpec((s, PAIR), lambda hp, qi: (0, 2 + hp)),
                  pl.BlockSpec((s, PAIR), lambda hp, qi: (0, 4 + hp)), blk, blk, _resident((tk, tk)), _resident((tk, tk))],
        out_specs=[blk, ANY_SPEC, ANY_SPEC],
        out_shape=[_sds((s, D_GROUP)), _sds((n_pairs, s, PAIR)), _sds((n_pairs, s, PAIR))],
        scratch_shapes=[pltpu.VMEM((s, PAIR), F32), pltpu.VMEM((s, PAIR), F32), pltpu.SemaphoreType.DMA((2,))],
        compiler_params=_params(("arbitrary", "arbitrary"), 56))


def _mix_bwd(z, conv, dya, dyb, dyd, dq, dk, dv, hb, vg, wm, wmt, bexp, scw, ccw, lng, lnb, tb, ride=None):
    s = z.shape[0]
    n_steps = s // tb
    prev_spec, next_spec = _halo_specs(s, tb, D_IN)
    _, next_grp = _halo_specs(s, tb, D_GROUP)
    ext = tb + HALO

    def body(z_ref, zp_ref, zn_ref, dya_ref, dyb_ref, dybn_ref, dyd_ref, dydn_ref, c_ref, cn_ref, dq_ref, dk0_ref, dk1_ref,
             dv0_ref, dv1_ref, hbp_ref, hb_ref, vg_ref, wm_ref, wmt_ref, bexp_ref, scw_ref, ccw_ref, lng_ref, lnb_ref,
             dz_ref, dvg_ref, dws_ref, dbs_ref, dscw_ref, dccw_ref, dlng_ref, dlnb_ref, dwin_hbm,
             pbuf, hbuf, gbuf, cbuf, dubuf, dvnbuf, shifted, win_acc, dz_prev, win_sem):
        i = pl.program_id(0)

        @pl.when(i == 0)
        def _():
            for ref in (dvg_ref, dws_ref, dbs_ref, dscw_ref, dccw_ref, dlng_ref, dlnb_ref, win_acc, dz_prev):
                ref[...] = jnp.zeros_like(ref)

        win_acc[...] += _mm_tn(dz_prev[...], hbp_ref[...])

        keep_prev = (i > 0).astype(F32)
        keep_next = (i < n_steps - 1).astype(F32)
        lane_h = _head_lane(D_GROUP)

        za = z_ref[:, 0:2 * D_GROUP]
        ga = _gelu(za)
        u, v = ga[:, :D_GROUP], ga[:, D_GROUP:]
        r = _rms(v)
        vh = v * r
        vn = (vh * vg_ref[...]).astype(BF16)
        tril = lax.broadcasted_iota(jnp.int32, (CHUNK, CHUNK), 0) >= lax.broadcasted_iota(jnp.int32, (CHUNK, CHUNK), 1)
        dbias = jnp.zeros((CHUNK, D_GROUP), F32)
        for n in range(tb // CHUNK):
            rows = slice(n * CHUNK, (n + 1) * CHUNK)
            vc = vn[rows]
            dy = dya_ref[rows, :]
            dubuf[rows, :] = dy * _gating_chunk(wm_ref, bexp_ref, vc, lane_h)
            df = dy * u[rows]
            dfb = df.astype(BF16)
            dvn = jnp.zeros((CHUNK, D_GROUP), F32)
            for h in range(D_GROUP // HEAD_DIM):
                dfh = jnp.where(lane_h == h, dfb, 0)
                dvn = dvn + _mm(wmt_ref[h], dfh)
                dws_ref[h] += jnp.where(tril, _mm_nt(dfh, vc), 0.0)
            dvnbuf[rows, :] = dvn
            dbias = dbias + df
        for h in range(D_GROUP // HEAD_DIM):
            per_head = jnp.sum(jnp.where(lane_h == h, dbias, 0.0), axis=1, keepdims=True)
            dbs_ref[...] += per_head * (lax.broadcasted_iota(jnp.int32, (1, CHUNK), 1) == h).astype(F32)
        dvn = dvnbuf[...]
        dvg_ref[...] += jnp.sum(dvn * vh, axis=0, keepdims=True)
        dgelu = _gelu_grad(za)
        dz_ref[:, 0:D_GROUP] = (dubuf[...] * dgelu[:, :D_GROUP]).astype(BF16)
        dz_ref[:, D_GROUP:2 * D_GROUP] = (_rms_bwd(dvn * vg_ref[...], vh, r) * dgelu[:, D_GROUP:]).astype(BF16)

        gate_b = z_ref[:, 2 * D_GROUP:3 * D_GROUP]
        gate_c = z_ref[:, 3 * D_GROUP:4 * D_GROUP]
        hh = z_ref[:, 4 * D_GROUP:5 * D_GROUP]
        p = gate_c * hh
        pbuf[0:HALO, :] = zp_ref[:, 3 * D_GROUP:4 * D_GROUP] * zp_ref[:, 4 * D_GROUP:5 * D_GROUP] * keep_prev
        pbuf[HALO:HALO + tb, :] = p
        dyb_v = dyb_ref[...]
        dcv = dyb_v * gate_b
        gbuf[0:tb, :] = dcv
        gbuf[tb:ext, :] = dybn_ref[...] * zn_ref[:, 2 * D_GROUP:3 * D_GROUP] * keep_next
        cv = scw_ref[K_SHORT - 1:K_SHORT, :] * p
        dp = scw_ref[K_SHORT - 1:K_SHORT, :] * dcv
        dscw_ref[K_SHORT - 1:K_SHORT, :] += jnp.sum(dcv * p, axis=0, keepdims=True)
        for k in range(K_SHORT - 1):
            earlier = pbuf[pl.ds(HALO - (K_SHORT - 1) + k, tb), :]
            cv = cv + scw_ref[k:k + 1, :] * earlier
            dp = dp + scw_ref[k:k + 1, :] * gbuf[pl.ds(K_SHORT - 1 - k, tb), :]
            dscw_ref[k:k + 1, :] += jnp.sum(dcv * earlier, axis=0, keepdims=True)
        dz_ref[:, 2 * D_GROUP:3 * D_GROUP] = (dyb_v * cv).astype(BF16)
        dz_ref[:, 3 * D_GROUP:4 * D_GROUP] = (dp * hh).astype(BF16)
        dz_ref[:, 4 * D_GROUP:5 * D_GROUP] = (dp * gate_c).astype(BF16)

        dz_ref[:, 5 * D_GROUP:6 * D_GROUP] = (dq_ref[...] * ATT_SCALE).astype(BF16)
        dz_ref[:, 6 * D_GROUP:6 * D_GROUP + PAIR] = dk0_ref[...].astype(BF16)
        dz_ref[:, 6 * D_GROUP + PAIR:7 * D_GROUP] = dk1_ref[...].astype(BF16)
        dz_ref[:, 7 * D_GROUP:7 * D_GROUP + PAIR] = dv0_ref[...].astype(BF16)
        dz_ref[:, 7 * D_GROUP + PAIR:8 * D_GROUP] = dv1_ref[...].astype(BF16)

        a = z_ref[:, 8 * D_GROUP:9 * D_GROUP]
        sg = _sigmoid(z_ref[:, 9 * D_GROUP:10 * D_GROUP])
        hbuf[0:HALO, :] = zp_ref[:, 8 * D_GROUP:9 * D_GROUP] * _sigmoid(zp_ref[:, 9 * D_GROUP:10 * D_GROUP]) * keep_prev
        hbuf[HALO:HALO + tb, :] = a * sg
        c = jnp.concatenate([c_ref[...], cn_ref[...]], axis=0)
        xc = c - jnp.mean(c, axis=-1, keepdims=True)
        rs = lax.rsqrt(jnp.mean(xc * xc, axis=-1, keepdims=True) + EPS)
        xh = xc * rs
        ln = xh * lng_ref[...] + lnb_ref[...]
        sl = _sigmoid(ln)
        dy_ext = jnp.concatenate([dyd_ref[...], dydn_ref[...] * keep_next], axis=0)
        dln = dy_ext * sl * (1.0 + ln * (1.0 - sl))
        dlng_ref[...] += jnp.sum(dln[:tb] * xh[:tb], axis=0, keepdims=True)
        dlnb_ref[...] += jnp.sum(dln[:tb], axis=0, keepdims=True)
        dxh = dln * lng_ref[...]
        dc = rs * (dxh - jnp.mean(dxh, axis=-1, keepdims=True) - xh * jnp.mean(dxh * xh, axis=-1, keepdims=True))
        cbuf[...] = dc
        dc_blk = dc[:tb]
        grad_in = [jnp.zeros((tb, D_GROUP), F32)]

        def tap_input(j, window):
            k = K_CONF - 1 - j
            grad_in[0] = grad_in[0] + ccw_ref[k:k + 1, :] * window

        def tap_filter(k, window):
            dccw_ref[k:k + 1, :] += jnp.sum(dc_blk * window, axis=0, keepdims=True)

        _taps(cbuf, shifted, 0, K_CONF, tb, tap_input)
        _taps(hbuf, shifted, HALO - (K_CONF - 1), K_CONF, tb, tap_filter)
        dhd = grad_in[0]
        dz_ref[:, 8 * D_GROUP:9 * D_GROUP] = (dhd * sg).astype(BF16)
        dz_ref[:, 9 * D_GROUP:10 * D_GROUP] = (dhd * a * sg * (1.0 - sg)).astype(BF16)

        dz_prev[...] = dz_ref[...]

        @pl.when(i == n_steps - 1)
        def _():
            win_acc[...] += _mm_tn(dz_ref[...], hb_ref[...])
            cp = pltpu.make_async_copy(win_acc, dwin_hbm, win_sem.at[0])
            cp.start()
            cp.wait()

    grp = _rows(tb, D_GROUP)
    pair0 = pl.BlockSpec((None, tb, PAIR), lambda i: (0, i, 0))
    pair1 = pl.BlockSpec((None, tb, PAIR), lambda i: (1, i, 0))
    small = [_sds((1, D_GROUP)), _sds((4, CHUNK, CHUNK)), _sds((CHUNK, CHUNK)), _sds((8, D_GROUP)),
             _sds((HALO, D_GROUP)), _sds((1, D_GROUP)), _sds((1, D_GROUP))]
    return _call(
        body, (z, z, z, dya, dyb, dyb, dyd, dyd, conv, conv, dq, dk, dk, dv, dv, hb, hb, vg, wm, wmt, bexp, scw, ccw, lng, lnb),
        ride, name="mix_bwd", grid=(n_steps,),
        in_specs=[_rows(tb, D_IN), prev_spec, next_spec, grp, grp, next_grp, grp, next_grp, grp, next_grp, grp,
                  pair0, pair1, pair0, pair1,
                  pl.BlockSpec((tb, D_MODEL), lambda i: (jnp.maximum(i - 1, 0), 0)), _rows(tb, D_MODEL),
                  _whole((1, D_GROUP)), _whole(wm.shape), _whole(wmt.shape), _whole(bexp.shape), _whole(scw.shape),
                  _whole(ccw.shape), _whole((1, D_GROUP)), _whole((1, D_GROUP))],
        out_specs=[_rows(tb, D_IN)] + [_whole(t.shape) for t in small] + [ANY_SPEC],
        out_shape=[_sds((s, D_IN), BF16)] + small + [_sds((D_IN, D_MODEL))],
        scratch_shapes=[pltpu.VMEM((HALO + tb, D_GROUP), F32), pltpu.VMEM((HALO + tb, D_GROUP), F32),
                        pltpu.VMEM((ext, D_GROUP), F32), pltpu.VMEM((ext, D_GROUP), F32),
                        pltpu.VMEM((tb, D_GROUP), F32), pltpu.VMEM((tb, D_GROUP), F32),
                        pltpu.VMEM((tb + TAP_SLACK, D_GROUP), F32), pltpu.VMEM((D_IN, D_MODEL), F32),
                        pltpu.VMEM((tb, D_IN), BF16), pltpu.SemaphoreType.DMA((1,))],
        compiler_params=_params(("arbitrary",), 56))


def _in_proj_bwd(x, dz, dres, g, wint, tb, ride=None):
    s = x.shape[0]

    def body(x_ref, dz_ref, dr_ref, g_ref, w_ref, dx_ref, dg_ref):
        @pl.when(pl.program_id(0) == 0)
        def _():
            dg_ref[...] = jnp.zeros_like(dg_ref)

        xv = x_ref[...]
        r = _rms(xv)
        xh = xv * r
        dh = _mm(dz_ref[...], w_ref[...])
        dg_ref[...] += jnp.sum(dh * xh, axis=0, keepdims=True)
        dx_ref[...] = dr_ref[...] + _rms_bwd(dh * g_ref[...], xh, r)

    return _call(
        body, (x, dz, dres, g, wint), ride, name="in_proj_bwd", grid=(s // tb,),
        in_specs=[_rows(tb, D_MODEL), _rows(tb, D_IN), _rows(tb, D_MODEL), _whole((1, D_MODEL)), _resident((D_IN, D_MODEL))],
        out_specs=[_rows(tb, D_MODEL), _whole((1, D_MODEL))],
        out_shape=[_sds((s, D_MODEL)), _sds((1, D_MODEL))],
        compiler_params=_params(("arbitrary",), 48))


def _adamw(w, g, m, v):
    m = ADAM_B1 * m + (1.0 - ADAM_B1) * g
    v = ADAM_B2 * v + (1.0 - ADAM_B2) * (g * g)
    m_hat = m / (1.0 - ADAM_B1 ** ADAM_STEP)
    v_hat = v / (1.0 - ADAM_B2 ** ADAM_STEP)
    delta = -ADAM_LR * (m_hat / (jnp.sqrt(v_hat) + ADAM_EPS) + ADAM_WD * w)
    return delta, m, v


def _reduce_adamw(parts, w, m, v, tb, name):
    rows, cols = w.shape

    def body(p_ref, w_ref, m_ref, v_ref, g_ref, d_ref, m2_ref, v2_ref):
        g = p_ref[0].astype(F32)
        for j in range(1, N_DEV):
            g = g + p_ref[j].astype(F32)
        g_ref[...] = g
        d_ref[...], m2_ref[...], v2_ref[...] = _adamw(w_ref[...], g, m_ref[...], v_ref[...])

    blk = _rows(tb, cols)
    return pl.pallas_call(
        body, name=name, grid=(rows // tb,),
        in_specs=[pl.BlockSpec((N_DEV, tb, cols), lambda i: (0, i, 0)), blk, blk, blk],
        out_specs=[blk] * 4, out_shape=[_sds((rows, cols))] * 4,
        compiler_params=_params(("parallel",), 32),
    )(parts, w, m, v)


def _reduce_adamw_layer(parts, w, m, v, layer, taken_over, tb, name):
    _, rows, cols = w.shape
    n_given = 4 if taken_over is None else 8

    def body(*refs):
        p_ref, w_ref, m_ref, v_ref = refs[:4]
        g_ref, d_ref, m2_ref, v2_ref = refs[n_given:]
        g = p_ref[0].astype(F32)
        for j in range(1, N_DEV):
            g = g + p_ref[j].astype(F32)
        g_ref[...] = g
        d_ref[...], m2_ref[...], v2_ref[...] = _adamw(w_ref[...], g, m_ref[...], v_ref[...])

    blk = pl.BlockSpec((None, tb, cols), lambda i: (layer, i, 0))
    return pl.pallas_call(
        body, name=name, grid=(rows // tb,),
        in_specs=[pl.BlockSpec((N_DEV, tb, cols), lambda i: (0, i, 0)), blk, blk, blk] + [ANY_SPEC] * (n_given - 4),
        out_specs=[blk] * 4, out_shape=[_sds(w.shape)] * 4,
        input_output_aliases={} if taken_over is None else {4 + k: k for k in range(4)},
        compiler_params=_params(("arbitrary",), 32),
    )(parts, w, m, v, *(taken_over or ()))


LANES = 128
PACK_ALIGN = 8 * LANES


def _pack(arrays):
    pieces = []
    for a in arrays:
        flat = a.reshape(-1)
        pieces.append(jnp.pad(flat, (0, -flat.shape[0] % PACK_ALIGN)).reshape(-1, LANES))
    return jnp.concatenate(pieces, axis=0)


def _unpack(packed, shapes):
    out, row = [], 0
    for shape in shapes:
        size = 1
        for dim in shape:
            size *= dim
        rows = -(-size // PACK_ALIGN) * 8
        out.append(packed[row:row + rows].reshape(-1)[:size].reshape(shape))
        row += rows
    return out


TB_PROJ = 512
TB_MIX_FWD = 1024
TB_MIX = 256
TB_TN = 1024
TQ = 256
TK = 256
TB_ADAM = 64


def kernel(x, norm_mix_g, w_in, gmlp_v_g, gmlp_w_s, gmlp_b_s, short_conv_w, conf_conv_w, conf_ln_g, conf_ln_b, mix_out_g, w_out, norm_ffn_g, w_up, w_down, final_norm_g, loss_target, m_norm_mix_g, m_w_in, m_gmlp_v_g, m_gmlp_w_s, m_gmlp_b_s, m_short_conv_w, m_conf_conv_w, m_conf_ln_g, m_conf_ln_b, m_mix_out_g, m_w_out, m_norm_ffn_g, m_w_up, m_w_down, m_final_norm_g, v_norm_mix_g, v_w_in, v_gmlp_v_g, v_gmlp_w_s, v_gmlp_b_s, v_short_conv_w, v_conf_conv_w, v_conf_ln_g, v_conf_ln_b, v_mix_out_g, v_w_out, v_norm_ffn_g, v_w_up, v_w_down, v_final_norm_g):
    me = 4 * lax.axis_index("x") + 2 * lax.axis_index("y") + lax.axis_index("c")
    x0, target = x[0], loss_target[0]
    s = x0.shape[0]
    tb_proj, tb_mix, tb_tn = min(TB_PROJ, s), min(TB_MIX, s), min(TB_TN, s)
    conv_cols = D_GROUP // N_DEV

    def pad_rows(a, rows):
        return jnp.pad(a, ((0, rows - a.shape[0]), (0, 0)))

    wint_loc = [w_in[l].T.astype(BF16) for l in range(N_LAYERS)]
    wout_loc = [w_out[l].astype(BF16) for l in range(N_LAYERS)]
    wup_loc = [w_up[l].astype(BF16) for l in range(N_LAYERS)]
    wdn_loc = [w_down[l].astype(BF16) for l in range(N_LAYERS)]
    conv_loc = jnp.concatenate([pad_rows(short_conv_w[l], 8) for l in range(N_LAYERS)]
                               + [pad_rows(conf_conv_w[l], HALO) for l in range(N_LAYERS)], axis=0)
    wint, wout, wup, wdn = [None] * N_LAYERS, [None] * N_LAYERS, [None] * N_LAYERS, [None] * N_LAYERS
    wint0, conv_all = _exchange([wint_loc[0], conv_loc], [GATHER, GATHER], "gather_first_weights")
    wint[0] = wint0.reshape(D_IN, D_MODEL)
    conv_full = conv_all.transpose(1, 0, 2).reshape(-1, D_GROUP)
    scw = [conv_full[8 * l:8 * (l + 1)] for l in range(N_LAYERS)]
    ccw = [conv_full[8 * N_LAYERS + HALO * l:8 * N_LAYERS + HALO * (l + 1)] for l in range(N_LAYERS)]

    tril = jnp.tril(jnp.ones((CHUNK, CHUNK), dtype=bool))
    wm = [jnp.where(tril, gmlp_w_s[l], 0.0).astype(BF16) for l in range(N_LAYERS)]
    wmt = [w.transpose(0, 2, 1) for w in wm]
    bexp = [jnp.repeat(gmlp_b_s[l].T, HEAD_DIM, axis=1) for l in range(N_LAYERS)]

    def row(vec):
        return vec.reshape(1, -1)

    saved = []
    xc = x0
    for l in range(N_LAYERS):
        first = l == 0
        (z, qkv, hb_in), moved = _in_proj_fwd(xc, row(norm_mix_g[l]), wint[l], tb_proj,
                                              ride=([wout_loc[0]], [GATHER]) if first else None)
        if first:
            wout[0] = moved[0].reshape(D_MODEL, D_MODEL)
        (ya, yb, yd, conv), moved = _mix_fwd(z, row(gmlp_v_g[l]), wm[l], bexp[l], scw[l], ccw[l], row(conf_ln_g[l]),
                                             row(conf_ln_b[l]), min(TB_MIX_FWD, s),
                                             ride=([wup_loc[0]], [GATHER]) if first else None)
        if first:
            wup[0] = moved[0]
        (yc,), moved = _attn_fwd(qkv, TQ, TK, ride=([wdn_loc[0], wint_loc[1]], [GATHER, GATHER]) if first else None)
        if first:
            wdn[0], wint[1] = moved[0], moved[1].reshape(D_IN, D_MODEL)
        ys = (ya, yb, yc, yd)
        x1 = _out_proj_fwd(ys, xc, row(mix_out_g[l]), wout[l], tb_proj)
        (x2, act), moved = _ffn_fwd(x1, row(norm_ffn_g[l]), wup[l], wdn[l], tb_proj,
                                    ride=([wout_loc[1], wup_loc[1], wdn_loc[1]], [GATHER] * 3) if first else None)
        saved.append((xc, z, qkv, ys, x1, act, hb_in, conv))
        xc = x2
        if first:
            wout[1], wup[1], wdn[1] = moved[0].reshape(D_MODEL, D_MODEL), moved[1], moved[2]
    dx, g_final, loss_part = _loss_head(xc, row(final_norm_g), target, tb_proj)
    loss = lax.psum(loss_part[0, 0], ("x", "y", "c"))

    parts = [None] * (4 * N_LAYERS)
    small_grads = [None] * N_LAYERS
    early_names = ["gmlp_v_g", "gmlp_w_s", "gmlp_b_s", "short_conv_w", "conf_conv_w", "conf_ln_g", "conf_ln_b",
                   "mix_out_g", "norm_ffn_g"]
    for l in reversed(range(N_LAYERS)):
        xin, z, qkv, ys, x1, act, hb_in, conv = saved[l]
        (dx1, hb_ffn, dpre, g_ffn), _ = _ffn_bwd(x1, act, dx, row(norm_ffn_g[l]), wup[l], wdn[l], tb_proj)
        grad_up = _tn_slabs(hb_ffn, dpre, tb_tn, False, "grad_w_up")
        grad_dn = _tn_slabs(dx, act, tb_proj, True, "grad_w_down", square_b=True)
        dya, dyb_mix, dyc, dyd, g_mixout, grad_out = _out_proj_bwd(dx1, ys, row(mix_out_g[l]), wout[l], tb_proj)
        grad_out = grad_out.reshape(N_DEV, D_MODEL // N_DEV, D_MODEL)
        (dq, dk, dv), moved = _attn_bwd(qkv, dyc, ys[2], TQ, TK, ride=([grad_up, grad_dn], [SCATTER] * 2))
        parts[4 * l + 2], parts[4 * l + 3] = moved
        (dz, g_vg, g_ws, g_bs, g_scw, g_ccw, g_lng, g_lnb, grad_in), moved = _mix_bwd(
            z, conv, dya, dyb_mix, dyd, dq, dk, dv, hb_in, row(gmlp_v_g[l]), wm[l], wmt[l], bexp[l], scw[l], ccw[l],
            row(conf_ln_g[l]), row(conf_ln_b[l]), tb_mix, ride=([grad_out], [SCATTER]))
        parts[4 * l + 1] = moved[0]
        small_grads[l] = dict(gmlp_v_g=g_vg[0], gmlp_w_s=g_ws, gmlp_b_s=g_bs[:, :4].T, short_conv_w=g_scw[:K_SHORT],
                              conf_conv_w=g_ccw[:K_CONF], conf_ln_g=g_lng[0], conf_ln_b=g_lnb[0],
                              mix_out_g=g_mixout[0], norm_ffn_g=g_ffn[0])
        riders, modes = [grad_in.reshape(N_DEV, D_IN // N_DEV, D_MODEL)], [SCATTER]
        if l == 0:
            early_list = [jnp.stack([small_grads[k][n] for k in range(N_LAYERS)]) for n in early_names] + [g_final[0]]
            riders, modes = [riders[0].astype(BF16), _pack(early_list).astype(BF16)], modes + [GATHER]
        (dx, g_mix), moved = _in_proj_bwd(xin, dz, dx1, row(norm_mix_g[l]), wint[l], tb_proj, ride=(riders, modes))
        parts[4 * l] = moved[0]
        small_grads[l]["norm_mix_g"] = g_mix[0]

    late_list = [jnp.stack([small_grads[l]["norm_mix_g"] for l in range(N_LAYERS)])]
    late_parts = _exchange([_pack(late_list)], [GATHER], "gather_last_grad")[0]
    small_groups = [(early_names + ["final_norm_g"], early_list, moved[1]), (["norm_mix_g"], late_list, late_parts)]

    given = dict(norm_mix_g=(norm_mix_g, m_norm_mix_g, v_norm_mix_g), gmlp_v_g=(gmlp_v_g, m_gmlp_v_g, v_gmlp_v_g),
                 gmlp_w_s=(gmlp_w_s, m_gmlp_w_s, v_gmlp_w_s), gmlp_b_s=(gmlp_b_s, m_gmlp_b_s, v_gmlp_b_s),
                 short_conv_w=(short_conv_w, m_short_conv_w, v_short_conv_w),
                 conf_conv_w=(conf_conv_w, m_conf_conv_w, v_conf_conv_w),
                 conf_ln_g=(conf_ln_g, m_conf_ln_g, v_conf_ln_g), conf_ln_b=(conf_ln_b, m_conf_ln_b, v_conf_ln_b),
                 mix_out_g=(mix_out_g, m_mix_out_g, v_mix_out_g), norm_ffn_g=(norm_ffn_g, m_norm_ffn_g, v_norm_ffn_g),
                 final_norm_g=(final_norm_g, m_final_norm_g, v_final_norm_g))
    sharded_small = ("short_conv_w", "conf_conv_w")

    def widen(a):
        full = jnp.zeros(a.shape[:-1] + (D_GROUP,), a.dtype)
        return lax.dynamic_update_slice(full, a, (0, 0, me * conv_cols))

    small_res = {}
    for names, grads, gathered in small_groups:
        state = [_pack([widen(given[n][k]) if n in sharded_small else given[n][k] for n in names]) for k in range(3)]
        outs = _reduce_adamw(gathered, *state, state[0].shape[0], "adamw_small")
        for kind, packed in zip(("grad", "delta", "new_m", "new_v"), outs):
            for n, val in zip(names, _unpack(packed, [a.shape for a in grads])):
                if n in sharded_small:
                    val = lax.dynamic_slice(val, (0, 0, me * conv_cols), val.shape[:-1] + (conv_cols,))
                small_res[kind, n] = val

    big_names = ["w_in", "w_out", "w_up", "w_down"]
    big_given = dict(w_in=[t.transpose(0, 2, 1) for t in (w_in, m_w_in, v_w_in)], w_out=(w_out, m_w_out, v_w_out),
                     w_up=(w_up, m_w_up, v_w_up), w_down=(w_down, m_w_down, v_w_down))
    big_res = {}
    for j, n in enumerate(big_names):
        outs = None
        for l in range(N_LAYERS):
            outs = _reduce_adamw_layer(parts[4 * l + j], *big_given[n], l, outs, TB_ADAM, "adamw_" + n)
        for kind, out in zip(("grad", "delta", "new_m", "new_v"), outs):
            big_res[kind, n] = out.transpose(0, 2, 1) if n == "w_in" else out

    order = ["norm_mix_g", "w_in", "gmlp_v_g", "gmlp_w_s", "gmlp_b_s", "short_conv_w", "conf_conv_w", "conf_ln_g",
             "conf_ln_b", "mix_out_g", "w_out", "norm_ffn_g", "w_up", "w_down", "final_norm_g"]
    result = [loss, dx.reshape(x.shape)]
    for kind in ("grad", "delta", "new_m", "new_v"):
        for n in order:
            result.append(big_res[kind, n] if n in big_given else small_res[kind, n])
    return tuple(result)
```

```python
import jax
import jax.numpy as jnp
from jax import lax
from jax.experimental import pallas as pl
from jax.experimental.pallas import tpu as pltpu

F32 = jnp.float32
BF16 = jnp.bfloat16

D_MODEL = 1024
D_GROUP = 256
D_IN = 10 * D_GROUP
D_FF = 4 * D_MODEL
N_DEV = 8
N_LAYERS = 2
HEAD_DIM = 64
HEADS_PER_PAIR = 2
PAIR = HEADS_PER_PAIR * HEAD_DIM
CHUNK = 128
K_SHORT = 3
K_CONF = 31
HALO = 32
EPS = 1e-6
ATT_SCALE = HEAD_DIM ** -0.5
LOG_CUT = -104.0
MIB = 2 ** 20

ADAM_LR = 0.001
ADAM_B1 = 0.9
ADAM_B2 = 0.999
ADAM_EPS = 1e-08
ADAM_WD = 0.01
ADAM_STEP = 10

GELU_C = 0.7978845608028654
GELU_A = 0.044715


def _mm(a, b):
    return jnp.dot(a, b, preferred_element_type=F32)


def _mm_nt(a, b):
    return lax.dot_general(a, b, (((1,), (1,)), ((), ())), preferred_element_type=F32)


def _mm_tn(a, b):
    return lax.dot_general(a, b, (((0,), (0,)), ((), ())), preferred_element_type=F32)


def _rms(x):
    return lax.rsqrt(jnp.mean(x * x, axis=-1, keepdims=True) + EPS)


def _rms_bwd(dy, xh, r):
    return r * (dy - xh * jnp.mean(dy * xh, axis=-1, keepdims=True))


def _sigmoid(x):
    return 1.0 / (1.0 + jnp.exp(-x))


def _whole(shape):
    return pl.BlockSpec(shape, lambda *_: (0,) * len(shape))


def _resident(shape):
    return pl.BlockSpec(shape, lambda *_: (0,) * len(shape), pipeline_mode=pl.Buffered(1))


def _rows(tb, width, col=0):
    return pl.BlockSpec((tb, width), lambda i: (i, col))


def _params(semantics, vmem_mib):
    return pltpu.CompilerParams(dimension_semantics=semantics, vmem_limit_bytes=vmem_mib * MIB)


def _sds(shape, dtype=F32):
    return jax.ShapeDtypeStruct(shape, dtype)


def _split_bf16(v):
    hi = v.astype(BF16)
    lo = (v - hi.astype(F32)).astype(BF16)
    return hi, lo


GATHER, SCATTER = "gather", "scatter"
ANY_SPEC = pl.BlockSpec(memory_space=pl.ANY)


def _exchange_copies(ins, outs, modes, send_sems, recv_sems, local_sems, with_arrivals=True):
    x, y, c = lax.axis_index("x"), lax.axis_index("y"), lax.axis_index("c")
    me = 4 * x + 2 * y + c
    local, sends, arrivals = [], [], []
    for a, mode in enumerate(modes):
        local.append(pltpu.make_async_copy(ins[a].at[me] if mode == SCATTER else ins[a], outs[a].at[me], local_sems.at[a]))
    for k in range(N_DEV - 1):
        flip = k + 1
        peer = (1 - x if flip & 4 else x, 1 - y if flip & 2 else y, 1 - c if flip & 1 else c)
        pf = 4 * peer[0] + 2 * peer[1] + peer[2]
        for a, mode in enumerate(modes):
            src = ins[a].at[pf] if mode == SCATTER else ins[a]
            for dst, group in ((outs[a].at[me], sends), (outs[a].at[pf], arrivals)):
                if group is sends or with_arrivals:
                    group.append(pltpu.make_async_remote_copy(
                        src_ref=src, dst_ref=dst, send_sem=send_sems.at[a, k], recv_sem=recv_sems.at[a, k],
                        device_id=peer, device_id_type=pl.DeviceIdType.MESH))
    return local, sends, arrivals


def _exchange_start(*refs_and_modes):
    local, sends, _ = _exchange_copies(*refs_and_modes, with_arrivals=False)
    for cp in local + sends:
        cp.start()


def _exchange_wait(*refs_and_modes):
    local, sends, arrivals = _exchange_copies(*refs_and_modes)
    for cp in sends:
        cp.wait_send()
    for cp in arrivals:
        cp.wait_recv()
    for cp in local:
        cp.wait()


def _exchange_shapes(arrays, modes):
    out_shape = [_sds(a.shape if mode == SCATTER else (N_DEV,) + a.shape, a.dtype) for a, mode in zip(arrays, modes)]
    n = len(arrays)
    sems = [pltpu.SemaphoreType.DMA((n, N_DEV - 1)), pltpu.SemaphoreType.DMA((n, N_DEV - 1)), pltpu.SemaphoreType.DMA((n,))]
    return out_shape, sems


def _exchange(arrays, modes, name):
    n = len(arrays)
    out_shape, sems = _exchange_shapes(arrays, modes)

    def body(*refs):
        _exchange_start(refs[:n], refs[n:2 * n], modes, *refs[2 * n:])
        _exchange_wait(refs[:n], refs[n:2 * n], modes, *refs[2 * n:])

    return pl.pallas_call(body, name=name, out_shape=out_shape, in_specs=[ANY_SPEC] * n, out_specs=[ANY_SPEC] * n,
                          scratch_shapes=sems)(*arrays)


def _call(body, args, ride, *, name, grid, in_specs, out_specs, out_shape, scratch_shapes=(), compiler_params):
    if ride is None:
        outs = pl.pallas_call(body, name=name, grid=grid, in_specs=in_specs, out_specs=out_specs, out_shape=out_shape,
                              scratch_shapes=scratch_shapes, compiler_params=compiler_params)(*args)
        return outs, []
    arrays, modes = ride
    n, n_in, n_out, n_scratch = len(arrays), len(in_specs), len(out_specs), len(scratch_shapes)
    moved_shape, sems = _exchange_shapes(arrays, modes)
    n_steps = 1
    for g in grid:
        n_steps *= g

    def riding(*refs):
        ins, refs = refs[:n_in], refs[n_in:]
        r_ins, refs = refs[:n], refs[n:]
        outs, refs = refs[:n_out], refs[n_out:]
        r_outs, refs = refs[:n], refs[n:]
        scratch, r_sems = refs[:n_scratch], refs[n_scratch:]
        step = pl.program_id(0)
        for axis in range(1, len(grid)):
            step = step * grid[axis] + pl.program_id(axis)

        @pl.when(step == 0)
        def _():
            _exchange_start(r_ins, r_outs, modes, *r_sems)

        body(*ins, *outs, *scratch)

        @pl.when(step == n_steps - 1)
        def _():
            _exchange_wait(r_ins, r_outs, modes, *r_sems)

    outs = pl.pallas_call(
        riding, name=name, grid=grid, in_specs=list(in_specs) + [ANY_SPEC] * n,
        out_specs=list(out_specs) + [ANY_SPEC] * n, out_shape=list(out_shape) + moved_shape,
        scratch_shapes=list(scratch_shapes) + sems, compiler_params=compiler_params)(*args, *arrays)
    return outs[:n_out], outs[n_out:]


def _in_proj_fwd(x, g, wint, tb, ride=None):
    s = x.shape[0]

    def body(x_ref, g_ref, w_ref, z_ref, qkv_ref, hb_ref):
        xv = x_ref[...]
        h = (xv * _rms(xv) * g_ref[...]).astype(BF16)
        hb_ref[...] = h
        z = _mm_nt(h, w_ref[...])
        z_ref[...] = z
        qkv_ref[:, 0:D_GROUP] = (z[:, 5 * D_GROUP:6 * D_GROUP] * ATT_SCALE).astype(BF16)
        qkv_ref[:, D_GROUP:3 * D_GROUP] = z[:, 6 * D_GROUP:8 * D_GROUP].astype(BF16)

    return _call(
        body, (x, g, wint), ride, name="in_proj_fwd", grid=(s // tb,),
        in_specs=[_rows(tb, D_MODEL), _whole((1, D_MODEL)), _resident((D_IN, D_MODEL))],
        out_specs=[_rows(tb, D_IN), _rows(tb, 3 * D_GROUP), _rows(tb, D_MODEL)],
        out_shape=[_sds((s, D_IN)), _sds((s, 3 * D_GROUP), BF16), _sds((s, D_MODEL), BF16)],
        compiler_params=_params(("arbitrary",), 48))


def _gelu(x):
    return 0.5 * x * (1.0 + jnp.tanh(GELU_C * (x + GELU_A * x * x * x)))


def _gelu_grad(x):
    t = jnp.tanh(GELU_C * (x + GELU_A * x * x * x))
    return 0.5 * (1.0 + t) + 0.5 * x * (1.0 - t * t) * GELU_C * (1.0 + 3.0 * GELU_A * x * x)


def _head_lane(width):
    return lax.broadcasted_iota(jnp.int32, (1, width), 1) // HEAD_DIM


def _gating_chunk(wm_ref, bexp_ref, vc, lane_h):
    f = bexp_ref[...]
    for h in range(D_GROUP // HEAD_DIM):
        f = f + _mm(wm_ref[h], jnp.where(lane_h == h, vc, 0))
    return f


def _halo_specs(s, tb, width_blocks):
    per = tb // HALO
    prev = pl.BlockSpec((HALO, width_blocks), lambda i: (jnp.maximum(i * per - 1, 0), 0))
    nxt = pl.BlockSpec((HALO, width_blocks), lambda i: (jnp.minimum((i + 1) * per, s // HALO - 1), 0))
    return prev, nxt


SUBLANES = 8
TAP_SLACK = 24


def _taps(buf_ref, shifted_ref, first, n_taps, rows, visit):
    for residue in range(SUBLANES):
        taps = [j for j in range(n_taps) if (first + j) % SUBLANES == residue]
        if not taps:
            continue
        lo = first + taps[0]
        span = first + taps[-1] - lo + rows
        shifted_ref[0:span, :] = buf_ref[pl.ds(lo, span), :]
        for j in taps:
            visit(j, shifted_ref[pl.ds(first + j - lo, rows), :])


def _mix_fwd(z, vg, wm, bexp, scw, ccw, lng, lnb, tb, ride=None):
    s = z.shape[0]
    prev_spec, _ = _halo_specs(s, tb, D_IN)

    def body(z_ref, zp_ref, vg_ref, wm_ref, bexp_ref, scw_ref, ccw_ref, lng_ref, lnb_ref,
             ya_ref, yb_ref, yd_ref, c_ref, pbuf, hbuf, shifted):
        keep = (pl.program_id(0) > 0).astype(F32)
        lane_h = _head_lane(D_GROUP)
        ga = _gelu(z_ref[:, 0:2 * D_GROUP])
        u, v = ga[:, :D_GROUP], ga[:, D_GROUP:]
        vn = (v * _rms(v) * vg_ref[...]).astype(BF16)
        for n in range(tb // CHUNK):
            rows = slice(n * CHUNK, (n + 1) * CHUNK)
            ya_ref[rows, :] = u[rows] * _gating_chunk(wm_ref, bexp_ref, vn[rows], lane_h)
        p = z_ref[:, 3 * D_GROUP:4 * D_GROUP] * z_ref[:, 4 * D_GROUP:5 * D_GROUP]
        pbuf[0:HALO, :] = zp_ref[:, 3 * D_GROUP:4 * D_GROUP] * zp_ref[:, 4 * D_GROUP:5 * D_GROUP] * keep
        pbuf[HALO:HALO + tb, :] = p
        cv = scw_ref[K_SHORT - 1:K_SHORT, :] * p
        for k in range(K_SHORT - 1):
            cv = cv + scw_ref[k:k + 1, :] * pbuf[pl.ds(HALO - (K_SHORT - 1) + k, tb), :]
        yb_ref[...] = z_ref[:, 2 * D_GROUP:3 * D_GROUP] * cv
        hbuf[0:HALO, :] = zp_ref[:, 8 * D_GROUP:9 * D_GROUP] * _sigmoid(zp_ref[:, 9 * D_GROUP:10 * D_GROUP]) * keep
        hbuf[HALO:HALO + tb, :] = z_ref[:, 8 * D_GROUP:9 * D_GROUP] * _sigmoid(z_ref[:, 9 * D_GROUP:10 * D_GROUP])
        conv = [jnp.zeros((tb, D_GROUP), F32)]

        def tap(k, window):
            conv[0] = conv[0] + ccw_ref[k:k + 1, :] * window

        _taps(hbuf, shifted, HALO - (K_CONF - 1), K_CONF, tb, tap)
        c = conv[0]
        c_ref[...] = c
        xc = c - jnp.mean(c, axis=-1, keepdims=True)
        ln = xc * lax.rsqrt(jnp.mean(xc * xc, axis=-1, keepdims=True) + EPS) * lng_ref[...] + lnb_ref[...]
        yd_ref[...] = ln * _sigmoid(ln)

    grp = _rows(tb, D_GROUP)
    return _call(
        body, (z, z, vg, wm, bexp, scw, ccw, lng, lnb), ride, name="mix_fwd", grid=(s // tb,),
        in_specs=[_rows(tb, D_IN), prev_spec, _whole((1, D_GROUP)), _whole(wm.shape), _whole(bexp.shape),
                  _whole(scw.shape), _whole(ccw.shape), _whole((1, D_GROUP)), _whole((1, D_GROUP))],
        out_specs=[grp, grp, grp, grp],
        out_shape=[_sds((s, D_GROUP))] * 4,
        scratch_shapes=[pltpu.VMEM((HALO + tb, D_GROUP), F32), pltpu.VMEM((HALO + tb, D_GROUP), F32),
                        pltpu.VMEM((tb + TAP_SLACK, D_GROUP), F32)],
        compiler_params=_params(("arbitrary",), 40))


def _stick_tile(qh, kt, causal, c, upper):
    x = _mm_nt(qh, kt)
    soft = jnp.log(1.0 + jnp.exp(-jnp.abs(x)))
    lb = jnp.minimum(x, 0.0) - soft
    lom = jnp.where(causal, -jnp.maximum(x, 0.0) - soft, 0.0)
    hi, lo = _split_bf16(lom)
    stick = c + _mm(hi, upper[...]) + _mm(lo, upper[...])
    w = jnp.where(causal, jnp.exp(lb + stick), 0.0)
    return w, lb, lom


def _triangle(n, diagonal):
    return jnp.tri(n, n, diagonal, dtype=BF16)


def _causal_tile(qi, tq, k0, tk):
    qpos = qi * tq + lax.broadcasted_iota(jnp.int32, (tq, 1), 0)
    return k0 + lax.broadcasted_iota(jnp.int32, (1, tk), 1) < qpos


def _sticks_alive(cs):
    longest = cs[0]
    for c in cs[1:]:
        longest = jnp.maximum(longest, c)
    return (jnp.max(longest) > LOG_CUT).astype(jnp.int32)


def _walk(body, qi, tq, tk, init):
    start = (((qi + 1) * tq - 1) // tk, jnp.int32(1)) + tuple(init)
    return lax.while_loop(lambda cr: jnp.logical_and(cr[0] >= 0, cr[1] > 0), body, start)[2:]


def _attn_fwd(qkv, tq, tk, ride=None):
    s = qkv.shape[0]
    n_heads = D_GROUP // HEAD_DIM

    def body(q_ref, k_ref, v_ref, upper, o_ref):
        qi = pl.program_id(0)
        q = q_ref[...]
        lane_h = _head_lane(D_GROUP)
        qhs = [jnp.where(lane_h == h, q, 0) for h in range(n_heads)]

        def step(carry):
            kb, _, acc = carry[:3]
            cs = list(carry[3:])
            k0 = pl.multiple_of(kb * tk, tk)
            kt = k_ref[pl.ds(k0, tk), :]
            vt = v_ref[pl.ds(k0, tk), :]
            causal = _causal_tile(qi, tq, k0, tk)
            for h in range(n_heads):
                w, _, lom = _stick_tile(qhs[h], kt, causal, cs[h], upper)
                acc = acc + _mm(w.astype(BF16), jnp.where(lane_h == h, vt, 0))
                cs[h] = cs[h] + jnp.sum(lom, axis=1, keepdims=True)
            return (kb - 1, _sticks_alive(cs), acc) + tuple(cs)

        init = [jnp.zeros((tq, D_GROUP), F32)] + [jnp.zeros((tq, 1), F32)] * n_heads
        o_ref[...] = _walk(step, qi, tq, tk, init)[0]

    return _call(
        body, (qkv, qkv, qkv, _triangle(tk, -1)), ride, name="attn_fwd", grid=(s // tq,),
        in_specs=[pl.BlockSpec((tq, D_GROUP), lambda qi: (qi, 0)),
                  pl.BlockSpec((s, D_GROUP), lambda qi: (0, 1), pipeline_mode=pl.Buffered(1)),
                  pl.BlockSpec((s, D_GROUP), lambda qi: (0, 2), pipeline_mode=pl.Buffered(1)),
                  _resident((tk, tk))],
        out_specs=[pl.BlockSpec((tq, D_GROUP), lambda qi: (qi, 0))],
        out_shape=[_sds((s, D_GROUP))],
        compiler_params=_params(("arbitrary",), 40))


def _out_proj_fwd(ys, x, mg, wout, tb):
    s = x.shape[0]

    def body(ya_ref, yb_ref, yc_ref, yd_ref, x_ref, mg_ref, w_ref, o_ref):
        acc = x_ref[...]
        for gi, y_ref in enumerate((ya_ref, yb_ref, yc_ref, yd_ref)):
            cols = slice(gi * D_GROUP, (gi + 1) * D_GROUP)
            y = y_ref[...]
            acc = acc + _mm((y * _rms(y) * mg_ref[:, cols]).astype(BF16), w_ref[cols, :])
        o_ref[...] = acc

    grp = _rows(tb, D_GROUP)
    return pl.pallas_call(
        body, name="out_proj_fwd", grid=(s // tb,),
        in_specs=[grp, grp, grp, grp, _rows(tb, D_MODEL), _whole((1, D_MODEL)), _whole((D_MODEL, D_MODEL))],
        out_specs=_rows(tb, D_MODEL), out_shape=_sds((s, D_MODEL)),
        compiler_params=_params(("parallel",), 44),
    )(*ys, x, mg, wout)


def _ffn_fwd(x, g, wup, wdn, tb, ride=None):
    s = x.shape[0]
    ff = D_FF // N_DEV

    def body(x_ref, g_ref, wu_ref, wd_ref, o_ref, a_ref):
        xv = x_ref[...]
        h = (xv * _rms(xv) * g_ref[...]).astype(BF16)
        acc = xv
        for d in range(N_DEV):
            a = jnp.maximum(_mm(h, wu_ref[d]), 0.0)
            a_ref[:, d * ff:(d + 1) * ff] = a.astype(BF16)
            acc = acc + _mm((a * a).astype(BF16), wd_ref[d])
        o_ref[...] = acc

    return _call(
        body, (x, g, wup, wdn), ride, name="ffn_fwd", grid=(s // tb,),
        in_specs=[_rows(tb, D_MODEL), _whole((1, D_MODEL)), _resident((N_DEV, D_MODEL, ff)), _resident((N_DEV, ff, D_MODEL))],
        out_specs=[_rows(tb, D_MODEL), _rows(tb, D_FF)], out_shape=[_sds((s, D_MODEL)), _sds((s, D_FF), BF16)],
        compiler_params=_params(("arbitrary",), 56))


def _loss_head(x, g, tgt, tb):
    s = x.shape[0]

    def body(x_ref, g_ref, t_ref, dx_ref, dg_ref, loss_ref):
        @pl.when(pl.program_id(0) == 0)
        def _():
            dg_ref[...] = jnp.zeros_like(dg_ref)
            loss_ref[...] = jnp.zeros_like(loss_ref)

        xv = x_ref[...]
        r = _rms(xv)
        xh = xv * r
        err = xh * g_ref[...] - t_ref[...]
        loss_ref[...] += 0.5 * jnp.sum(jnp.mean(err * err, axis=-1, keepdims=True))
        dy = err * (1.0 / D_MODEL)
        dg_ref[...] += jnp.sum(dy * xh, axis=0, keepdims=True)
        dx_ref[...] = _rms_bwd(dy * g_ref[...], xh, r)

    return pl.pallas_call(
        body, name="loss_head", grid=(s // tb,),
        in_specs=[_rows(tb, D_MODEL), _whole((1, D_MODEL)), _rows(tb, D_MODEL)],
        out_specs=[_rows(tb, D_MODEL), _whole((1, D_MODEL)), _whole((8, 128))],
        out_shape=[_sds((s, D_MODEL)), _sds((1, D_MODEL)), _sds((8, 128))],
        compiler_params=_params(("arbitrary",), 44),
    )(x, g, tgt)


def _ffn_bwd(x1, act, dx2, g, wup, wdn, tb, ride=None):
    s = x1.shape[0]
    ff = D_FF // N_DEV

    def body(x_ref, a_ref, dy_ref, g_ref, wu_ref, wd_ref, dx_ref, hb_ref, dpre_ref, dg_ref):
        @pl.when(pl.program_id(0) == 0)
        def _():
            dg_ref[...] = jnp.zeros_like(dg_ref)

        xv = x_ref[...]
        r = _rms(xv)
        xh = xv * r
        hb_ref[...] = (xh * g_ref[...]).astype(BF16)
        dyv = dy_ref[...]
        dyb = dyv.astype(BF16)
        dh = jnp.zeros((tb, D_MODEL), F32)
        for d in range(N_DEV):
            cols = slice(d * ff, (d + 1) * ff)
            a = a_ref[:, cols].astype(F32)
            dpre = (_mm_nt(dyb, wd_ref[d]) * (2.0 * a)).astype(BF16)
            dpre_ref[:, cols] = dpre
            dh = dh + _mm_nt(dpre, wu_ref[d])
        dg_ref[...] += jnp.sum(dh * xh, axis=0, keepdims=True)
        dx_ref[...] = dyv + _rms_bwd(dh * g_ref[...], xh, r)

    return _call(
        body, (x1, act, dx2, g, wup, wdn), ride, name="ffn_bwd", grid=(s // tb,),
        in_specs=[_rows(tb, D_MODEL), _rows(tb, D_FF), _rows(tb, D_MODEL), _whole((1, D_MODEL)),
                  _resident((N_DEV, D_MODEL, ff)), _resident((N_DEV, ff, D_MODEL))],
        out_specs=[_rows(tb, D_MODEL), _rows(tb, D_MODEL), _rows(tb, D_FF), _whole((1, D_MODEL))],
        out_shape=[_sds((s, D_MODEL)), _sds((s, D_MODEL), BF16), _sds((s, D_FF), BF16), _sds((1, D_MODEL))],
        compiler_params=_params(("arbitrary",), 58))


def _tn_slabs(a, b, tb, transpose_slabs, name, square_b=False):
    s, m = a.shape
    width = b.shape[1] // N_DEV
    n_steps = s // tb
    slab = (width, m) if transpose_slabs else (m, width)

    def body(a_ref, b_ref, o_hbm, acc, stage, sem):
        step = pl.program_id(0)

        @pl.when(step == 0)
        def _():
            acc[...] = jnp.zeros_like(acc)

        bv = b_ref[...]
        if square_b:
            bv = bv.astype(F32)
            bv = bv * bv
        acc[...] += _mm_tn(a_ref[...].astype(BF16), bv.astype(BF16))

        @pl.when(step == n_steps - 1)
        def _():
            for d in range(N_DEV):
                cols = acc.at[:, pl.ds(d * width, width)]
                if transpose_slabs:
                    stage[...] = cols[...].T
                cp = pltpu.make_async_copy(stage if transpose_slabs else cols, o_hbm.at[d], sem.at[0])
                cp.start()
                cp.wait()

    return pl.pallas_call(
        body, name=name, grid=(n_steps,),
        in_specs=[_rows(tb, m), _rows(tb, b.shape[1])], out_specs=ANY_SPEC, out_shape=_sds((N_DEV,) + slab),
        scratch_shapes=[pltpu.VMEM((m, b.shape[1]), F32), pltpu.VMEM(slab, F32), pltpu.SemaphoreType.DMA((1,))],
        compiler_params=_params(("arbitrary",), 56),
    )(a, b)


def _out_proj_bwd(dx1, ys, mg, wout, tb):
    s = dx1.shape[0]

    def body(dx_ref, ya_ref, yb_ref, yc_ref, yd_ref, mg_ref, w_ref,
             dya_ref, dyb_ref, dyc_ref, dyd_ref, dmg_ref, dw_ref, yn_ref):
        @pl.when(pl.program_id(0) == 0)
        def _():
            dmg_ref[...] = jnp.zeros_like(dmg_ref)
            dw_ref[...] = jnp.zeros_like(dw_ref)

        dxb = dx_ref[...].astype(BF16)
        dyn = _mm_nt(dxb, w_ref[...])
        groups = ((ya_ref, dya_ref), (yb_ref, dyb_ref), (yc_ref, dyc_ref), (yd_ref, dyd_ref))
        for gi, (y_ref, dy_ref) in enumerate(groups):
            cols = slice(gi * D_GROUP, (gi + 1) * D_GROUP)
            y = y_ref[...]
            r = _rms(y)
            n = y * r
            gain = mg_ref[:, cols]
            dn = dyn[:, cols]
            yn_ref[:, cols] = (n * gain).astype(BF16)
            dmg_ref[:, cols] += jnp.sum(dn * n, axis=0, keepdims=True)
            dy_ref[...] = _rms_bwd(dn * gain, n, r)
        dw_ref[...] += _mm_tn(yn_ref[...], dxb)

    grp = _rows(tb, D_GROUP)
    return pl.pallas_call(
        body, name="out_proj_bwd", grid=(s // tb,),
        in_specs=[_rows(tb, D_MODEL), grp, grp, grp, grp, _whole((1, D_MODEL)), _resident((D_MODEL, D_MODEL))],
        out_specs=[grp, grp, grp, grp, _whole((1, D_MODEL)), _whole((D_MODEL, D_MODEL))],
        out_shape=[_sds((s, D_GROUP))] * 4 + [_sds((1, D_MODEL)), _sds((D_MODEL, D_MODEL))],
        scratch_shapes=[pltpu.VMEM((tb, D_MODEL), BF16)],
        compiler_params=_params(("arbitrary",), 52),
    )(dx1, *ys, mg, wout)


def _attn_bwd(qkv, do, o, tq, tk, ride=None):
    s = qkv.shape[0]
    nq = s // tq
    n_pairs = D_GROUP // PAIR

    def body(q_ref, k_ref, v_ref, do_ref, o_ref, upper, upper_eq, dq_ref, dk_hbm, dv_hbm, dk_acc, dv_acc, sems):
        hp, qi = pl.program_id(0), pl.program_id(1)

        @pl.when(qi == 0)
        def _():
            dk_acc[...] = jnp.zeros_like(dk_acc)
            dv_acc[...] = jnp.zeros_like(dv_acc)

        q = q_ref[...]
        dob = do_ref[...].astype(BF16)
        prod = dob.astype(F32) * o_ref[...]
        lane_h = _head_lane(PAIR)
        heads = []
        for h in range(HEADS_PER_PAIR):
            in_head = lane_h == h
            total = jnp.sum(jnp.where(in_head, prod, 0.0), axis=1, keepdims=True)
            heads.append((in_head, jnp.where(in_head, q, 0), jnp.where(in_head, dob, 0), total))

        def step(carry):
            kb, _, acc = carry[:3]
            cs = list(carry[3:3 + HEADS_PER_PAIR])
            nears = list(carry[3 + HEADS_PER_PAIR:])
            k0 = pl.multiple_of(kb * tk, tk)
            kt = k_ref[pl.ds(k0, tk), :]
            vt = v_ref[pl.ds(k0, tk), :]
            causal = _causal_tile(qi, tq, k0, tk)
            dk_t = jnp.zeros((tk, PAIR), F32)
            dv_t = jnp.zeros((tk, PAIR), F32)
            for h, (in_head, qh, doh, total) in enumerate(heads):
                w, lb, lom = _stick_tile(qh, kt, causal, cs[h], upper)
                wb = w.astype(BF16)
                gw = _mm_nt(doh, vt) * wb.astype(F32)
                hi, lo = _split_bf16(gw)
                far = total - nears[h] - _mm(hi, upper_eq[...]) - _mm(lo, upper_eq[...])
                beta = jnp.exp(lb)
                dxb = jnp.where(causal, gw - beta * (gw + far), 0.0).astype(BF16)
                acc = acc + _mm(dxb, jnp.where(in_head, kt, 0))
                dk_t = dk_t + _mm_tn(dxb, qh)
                dv_t = dv_t + _mm_tn(wb, doh)
                cs[h] = cs[h] + jnp.sum(lom, axis=1, keepdims=True)
                nears[h] = nears[h] + jnp.sum(gw, axis=1, keepdims=True)
            dk_acc[pl.ds(k0, tk), :] += dk_t
            dv_acc[pl.ds(k0, tk), :] += dv_t
            return (kb - 1, _sticks_alive(cs), acc) + tuple(cs) + tuple(nears)

        init = [jnp.zeros((tq, PAIR), F32)] + [jnp.zeros((tq, 1), F32)] * (2 * HEADS_PER_PAIR)
        dq_ref[...] = _walk(step, qi, tq, tk, init)[0]

        @pl.when(qi == nq - 1)
        def _():
            ck = pltpu.make_async_copy(dk_acc, dk_hbm.at[hp], sems.at[0])
            cv = pltpu.make_async_copy(dv_acc, dv_hbm.at[hp], sems.at[1])
            ck.start()
            cv.start()
            ck.wait()
            cv.wait()

    blk = pl.BlockSpec((tq, PAIR), lambda hp, qi: (qi, hp))
    return _call(
        body, (qkv, qkv, qkv, do, o, _triangle(tk, -1), _triangle(tk, 0)), ride, name="attn_bwd", grid=(n_pairs, nq),
        in_specs=[blk, pl.BlockSpec((s, PAIR), lambda hp, qi: (0, 2 + hp)),
                  pl.BlockSpec((s, PAIR), lambda hp, qi: (0, 4 + hp)), blk, blk, _resident((tk, tk)), _resident((tk, tk))],
        out_specs=[blk, ANY_SPEC, ANY_SPEC],
        out_shape=[_sds((s, D_GROUP)), _sds((n_pairs, s, PAIR)), _sds((n_pairs, s, PAIR))],
        scratch_shapes=[pltpu.VMEM((s, PAIR), F32), pltpu.VMEM((s, PAIR), F32), pltpu.SemaphoreType.DMA((2,))],
        compiler_params=_params(("arbitrary", "arbitrary"), 56))


def _mix_bwd(z, conv, dya, dyb, dyd, dq, dk, dv, hb, vg, wm, wmt, bexp, scw, ccw, lng, lnb, tb, ride=None):
    s = z.shape[0]
    n_steps = s // tb
    prev_spec, next_spec = _halo_specs(s, tb, D_IN)
    _, next_grp = _halo_specs(s, tb, D_GROUP)
    ext = tb + HALO

    def body(z_ref, zp_ref, zn_ref, dya_ref, dyb_ref, dybn_ref, dyd_ref, dydn_ref, c_ref, cn_ref, dq_ref, dk0_ref, dk1_ref,
             dv0_ref, dv1_ref, hbp_ref, hb_ref, vg_ref, wm_ref, wmt_ref, bexp_ref, scw_ref, ccw_ref, lng_ref, lnb_ref,
             dz_ref, dvg_ref, dws_ref, dbs_ref, dscw_ref, dccw_ref, dlng_ref, dlnb_ref, dwin_hbm,
             pbuf, hbuf, gbuf, cbuf, dubuf, dvnbuf, shifted, win_acc, dz_prev, win_sem):
        i = pl.program_id(0)

        @pl.when(i == 0)
        def _():
            for ref in (dvg_ref, dws_ref, dbs_ref, dscw_ref, dccw_ref, dlng_ref, dlnb_ref, win_acc, dz_prev):
                ref[...] = jnp.zeros_like(ref)

        win_acc[...] += _mm_tn(dz_prev[...], hbp_ref[...])

        keep_prev = (i > 0).astype(F32)
        keep_next = (i < n_steps - 1).astype(F32)
        lane_h = _head_lane(D_GROUP)

        za = z_ref[:, 0:2 * D_GROUP]
        ga = _gelu(za)
        u, v = ga[:, :D_GROUP], ga[:, D_GROUP:]
        r = _rms(v)
        vh = v * r
        vn = (vh * vg_ref[...]).astype(BF16)
        tril = lax.broadcasted_iota(jnp.int32, (CHUNK, CHUNK), 0) >= lax.broadcasted_iota(jnp.int32, (CHUNK, CHUNK), 1)
        dbias = jnp.zeros((CHUNK, D_GROUP), F32)
        for n in range(tb // CHUNK):
            rows = slice(n * CHUNK, (n + 1) * CHUNK)
            vc = vn[rows]
            dy = dya_ref[rows, :]
            dubuf[rows, :] = dy * _gating_chunk(wm_ref, bexp_ref, vc, lane_h)
            df = dy * u[rows]
            dfb = df.astype(BF16)
            dvn = jnp.zeros((CHUNK, D_GROUP), F32)
            for h in range(D_GROUP // HEAD_DIM):
                dfh = jnp.where(lane_h == h, dfb, 0)
                dvn = dvn + _mm(wmt_ref[h], dfh)
                dws_ref[h] += jnp.where(tril, _mm_nt(dfh, vc), 0.0)
            dvnbuf[rows, :] = dvn
            dbias = dbias + df
        for h in range(D_GROUP // HEAD_DIM):
            per_head = jnp.sum(jnp.where(lane_h == h, dbias, 0.0), axis=1, keepdims=True)
            dbs_ref[...] += per_head * (lax.broadcasted_iota(jnp.int32, (1, CHUNK), 1) == h).astype(F32)
        dvn = dvnbuf[...]
        dvg_ref[...] += jnp.sum(dvn * vh, axis=0, keepdims=True)
        dgelu = _gelu_grad(za)
        dz_ref[:, 0:D_GROUP] = (dubuf[...] * dgelu[:, :D_GROUP]).astype(BF16)
        dz_ref[:, D_GROUP:2 * D_GROUP] = (_rms_bwd(dvn * vg_ref[...], vh, r) * dgelu[:, D_GROUP:]).astype(BF16)

        gate_b = z_ref[:, 2 * D_GROUP:3 * D_GROUP]
        gate_c = z_ref[:, 3 * D_GROUP:4 * D_GROUP]
        hh = z_ref[:, 4 * D_GROUP:5 * D_GROUP]
        p = gate_c * hh
        pbuf[0:HALO, :] = zp_ref[:, 3 * D_GROUP:4 * D_GROUP] * zp_ref[:, 4 * D_GROUP:5 * D_GROUP] * keep_prev
        pbuf[HALO:HALO + tb, :] = p
        dyb_v = dyb_ref[...]
        dcv = dyb_v * gate_b
        gbuf[0:tb, :] = dcv
        gbuf[tb:ext, :] = dybn_ref[...] * zn_ref[:, 2 * D_GROUP:3 * D_GROUP] * keep_next
        cv = scw_ref[K_SHORT - 1:K_SHORT, :] * p
        dp = scw_ref[K_SHORT - 1:K_SHORT, :] * dcv
        dscw_ref[K_SHORT - 1:K_SHORT, :] += jnp.sum(dcv * p, axis=0, keepdims=True)
        for k in range(K_SHORT - 1):
            earlier = pbuf[pl.ds(HALO - (K_SHORT - 1) + k, tb), :]
            cv = cv + scw_ref[k:k + 1, :] * earlier
            dp = dp + scw_ref[k:k + 1, :] * gbuf[pl.ds(K_SHORT - 1 - k, tb), :]
            dscw_ref[k:k + 1, :] += jnp.sum(dcv * earlier, axis=0, keepdims=True)
        dz_ref[:, 2 * D_GROUP:3 * D_GROUP] = (dyb_v * cv).astype(BF16)
        dz_ref[:, 3 * D_GROUP:4 * D_GROUP] = (dp * hh).astype(BF16)
        dz_ref[:, 4 * D_GROUP:5 * D_GROUP] = (dp * gate_c).astype(BF16)

        dz_ref[:, 5 * D_GROUP:6 * D_GROUP] = (dq_ref[...] * ATT_SCALE).astype(BF16)
        dz_ref[:, 6 * D_GROUP:6 * D_GROUP + PAIR] = dk0_ref[...].astype(BF16)
        dz_ref[:, 6 * D_GROUP + PAIR:7 * D_GROUP] = dk1_ref[...].astype(BF16)
        dz_ref[:, 7 * D_GROUP:7 * D_GROUP + PAIR] = dv0_ref[...].astype(BF16)
        dz_ref[:, 7 * D_GROUP + PAIR:8 * D_GROUP] = dv1_ref[...].astype(BF16)

        a = z_ref[:, 8 * D_GROUP:9 * D_GROUP]
        sg = _sigmoid(z_ref[:, 9 * D_GROUP:10 * D_GROUP])
        hbuf[0:HALO, :] = zp_ref[:, 8 * D_GROUP:9 * D_GROUP] * _sigmoid(zp_ref[:, 9 * D_GROUP:10 * D_GROUP]) * keep_prev
        hbuf[HALO:HALO + tb, :] = a * sg
        c = jnp.concatenate([c_ref[...], cn_ref[...]], axis=0)
        xc = c - jnp.mean(c, axis=-1, keepdims=True)
        rs = lax.rsqrt(jnp.mean(xc * xc, axis=-1, keepdims=True) + EPS)
        xh = xc * rs
        ln = xh * lng_ref[...] + lnb_ref[...]
        sl = _sigmoid(ln)
        dy_ext = jnp.concatenate([dyd_ref[...], dydn_ref[...] * keep_next], axis=0)
        dln = dy_ext * sl * (1.0 + ln * (1.0 - sl))
        dlng_ref[...] += jnp.sum(dln[:tb] * xh[:tb], axis=0, keepdims=True)
        dlnb_ref[...] += jnp.sum(dln[:tb], axis=0, keepdims=True)
        dxh = dln * lng_ref[...]
        dc = rs * (dxh - jnp.mean(dxh, axis=-1, keepdims=True) - xh * jnp.mean(dxh * xh, axis=-1, keepdims=True))
        cbuf[...] = dc
        dc_blk = dc[:tb]
        grad_in = [jnp.zeros((tb, D_GROUP), F32)]

        def tap_input(j, window):
            k = K_CONF - 1 - j
            grad_in[0] = grad_in[0] + ccw_ref[k:k + 1, :] * window

        def tap_filter(k, window):
            dccw_ref[k:k + 1, :] += jnp.sum(dc_blk * window, axis=0, keepdims=True)

        _taps(cbuf, shifted, 0, K_CONF, tb, tap_input)
        _taps(hbuf, shifted, HALO - (K_CONF - 1), K_CONF, tb, tap_filter)
        dhd = grad_in[0]
        dz_ref[:, 8 * D_GROUP:9 * D_GROUP] = (dhd * sg).astype(BF16)
        dz_ref[:, 9 * D_GROUP:10 * D_GROUP] = (dhd * a * sg * (1.0 - sg)).astype(BF16)

        dz_prev[...] = dz_ref[...]

        @pl.when(i == n_steps - 1)
        def _():
            win_acc[...] += _mm_tn(dz_ref[...], hb_ref[...])
            cp = pltpu.make_async_copy(win_acc, dwin_hbm, win_sem.at[0])
            cp.start()
            cp.wait()

    grp = _rows(tb, D_GROUP)
    pair0 = pl.BlockSpec((None, tb, PAIR), lambda i: (0, i, 0))
    pair1 = pl.BlockSpec((None, tb, PAIR), lambda i: (1, i, 0))
    small = [_sds((1, D_GROUP)), _sds((4, CHUNK, CHUNK)), _sds((CHUNK, CHUNK)), _sds((8, D_GROUP)),
             _sds((HALO, D_GROUP)), _sds((1, D_GROUP)), _sds((1, D_GROUP))]
    return _call(
        body, (z, z, z, dya, dyb, dyb, dyd, dyd, conv, conv, dq, dk, dk, dv, dv, hb, hb, vg, wm, wmt, bexp, scw, ccw, lng, lnb),
        ride, name="mix_bwd", grid=(n_steps,),
        in_specs=[_rows(tb, D_IN), prev_spec, next_spec, grp, grp, next_grp, grp, next_grp, grp, next_grp, grp,
                  pair0, pair1, pair0, pair1,
                  pl.BlockSpec((tb, D_MODEL), lambda i: (jnp.maximum(i - 1, 0), 0)), _rows(tb, D_MODEL),
                  _whole((1, D_GROUP)), _whole(wm.shape), _whole(wmt.shape), _whole(bexp.shape), _whole(scw.shape),
                  _whole(ccw.shape), _whole((1, D_GROUP)), _whole((1, D_GROUP))],
        out_specs=[_rows(tb, D_IN)] + [_whole(t.shape) for t in small] + [ANY_SPEC],
        out_shape=[_sds((s, D_IN), BF16)] + small + [_sds((D_IN, D_MODEL))],
        scratch_shapes=[pltpu.VMEM((HALO + tb, D_GROUP), F32), pltpu.VMEM((HALO + tb, D_GROUP), F32),
                        pltpu.VMEM((ext, D_GROUP), F32), pltpu.VMEM((ext, D_GROUP), F32),
                        pltpu.VMEM((tb, D_GROUP), F32), pltpu.VMEM((tb, D_GROUP), F32),
                        pltpu.VMEM((tb + TAP_SLACK, D_GROUP), F32), pltpu.VMEM((D_IN, D_MODEL), F32),
                        pltpu.VMEM((tb, D_IN), BF16), pltpu.SemaphoreType.DMA((1,))],
        compiler_params=_params(("arbitrary",), 56))


def _in_proj_bwd(x, dz, dres, g, wint, tb, ride=None):
    s = x.shape[0]

    def body(x_ref, dz_ref, dr_ref, g_ref, w_ref, dx_ref, dg_ref):
        @pl.when(pl.program_id(0) == 0)
        def _():
            dg_ref[...] = jnp.zeros_like(dg_ref)

        xv = x_ref[...]
        r = _rms(xv)
        xh = xv * r
        dh = _mm(dz_ref[...], w_ref[...])
        dg_ref[...] += jnp.sum(dh * xh, axis=0, keepdims=True)
        dx_ref[...] = dr_ref[...] + _rms_bwd(dh * g_ref[...], xh, r)

    return _call(
        body, (x, dz, dres, g, wint), ride, name="in_proj_bwd", grid=(s // tb,),
        in_specs=[_rows(tb, D_MODEL), _rows(tb, D_IN), _rows(tb, D_MODEL), _whole((1, D_MODEL)), _resident((D_IN, D_MODEL))],
        out_specs=[_rows(tb, D_MODEL), _whole((1, D_MODEL))],
        out_shape=[_sds((s, D_MODEL)), _sds((1, D_MODEL))],
        compiler_params=_params(("arbitrary",), 56))


def _adamw(w, g, m, v):
    m = ADAM_B1 * m + (1.0 - ADAM_B1) * g
    v = ADAM_B2 * v + (1.0 - ADAM_B2) * (g * g)
    m_hat = m / (1.0 - ADAM_B1 ** ADAM_STEP)
    v_hat = v / (1.0 - ADAM_B2 ** ADAM_STEP)
    delta = -ADAM_LR * (m_hat / (jnp.sqrt(v_hat) + ADAM_EPS) + ADAM_WD * w)
    return delta, m, v


def _reduce_adamw(parts, w, m, v, tb, name):
    rows, cols = w.shape

    def body(p_ref, w_ref, m_ref, v_ref, g_ref, d_ref, m2_ref, v2_ref):
        g = p_ref[0].astype(F32)
        for j in range(1, N_DEV):
            g = g + p_ref[j].astype(F32)
        g_ref[...] = g
        d_ref[...], m2_ref[...], v2_ref[...] = _adamw(w_ref[...], g, m_ref[...], v_ref[...])

    blk = _rows(tb, cols)
    return pl.pallas_call(
        body, name=name, grid=(rows // tb,),
        in_specs=[pl.BlockSpec((N_DEV, tb, cols), lambda i: (0, i, 0)), blk, blk, blk],
        out_specs=[blk] * 4, out_shape=[_sds((rows, cols))] * 4,
        compiler_params=_params(("parallel",), 32),
    )(parts, w, m, v)


def _reduce_adamw_layer(parts, w, m, v, layer, taken_over, tb, name):
    _, rows, cols = w.shape
    n_given = 4 if taken_over is None else 8

    def body(*refs):
        p_ref, w_ref, m_ref, v_ref = refs[:4]
        g_ref, d_ref, m2_ref, v2_ref = refs[n_given:]
        g = p_ref[0].astype(F32)
        for j in range(1, N_DEV):
            g = g + p_ref[j].astype(F32)
        g_ref[...] = g
        d_ref[...], m2_ref[...], v2_ref[...] = _adamw(w_ref[...], g, m_ref[...], v_ref[...])

    blk = pl.BlockSpec((None, tb, cols), lambda i: (layer, i, 0))
    return pl.pallas_call(
        body, name=name, grid=(rows // tb,),
        in_specs=[pl.BlockSpec((N_DEV, tb, cols), lambda i: (0, i, 0)), blk, blk, blk] + [ANY_SPEC] * (n_given - 4),
        out_specs=[blk] * 4, out_shape=[_sds(w.shape)] * 4,
        input_output_aliases={} if taken_over is None else {4 + k: k for k in range(4)},
        compiler_params=_params(("arbitrary",), 32),
    )(parts, w, m, v, *(taken_over or ()))


LANES = 128
PACK_ALIGN = 8 * LANES


def _pack(arrays):
    pieces = []
    for a in arrays:
        flat = a.reshape(-1)
        pieces.append(jnp.pad(flat, (0, -flat.shape[0] % PACK_ALIGN)).reshape(-1, LANES))
    return jnp.concatenate(pieces, axis=0)


def _unpack(packed, shapes):
    out, row = [], 0
    for shape in shapes:
        size = 1
        for dim in shape:
            size *= dim
        rows = -(-size // PACK_ALIGN) * 8
        out.append(packed[row:row + rows].reshape(-1)[:size].reshape(shape))
        row += rows
    return out


TB_PROJ = 512
TB_WIDE = 1024
TB_MIX_FWD = 1024
TB_MIX = 256
TB_TN = 1024
TQ = 256
TK = 256
ADAM_PARTS_BLOCK_BYTES = 6 * MIB
BF16_SUBLANES = 16


def _adam_rows(rows, cols):
    best = BF16_SUBLANES
    for tb in range(BF16_SUBLANES, rows + 1, BF16_SUBLANES):
        if rows % tb == 0 and N_DEV * tb * cols * 4 <= ADAM_PARTS_BLOCK_BYTES:
            best = tb
    return best


def kernel(x, norm_mix_g, w_in, gmlp_v_g, gmlp_w_s, gmlp_b_s, short_conv_w, conf_conv_w, conf_ln_g, conf_ln_b, mix_out_g, w_out, norm_ffn_g, w_up, w_down, final_norm_g, loss_target, m_norm_mix_g, m_w_in, m_gmlp_v_g, m_gmlp_w_s, m_gmlp_b_s, m_short_conv_w, m_conf_conv_w, m_conf_ln_g, m_conf_ln_b, m_mix_out_g, m_w_out, m_norm_ffn_g, m_w_up, m_w_down, m_final_norm_g, v_norm_mix_g, v_w_in, v_gmlp_v_g, v_gmlp_w_s, v_gmlp_b_s, v_short_conv_w, v_conf_conv_w, v_conf_ln_g, v_conf_ln_b, v_mix_out_g, v_w_out, v_norm_ffn_g, v_w_up, v_w_down, v_final_norm_g):
    me = 4 * lax.axis_index("x") + 2 * lax.axis_index("y") + lax.axis_index("c")
    x0, target = x[0], loss_target[0]
    s = x0.shape[0]
    tb_proj, tb_mix, tb_tn, tb_wide = min(TB_PROJ, s), min(TB_MIX, s), min(TB_TN, s), min(TB_WIDE, s)
    conv_cols = D_GROUP // N_DEV

    def pad_rows(a, rows):
        return jnp.pad(a, ((0, rows - a.shape[0]), (0, 0)))

    wint_loc = [w_in[l].T.astype(BF16) for l in range(N_LAYERS)]
    wout_loc = [w_out[l].astype(BF16) for l in range(N_LAYERS)]
    wup_loc = [w_up[l].astype(BF16) for l in range(N_LAYERS)]
    wdn_loc = [w_down[l].astype(BF16) for l in range(N_LAYERS)]
    conv_loc = jnp.concatenate([pad_rows(short_conv_w[l], 8) for l in range(N_LAYERS)]
                               + [pad_rows(conf_conv_w[l], HALO) for l in range(N_LAYERS)], axis=0)
    wint, wout, wup, wdn = [None] * N_LAYERS, [None] * N_LAYERS, [None] * N_LAYERS, [None] * N_LAYERS
    wint0, conv_all = _exchange([wint_loc[0], conv_loc], [GATHER, GATHER], "gather_first_weights")
    wint[0] = wint0.reshape(D_IN, D_MODEL)
    conv_full = conv_all.transpose(1, 0, 2).reshape(-1, D_GROUP)
    scw = [conv_full[8 * l:8 * (l + 1)] for l in range(N_LAYERS)]
    ccw = [conv_full[8 * N_LAYERS + HALO * l:8 * N_LAYERS + HALO * (l + 1)] for l in range(N_LAYERS)]

    tril = jnp.tril(jnp.ones((CHUNK, CHUNK), dtype=bool))
    wm = [jnp.where(tril, gmlp_w_s[l], 0.0).astype(BF16) for l in range(N_LAYERS)]
    wmt = [w.transpose(0, 2, 1) for w in wm]
    bexp = [jnp.repeat(gmlp_b_s[l].T, HEAD_DIM, axis=1) for l in range(N_LAYERS)]

    def row(vec):
        return vec.reshape(1, -1)

    saved = []
    xc = x0
    for l in range(N_LAYERS):
        first = l == 0
        (z, qkv, hb_in), moved = _in_proj_fwd(xc, row(norm_mix_g[l]), wint[l], tb_proj,
                                              ride=([wout_loc[0]], [GATHER]) if first else None)
        if first:
            wout[0] = moved[0].reshape(D_MODEL, D_MODEL)
        (ya, yb, yd, conv), moved = _mix_fwd(z, row(gmlp_v_g[l]), wm[l], bexp[l], scw[l], ccw[l], row(conf_ln_g[l]),
                                             row(conf_ln_b[l]), min(TB_MIX_FWD, s),
                                             ride=([wup_loc[0]], [GATHER]) if first else None)
        if first:
            wup[0] = moved[0]
        (yc,), moved = _attn_fwd(qkv, TQ, TK, ride=([wdn_loc[0], wint_loc[1]], [GATHER, GATHER]) if first else None)
        if first:
            wdn[0], wint[1] = moved[0], moved[1].reshape(D_IN, D_MODEL)
        ys = (ya, yb, yc, yd)
        x1 = _out_proj_fwd(ys, xc, row(mix_out_g[l]), wout[l], tb_wide)
        (x2, act), moved = _ffn_fwd(x1, row(norm_ffn_g[l]), wup[l], wdn[l], tb_proj,
                                    ride=([wout_loc[1], wup_loc[1], wdn_loc[1]], [GATHER] * 3) if first else None)
        saved.append((xc, z, qkv, ys, x1, act, hb_in, conv))
        xc = x2
        if first:
            wout[1], wup[1], wdn[1] = moved[0].reshape(D_MODEL, D_MODEL), moved[1], moved[2]
    dx, g_final, loss_part = _loss_head(xc, row(final_norm_g), target, tb_wide)
    loss = lax.psum(loss_part[0, 0], ("x", "y", "c"))

    parts = [None] * (4 * N_LAYERS)
    small_grads = [None] * N_LAYERS
    early_names = ["gmlp_v_g", "gmlp_w_s", "gmlp_b_s", "short_conv_w", "conf_conv_w", "conf_ln_g", "conf_ln_b",
                   "mix_out_g", "norm_ffn_g"]
    for l in reversed(range(N_LAYERS)):
        xin, z, qkv, ys, x1, act, hb_in, conv = saved[l]
        (dx1, hb_ffn, dpre, g_ffn), _ = _ffn_bwd(x1, act, dx, row(norm_ffn_g[l]), wup[l], wdn[l], tb_proj)
        grad_up = _tn_slabs(hb_ffn, dpre, tb_tn, False, "grad_w_up")
        grad_dn = _tn_slabs(dx, act, tb_proj, True, "grad_w_down", square_b=True)
        dya, dyb_mix, dyc, dyd, g_mixout, grad_out = _out_proj_bwd(dx1, ys, row(mix_out_g[l]), wout[l], tb_wide)
        grad_out = grad_out.reshape(N_DEV, D_MODEL // N_DEV, D_MODEL)
        (dq, dk, dv), moved = _attn_bwd(qkv, dyc, ys[2], TQ, TK, ride=([grad_up, grad_dn], [SCATTER] * 2))
        parts[4 * l + 2], parts[4 * l + 3] = moved
        (dz, g_vg, g_ws, g_bs, g_scw, g_ccw, g_lng, g_lnb, grad_in), moved = _mix_bwd(
            z, conv, dya, dyb_mix, dyd, dq, dk, dv, hb_in, row(gmlp_v_g[l]), wm[l], wmt[l], bexp[l], scw[l], ccw[l],
            row(conf_ln_g[l]), row(conf_ln_b[l]), tb_mix, ride=([grad_out], [SCATTER]))
        parts[4 * l + 1] = moved[0]
        small_grads[l] = dict(gmlp_v_g=g_vg[0], gmlp_w_s=g_ws, gmlp_b_s=g_bs[:, :4].T, short_conv_w=g_scw[:K_SHORT],
                              conf_conv_w=g_ccw[:K_CONF], conf_ln_g=g_lng[0], conf_ln_b=g_lnb[0],
                              mix_out_g=g_mixout[0], norm_ffn_g=g_ffn[0])
        riders, modes = [grad_in.reshape(N_DEV, D_IN // N_DEV, D_MODEL)], [SCATTER]
        if l == 0:
            early_list = [jnp.stack([small_grads[k][n] for k in range(N_LAYERS)]) for n in early_names] + [g_final[0]]
            riders, modes = [riders[0].astype(BF16), _pack(early_list).astype(BF16)], modes + [GATHER]
        (dx, g_mix), moved = _in_proj_bwd(xin, dz, dx1, row(norm_mix_g[l]), wint[l], tb_wide, ride=(riders, modes))
        parts[4 * l] = moved[0]
        small_grads[l]["norm_mix_g"] = g_mix[0]

    late_list = [jnp.stack([small_grads[l]["norm_mix_g"] for l in range(N_LAYERS)])]
    late_parts = _exchange([_pack(late_list)], [GATHER], "gather_last_grad")[0]
    small_groups = [(early_names + ["final_norm_g"], early_list, moved[1]), (["norm_mix_g"], late_list, late_parts)]

    given = dict(norm_mix_g=(norm_mix_g, m_norm_mix_g, v_norm_mix_g), gmlp_v_g=(gmlp_v_g, m_gmlp_v_g, v_gmlp_v_g),
                 gmlp_w_s=(gmlp_w_s, m_gmlp_w_s, v_gmlp_w_s), gmlp_b_s=(gmlp_b_s, m_gmlp_b_s, v_gmlp_b_s),
                 short_conv_w=(short_conv_w, m_short_conv_w, v_short_conv_w),
                 conf_conv_w=(conf_conv_w, m_conf_conv_w, v_conf_conv_w),
                 conf_ln_g=(conf_ln_g, m_conf_ln_g, v_conf_ln_g), conf_ln_b=(conf_ln_b, m_conf_ln_b, v_conf_ln_b),
                 mix_out_g=(mix_out_g, m_mix_out_g, v_mix_out_g), norm_ffn_g=(norm_ffn_g, m_norm_ffn_g, v_norm_ffn_g),
                 final_norm_g=(final_norm_g, m_final_norm_g, v_final_norm_g))
    sharded_small = ("short_conv_w", "conf_conv_w")

    def widen(a):
        full = jnp.zeros(a.shape[:-1] + (D_GROUP,), a.dtype)
        return lax.dynamic_update_slice(full, a, (0, 0, me * conv_cols))

    small_res = {}
    for names, grads, gathered in small_groups:
        state = [_pack([widen(given[n][k]) if n in sharded_small else given[n][k] for n in names]) for k in range(3)]
        outs = _reduce_adamw(gathered, *state, state[0].shape[0], "adamw_small")
        for kind, packed in zip(("grad", "delta", "new_m", "new_v"), outs):
            for n, val in zip(names, _unpack(packed, [a.shape for a in grads])):
                if n in sharded_small:
                    val = lax.dynamic_slice(val, (0, 0, me * conv_cols), val.shape[:-1] + (conv_cols,))
                small_res[kind, n] = val

    big_names = ["w_in", "w_out", "w_up", "w_down"]
    big_given = dict(w_in=[t.transpose(0, 2, 1) for t in (w_in, m_w_in, v_w_in)], w_out=(w_out, m_w_out, v_w_out),
                     w_up=(w_up, m_w_up, v_w_up), w_down=(w_down, m_w_down, v_w_down))
    big_res = {}
    for j, n in enumerate(big_names):
        outs = None
        for l in range(N_LAYERS):
            outs = _reduce_adamw_layer(parts[4 * l + j], *big_given[n], l, outs,
                                       _adam_rows(*big_given[n][0].shape[1:]), "adamw_" + n)
        for kind, out in zip(("grad", "delta", "new_m", "new_v"), outs):
            big_res[kind, n] = out.transpose(0, 2, 1) if n == "w_in" else out

    order = ["norm_mix_g", "w_in", "gmlp_v_g", "gmlp_w_s", "gmlp_b_s", "short_conv_w", "conf_conv_w", "conf_ln_g",
             "conf_ln_b", "mix_out_g", "w_out", "norm_ffn_g", "w_up", "w_down", "final_norm_g"]
    result = [loss, dx.reshape(x.shape)]
    for kind in ("grad", "delta", "new_m", "new_v"):
        for n in order:
            result.append(big_res[kind, n] if n in big_given else small_res[kind, n])
    return tuple(result)
```

```python
import jax
import jax.numpy as jnp
from jax import lax
from jax.experimental import pallas as pl
from jax.experimental.pallas import tpu as pltpu

F32 = jnp.float32
BF16 = jnp.bfloat16

D_MODEL = 1024
D_GROUP = 256
D_IN = 10 * D_GROUP
D_FF = 4 * D_MODEL
N_DEV = 8
N_LAYERS = 2
HEAD_DIM = 64
HEADS_PER_PAIR = 2
PAIR = HEADS_PER_PAIR * HEAD_DIM
CHUNK = 128
K_SHORT = 3
K_CONF = 31
HALO = 32
EPS = 1e-6
ATT_SCALE = HEAD_DIM ** -0.5
LOG_CUT = -104.0
MIB = 2 ** 20

ADAM_LR = 0.001
ADAM_B1 = 0.9
ADAM_B2 = 0.999
ADAM_EPS = 1e-08
ADAM_WD = 0.01
ADAM_STEP = 10

GELU_C = 0.7978845608028654
GELU_A = 0.044715


def _mm(a, b):
    return jnp.dot(a, b, preferred_element_type=F32)


def _mm_nt(a, b):
    return lax.dot_general(a, b, (((1,), (1,)), ((), ())), preferred_element_type=F32)


def _mm_tn(a, b):
    return lax.dot_general(a, b, (((0,), (0,)), ((), ())), preferred_element_type=F32)


def _rms(x):
    return lax.rsqrt(jnp.mean(x * x, axis=-1, keepdims=True) + EPS)


def _rms_bwd(dy, xh, r):
    return r * (dy - xh * jnp.mean(dy * xh, axis=-1, keepdims=True))


def _sigmoid(x):
    return 1.0 / (1.0 + jnp.exp(-x))


def _whole(shape):
    return pl.BlockSpec(shape, lambda *_: (0,) * len(shape))


def _resident(shape):
    return pl.BlockSpec(shape, lambda *_: (0,) * len(shape), pipeline_mode=pl.Buffered(1))


def _rows(tb, width, col=0):
    return pl.BlockSpec((tb, width), lambda i: (i, col))


def _params(semantics, vmem_mib):
    return pltpu.CompilerParams(dimension_semantics=semantics, vmem_limit_bytes=vmem_mib * MIB)


def _sds(shape, dtype=F32):
    return jax.ShapeDtypeStruct(shape, dtype)


def _split_bf16(v):
    hi = v.astype(BF16)
    lo = (v - hi.astype(F32)).astype(BF16)
    return hi, lo


GATHER, SCATTER = "gather", "scatter"
ANY_SPEC = pl.BlockSpec(memory_space=pl.ANY)


def _exchange_copies(ins, outs, modes, send_sems, recv_sems, local_sems, with_arrivals=True):
    x, y, c = lax.axis_index("x"), lax.axis_index("y"), lax.axis_index("c")
    me = 4 * x + 2 * y + c
    local, sends, arrivals = [], [], []
    for a, mode in enumerate(modes):
        local.append(pltpu.make_async_copy(ins[a].at[me] if mode == SCATTER else ins[a], outs[a].at[me], local_sems.at[a]))
    for k in range(N_DEV - 1):
        flip = k + 1
        peer = (1 - x if flip & 4 else x, 1 - y if flip & 2 else y, 1 - c if flip & 1 else c)
        pf = 4 * peer[0] + 2 * peer[1] + peer[2]
        for a, mode in enumerate(modes):
            src = ins[a].at[pf] if mode == SCATTER else ins[a]
            for dst, group in ((outs[a].at[me], sends), (outs[a].at[pf], arrivals)):
                if group is sends or with_arrivals:
                    group.append(pltpu.make_async_remote_copy(
                        src_ref=src, dst_ref=dst, send_sem=send_sems.at[a, k], recv_sem=recv_sems.at[a, k],
                        device_id=peer, device_id_type=pl.DeviceIdType.MESH))
    return local, sends, arrivals


def _exchange_start(*refs_and_modes):
    local, sends, _ = _exchange_copies(*refs_and_modes, with_arrivals=False)
    for cp in local + sends:
        cp.start()


def _exchange_wait(*refs_and_modes):
    local, sends, arrivals = _exchange_copies(*refs_and_modes)
    for cp in sends:
        cp.wait_send()
    for cp in arrivals:
        cp.wait_recv()
    for cp in local:
        cp.wait()


def _exchange_shapes(arrays, modes):
    out_shape = [_sds(a.shape if mode == SCATTER else (N_DEV,) + a.shape, a.dtype) for a, mode in zip(arrays, modes)]
    n = len(arrays)
    sems = [pltpu.SemaphoreType.DMA((n, N_DEV - 1)), pltpu.SemaphoreType.DMA((n, N_DEV - 1)), pltpu.SemaphoreType.DMA((n,))]
    return out_shape, sems


def _exchange(arrays, modes, name):
    n = len(arrays)
    out_shape, sems = _exchange_shapes(arrays, modes)

    def body(*refs):
        _exchange_start(refs[:n], refs[n:2 * n], modes, *refs[2 * n:])
        _exchange_wait(refs[:n], refs[n:2 * n], modes, *refs[2 * n:])

    return pl.pallas_call(body, name=name, out_shape=out_shape, in_specs=[ANY_SPEC] * n, out_specs=[ANY_SPEC] * n,
                          scratch_shapes=sems)(*arrays)


def _call(body, args, ride, *, name, grid, in_specs, out_specs, out_shape, scratch_shapes=(), compiler_params):
    if ride is None:
        outs = pl.pallas_call(body, name=name, grid=grid, in_specs=in_specs, out_specs=out_specs, out_shape=out_shape,
                              scratch_shapes=scratch_shapes, compiler_params=compiler_params)(*args)
        return outs, []
    arrays, modes = ride
    n, n_in, n_out, n_scratch = len(arrays), len(in_specs), len(out_specs), len(scratch_shapes)
    moved_shape, sems = _exchange_shapes(arrays, modes)
    n_steps = 1
    for g in grid:
        n_steps *= g

    def riding(*refs):
        ins, refs = refs[:n_in], refs[n_in:]
        r_ins, refs = refs[:n], refs[n:]
        outs, refs = refs[:n_out], refs[n_out:]
        r_outs, refs = refs[:n], refs[n:]
        scratch, r_sems = refs[:n_scratch], refs[n_scratch:]
        step = pl.program_id(0)
        for axis in range(1, len(grid)):
            step = step * grid[axis] + pl.program_id(axis)

        @pl.when(step == 0)
        def _():
            _exchange_start(r_ins, r_outs, modes, *r_sems)

        body(*ins, *outs, *scratch)

        @pl.when(step == n_steps - 1)
        def _():
            _exchange_wait(r_ins, r_outs, modes, *r_sems)

    outs = pl.pallas_call(
        riding, name=name, grid=grid, in_specs=list(in_specs) + [ANY_SPEC] * n,
        out_specs=list(out_specs) + [ANY_SPEC] * n, out_shape=list(out_shape) + moved_shape,
        scratch_shapes=list(scratch_shapes) + sems, compiler_params=compiler_params)(*args, *arrays)
    return outs[:n_out], outs[n_out:]


def _in_proj_fwd(x, g, wint, tb, ride=None):
    s = x.shape[0]

    def body(x_ref, g_ref, w_ref, z_ref, qkv_ref, hb_ref):
        xv = x_ref[...]
        h = (xv * _rms(xv) * g_ref[...]).astype(BF16)
        hb_ref[...] = h
        z = _mm_nt(h, w_ref[...])
        z_ref[...] = z
        qkv_ref[:, 0:D_GROUP] = (z[:, 5 * D_GROUP:6 * D_GROUP] * ATT_SCALE).astype(BF16)
        qkv_ref[:, D_GROUP:3 * D_GROUP] = z[:, 6 * D_GROUP:8 * D_GROUP].astype(BF16)

    return _call(
        body, (x, g, wint), ride, name="in_proj_fwd", grid=(s // tb,),
        in_specs=[_rows(tb, D_MODEL), _whole((1, D_MODEL)), _resident((D_IN, D_MODEL))],
        out_specs=[_rows(tb, D_IN), _rows(tb, 3 * D_GROUP), _rows(tb, D_MODEL)],
        out_shape=[_sds((s, D_IN)), _sds((s, 3 * D_GROUP), BF16), _sds((s, D_MODEL), BF16)],
        compiler_params=_params(("arbitrary",), 58))


def _gelu(x):
    return 0.5 * x * (1.0 + jnp.tanh(GELU_C * (x + GELU_A * x * x * x)))


def _gelu_grad(x):
    t = jnp.tanh(GELU_C * (x + GELU_A * x * x * x))
    return 0.5 * (1.0 + t) + 0.5 * x * (1.0 - t * t) * GELU_C * (1.0 + 3.0 * GELU_A * x * x)


def _head_lane(width):
    return lax.broadcasted_iota(jnp.int32, (1, width), 1) // HEAD_DIM


def _gating_chunk(wm_ref, bexp_ref, vc, lane_h):
    f = bexp_ref[...]
    for h in range(D_GROUP // HEAD_DIM):
        f = f + _mm(wm_ref[h], jnp.where(lane_h == h, vc, 0))
    return f


def _halo_specs(s, tb, width_blocks):
    per = tb // HALO
    prev = pl.BlockSpec((HALO, width_blocks), lambda i: (jnp.maximum(i * per - 1, 0), 0))
    nxt = pl.BlockSpec((HALO, width_blocks), lambda i: (jnp.minimum((i + 1) * per, s // HALO - 1), 0))
    return prev, nxt


SUBLANES = 8
TAP_SLACK = 24


def _taps(buf_ref, shifted_ref, first, n_taps, rows, visit):
    for residue in range(SUBLANES):
        taps = [j for j in range(n_taps) if (first + j) % SUBLANES == residue]
        if not taps:
            continue
        lo = first + taps[0]
        span = first + taps[-1] - lo + rows
        shifted_ref[0:span, :] = buf_ref[pl.ds(lo, span), :]
        for j in taps:
            visit(j, shifted_ref[pl.ds(first + j - lo, rows), :])


def _mix_fwd(z, vg, wm, bexp, scw, ccw, lng, lnb, tb, ride=None):
    s = z.shape[0]
    prev_spec, _ = _halo_specs(s, tb, D_IN)

    def body(z_ref, zp_ref, vg_ref, wm_ref, bexp_ref, scw_ref, ccw_ref, lng_ref, lnb_ref,
             ya_ref, yb_ref, yd_ref, c_ref, pbuf, hbuf, shifted):
        keep = (pl.program_id(0) > 0).astype(F32)
        lane_h = _head_lane(D_GROUP)
        ga = _gelu(z_ref[:, 0:2 * D_GROUP])
        u, v = ga[:, :D_GROUP], ga[:, D_GROUP:]
        vn = (v * _rms(v) * vg_ref[...]).astype(BF16)
        for n in range(tb // CHUNK):
            rows = slice(n * CHUNK, (n + 1) * CHUNK)
            ya_ref[rows, :] = u[rows] * _gating_chunk(wm_ref, bexp_ref, vn[rows], lane_h)
        p = z_ref[:, 3 * D_GROUP:4 * D_GROUP] * z_ref[:, 4 * D_GROUP:5 * D_GROUP]
        pbuf[0:HALO, :] = zp_ref[:, 3 * D_GROUP:4 * D_GROUP] * zp_ref[:, 4 * D_GROUP:5 * D_GROUP] * keep
        pbuf[HALO:HALO + tb, :] = p
        cv = scw_ref[K_SHORT - 1:K_SHORT, :] * p
        for k in range(K_SHORT - 1):
            cv = cv + scw_ref[k:k + 1, :] * pbuf[pl.ds(HALO - (K_SHORT - 1) + k, tb), :]
        yb_ref[...] = z_ref[:, 2 * D_GROUP:3 * D_GROUP] * cv
        hbuf[0:HALO, :] = zp_ref[:, 8 * D_GROUP:9 * D_GROUP] * _sigmoid(zp_ref[:, 9 * D_GROUP:10 * D_GROUP]) * keep
        hbuf[HALO:HALO + tb, :] = z_ref[:, 8 * D_GROUP:9 * D_GROUP] * _sigmoid(z_ref[:, 9 * D_GROUP:10 * D_GROUP])
        conv = [jnp.zeros((tb, D_GROUP), F32)]

        def tap(k, window):
            conv[0] = conv[0] + ccw_ref[k:k + 1, :] * window

        _taps(hbuf, shifted, HALO - (K_CONF - 1), K_CONF, tb, tap)
        c = conv[0]
        c_ref[...] = c
        xc = c - jnp.mean(c, axis=-1, keepdims=True)
        ln = xc * lax.rsqrt(jnp.mean(xc * xc, axis=-1, keepdims=True) + EPS) * lng_ref[...] + lnb_ref[...]
        yd_ref[...] = ln * _sigmoid(ln)

    grp = _rows(tb, D_GROUP)
    return _call(
        body, (z, z, vg, wm, bexp, scw, ccw, lng, lnb), ride, name="mix_fwd", grid=(s // tb,),
        in_specs=[_rows(tb, D_IN), prev_spec, _whole((1, D_GROUP)), _whole(wm.shape), _whole(bexp.shape),
                  _whole(scw.shape), _whole(ccw.shape), _whole((1, D_GROUP)), _whole((1, D_GROUP))],
        out_specs=[grp, grp, grp, grp],
        out_shape=[_sds((s, D_GROUP))] * 4,
        scratch_shapes=[pltpu.VMEM((HALO + tb, D_GROUP), F32), pltpu.VMEM((HALO + tb, D_GROUP), F32),
                        pltpu.VMEM((tb + TAP_SLACK, D_GROUP), F32)],
        compiler_params=_params(("arbitrary",), 40))


def _stick_tile(qh, kt, causal, c, upper):
    x = _mm_nt(qh, kt)
    soft = jnp.log(1.0 + jnp.exp(-jnp.abs(x)))
    lb = jnp.minimum(x, 0.0) - soft
    lom = jnp.where(causal, -jnp.maximum(x, 0.0) - soft, 0.0)
    hi, lo = _split_bf16(lom)
    stick = c + _mm(hi, upper[...]) + _mm(lo, upper[...])
    w = jnp.where(causal, jnp.exp(lb + stick), 0.0)
    return w, lb, lom


def _triangle(n, diagonal):
    return jnp.tri(n, n, diagonal, dtype=BF16)


def _causal_tile(qi, tq, k0, tk):
    qpos = qi * tq + lax.broadcasted_iota(jnp.int32, (tq, 1), 0)
    return k0 + lax.broadcasted_iota(jnp.int32, (1, tk), 1) < qpos


def _sticks_alive(cs):
    longest = cs[0]
    for c in cs[1:]:
        longest = jnp.maximum(longest, c)
    return (jnp.max(longest) > LOG_CUT).astype(jnp.int32)


def _walk(body, qi, tq, tk, init):
    start = (((qi + 1) * tq - 1) // tk, jnp.int32(1)) + tuple(init)
    return lax.while_loop(lambda cr: jnp.logical_and(cr[0] >= 0, cr[1] > 0), body, start)[2:]


def _attn_fwd(qkv, tq, tk, ride=None):
    s = qkv.shape[0]
    n_heads = D_GROUP // HEAD_DIM

    def body(q_ref, k_ref, v_ref, upper, o_ref):
        qi = pl.program_id(0)
        q = q_ref[...]
        lane_h = _head_lane(D_GROUP)
        qhs = [jnp.where(lane_h == h, q, 0) for h in range(n_heads)]

        def step(carry):
            kb, _, acc = carry[:3]
            cs = list(carry[3:])
            k0 = pl.multiple_of(kb * tk, tk)
            kt = k_ref[pl.ds(k0, tk), :]
            vt = v_ref[pl.ds(k0, tk), :]
            causal = _causal_tile(qi, tq, k0, tk)
            for h in range(n_heads):
                w, _, lom = _stick_tile(qhs[h], kt, causal, cs[h], upper)
                acc = acc + _mm(w.astype(BF16), jnp.where(lane_h == h, vt, 0))
                cs[h] = cs[h] + jnp.sum(lom, axis=1, keepdims=True)
            return (kb - 1, _sticks_alive(cs), acc) + tuple(cs)

        init = [jnp.zeros((tq, D_GROUP), F32)] + [jnp.zeros((tq, 1), F32)] * n_heads
        o_ref[...] = _walk(step, qi, tq, tk, init)[0]

    return _call(
        body, (qkv, qkv, qkv, _triangle(tk, -1)), ride, name="attn_fwd", grid=(s // tq,),
        in_specs=[pl.BlockSpec((tq, D_GROUP), lambda qi: (qi, 0)),
                  pl.BlockSpec((s, D_GROUP), lambda qi: (0, 1), pipeline_mode=pl.Buffered(1)),
                  pl.BlockSpec((s, D_GROUP), lambda qi: (0, 2), pipeline_mode=pl.Buffered(1)),
                  _resident((tk, tk))],
        out_specs=[pl.BlockSpec((tq, D_GROUP), lambda qi: (qi, 0))],
        out_shape=[_sds((s, D_GROUP))],
        compiler_params=_params(("arbitrary",), 40))


def _out_proj_fwd(ys, x, mg, wout, tb):
    s = x.shape[0]

    def body(ya_ref, yb_ref, yc_ref, yd_ref, x_ref, mg_ref, w_ref, o_ref):
        acc = x_ref[...]
        for gi, y_ref in enumerate((ya_ref, yb_ref, yc_ref, yd_ref)):
            cols = slice(gi * D_GROUP, (gi + 1) * D_GROUP)
            y = y_ref[...]
            acc = acc + _mm((y * _rms(y) * mg_ref[:, cols]).astype(BF16), w_ref[cols, :])
        o_ref[...] = acc

    grp = _rows(tb, D_GROUP)
    return pl.pallas_call(
        body, name="out_proj_fwd", grid=(s // tb,),
        in_specs=[grp, grp, grp, grp, _rows(tb, D_MODEL), _whole((1, D_MODEL)), _whole((D_MODEL, D_MODEL))],
        out_specs=_rows(tb, D_MODEL), out_shape=_sds((s, D_MODEL)),
        compiler_params=_params(("parallel",), 44),
    )(*ys, x, mg, wout)


def _ffn_fwd(x, g, wup, wdn, tb, ride=None):
    s = x.shape[0]
    ff = D_FF // N_DEV

    def body(x_ref, g_ref, wu_ref, wd_ref, o_ref, a_ref):
        xv = x_ref[...]
        h = (xv * _rms(xv) * g_ref[...]).astype(BF16)
        acc = xv
        for d in range(N_DEV):
            a = jnp.maximum(_mm(h, wu_ref[d]), 0.0)
            a_ref[:, d * ff:(d + 1) * ff] = a.astype(BF16)
            acc = acc + _mm((a * a).astype(BF16), wd_ref[d])
        o_ref[...] = acc

    return _call(
        body, (x, g, wup, wdn), ride, name="ffn_fwd", grid=(s // tb,),
        in_specs=[_rows(tb, D_MODEL), _whole((1, D_MODEL)), _resident((N_DEV, D_MODEL, ff)), _resident((N_DEV, ff, D_MODEL))],
        out_specs=[_rows(tb, D_MODEL), _rows(tb, D_FF)], out_shape=[_sds((s, D_MODEL)), _sds((s, D_FF), BF16)],
        compiler_params=_params(("arbitrary",), 56))


def _loss_head(x, g, tgt, tb):
    s = x.shape[0]

    def body(x_ref, g_ref, t_ref, dx_ref, dg_ref, loss_ref):
        @pl.when(pl.program_id(0) == 0)
        def _():
            dg_ref[...] = jnp.zeros_like(dg_ref)
            loss_ref[...] = jnp.zeros_like(loss_ref)

        xv = x_ref[...]
        r = _rms(xv)
        xh = xv * r
        err = xh * g_ref[...] - t_ref[...]
        loss_ref[...] += 0.5 * jnp.sum(jnp.mean(err * err, axis=-1, keepdims=True))
        dy = err * (1.0 / D_MODEL)
        dg_ref[...] += jnp.sum(dy * xh, axis=0, keepdims=True)
        dx_ref[...] = _rms_bwd(dy * g_ref[...], xh, r)

    return pl.pallas_call(
        body, name="loss_head", grid=(s // tb,),
        in_specs=[_rows(tb, D_MODEL), _whole((1, D_MODEL)), _rows(tb, D_MODEL)],
        out_specs=[_rows(tb, D_MODEL), _whole((1, D_MODEL)), _whole((8, 128))],
        out_shape=[_sds((s, D_MODEL)), _sds((1, D_MODEL)), _sds((8, 128))],
        compiler_params=_params(("arbitrary",), 44),
    )(x, g, tgt)


def _ffn_bwd(x1, act, dx2, g, wup, wdn, tb, ride=None):
    s = x1.shape[0]
    ff = D_FF // N_DEV

    def body(x_ref, a_ref, dy_ref, g_ref, wu_ref, wd_ref, dx_ref, hb_ref, dpre_ref, dg_ref):
        @pl.when(pl.program_id(0) == 0)
        def _():
            dg_ref[...] = jnp.zeros_like(dg_ref)

        xv = x_ref[...]
        r = _rms(xv)
        xh = xv * r
        hb_ref[...] = (xh * g_ref[...]).astype(BF16)
        dyv = dy_ref[...]
        dyb = dyv.astype(BF16)
        dh = jnp.zeros((tb, D_MODEL), F32)
        for d in range(N_DEV):
            cols = slice(d * ff, (d + 1) * ff)
            a = a_ref[:, cols].astype(F32)
            dpre = (_mm_nt(dyb, wd_ref[d]) * (2.0 * a)).astype(BF16)
            dpre_ref[:, cols] = dpre
            dh = dh + _mm_nt(dpre, wu_ref[d])
        dg_ref[...] += jnp.sum(dh * xh, axis=0, keepdims=True)
        dx_ref[...] = dyv + _rms_bwd(dh * g_ref[...], xh, r)

    return _call(
        body, (x1, act, dx2, g, wup, wdn), ride, name="ffn_bwd", grid=(s // tb,),
        in_specs=[_rows(tb, D_MODEL), _rows(tb, D_FF), _rows(tb, D_MODEL), _whole((1, D_MODEL)),
                  _resident((N_DEV, D_MODEL, ff)), _resident((N_DEV, ff, D_MODEL))],
        out_specs=[_rows(tb, D_MODEL), _rows(tb, D_MODEL), _rows(tb, D_FF), _whole((1, D_MODEL))],
        out_shape=[_sds((s, D_MODEL)), _sds((s, D_MODEL), BF16), _sds((s, D_FF), BF16), _sds((1, D_MODEL))],
        compiler_params=_params(("arbitrary",), 58))


def _tn_slabs(a, b, tb, transpose_slabs, name, square_b=False):
    s, m = a.shape
    width = b.shape[1] // N_DEV
    n_steps = s // tb
    slab = (width, m) if transpose_slabs else (m, width)

    def body(a_ref, b_ref, o_hbm, acc, stage, sem):
        step = pl.program_id(0)

        @pl.when(step == 0)
        def _():
            acc[...] = jnp.zeros_like(acc)

        bv = b_ref[...]
        if square_b:
            bv = bv.astype(F32)
            bv = bv * bv
        acc[...] += _mm_tn(a_ref[...].astype(BF16), bv.astype(BF16))

        @pl.when(step == n_steps - 1)
        def _():
            for d in range(N_DEV):
                cols = acc.at[:, pl.ds(d * width, width)]
                if transpose_slabs:
                    stage[...] = cols[...].T
                cp = pltpu.make_async_copy(stage if transpose_slabs else cols, o_hbm.at[d], sem.at[0])
                cp.start()
                cp.wait()

    return pl.pallas_call(
        body, name=name, grid=(n_steps,),
        in_specs=[_rows(tb, m), _rows(tb, b.shape[1])], out_specs=ANY_SPEC, out_shape=_sds((N_DEV,) + slab),
        scratch_shapes=[pltpu.VMEM((m, b.shape[1]), F32), pltpu.VMEM(slab, F32), pltpu.SemaphoreType.DMA((1,))],
        compiler_params=_params(("arbitrary",), 56),
    )(a, b)


def _out_proj_bwd(dx1, ys, mg, wout, tb):
    s = dx1.shape[0]

    def body(dx_ref, ya_ref, yb_ref, yc_ref, yd_ref, mg_ref, w_ref,
             dya_ref, dyb_ref, dyc_ref, dyd_ref, dmg_ref, dw_ref, yn_ref):
        @pl.when(pl.program_id(0) == 0)
        def _():
            dmg_ref[...] = jnp.zeros_like(dmg_ref)
            dw_ref[...] = jnp.zeros_like(dw_ref)

        dxb = dx_ref[...].astype(BF16)
        dyn = _mm_nt(dxb, w_ref[...])
        groups = ((ya_ref, dya_ref), (yb_ref, dyb_ref), (yc_ref, dyc_ref), (yd_ref, dyd_ref))
        for gi, (y_ref, dy_ref) in enumerate(groups):
            cols = slice(gi * D_GROUP, (gi + 1) * D_GROUP)
            y = y_ref[...]
            r = _rms(y)
            n = y * r
            gain = mg_ref[:, cols]
            dn = dyn[:, cols]
            yn_ref[:, cols] = (n * gain).astype(BF16)
            dmg_ref[:, cols] += jnp.sum(dn * n, axis=0, keepdims=True)
            dy_ref[...] = _rms_bwd(dn * gain, n, r)
        dw_ref[...] += _mm_tn(yn_ref[...], dxb)

    grp = _rows(tb, D_GROUP)
    return pl.pallas_call(
        body, name="out_proj_bwd", grid=(s // tb,),
        in_specs=[_rows(tb, D_MODEL), grp, grp, grp, grp, _whole((1, D_MODEL)), _resident((D_MODEL, D_MODEL))],
        out_specs=[grp, grp, grp, grp, _whole((1, D_MODEL)), _whole((D_MODEL, D_MODEL))],
        out_shape=[_sds((s, D_GROUP))] * 4 + [_sds((1, D_MODEL)), _sds((D_MODEL, D_MODEL))],
        scratch_shapes=[pltpu.VMEM((tb, D_MODEL), BF16)],
        compiler_params=_params(("arbitrary",), 52),
    )(dx1, *ys, mg, wout)


def _attn_bwd(qkv, do, o, tq, tk, ride=None):
    s = qkv.shape[0]
    nq = s // tq
    n_pairs = D_GROUP // PAIR

    def body(q_ref, k_ref, v_ref, do_ref, o_ref, upper, upper_eq, dq_ref, dk_hbm, dv_hbm, dk_acc, dv_acc, sems):
        hp, qi = pl.program_id(0), pl.program_id(1)

        @pl.when(qi == 0)
        def _():
            dk_acc[...] = jnp.zeros_like(dk_acc)
            dv_acc[...] = jnp.zeros_like(dv_acc)

        q = q_ref[...]
        dob = do_ref[...].astype(BF16)
        prod = dob.astype(F32) * o_ref[...]
        lane_h = _head_lane(PAIR)
        heads = []
        for h in range(HEADS_PER_PAIR):
            in_head = lane_h == h
            total = jnp.sum(jnp.where(in_head, prod, 0.0), axis=1, keepdims=True)
            heads.append((in_head, jnp.where(in_head, q, 0), jnp.where(in_head, dob, 0), total))

        def step(carry):
            kb, _, acc = carry[:3]
            cs = list(carry[3:3 + HEADS_PER_PAIR])
            nears = list(carry[3 + HEADS_PER_PAIR:])
            k0 = pl.multiple_of(kb * tk, tk)
            kt = k_ref[pl.ds(k0, tk), :]
            vt = v_ref[pl.ds(k0, tk), :]
            causal = _causal_tile(qi, tq, k0, tk)
            dk_t = jnp.zeros((tk, PAIR), F32)
            dv_t = jnp.zeros((tk, PAIR), F32)
            for h, (in_head, qh, doh, total) in enumerate(heads):
                w, lb, lom = _stick_tile(qh, kt, causal, cs[h], upper)
                wb = w.astype(BF16)
                gw = _mm_nt(doh, vt) * wb.astype(F32)
                hi, lo = _split_bf16(gw)
                far = total - nears[h] - _mm(hi, upper_eq[...]) - _mm(lo, upper_eq[...])
                beta = jnp.exp(lb)
                dxb = jnp.where(causal, gw - beta * (gw + far), 0.0).astype(BF16)
                acc = acc + _mm(dxb, jnp.where(in_head, kt, 0))
                dk_t = dk_t + _mm_tn(dxb, qh)
                dv_t = dv_t + _mm_tn(wb, doh)
                cs[h] = cs[h] + jnp.sum(lom, axis=1, keepdims=True)
                nears[h] = nears[h] + jnp.sum(gw, axis=1, keepdims=True)
            dk_acc[pl.ds(k0, tk), :] += dk_t
            dv_acc[pl.ds(k0, tk), :] += dv_t
            return (kb - 1, _sticks_alive(cs), acc) + tuple(cs) + tuple(nears)

        init = [jnp.zeros((tq, PAIR), F32)] + [jnp.zeros((tq, 1), F32)] * (2 * HEADS_PER_PAIR)
        dq_ref[...] = _walk(step, qi, tq, tk, init)[0]

        @pl.when(qi == nq - 1)
        def _():
            ck = pltpu.make_async_copy(dk_acc, dk_hbm.at[hp], sems.at[0])
            cv = pltpu.make_async_copy(dv_acc, dv_hbm.at[hp], sems.at[1])
            ck.start()
            cv.start()
            ck.wait()
            cv.wait()

    blk = pl.BlockSpec((tq, PAIR), lambda hp, qi: (qi, hp))
    return _call(
        body, (qkv, qkv, qkv, do, o, _triangle(tk, -1), _triangle(tk, 0)), ride, name="attn_bwd", grid=(n_pairs, nq),
        in_specs=[blk, pl.BlockSpec((s, PAIR), lambda hp, qi: (0, 2 + hp)),
                  pl.BlockSpec((s, PAIR), lambda hp, qi: (0, 4 + hp)), blk, blk, _resident((tk, tk)), _resident((tk, tk))],
        out_specs=[blk, ANY_SPEC, ANY_SPEC],
        out_shape=[_sds((s, D_GROUP)), _sds((n_pairs, s, PAIR)), _sds((n_pairs, s, PAIR))],
        scratch_shapes=[pltpu.VMEM((s, PAIR), F32), pltpu.VMEM((s, PAIR), F32), pltpu.SemaphoreType.DMA((2,))],
        compiler_params=_params(("arbitrary", "arbitrary"), 56))


def _mix_bwd(z, conv, dya, dyb, dyd, dq, dk, dv, hb, vg, wm, wmt, bexp, scw, ccw, lng, lnb, tb, ride=None):
    s = z.shape[0]
    n_steps = s // tb
    prev_spec, next_spec = _halo_specs(s, tb, D_IN)
    _, next_grp = _halo_specs(s, tb, D_GROUP)
    ext = tb + HALO

    def body(z_ref, zp_ref, zn_ref, dya_ref, dyb_ref, dybn_ref, dyd_ref, dydn_ref, c_ref, cn_ref, dq_ref, dk0_ref, dk1_ref,
             dv0_ref, dv1_ref, hbp_ref, hb_ref, vg_ref, wm_ref, wmt_ref, bexp_ref, scw_ref, ccw_ref, lng_ref, lnb_ref,
             dz_ref, dvg_ref, dws_ref, dbs_ref, dscw_ref, dccw_ref, dlng_ref, dlnb_ref, dwin_hbm,
             pbuf, hbuf, gbuf, cbuf, dubuf, dvnbuf, shifted, win_acc, dz_prev, win_sem):
        i = pl.program_id(0)

        @pl.when(i == 0)
        def _():
            for ref in (dvg_ref, dws_ref, dbs_ref, dscw_ref, dccw_ref, dlng_ref, dlnb_ref, win_acc, dz_prev):
                ref[...] = jnp.zeros_like(ref)

        win_acc[...] += _mm_tn(dz_prev[...], hbp_ref[...])

        keep_prev = (i > 0).astype(F32)
        keep_next = (i < n_steps - 1).astype(F32)
        lane_h = _head_lane(D_GROUP)

        za = z_ref[:, 0:2 * D_GROUP]
        ga = _gelu(za)
        u, v = ga[:, :D_GROUP], ga[:, D_GROUP:]
        r = _rms(v)
        vh = v * r
        vn = (vh * vg_ref[...]).astype(BF16)
        tril = lax.broadcasted_iota(jnp.int32, (CHUNK, CHUNK), 0) >= lax.broadcasted_iota(jnp.int32, (CHUNK, CHUNK), 1)
        dbias = jnp.zeros((CHUNK, D_GROUP), F32)
        for n in range(tb // CHUNK):
            rows = slice(n * CHUNK, (n + 1) * CHUNK)
            vc = vn[rows]
            dy = dya_ref[rows, :]
            dubuf[rows, :] = dy * _gating_chunk(wm_ref, bexp_ref, vc, lane_h)
            df = dy * u[rows]
            dfb = df.astype(BF16)
            dvn = jnp.zeros((CHUNK, D_GROUP), F32)
            for h in range(D_GROUP // HEAD_DIM):
                dfh = jnp.where(lane_h == h, dfb, 0)
                dvn = dvn + _mm(wmt_ref[h], dfh)
                dws_ref[h] += jnp.where(tril, _mm_nt(dfh, vc), 0.0)
            dvnbuf[rows, :] = dvn
            dbias = dbias + df
        for h in range(D_GROUP // HEAD_DIM):
            per_head = jnp.sum(jnp.where(lane_h == h, dbias, 0.0), axis=1, keepdims=True)
            dbs_ref[...] += per_head * (lax.broadcasted_iota(jnp.int32, (1, CHUNK), 1) == h).astype(F32)
        dvn = dvnbuf[...]
        dvg_ref[...] += jnp.sum(dvn * vh, axis=0, keepdims=True)
        dgelu = _gelu_grad(za)
        dz_ref[:, 0:D_GROUP] = (dubuf[...] * dgelu[:, :D_GROUP]).astype(BF16)
        dz_ref[:, D_GROUP:2 * D_GROUP] = (_rms_bwd(dvn * vg_ref[...], vh, r) * dgelu[:, D_GROUP:]).astype(BF16)

        gate_b = z_ref[:, 2 * D_GROUP:3 * D_GROUP]
        gate_c = z_ref[:, 3 * D_GROUP:4 * D_GROUP]
        hh = z_ref[:, 4 * D_GROUP:5 * D_GROUP]
        p = gate_c * hh
        pbuf[0:HALO, :] = zp_ref[:, 3 * D_GROUP:4 * D_GROUP] * zp_ref[:, 4 * D_GROUP:5 * D_GROUP] * keep_prev
        pbuf[HALO:HALO + tb, :] = p
        dyb_v = dyb_ref[...]
        dcv = dyb_v * gate_b
        gbuf[0:tb, :] = dcv
        gbuf[tb:ext, :] = dybn_ref[...] * zn_ref[:, 2 * D_GROUP:3 * D_GROUP] * keep_next
        cv = scw_ref[K_SHORT - 1:K_SHORT, :] * p
        dp = scw_ref[K_SHORT - 1:K_SHORT, :] * dcv
        dscw_ref[K_SHORT - 1:K_SHORT, :] += jnp.sum(dcv * p, axis=0, keepdims=True)
        for k in range(K_SHORT - 1):
            earlier = pbuf[pl.ds(HALO - (K_SHORT - 1) + k, tb), :]
            cv = cv + scw_ref[k:k + 1, :] * earlier
            dp = dp + scw_ref[k:k + 1, :] * gbuf[pl.ds(K_SHORT - 1 - k, tb), :]
            dscw_ref[k:k + 1, :] += jnp.sum(dcv * earlier, axis=0, keepdims=True)
        dz_ref[:, 2 * D_GROUP:3 * D_GROUP] = (dyb_v * cv).astype(BF16)
        dz_ref[:, 3 * D_GROUP:4 * D_GROUP] = (dp * hh).astype(BF16)
        dz_ref[:, 4 * D_GROUP:5 * D_GROUP] = (dp * gate_c).astype(BF16)

        dz_ref[:, 5 * D_GROUP:6 * D_GROUP] = (dq_ref[...] * ATT_SCALE).astype(BF16)
        dz_ref[:, 6 * D_GROUP:6 * D_GROUP + PAIR] = dk0_ref[...].astype(BF16)
        dz_ref[:, 6 * D_GROUP + PAIR:7 * D_GROUP] = dk1_ref[...].astype(BF16)
        dz_ref[:, 7 * D_GROUP:7 * D_GROUP + PAIR] = dv0_ref[...].astype(BF16)
        dz_ref[:, 7 * D_GROUP + PAIR:8 * D_GROUP] = dv1_ref[...].astype(BF16)

        a = z_ref[:, 8 * D_GROUP:9 * D_GROUP]
        sg = _sigmoid(z_ref[:, 9 * D_GROUP:10 * D_GROUP])
        hbuf[0:HALO, :] = zp_ref[:, 8 * D_GROUP:9 * D_GROUP] * _sigmoid(zp_ref[:, 9 * D_GROUP:10 * D_GROUP]) * keep_prev
        hbuf[HALO:HALO + tb, :] = a * sg
        c = jnp.concatenate([c_ref[...], cn_ref[...]], axis=0)
        xc = c - jnp.mean(c, axis=-1, keepdims=True)
        rs = lax.rsqrt(jnp.mean(xc * xc, axis=-1, keepdims=True) + EPS)
        xh = xc * rs
        ln = xh * lng_ref[...] + lnb_ref[...]
        sl = _sigmoid(ln)
        dy_ext = jnp.concatenate([dyd_ref[...], dydn_ref[...] * keep_next], axis=0)
        dln = dy_ext * sl * (1.0 + ln * (1.0 - sl))
        dlng_ref[...] += jnp.sum(dln[:tb] * xh[:tb], axis=0, keepdims=True)
        dlnb_ref[...] += jnp.sum(dln[:tb], axis=0, keepdims=True)
        dxh = dln * lng_ref[...]
        dc = rs * (dxh - jnp.mean(dxh, axis=-1, keepdims=True) - xh * jnp.mean(dxh * xh, axis=-1, keepdims=True))
        cbuf[...] = dc
        dc_blk = dc[:tb]
        grad_in = [jnp.zeros((tb, D_GROUP), F32)]

        def tap_input(j, window):
            k = K_CONF - 1 - j
            grad_in[0] = grad_in[0] + ccw_ref[k:k + 1, :] * window

        def tap_filter(k, window):
            dccw_ref[k:k + 1, :] += jnp.sum(dc_blk * window, axis=0, keepdims=True)

        _taps(cbuf, shifted, 0, K_CONF, tb, tap_input)
        _taps(hbuf, shifted, HALO - (K_CONF - 1), K_CONF, tb, tap_filter)
        dhd = grad_in[0]
        dz_ref[:, 8 * D_GROUP:9 * D_GROUP] = (dhd * sg).astype(BF16)
        dz_ref[:, 9 * D_GROUP:10 * D_GROUP] = (dhd * a * sg * (1.0 - sg)).astype(BF16)

        dz_prev[...] = dz_ref[...]

        @pl.when(i == n_steps - 1)
        def _():
            win_acc[...] += _mm_tn(dz_ref[...], hb_ref[...])
            cp = pltpu.make_async_copy(win_acc, dwin_hbm, win_sem.at[0])
            cp.start()
            cp.wait()

    grp = _rows(tb, D_GROUP)
    pair0 = pl.BlockSpec((None, tb, PAIR), lambda i: (0, i, 0))
    pair1 = pl.BlockSpec((None, tb, PAIR), lambda i: (1, i, 0))
    small = [_sds((1, D_GROUP)), _sds((4, CHUNK, CHUNK)), _sds((CHUNK, CHUNK)), _sds((8, D_GROUP)),
             _sds((HALO, D_GROUP)), _sds((1, D_GROUP)), _sds((1, D_GROUP))]
    return _call(
        body, (z, z, z, dya, dyb, dyb, dyd, dyd, conv, conv, dq, dk, dk, dv, dv, hb, hb, vg, wm, wmt, bexp, scw, ccw, lng, lnb),
        ride, name="mix_bwd", grid=(n_steps,),
        in_specs=[_rows(tb, D_IN), prev_spec, next_spec, grp, grp, next_grp, grp, next_grp, grp, next_grp, grp,
                  pair0, pair1, pair0, pair1,
                  pl.BlockSpec((tb, D_MODEL), lambda i: (jnp.maximum(i - 1, 0), 0)), _rows(tb, D_MODEL),
                  _whole((1, D_GROUP)), _whole(wm.shape), _whole(wmt.shape), _whole(bexp.shape), _whole(scw.shape),
                  _whole(ccw.shape), _whole((1, D_GROUP)), _whole((1, D_GROUP))],
        out_specs=[_rows(tb, D_IN)] + [_whole(t.shape) for t in small] + [ANY_SPEC],
        out_shape=[_sds((s, D_IN), BF16)] + small + [_sds((D_IN, D_MODEL))],
        scratch_shapes=[pltpu.VMEM((HALO + tb, D_GROUP), F32), pltpu.VMEM((HALO + tb, D_GROUP), F32),
                        pltpu.VMEM((ext, D_GROUP), F32), pltpu.VMEM((ext, D_GROUP), F32),
                        pltpu.VMEM((tb, D_GROUP), F32), pltpu.VMEM((tb, D_GROUP), F32),
                        pltpu.VMEM((tb + TAP_SLACK, D_GROUP), F32), pltpu.VMEM((D_IN, D_MODEL), F32),
                        pltpu.VMEM((tb, D_IN), BF16), pltpu.SemaphoreType.DMA((1,))],
        compiler_params=_params(("arbitrary",), 56))


def _in_proj_bwd(x, dz, dres, g, wint, tb, ride=None):
    s = x.shape[0]

    def body(x_ref, dz_ref, dr_ref, g_ref, w_ref, dx_ref, dg_ref):
        @pl.when(pl.program_id(0) == 0)
        def _():
            dg_ref[...] = jnp.zeros_like(dg_ref)

        xv = x_ref[...]
        r = _rms(xv)
        xh = xv * r
        dh = _mm(dz_ref[...], w_ref[...])
        dg_ref[...] += jnp.sum(dh * xh, axis=0, keepdims=True)
        dx_ref[...] = dr_ref[...] + _rms_bwd(dh * g_ref[...], xh, r)

    return _call(
        body, (x, dz, dres, g, wint), ride, name="in_proj_bwd", grid=(s // tb,),
        in_specs=[_rows(tb, D_MODEL), _rows(tb, D_IN), _rows(tb, D_MODEL), _whole((1, D_MODEL)), _resident((D_IN, D_MODEL))],
        out_specs=[_rows(tb, D_MODEL), _whole((1, D_MODEL))],
        out_shape=[_sds((s, D_MODEL)), _sds((1, D_MODEL))],
        compiler_params=_params(("arbitrary",), 56))


def _adamw(w, g, m, v):
    m = ADAM_B1 * m + (1.0 - ADAM_B1) * g
    v = ADAM_B2 * v + (1.0 - ADAM_B2) * (g * g)
    m_hat = m / (1.0 - ADAM_B1 ** ADAM_STEP)
    v_hat = v / (1.0 - ADAM_B2 ** ADAM_STEP)
    delta = -ADAM_LR * (m_hat / (jnp.sqrt(v_hat) + ADAM_EPS) + ADAM_WD * w)
    return delta, m, v


def _reduce_adamw(parts, w, m, v, tb, name):
    rows, cols = w.shape

    def body(p_ref, w_ref, m_ref, v_ref, g_ref, d_ref, m2_ref, v2_ref):
        g = p_ref[0].astype(F32)
        for j in range(1, N_DEV):
            g = g + p_ref[j].astype(F32)
        g_ref[...] = g
        d_ref[...], m2_ref[...], v2_ref[...] = _adamw(w_ref[...], g, m_ref[...], v_ref[...])

    blk = _rows(tb, cols)
    return pl.pallas_call(
        body, name=name, grid=(rows // tb,),
        in_specs=[pl.BlockSpec((N_DEV, tb, cols), lambda i: (0, i, 0)), blk, blk, blk],
        out_specs=[blk] * 4, out_shape=[_sds((rows, cols))] * 4,
        compiler_params=_params(("parallel",), 32),
    )(parts, w, m, v)


def _reduce_adamw_layer(parts, w, m, v, layer, taken_over, tb, name):
    _, rows, cols = w.shape
    n_given = 4 if taken_over is None else 8

    def body(*refs):
        p_ref, w_ref, m_ref, v_ref = refs[:4]
        g_ref, d_ref, m2_ref, v2_ref = refs[n_given:]
        g = p_ref[0].astype(F32)
        for j in range(1, N_DEV):
            g = g + p_ref[j].astype(F32)
        g_ref[...] = g
        d_ref[...], m2_ref[...], v2_ref[...] = _adamw(w_ref[...], g, m_ref[...], v_ref[...])

    blk = pl.BlockSpec((None, tb, cols), lambda i: (layer, i, 0))
    return pl.pallas_call(
        body, name=name, grid=(rows // tb,),
        in_specs=[pl.BlockSpec((N_DEV, tb, cols), lambda i: (0, i, 0)), blk, blk, blk] + [ANY_SPEC] * (n_given - 4),
        out_specs=[blk] * 4, out_shape=[_sds(w.shape)] * 4,
        input_output_aliases={} if taken_over is None else {4 + k: k for k in range(4)},
        compiler_params=_params(("arbitrary",), 32),
    )(parts, w, m, v, *(taken_over or ()))


LANES = 128
PACK_ALIGN = 8 * LANES


def _pack(arrays):
    pieces = []
    for a in arrays:
        flat = a.reshape(-1)
        pieces.append(jnp.pad(flat, (0, -flat.shape[0] % PACK_ALIGN)).reshape(-1, LANES))
    return jnp.concatenate(pieces, axis=0)


def _unpack(packed, shapes):
    out, row = [], 0
    for shape in shapes:
        size = 1
        for dim in shape:
            size *= dim
        rows = -(-size // PACK_ALIGN) * 8
        out.append(packed[row:row + rows].reshape(-1)[:size].reshape(shape))
        row += rows
    return out


TB_PROJ = 512
TB_WIDE = 1024
TB_MIX_FWD = 1024
TB_MIX = 256
TB_TN = 1024
TQ = 256
TK = 256
ADAM_PARTS_BLOCK_BYTES = 6 * MIB
BF16_SUBLANES = 16


def _adam_rows(rows, cols):
    best = BF16_SUBLANES
    for tb in range(BF16_SUBLANES, rows + 1, BF16_SUBLANES):
        if rows % tb == 0 and N_DEV * tb * cols * 4 <= ADAM_PARTS_BLOCK_BYTES:
            best = tb
    return best


def kernel(x, norm_mix_g, w_in, gmlp_v_g, gmlp_w_s, gmlp_b_s, short_conv_w, conf_conv_w, conf_ln_g, conf_ln_b, mix_out_g, w_out, norm_ffn_g, w_up, w_down, final_norm_g, loss_target, m_norm_mix_g, m_w_in, m_gmlp_v_g, m_gmlp_w_s, m_gmlp_b_s, m_short_conv_w, m_conf_conv_w, m_conf_ln_g, m_conf_ln_b, m_mix_out_g, m_w_out, m_norm_ffn_g, m_w_up, m_w_down, m_final_norm_g, v_norm_mix_g, v_w_in, v_gmlp_v_g, v_gmlp_w_s, v_gmlp_b_s, v_short_conv_w, v_conf_conv_w, v_conf_ln_g, v_conf_ln_b, v_mix_out_g, v_w_out, v_norm_ffn_g, v_w_up, v_w_down, v_final_norm_g):
    me = 4 * lax.axis_index("x") + 2 * lax.axis_index("y") + lax.axis_index("c")
    x0, target = x[0], loss_target[0]
    s = x0.shape[0]
    tb_proj, tb_mix, tb_tn, tb_wide = min(TB_PROJ, s), min(TB_MIX, s), min(TB_TN, s), min(TB_WIDE, s)
    conv_cols = D_GROUP // N_DEV

    def pad_rows(a, rows):
        return jnp.pad(a, ((0, rows - a.shape[0]), (0, 0)))

    wint_loc = [w_in[l].T.astype(BF16) for l in range(N_LAYERS)]
    wout_loc = [w_out[l].astype(BF16) for l in range(N_LAYERS)]
    wup_loc = [w_up[l].astype(BF16) for l in range(N_LAYERS)]
    wdn_loc = [w_down[l].astype(BF16) for l in range(N_LAYERS)]
    conv_loc = jnp.concatenate([pad_rows(short_conv_w[l], 8) for l in range(N_LAYERS)]
                               + [pad_rows(conf_conv_w[l], HALO) for l in range(N_LAYERS)], axis=0)
    wint, wout, wup, wdn = [None] * N_LAYERS, [None] * N_LAYERS, [None] * N_LAYERS, [None] * N_LAYERS
    wint0, conv_all = _exchange([wint_loc[0], conv_loc], [GATHER, GATHER], "gather_first_weights")
    wint[0] = wint0.reshape(D_IN, D_MODEL)
    conv_full = conv_all.transpose(1, 0, 2).reshape(-1, D_GROUP)
    scw = [conv_full[8 * l:8 * (l + 1)] for l in range(N_LAYERS)]
    ccw = [conv_full[8 * N_LAYERS + HALO * l:8 * N_LAYERS + HALO * (l + 1)] for l in range(N_LAYERS)]

    tril = jnp.tril(jnp.ones((CHUNK, CHUNK), dtype=bool))
    wm = [jnp.where(tril, gmlp_w_s[l], 0.0).astype(BF16) for l in range(N_LAYERS)]
    wmt = [w.transpose(0, 2, 1) for w in wm]
    bexp = [jnp.repeat(gmlp_b_s[l].T, HEAD_DIM, axis=1) for l in range(N_LAYERS)]

    def row(vec):
        return vec.reshape(1, -1)

    saved = []
    xc = x0
    for l in range(N_LAYERS):
        first = l == 0
        (z, qkv, hb_in), moved = _in_proj_fwd(xc, row(norm_mix_g[l]), wint[l], tb_wide,
                                              ride=([wout_loc[0]], [GATHER]) if first else None)
        if first:
            wout[0] = moved[0].reshape(D_MODEL, D_MODEL)
        (ya, yb, yd, conv), moved = _mix_fwd(z, row(gmlp_v_g[l]), wm[l], bexp[l], scw[l], ccw[l], row(conf_ln_g[l]),
                                             row(conf_ln_b[l]), min(TB_MIX_FWD, s),
                                             ride=([wup_loc[0]], [GATHER]) if first else None)
        if first:
            wup[0] = moved[0]
        (yc,), moved = _attn_fwd(qkv, TQ, TK, ride=([wdn_loc[0], wint_loc[1]], [GATHER, GATHER]) if first else None)
        if first:
            wdn[0], wint[1] = moved[0], moved[1].reshape(D_IN, D_MODEL)
        ys = (ya, yb, yc, yd)
        x1 = _out_proj_fwd(ys, xc, row(mix_out_g[l]), wout[l], tb_wide)
        (x2, act), moved = _ffn_fwd(x1, row(norm_ffn_g[l]), wup[l], wdn[l], tb_proj,
                                    ride=([wout_loc[1], wup_loc[1], wdn_loc[1]], [GATHER] * 3) if first else None)
        saved.append((xc, z, qkv, ys, x1, act, hb_in, conv))
        xc = x2
        if first:
            wout[1], wup[1], wdn[1] = moved[0].reshape(D_MODEL, D_MODEL), moved[1], moved[2]
    dx, g_final, loss_part = _loss_head(xc, row(final_norm_g), target, tb_wide)

    parts = [None] * (4 * N_LAYERS)
    small_grads = [None] * N_LAYERS
    early_names = ["gmlp_v_g", "gmlp_w_s", "gmlp_b_s", "short_conv_w", "conf_conv_w", "conf_ln_g", "conf_ln_b",
                   "mix_out_g", "norm_ffn_g"]
    for l in reversed(range(N_LAYERS)):
        xin, z, qkv, ys, x1, act, hb_in, conv = saved[l]
        (dx1, hb_ffn, dpre, g_ffn), _ = _ffn_bwd(x1, act, dx, row(norm_ffn_g[l]), wup[l], wdn[l], tb_proj)
        grad_up = _tn_slabs(hb_ffn, dpre, tb_tn, False, "grad_w_up")
        grad_dn = _tn_slabs(dx, act, tb_proj, True, "grad_w_down", square_b=True)
        dya, dyb_mix, dyc, dyd, g_mixout, grad_out = _out_proj_bwd(dx1, ys, row(mix_out_g[l]), wout[l], tb_wide)
        grad_out = grad_out.reshape(N_DEV, D_MODEL // N_DEV, D_MODEL)
        (dq, dk, dv), moved = _attn_bwd(qkv, dyc, ys[2], TQ, TK, ride=([grad_up, grad_dn], [SCATTER] * 2))
        parts[4 * l + 2], parts[4 * l + 3] = moved
        (dz, g_vg, g_ws, g_bs, g_scw, g_ccw, g_lng, g_lnb, grad_in), moved = _mix_bwd(
            z, conv, dya, dyb_mix, dyd, dq, dk, dv, hb_in, row(gmlp_v_g[l]), wm[l], wmt[l], bexp[l], scw[l], ccw[l],
            row(conf_ln_g[l]), row(conf_ln_b[l]), tb_mix, ride=([grad_out], [SCATTER]))
        parts[4 * l + 1] = moved[0]
        small_grads[l] = dict(gmlp_v_g=g_vg[0], gmlp_w_s=g_ws, gmlp_b_s=g_bs[:, :4].T, short_conv_w=g_scw[:K_SHORT],
                              conf_conv_w=g_ccw[:K_CONF], conf_ln_g=g_lng[0], conf_ln_b=g_lnb[0],
                              mix_out_g=g_mixout[0], norm_ffn_g=g_ffn[0])
        riders, modes = [grad_in.reshape(N_DEV, D_IN // N_DEV, D_MODEL)], [SCATTER]
        if l == 0:
            early_list = [jnp.stack([small_grads[k][n] for k in range(N_LAYERS)]) for n in early_names] + [g_final[0]]
            riders, modes = [riders[0].astype(BF16), _pack(early_list).astype(BF16)], modes + [GATHER]
        (dx, g_mix), moved = _in_proj_bwd(xin, dz, dx1, row(norm_mix_g[l]), wint[l], tb_wide, ride=(riders, modes))
        parts[4 * l] = moved[0]
        small_grads[l]["norm_mix_g"] = g_mix[0]

    late_list = [jnp.stack([small_grads[l]["norm_mix_g"] for l in range(N_LAYERS)]), loss_part[0, :1]]
    late_parts = _exchange([_pack(late_list)], [GATHER], "gather_last_grad")[0]
    small_groups = [(early_names + ["final_norm_g"], early_list, moved[1]),
                    (["norm_mix_g", "loss"], late_list, late_parts)]

    given = dict(norm_mix_g=(norm_mix_g, m_norm_mix_g, v_norm_mix_g), gmlp_v_g=(gmlp_v_g, m_gmlp_v_g, v_gmlp_v_g),
                 gmlp_w_s=(gmlp_w_s, m_gmlp_w_s, v_gmlp_w_s), gmlp_b_s=(gmlp_b_s, m_gmlp_b_s, v_gmlp_b_s),
                 short_conv_w=(short_conv_w, m_short_conv_w, v_short_conv_w),
                 conf_conv_w=(conf_conv_w, m_conf_conv_w, v_conf_conv_w),
                 conf_ln_g=(conf_ln_g, m_conf_ln_g, v_conf_ln_g), conf_ln_b=(conf_ln_b, m_conf_ln_b, v_conf_ln_b),
                 mix_out_g=(mix_out_g, m_mix_out_g, v_mix_out_g), norm_ffn_g=(norm_ffn_g, m_norm_ffn_g, v_norm_ffn_g),
                 final_norm_g=(final_norm_g, m_final_norm_g, v_final_norm_g), loss=(jnp.zeros((1,), F32),) * 3)
    sharded_small = ("short_conv_w", "conf_conv_w")

    def widen(a):
        full = jnp.zeros(a.shape[:-1] + (D_GROUP,), a.dtype)
        return lax.dynamic_update_slice(full, a, (0, 0, me * conv_cols))

    small_res = {}
    for names, grads, gathered in small_groups:
        state = [_pack([widen(given[n][k]) if n in sharded_small else given[n][k] for n in names]) for k in range(3)]
        outs = _reduce_adamw(gathered, *state, state[0].shape[0], "adamw_small")
        for kind, packed in zip(("grad", "delta", "new_m", "new_v"), outs):
            for n, val in zip(names, _unpack(packed, [a.shape for a in grads])):
                if n in sharded_small:
                    val = lax.dynamic_slice(val, (0, 0, me * conv_cols), val.shape[:-1] + (conv_cols,))
                small_res[kind, n] = val

    big_names = ["w_in", "w_out", "w_up", "w_down"]
    big_given = dict(w_in=[t.transpose(0, 2, 1) for t in (w_in, m_w_in, v_w_in)], w_out=(w_out, m_w_out, v_w_out),
                     w_up=(w_up, m_w_up, v_w_up), w_down=(w_down, m_w_down, v_w_down))
    big_res = {}
    for j, n in enumerate(big_names):
        outs = None
        for l in range(N_LAYERS):
            outs = _reduce_adamw_layer(parts[4 * l + j], *big_given[n], l, outs,
                                       _adam_rows(*big_given[n][0].shape[1:]), "adamw_" + n)
        for kind, out in zip(("grad", "delta", "new_m", "new_v"), outs):
            big_res[kind, n] = out.transpose(0, 2, 1) if n == "w_in" else out

    order = ["norm_mix_g", "w_in", "gmlp_v_g", "gmlp_w_s", "gmlp_b_s", "short_conv_w", "conf_conv_w", "conf_ln_g",
             "conf_ln_b", "mix_out_g", "w_out", "norm_ffn_g", "w_up", "w_down", "final_norm_g"]
    result = [small_res["grad", "loss"][0], dx.reshape(x.shape)]
    for kind in ("grad", "delta", "new_m", "new_v"):
        for n in order:
            result.append(big_res[kind, n] if n in big_given else small_res[kind, n])
    return tuple(result)
```

```python
import jax
import jax.numpy as jnp
from jax import lax
from jax.experimental import pallas as pl
from jax.experimental.pallas import tpu as pltpu

F32 = jnp.float32
BF16 = jnp.bfloat16

D_MODEL = 1024
D_GROUP = 256
D_IN = 10 * D_GROUP
D_FF = 4 * D_MODEL
N_DEV = 8
N_LAYERS = 2
HEAD_DIM = 64
HEADS_PER_PAIR = 2
PAIR = HEADS_PER_PAIR * HEAD_DIM
CHUNK = 128
K_SHORT = 3
K_CONF = 31
HALO = 32
EPS = 1e-6
ATT_SCALE = HEAD_DIM ** -0.5
LOG_CUT = -104.0
MIB = 2 ** 20

ADAM_LR = 0.001
ADAM_B1 = 0.9
ADAM_B2 = 0.999
ADAM_EPS = 1e-08
ADAM_WD = 0.01
ADAM_STEP = 10

GELU_C = 0.7978845608028654
GELU_A = 0.044715


def _mm(a, b):
    return jnp.dot(a, b, preferred_element_type=F32)


def _mm_nt(a, b):
    return lax.dot_general(a, b, (((1,), (1,)), ((), ())), preferred_element_type=F32)


def _mm_tn(a, b):
    return lax.dot_general(a, b, (((0,), (0,)), ((), ())), preferred_element_type=F32)


def _rms(x):
    return lax.rsqrt(jnp.mean(x * x, axis=-1, keepdims=True) + EPS)


def _rms_bwd(dy, xh, r):
    return r * (dy - xh * jnp.mean(dy * xh, axis=-1, keepdims=True))


def _sigmoid(x):
    return 1.0 / (1.0 + jnp.exp(-x))


def _whole(shape):
    return pl.BlockSpec(shape, lambda *_: (0,) * len(shape))


def _resident(shape):
    return pl.BlockSpec(shape, lambda *_: (0,) * len(shape), pipeline_mode=pl.Buffered(1))


def _rows(tb, width, col=0):
    return pl.BlockSpec((tb, width), lambda i: (i, col))


def _params(semantics, vmem_mib):
    return pltpu.CompilerParams(dimension_semantics=semantics, vmem_limit_bytes=vmem_mib * MIB)


def _sds(shape, dtype=F32):
    return jax.ShapeDtypeStruct(shape, dtype)


def _split_bf16(v):
    hi = v.astype(BF16)
    lo = (v - hi.astype(F32)).astype(BF16)
    return hi, lo


GATHER, SCATTER = "gather", "scatter"
ANY_SPEC = pl.BlockSpec(memory_space=pl.ANY)


def _exchange_copies(ins, outs, modes, send_sems, recv_sems, local_sems, with_arrivals=True):
    x, y, c = lax.axis_index("x"), lax.axis_index("y"), lax.axis_index("c")
    me = 4 * x + 2 * y + c
    local, sends, arrivals = [], [], []
    for a, mode in enumerate(modes):
        local.append(pltpu.make_async_copy(ins[a].at[me] if mode == SCATTER else ins[a], outs[a].at[me], local_sems.at[a]))
    for k in range(N_DEV - 1):
        flip = k + 1
        peer = (1 - x if flip & 4 else x, 1 - y if flip & 2 else y, 1 - c if flip & 1 else c)
        pf = 4 * peer[0] + 2 * peer[1] + peer[2]
        for a, mode in enumerate(modes):
            src = ins[a].at[pf] if mode == SCATTER else ins[a]
            for dst, group in ((outs[a].at[me], sends), (outs[a].at[pf], arrivals)):
                if group is sends or with_arrivals:
                    group.append(pltpu.make_async_remote_copy(
                        src_ref=src, dst_ref=dst, send_sem=send_sems.at[a, k], recv_sem=recv_sems.at[a, k],
                        device_id=peer, device_id_type=pl.DeviceIdType.MESH))
    return local, sends, arrivals


def _exchange_start(*refs_and_modes):
    local, sends, _ = _exchange_copies(*refs_and_modes, with_arrivals=False)
    for cp in local + sends:
        cp.start()


def _exchange_wait(*refs_and_modes):
    local, sends, arrivals = _exchange_copies(*refs_and_modes)
    for cp in sends:
        cp.wait_send()
    for cp in arrivals:
        cp.wait_recv()
    for cp in local:
        cp.wait()


def _exchange_shapes(arrays, modes):
    out_shape = [_sds(a.shape if mode == SCATTER else (N_DEV,) + a.shape, a.dtype) for a, mode in zip(arrays, modes)]
    n = len(arrays)
    sems = [pltpu.SemaphoreType.DMA((n, N_DEV - 1)), pltpu.SemaphoreType.DMA((n, N_DEV - 1)), pltpu.SemaphoreType.DMA((n,))]
    return out_shape, sems


def _exchange(arrays, modes, name):
    n = len(arrays)
    out_shape, sems = _exchange_shapes(arrays, modes)

    def body(*refs):
        _exchange_start(refs[:n], refs[n:2 * n], modes, *refs[2 * n:])
        _exchange_wait(refs[:n], refs[n:2 * n], modes, *refs[2 * n:])

    return pl.pallas_call(body, name=name, out_shape=out_shape, in_specs=[ANY_SPEC] * n, out_specs=[ANY_SPEC] * n,
                          scratch_shapes=sems)(*arrays)


def _call(body, args, ride, *, name, grid, in_specs, out_specs, out_shape, scratch_shapes=(), compiler_params):
    if ride is None:
        outs = pl.pallas_call(body, name=name, grid=grid, in_specs=in_specs, out_specs=out_specs, out_shape=out_shape,
                              scratch_shapes=scratch_shapes, compiler_params=compiler_params)(*args)
        return outs, []
    arrays, modes = ride
    n, n_in, n_out, n_scratch = len(arrays), len(in_specs), len(out_specs), len(scratch_shapes)
    moved_shape, sems = _exchange_shapes(arrays, modes)
    n_steps = 1
    for g in grid:
        n_steps *= g

    def riding(*refs):
        ins, refs = refs[:n_in], refs[n_in:]
        r_ins, refs = refs[:n], refs[n:]
        outs, refs = refs[:n_out], refs[n_out:]
        r_outs, refs = refs[:n], refs[n:]
        scratch, r_sems = refs[:n_scratch], refs[n_scratch:]
        step = pl.program_id(0)
        for axis in range(1, len(grid)):
            step = step * grid[axis] + pl.program_id(axis)

        @pl.when(step == 0)
        def _():
            _exchange_start(r_ins, r_outs, modes, *r_sems)

        body(*ins, *outs, *scratch)

        @pl.when(step == n_steps - 1)
        def _():
            _exchange_wait(r_ins, r_outs, modes, *r_sems)

    outs = pl.pallas_call(
        riding, name=name, grid=grid, in_specs=list(in_specs) + [ANY_SPEC] * n,
        out_specs=list(out_specs) + [ANY_SPEC] * n, out_shape=list(out_shape) + moved_shape,
        scratch_shapes=list(scratch_shapes) + sems, compiler_params=compiler_params)(*args, *arrays)
    return outs[:n_out], outs[n_out:]


def _in_proj_fwd(x, g, wint, tb, ride=None):
    s = x.shape[0]

    def body(x_ref, g_ref, w_ref, z_ref, qkv_ref, hb_ref):
        xv = x_ref[...]
        h = (xv * _rms(xv) * g_ref[...]).astype(BF16)
        hb_ref[...] = h
        z = _mm_nt(h, w_ref[...])
        z_ref[...] = z
        qkv_ref[:, 0:D_GROUP] = (z[:, 5 * D_GROUP:6 * D_GROUP] * ATT_SCALE).astype(BF16)
        qkv_ref[:, D_GROUP:3 * D_GROUP] = z[:, 6 * D_GROUP:8 * D_GROUP].astype(BF16)

    return _call(
        body, (x, g, wint), ride, name="in_proj_fwd", grid=(s // tb,),
        in_specs=[_rows(tb, D_MODEL), _whole((1, D_MODEL)), _resident((D_IN, D_MODEL))],
        out_specs=[_rows(tb, D_IN), _rows(tb, 3 * D_GROUP), _rows(tb, D_MODEL)],
        out_shape=[_sds((s, D_IN)), _sds((s, 3 * D_GROUP), BF16), _sds((s, D_MODEL), BF16)],
        compiler_params=_params(("arbitrary",), 58))


def _gelu(x):
    return 0.5 * x * (1.0 + jnp.tanh(GELU_C * (x + GELU_A * x * x * x)))


def _gelu_grad(x):
    t = jnp.tanh(GELU_C * (x + GELU_A * x * x * x))
    return 0.5 * (1.0 + t) + 0.5 * x * (1.0 - t * t) * GELU_C * (1.0 + 3.0 * GELU_A * x * x)


def _head_lane(width):
    return lax.broadcasted_iota(jnp.int32, (1, width), 1) // HEAD_DIM


def _gating_chunk(wm_ref, bexp_ref, vc, lane_h):
    f = bexp_ref[...]
    for h in range(D_GROUP // HEAD_DIM):
        f = f + _mm(wm_ref[h], jnp.where(lane_h == h, vc, 0))
    return f


def _halo_specs(s, tb, width_blocks):
    per = tb // HALO
    prev = pl.BlockSpec((HALO, width_blocks), lambda i: (jnp.maximum(i * per - 1, 0), 0))
    nxt = pl.BlockSpec((HALO, width_blocks), lambda i: (jnp.minimum((i + 1) * per, s // HALO - 1), 0))
    return prev, nxt


SUBLANES = 8
TAP_SLACK = 24


ROW_CHUNK = 32


def _shifted_copies(buf_ref, copies_ref, first, n_taps, rows):
    where = {}
    for residue in range(SUBLANES):
        taps = [j for j in range(n_taps) if (first + j) % SUBLANES == residue]
        if not taps:
            continue
        lo = first + taps[0]
        span = first + taps[-1] - lo + rows
        copies_ref[residue, 0:span, :] = buf_ref[pl.ds(lo, span), :]
        for j in taps:
            where[j] = (residue, first + j - lo)
    return where


def _conv_chunks(copies_ref, where, rows, weight, out_ref):
    for c0 in range(0, rows, ROW_CHUNK):
        acc = jnp.zeros((ROW_CHUNK, D_GROUP), F32)
        for j, (residue, offset) in sorted(where.items()):
            acc = acc + weight(j) * copies_ref[residue, pl.ds(offset + c0, ROW_CHUNK), :]
        out_ref[pl.ds(c0, ROW_CHUNK), :] = acc


def _corr_chunks(copies_ref, where, rows, dc_ref, sums_ref):
    for c0 in range(0, rows, ROW_CHUNK):
        dc = dc_ref[pl.ds(c0, ROW_CHUNK), :]
        for j, (residue, offset) in sorted(where.items()):
            prod = dc * copies_ref[residue, pl.ds(offset + c0, ROW_CHUNK), :]
            part = prod[0:SUBLANES]
            for r0 in range(SUBLANES, ROW_CHUNK, SUBLANES):
                part = part + prod[r0:r0 + SUBLANES]
            sums_ref[j] += part


def _mix_fwd(z, vg, wm, bexp, scw, ccw, lng, lnb, tb, ride=None):
    s = z.shape[0]
    prev_spec, _ = _halo_specs(s, tb, D_IN)

    def body(z_ref, zp_ref, vg_ref, wm_ref, bexp_ref, scw_ref, ccw_ref, lng_ref, lnb_ref,
             ya_ref, yb_ref, yd_ref, c_ref, pbuf, hbuf, copies):
        keep = (pl.program_id(0) > 0).astype(F32)
        lane_h = _head_lane(D_GROUP)
        ga = _gelu(z_ref[:, 0:2 * D_GROUP])
        u, v = ga[:, :D_GROUP], ga[:, D_GROUP:]
        vn = (v * _rms(v) * vg_ref[...]).astype(BF16)
        for n in range(tb // CHUNK):
            rows = slice(n * CHUNK, (n + 1) * CHUNK)
            ya_ref[rows, :] = u[rows] * _gating_chunk(wm_ref, bexp_ref, vn[rows], lane_h)
        p = z_ref[:, 3 * D_GROUP:4 * D_GROUP] * z_ref[:, 4 * D_GROUP:5 * D_GROUP]
        pbuf[0:HALO, :] = zp_ref[:, 3 * D_GROUP:4 * D_GROUP] * zp_ref[:, 4 * D_GROUP:5 * D_GROUP] * keep
        pbuf[HALO:HALO + tb, :] = p
        cv = scw_ref[K_SHORT - 1:K_SHORT, :] * p
        for k in range(K_SHORT - 1):
            cv = cv + scw_ref[k:k + 1, :] * pbuf[pl.ds(HALO - (K_SHORT - 1) + k, tb), :]
        yb_ref[...] = z_ref[:, 2 * D_GROUP:3 * D_GROUP] * cv
        hbuf[0:HALO, :] = zp_ref[:, 8 * D_GROUP:9 * D_GROUP] * _sigmoid(zp_ref[:, 9 * D_GROUP:10 * D_GROUP]) * keep
        hbuf[HALO:HALO + tb, :] = z_ref[:, 8 * D_GROUP:9 * D_GROUP] * _sigmoid(z_ref[:, 9 * D_GROUP:10 * D_GROUP])
        where = _shifted_copies(hbuf, copies, HALO - (K_CONF - 1), K_CONF, tb)
        _conv_chunks(copies, where, tb, lambda k: ccw_ref[k:k + 1, :], c_ref)
        c = c_ref[...]
        xc = c - jnp.mean(c, axis=-1, keepdims=True)
        ln = xc * lax.rsqrt(jnp.mean(xc * xc, axis=-1, keepdims=True) + EPS) * lng_ref[...] + lnb_ref[...]
        yd_ref[...] = ln * _sigmoid(ln)

    grp = _rows(tb, D_GROUP)
    return _call(
        body, (z, z, vg, wm, bexp, scw, ccw, lng, lnb), ride, name="mix_fwd", grid=(s // tb,),
        in_specs=[_rows(tb, D_IN), prev_spec, _whole((1, D_GROUP)), _whole(wm.shape), _whole(bexp.shape),
                  _whole(scw.shape), _whole(ccw.shape), _whole((1, D_GROUP)), _whole((1, D_GROUP))],
        out_specs=[grp, grp, grp, grp],
        out_shape=[_sds((s, D_GROUP))] * 4,
        scratch_shapes=[pltpu.VMEM((HALO + tb, D_GROUP), F32), pltpu.VMEM((HALO + tb, D_GROUP), F32),
                        pltpu.VMEM((SUBLANES, tb + TAP_SLACK, D_GROUP), F32)],
        compiler_params=_params(("arbitrary",), 52))


def _stick_tile(qh, kt, causal, c, upper):
    x = _mm_nt(qh, kt)
    soft = jnp.log(1.0 + jnp.exp(-jnp.abs(x)))
    lb = jnp.minimum(x, 0.0) - soft
    lom = jnp.where(causal, -jnp.maximum(x, 0.0) - soft, 0.0)
    hi, lo = _split_bf16(lom)
    stick = c + _mm(hi, upper[...]) + _mm(lo, upper[...])
    w = jnp.where(causal, jnp.exp(lb + stick), 0.0)
    return w, lb, lom


def _triangle(n, diagonal):
    return jnp.tri(n, n, diagonal, dtype=BF16)


def _causal_tile(qi, tq, k0, tk):
    qpos = qi * tq + lax.broadcasted_iota(jnp.int32, (tq, 1), 0)
    return k0 + lax.broadcasted_iota(jnp.int32, (1, tk), 1) < qpos


def _sticks_alive(cs):
    longest = cs[0]
    for c in cs[1:]:
        longest = jnp.maximum(longest, c)
    return (jnp.max(longest) > LOG_CUT).astype(jnp.int32)


def _walk(body, qi, tq, tk, init):
    start = (((qi + 1) * tq - 1) // tk, jnp.int32(1)) + tuple(init)
    return lax.while_loop(lambda cr: jnp.logical_and(cr[0] >= 0, cr[1] > 0), body, start)[2:]


def _attn_fwd(qkv, tq, tk, ride=None):
    s = qkv.shape[0]
    n_heads = D_GROUP // HEAD_DIM

    def body(q_ref, k_ref, v_ref, upper, o_ref):
        qi = pl.program_id(0)
        q = q_ref[...]
        lane_h = _head_lane(D_GROUP)
        qhs = [jnp.where(lane_h == h, q, 0) for h in range(n_heads)]

        def step(carry):
            kb, _, acc = carry[:3]
            cs = list(carry[3:])
            k0 = pl.multiple_of(kb * tk, tk)
            kt = k_ref[pl.ds(k0, tk), :]
            vt = v_ref[pl.ds(k0, tk), :]
            causal = _causal_tile(qi, tq, k0, tk)
            for h in range(n_heads):
                w, _, lom = _stick_tile(qhs[h], kt, causal, cs[h], upper)
                acc = acc + _mm(w.astype(BF16), jnp.where(lane_h == h, vt, 0))
                cs[h] = cs[h] + jnp.sum(lom, axis=1, keepdims=True)
            return (kb - 1, _sticks_alive(cs), acc) + tuple(cs)

        init = [jnp.zeros((tq, D_GROUP), F32)] + [jnp.zeros((tq, 1), F32)] * n_heads
        o_ref[...] = _walk(step, qi, tq, tk, init)[0]

    return _call(
        body, (qkv, qkv, qkv, _triangle(tk, -1)), ride, name="attn_fwd", grid=(s // tq,),
        in_specs=[pl.BlockSpec((tq, D_GROUP), lambda qi: (qi, 0)),
                  pl.BlockSpec((s, D_GROUP), lambda qi: (0, 1), pipeline_mode=pl.Buffered(1)),
                  pl.BlockSpec((s, D_GROUP), lambda qi: (0, 2), pipeline_mode=pl.Buffered(1)),
                  _resident((tk, tk))],
        out_specs=[pl.BlockSpec((tq, D_GROUP), lambda qi: (qi, 0))],
        out_shape=[_sds((s, D_GROUP))],
        compiler_params=_params(("arbitrary",), 40))


def _out_proj_fwd(ys, x, mg, wout, tb):
    s = x.shape[0]

    def body(ya_ref, yb_ref, yc_ref, yd_ref, x_ref, mg_ref, w_ref, o_ref):
        acc = x_ref[...]
        for gi, y_ref in enumerate((ya_ref, yb_ref, yc_ref, yd_ref)):
            cols = slice(gi * D_GROUP, (gi + 1) * D_GROUP)
            y = y_ref[...]
            acc = acc + _mm((y * _rms(y) * mg_ref[:, cols]).astype(BF16), w_ref[cols, :])
        o_ref[...] = acc

    grp = _rows(tb, D_GROUP)
    return pl.pallas_call(
        body, name="out_proj_fwd", grid=(s // tb,),
        in_specs=[grp, grp, grp, grp, _rows(tb, D_MODEL), _whole((1, D_MODEL)), _whole((D_MODEL, D_MODEL))],
        out_specs=_rows(tb, D_MODEL), out_shape=_sds((s, D_MODEL)),
        compiler_params=_params(("parallel",), 44),
    )(*ys, x, mg, wout)


def _ffn_fwd(x, g, wup, wdn, tb, ride=None):
    s = x.shape[0]
    ff = D_FF // N_DEV

    def body(x_ref, g_ref, wu_ref, wd_ref, o_ref, a_ref):
        xv = x_ref[...]
        h = (xv * _rms(xv) * g_ref[...]).astype(BF16)
        acc = xv
        for d in range(N_DEV):
            a = jnp.maximum(_mm(h, wu_ref[d]), 0.0)
            a_ref[:, d * ff:(d + 1) * ff] = a.astype(BF16)
            acc = acc + _mm((a * a).astype(BF16), wd_ref[d])
        o_ref[...] = acc

    return _call(
        body, (x, g, wup, wdn), ride, name="ffn_fwd", grid=(s // tb,),
        in_specs=[_rows(tb, D_MODEL), _whole((1, D_MODEL)), _resident((N_DEV, D_MODEL, ff)), _resident((N_DEV, ff, D_MODEL))],
        out_specs=[_rows(tb, D_MODEL), _rows(tb, D_FF)], out_shape=[_sds((s, D_MODEL)), _sds((s, D_FF), BF16)],
        compiler_params=_params(("arbitrary",), 56))


def _loss_head(x, g, tgt, tb):
    s = x.shape[0]

    def body(x_ref, g_ref, t_ref, dx_ref, dg_ref, loss_ref):
        @pl.when(pl.program_id(0) == 0)
        def _():
            dg_ref[...] = jnp.zeros_like(dg_ref)
            loss_ref[...] = jnp.zeros_like(loss_ref)

        xv = x_ref[...]
        r = _rms(xv)
        xh = xv * r
        err = xh * g_ref[...] - t_ref[...]
        loss_ref[...] += 0.5 * jnp.sum(jnp.mean(err * err, axis=-1, keepdims=True))
        dy = err * (1.0 / D_MODEL)
        dg_ref[...] += jnp.sum(dy * xh, axis=0, keepdims=True)
        dx_ref[...] = _rms_bwd(dy * g_ref[...], xh, r)

    return pl.pallas_call(
        body, name="loss_head", grid=(s // tb,),
        in_specs=[_rows(tb, D_MODEL), _whole((1, D_MODEL)), _rows(tb, D_MODEL)],
        out_specs=[_rows(tb, D_MODEL), _whole((1, D_MODEL)), _whole((8, 128))],
        out_shape=[_sds((s, D_MODEL)), _sds((1, D_MODEL)), _sds((8, 128))],
        compiler_params=_params(("arbitrary",), 44),
    )(x, g, tgt)


def _ffn_bwd(x1, act, dx2, g, wup, wdn, tb, ride=None):
    s = x1.shape[0]
    ff = D_FF // N_DEV

    def body(x_ref, a_ref, dy_ref, g_ref, wu_ref, wd_ref, dx_ref, hb_ref, dpre_ref, dg_ref):
        @pl.when(pl.program_id(0) == 0)
        def _():
            dg_ref[...] = jnp.zeros_like(dg_ref)

        xv = x_ref[...]
        r = _rms(xv)
        xh = xv * r
        hb_ref[...] = (xh * g_ref[...]).astype(BF16)
        dyv = dy_ref[...]
        dyb = dyv.astype(BF16)
        dh = jnp.zeros((tb, D_MODEL), F32)
        for d in range(N_DEV):
            cols = slice(d * ff, (d + 1) * ff)
            a = a_ref[:, cols].astype(F32)
            dpre = (_mm_nt(dyb, wd_ref[d]) * (2.0 * a)).astype(BF16)
            dpre_ref[:, cols] = dpre
            dh = dh + _mm_nt(dpre, wu_ref[d])
        dg_ref[...] += jnp.sum(dh * xh, axis=0, keepdims=True)
        dx_ref[...] = dyv + _rms_bwd(dh * g_ref[...], xh, r)

    return _call(
        body, (x1, act, dx2, g, wup, wdn), ride, name="ffn_bwd", grid=(s // tb,),
        in_specs=[_rows(tb, D_MODEL), _rows(tb, D_FF), _rows(tb, D_MODEL), _whole((1, D_MODEL)),
                  _resident((N_DEV, D_MODEL, ff)), _resident((N_DEV, ff, D_MODEL))],
        out_specs=[_rows(tb, D_MODEL), _rows(tb, D_MODEL), _rows(tb, D_FF), _whole((1, D_MODEL))],
        out_shape=[_sds((s, D_MODEL)), _sds((s, D_MODEL), BF16), _sds((s, D_FF), BF16), _sds((1, D_MODEL))],
        compiler_params=_params(("arbitrary",), 58))


def _tn_slabs(a, b, tb, transpose_slabs, name, square_b=False):
    s, m = a.shape
    width = b.shape[1] // N_DEV
    n_steps = s // tb
    slab = (width, m) if transpose_slabs else (m, width)

    def body(a_ref, b_ref, o_hbm, acc, stage, sem):
        step = pl.program_id(0)

        @pl.when(step == 0)
        def _():
            acc[...] = jnp.zeros_like(acc)

        bv = b_ref[...]
        if square_b:
            bv = bv.astype(F32)
            bv = bv * bv
        acc[...] += _mm_tn(a_ref[...].astype(BF16), bv.astype(BF16))

        @pl.when(step == n_steps - 1)
        def _():
            for d in range(N_DEV):
                cols = acc.at[:, pl.ds(d * width, width)]
                if transpose_slabs:
                    stage[...] = cols[...].T
                cp = pltpu.make_async_copy(stage if transpose_slabs else cols, o_hbm.at[d], sem.at[0])
                cp.start()
                cp.wait()

    return pl.pallas_call(
        body, name=name, grid=(n_steps,),
        in_specs=[_rows(tb, m), _rows(tb, b.shape[1])], out_specs=ANY_SPEC, out_shape=_sds((N_DEV,) + slab),
        scratch_shapes=[pltpu.VMEM((m, b.shape[1]), F32), pltpu.VMEM(slab, F32), pltpu.SemaphoreType.DMA((1,))],
        compiler_params=_params(("arbitrary",), 56),
    )(a, b)


def _out_proj_bwd(dx1, ys, mg, wout, tb):
    s = dx1.shape[0]

    def body(dx_ref, ya_ref, yb_ref, yc_ref, yd_ref, mg_ref, w_ref,
             dya_ref, dyb_ref, dyc_ref, dyd_ref, dmg_ref, dw_ref, yn_ref):
        @pl.when(pl.program_id(0) == 0)
        def _():
            dmg_ref[...] = jnp.zeros_like(dmg_ref)
            dw_ref[...] = jnp.zeros_like(dw_ref)

        dxb = dx_ref[...].astype(BF16)
        dyn = _mm_nt(dxb, w_ref[...])
        groups = ((ya_ref, dya_ref), (yb_ref, dyb_ref), (yc_ref, dyc_ref), (yd_ref, dyd_ref))
        for gi, (y_ref, dy_ref) in enumerate(groups):
            cols = slice(gi * D_GROUP, (gi + 1) * D_GROUP)
            y = y_ref[...]
            r = _rms(y)
            n = y * r
            gain = mg_ref[:, cols]
            dn = dyn[:, cols]
            yn_ref[:, cols] = (n * gain).astype(BF16)
            dmg_ref[:, cols] += jnp.sum(dn * n, axis=0, keepdims=True)
            dy_ref[...] = _rms_bwd(dn * gain, n, r)
        dw_ref[...] += _mm_tn(yn_ref[...], dxb)

    grp = _rows(tb, D_GROUP)
    return pl.pallas_call(
        body, name="out_proj_bwd", grid=(s // tb,),
        in_specs=[_rows(tb, D_MODEL), grp, grp, grp, grp, _whole((1, D_MODEL)), _resident((D_MODEL, D_MODEL))],
        out_specs=[grp, grp, grp, grp, _whole((1, D_MODEL)), _whole((D_MODEL, D_MODEL))],
        out_shape=[_sds((s, D_GROUP))] * 4 + [_sds((1, D_MODEL)), _sds((D_MODEL, D_MODEL))],
        scratch_shapes=[pltpu.VMEM((tb, D_MODEL), BF16)],
        compiler_params=_params(("arbitrary",), 52),
    )(dx1, *ys, mg, wout)


def _attn_bwd(qkv, do, o, tq, tk, ride=None):
    s = qkv.shape[0]
    nq = s // tq
    n_pairs = D_GROUP // PAIR

    def body(q_ref, k_ref, v_ref, do_ref, o_ref, upper, upper_eq, dq_ref, dk_hbm, dv_hbm, dk_acc, dv_acc, sems):
        hp, qi = pl.program_id(0), pl.program_id(1)

        @pl.when(qi == 0)
        def _():
            dk_acc[...] = jnp.zeros_like(dk_acc)
            dv_acc[...] = jnp.zeros_like(dv_acc)

        q = q_ref[...]
        dob = do_ref[...].astype(BF16)
        prod = dob.astype(F32) * o_ref[...]
        lane_h = _head_lane(PAIR)
        heads = []
        for h in range(HEADS_PER_PAIR):
            in_head = lane_h == h
            total = jnp.sum(jnp.where(in_head, prod, 0.0), axis=1, keepdims=True)
            heads.append((in_head, jnp.where(in_head, q, 0), jnp.where(in_head, dob, 0), total))

        def step(carry):
            kb, _, acc = carry[:3]
            cs = list(carry[3:3 + HEADS_PER_PAIR])
            nears = list(carry[3 + HEADS_PER_PAIR:])
            k0 = pl.multiple_of(kb * tk, tk)
            kt = k_ref[pl.ds(k0, tk), :]
            vt = v_ref[pl.ds(k0, tk), :]
            causal = _causal_tile(qi, tq, k0, tk)
            dk_t = jnp.zeros((tk, PAIR), F32)
            dv_t = jnp.zeros((tk, PAIR), F32)
            for h, (in_head, qh, doh, total) in enumerate(heads):
                w, lb, lom = _stick_tile(qh, kt, causal, cs[h], upper)
                wb = w.astype(BF16)
                gw = _mm_nt(doh, vt) * wb.astype(F32)
                hi, lo = _split_bf16(gw)
                far = total - nears[h] - _mm(hi, upper_eq[...]) - _mm(lo, upper_eq[...])
                beta = jnp.exp(lb)
                dxb = jnp.where(causal, gw - beta * (gw + far), 0.0).astype(BF16)
                acc = acc + _mm(dxb, jnp.where(in_head, kt, 0))
                dk_t = dk_t + _mm_tn(dxb, qh)
                dv_t = dv_t + _mm_tn(wb, doh)
                cs[h] = cs[h] + jnp.sum(lom, axis=1, keepdims=True)
                nears[h] = nears[h] + jnp.sum(gw, axis=1, keepdims=True)
            dk_acc[pl.ds(k0, tk), :] += dk_t
            dv_acc[pl.ds(k0, tk), :] += dv_t
            return (kb - 1, _sticks_alive(cs), acc) + tuple(cs) + tuple(nears)

        init = [jnp.zeros((tq, PAIR), F32)] + [jnp.zeros((tq, 1), F32)] * (2 * HEADS_PER_PAIR)
        dq_ref[...] = _walk(step, qi, tq, tk, init)[0]

        @pl.when(qi == nq - 1)
        def _():
            ck = pltpu.make_async_copy(dk_acc, dk_hbm.at[hp], sems.at[0])
            cv = pltpu.make_async_copy(dv_acc, dv_hbm.at[hp], sems.at[1])
            ck.start()
            cv.start()
            ck.wait()
            cv.wait()

    blk = pl.BlockSpec((tq, PAIR), lambda hp, qi: (qi, hp))
    return _call(
        body, (qkv, qkv, qkv, do, o, _triangle(tk, -1), _triangle(tk, 0)), ride, name="attn_bwd", grid=(n_pairs, nq),
        in_specs=[blk, pl.BlockSpec((s, PAIR), lambda hp, qi: (0, 2 + hp)),
                  pl.BlockSpec((s, PAIR), lambda hp, qi: (0, 4 + hp)), blk, blk, _resident((tk, tk)), _resident((tk, tk))],
        out_specs=[blk, ANY_SPEC, ANY_SPEC],
        out_shape=[_sds((s, D_GROUP)), _sds((n_pairs, s, PAIR)), _sds((n_pairs, s, PAIR))],
        scratch_shapes=[pltpu.VMEM((s, PAIR), F32), pltpu.VMEM((s, PAIR), F32), pltpu.SemaphoreType.DMA((2,))],
        compiler_params=_params(("arbitrary", "arbitrary"), 56))


def _mix_bwd(z, conv, dya, dyb, dyd, dq, dk, dv, hb, vg, wm, wmt, bexp, scw, ccw, lng, lnb, tb, ride=None):
    s = z.shape[0]
    n_steps = s // tb
    prev_spec, next_spec = _halo_specs(s, tb, D_IN)
    _, next_grp = _halo_specs(s, tb, D_GROUP)
    ext = tb + HALO

    def body(z_ref, zp_ref, zn_ref, dya_ref, dyb_ref, dybn_ref, dyd_ref, dydn_ref, c_ref, cn_ref, dq_ref, dk0_ref, dk1_ref,
             dv0_ref, dv1_ref, hbp_ref, hb_ref, vg_ref, wm_ref, wmt_ref, bexp_ref, scw_ref, ccw_ref, lng_ref, lnb_ref,
             dz_ref, dvg_ref, dws_ref, dbs_ref, dscw_ref, dccw_ref, dlng_ref, dlnb_ref, dwin_hbm,
             pbuf, hbuf, gbuf, cbuf, dubuf, dvnbuf, copies, tap_sums, win_acc, dz_prev, win_sem):
        i = pl.program_id(0)

        @pl.when(i == 0)
        def _():
            for ref in (dvg_ref, dws_ref, dbs_ref, dscw_ref, dccw_ref, dlng_ref, dlnb_ref, win_acc, dz_prev):
                ref[...] = jnp.zeros_like(ref)

        win_acc[...] += _mm_tn(dz_prev[...], hbp_ref[...])

        keep_prev = (i > 0).astype(F32)
        keep_next = (i < n_steps - 1).astype(F32)
        lane_h = _head_lane(D_GROUP)

        za = z_ref[:, 0:2 * D_GROUP]
        ga = _gelu(za)
        u, v = ga[:, :D_GROUP], ga[:, D_GROUP:]
        r = _rms(v)
        vh = v * r
        vn = (vh * vg_ref[...]).astype(BF16)
        tril = lax.broadcasted_iota(jnp.int32, (CHUNK, CHUNK), 0) >= lax.broadcasted_iota(jnp.int32, (CHUNK, CHUNK), 1)
        dbias = jnp.zeros((CHUNK, D_GROUP), F32)
        for n in range(tb // CHUNK):
            rows = slice(n * CHUNK, (n + 1) * CHUNK)
            vc = vn[rows]
            dy = dya_ref[rows, :]
            dubuf[rows, :] = dy * _gating_chunk(wm_ref, bexp_ref, vc, lane_h)
            df = dy * u[rows]
            dfb = df.astype(BF16)
            dvn = jnp.zeros((CHUNK, D_GROUP), F32)
            for h in range(D_GROUP // HEAD_DIM):
                dfh = jnp.where(lane_h == h, dfb, 0)
                dvn = dvn + _mm(wmt_ref[h], dfh)
                dws_ref[h] += jnp.where(tril, _mm_nt(dfh, vc), 0.0)
            dvnbuf[rows, :] = dvn
            dbias = dbias + df
        for h in range(D_GROUP // HEAD_DIM):
            per_head = jnp.sum(jnp.where(lane_h == h, dbias, 0.0), axis=1, keepdims=True)
            dbs_ref[...] += per_head * (lax.broadcasted_iota(jnp.int32, (1, CHUNK), 1) == h).astype(F32)
        dvn = dvnbuf[...]
        dvg_ref[...] += jnp.sum(dvn * vh, axis=0, keepdims=True)
        dgelu = _gelu_grad(za)
        dz_ref[:, 0:D_GROUP] = (dubuf[...] * dgelu[:, :D_GROUP]).astype(BF16)
        dz_ref[:, D_GROUP:2 * D_GROUP] = (_rms_bwd(dvn * vg_ref[...], vh, r) * dgelu[:, D_GROUP:]).astype(BF16)

        gate_b = z_ref[:, 2 * D_GROUP:3 * D_GROUP]
        gate_c = z_ref[:, 3 * D_GROUP:4 * D_GROUP]
        hh = z_ref[:, 4 * D_GROUP:5 * D_GROUP]
        p = gate_c * hh
        pbuf[0:HALO, :] = zp_ref[:, 3 * D_GROUP:4 * D_GROUP] * zp_ref[:, 4 * D_GROUP:5 * D_GROUP] * keep_prev
        pbuf[HALO:HALO + tb, :] = p
        dyb_v = dyb_ref[...]
        dcv = dyb_v * gate_b
        gbuf[0:tb, :] = dcv
        gbuf[tb:ext, :] = dybn_ref[...] * zn_ref[:, 2 * D_GROUP:3 * D_GROUP] * keep_next
        cv = scw_ref[K_SHORT - 1:K_SHORT, :] * p
        dp = scw_ref[K_SHORT - 1:K_SHORT, :] * dcv
        dscw_ref[K_SHORT - 1:K_SHORT, :] += jnp.sum(dcv * p, axis=0, keepdims=True)
        for k in range(K_SHORT - 1):
            earlier = pbuf[pl.ds(HALO - (K_SHORT - 1) + k, tb), :]
            cv = cv + scw_ref[k:k + 1, :] * earlier
            dp = dp + scw_ref[k:k + 1, :] * gbuf[pl.ds(K_SHORT - 1 - k, tb), :]
            dscw_ref[k:k + 1, :] += jnp.sum(dcv * earlier, axis=0, keepdims=True)
        dz_ref[:, 2 * D_GROUP:3 * D_GROUP] = (dyb_v * cv).astype(BF16)
        dz_ref[:, 3 * D_GROUP:4 * D_GROUP] = (dp * hh).astype(BF16)
        dz_ref[:, 4 * D_GROUP:5 * D_GROUP] = (dp * gate_c).astype(BF16)

        dz_ref[:, 5 * D_GROUP:6 * D_GROUP] = (dq_ref[...] * ATT_SCALE).astype(BF16)
        dz_ref[:, 6 * D_GROUP:6 * D_GROUP + PAIR] = dk0_ref[...].astype(BF16)
        dz_ref[:, 6 * D_GROUP + PAIR:7 * D_GROUP] = dk1_ref[...].astype(BF16)
        dz_ref[:, 7 * D_GROUP:7 * D_GROUP + PAIR] = dv0_ref[...].astype(BF16)
        dz_ref[:, 7 * D_GROUP + PAIR:8 * D_GROUP] = dv1_ref[...].astype(BF16)

        a = z_ref[:, 8 * D_GROUP:9 * D_GROUP]
        sg = _sigmoid(z_ref[:, 9 * D_GROUP:10 * D_GROUP])
        hbuf[0:HALO, :] = zp_ref[:, 8 * D_GROUP:9 * D_GROUP] * _sigmoid(zp_ref[:, 9 * D_GROUP:10 * D_GROUP]) * keep_prev
        hbuf[HALO:HALO + tb, :] = a * sg
        c = jnp.concatenate([c_ref[...], cn_ref[...]], axis=0)
        xc = c - jnp.mean(c, axis=-1, keepdims=True)
        rs = lax.rsqrt(jnp.mean(xc * xc, axis=-1, keepdims=True) + EPS)
        xh = xc * rs
        ln = xh * lng_ref[...] + lnb_ref[...]
        sl = _sigmoid(ln)
        dy_ext = jnp.concatenate([dyd_ref[...], dydn_ref[...] * keep_next], axis=0)
        dln = dy_ext * sl * (1.0 + ln * (1.0 - sl))
        dlng_ref[...] += jnp.sum(dln[:tb] * xh[:tb], axis=0, keepdims=True)
        dlnb_ref[...] += jnp.sum(dln[:tb], axis=0, keepdims=True)
        dxh = dln * lng_ref[...]
        dc = rs * (dxh - jnp.mean(dxh, axis=-1, keepdims=True) - xh * jnp.mean(dxh * xh, axis=-1, keepdims=True))
        cbuf[...] = dc
        where = _shifted_copies(cbuf, copies, 0, K_CONF, tb)
        _conv_chunks(copies, where, tb, lambda j: ccw_ref[K_CONF - 1 - j:K_CONF - j, :], dubuf)
        dhd = dubuf[...]
        tap_sums[...] = jnp.zeros_like(tap_sums)
        where = _shifted_copies(hbuf, copies, HALO - (K_CONF - 1), K_CONF, tb)
        _corr_chunks(copies, where, tb, cbuf, tap_sums)
        for k in range(K_CONF):
            dccw_ref[k:k + 1, :] += jnp.sum(tap_sums[k], axis=0, keepdims=True)
        dz_ref[:, 8 * D_GROUP:9 * D_GROUP] = (dhd * sg).astype(BF16)
        dz_ref[:, 9 * D_GROUP:10 * D_GROUP] = (dhd * a * sg * (1.0 - sg)).astype(BF16)

        dz_prev[...] = dz_ref[...]

        @pl.when(i == n_steps - 1)
        def _():
            win_acc[...] += _mm_tn(dz_ref[...], hb_ref[...])
            cp = pltpu.make_async_copy(win_acc, dwin_hbm, win_sem.at[0])
            cp.start()
            cp.wait()

    grp = _rows(tb, D_GROUP)
    pair0 = pl.BlockSpec((None, tb, PAIR), lambda i: (0, i, 0))
    pair1 = pl.BlockSpec((None, tb, PAIR), lambda i: (1, i, 0))
    small = [_sds((1, D_GROUP)), _sds((4, CHUNK, CHUNK)), _sds((CHUNK, CHUNK)), _sds((8, D_GROUP)),
             _sds((HALO, D_GROUP)), _sds((1, D_GROUP)), _sds((1, D_GROUP))]
    return _call(
        body, (z, z, z, dya, dyb, dyb, dyd, dyd, conv, conv, dq, dk, dk, dv, dv, hb, hb, vg, wm, wmt, bexp, scw, ccw, lng, lnb),
        ride, name="mix_bwd", grid=(n_steps,),
        in_specs=[_rows(tb, D_IN), prev_spec, next_spec, grp, grp, next_grp, grp, next_grp, grp, next_grp, grp,
                  pair0, pair1, pair0, pair1,
                  pl.BlockSpec((tb, D_MODEL), lambda i: (jnp.maximum(i - 1, 0), 0)), _rows(tb, D_MODEL),
                  _whole((1, D_GROUP)), _whole(wm.shape), _whole(wmt.shape), _whole(bexp.shape), _whole(scw.shape),
                  _whole(ccw.shape), _whole((1, D_GROUP)), _whole((1, D_GROUP))],
        out_specs=[_rows(tb, D_IN)] + [_whole(t.shape) for t in small] + [ANY_SPEC],
        out_shape=[_sds((s, D_IN), BF16)] + small + [_sds((D_IN, D_MODEL))],
        scratch_shapes=[pltpu.VMEM((HALO + tb, D_GROUP), F32), pltpu.VMEM((HALO + tb, D_GROUP), F32),
                        pltpu.VMEM((ext, D_GROUP), F32), pltpu.VMEM((ext, D_GROUP), F32),
                        pltpu.VMEM((tb, D_GROUP), F32), pltpu.VMEM((tb, D_GROUP), F32),
                        pltpu.VMEM((SUBLANES, tb + TAP_SLACK, D_GROUP), F32),
                        pltpu.VMEM((HALO, SUBLANES, D_GROUP), F32), pltpu.VMEM((D_IN, D_MODEL), F32),
                        pltpu.VMEM((tb, D_IN), BF16), pltpu.SemaphoreType.DMA((1,))],
        compiler_params=_params(("arbitrary",), 56))


def _in_proj_bwd(x, dz, dres, g, wint, tb, ride=None):
    s = x.shape[0]

    def body(x_ref, dz_ref, dr_ref, g_ref, w_ref, dx_ref, dg_ref):
        @pl.when(pl.program_id(0) == 0)
        def _():
            dg_ref[...] = jnp.zeros_like(dg_ref)

        xv = x_ref[...]
        r = _rms(xv)
        xh = xv * r
        dh = _mm(dz_ref[...], w_ref[...])
        dg_ref[...] += jnp.sum(dh * xh, axis=0, keepdims=True)
        dx_ref[...] = dr_ref[...] + _rms_bwd(dh * g_ref[...], xh, r)

    return _call(
        body, (x, dz, dres, g, wint), ride, name="in_proj_bwd", grid=(s // tb,),
        in_specs=[_rows(tb, D_MODEL), _rows(tb, D_IN), _rows(tb, D_MODEL), _whole((1, D_MODEL)), _resident((D_IN, D_MODEL))],
        out_specs=[_rows(tb, D_MODEL), _whole((1, D_MODEL))],
        out_shape=[_sds((s, D_MODEL)), _sds((1, D_MODEL))],
        compiler_params=_params(("arbitrary",), 56))


def _adamw(w, g, m, v):
    m = ADAM_B1 * m + (1.0 - ADAM_B1) * g
    v = ADAM_B2 * v + (1.0 - ADAM_B2) * (g * g)
    m_hat = m / (1.0 - ADAM_B1 ** ADAM_STEP)
    v_hat = v / (1.0 - ADAM_B2 ** ADAM_STEP)
    delta = -ADAM_LR * (m_hat / (jnp.sqrt(v_hat) + ADAM_EPS) + ADAM_WD * w)
    return delta, m, v


def _reduce_adamw(parts, w, m, v, tb, name):
    rows, cols = w.shape

    def body(p_ref, w_ref, m_ref, v_ref, g_ref, d_ref, m2_ref, v2_ref):
        g = p_ref[0].astype(F32)
        for j in range(1, N_DEV):
            g = g + p_ref[j].astype(F32)
        g_ref[...] = g
        d_ref[...], m2_ref[...], v2_ref[...] = _adamw(w_ref[...], g, m_ref[...], v_ref[...])

    blk = _rows(tb, cols)
    return pl.pallas_call(
        body, name=name, grid=(rows // tb,),
        in_specs=[pl.BlockSpec((N_DEV, tb, cols), lambda i: (0, i, 0)), blk, blk, blk],
        out_specs=[blk] * 4, out_shape=[_sds((rows, cols))] * 4,
        compiler_params=_params(("parallel",), 32),
    )(parts, w, m, v)


def _reduce_adamw_layer(parts, w, m, v, layer, taken_over, tb, name):
    _, rows, cols = w.shape
    n_given = 4 if taken_over is None else 8

    def body(*refs):
        p_ref, w_ref, m_ref, v_ref = refs[:4]
        g_ref, d_ref, m2_ref, v2_ref = refs[n_given:]
        g = p_ref[0].astype(F32)
        for j in range(1, N_DEV):
            g = g + p_ref[j].astype(F32)
        g_ref[...] = g
        d_ref[...], m2_ref[...], v2_ref[...] = _adamw(w_ref[...], g, m_ref[...], v_ref[...])

    blk = pl.BlockSpec((None, tb, cols), lambda i: (layer, i, 0))
    return pl.pallas_call(
        body, name=name, grid=(rows // tb,),
        in_specs=[pl.BlockSpec((N_DEV, tb, cols), lambda i: (0, i, 0)), blk, blk, blk] + [ANY_SPEC] * (n_given - 4),
        out_specs=[blk] * 4, out_shape=[_sds(w.shape)] * 4,
        input_output_aliases={} if taken_over is None else {4 + k: k for k in range(4)},
        compiler_params=_params(("arbitrary",), 32),
    )(parts, w, m, v, *(taken_over or ()))


LANES = 128
PACK_ALIGN = 8 * LANES


def _pack(arrays):
    pieces = []
    for a in arrays:
        flat = a.reshape(-1)
        pieces.append(jnp.pad(flat, (0, -flat.shape[0] % PACK_ALIGN)).reshape(-1, LANES))
    return jnp.concatenate(pieces, axis=0)


def _unpack(packed, shapes):
    out, row = [], 0
    for shape in shapes:
        size = 1
        for dim in shape:
            size *= dim
        rows = -(-size // PACK_ALIGN) * 8
        out.append(packed[row:row + rows].reshape(-1)[:size].reshape(shape))
        row += rows
    return out


TB_PROJ = 512
TB_WIDE = 1024
TB_MIX_FWD = 1024
TB_MIX = 256
TB_TN = 1024
TQ = 256
TK = 256
ADAM_PARTS_BLOCK_BYTES = 6 * MIB
BF16_SUBLANES = 16


def _adam_rows(rows, cols):
    best = BF16_SUBLANES
    for tb in range(BF16_SUBLANES, rows + 1, BF16_SUBLANES):
        if rows % tb == 0 and N_DEV * tb * cols * 4 <= ADAM_PARTS_BLOCK_BYTES:
            best = tb
    return best


def kernel(x, norm_mix_g, w_in, gmlp_v_g, gmlp_w_s, gmlp_b_s, short_conv_w, conf_conv_w, conf_ln_g, conf_ln_b, mix_out_g, w_out, norm_ffn_g, w_up, w_down, final_norm_g, loss_target, m_norm_mix_g, m_w_in, m_gmlp_v_g, m_gmlp_w_s, m_gmlp_b_s, m_short_conv_w, m_conf_conv_w, m_conf_ln_g, m_conf_ln_b, m_mix_out_g, m_w_out, m_norm_ffn_g, m_w_up, m_w_down, m_final_norm_g, v_norm_mix_g, v_w_in, v_gmlp_v_g, v_gmlp_w_s, v_gmlp_b_s, v_short_conv_w, v_conf_conv_w, v_conf_ln_g, v_conf_ln_b, v_mix_out_g, v_w_out, v_norm_ffn_g, v_w_up, v_w_down, v_final_norm_g):
    me = 4 * lax.axis_index("x") + 2 * lax.axis_index("y") + lax.axis_index("c")
    x0, target = x[0], loss_target[0]
    s = x0.shape[0]
    tb_proj, tb_mix, tb_tn, tb_wide = min(TB_PROJ, s), min(TB_MIX, s), min(TB_TN, s), min(TB_WIDE, s)
    conv_cols = D_GROUP // N_DEV

    def pad_rows(a, rows):
        return jnp.pad(a, ((0, rows - a.shape[0]), (0, 0)))

    wint_loc = [w_in[l].T.astype(BF16) for l in range(N_LAYERS)]
    wout_loc = [w_out[l].astype(BF16) for l in range(N_LAYERS)]
    wup_loc = [w_up[l].astype(BF16) for l in range(N_LAYERS)]
    wdn_loc = [w_down[l].astype(BF16) for l in range(N_LAYERS)]
    conv_loc = jnp.concatenate([pad_rows(short_conv_w[l], 8) for l in range(N_LAYERS)]
                               + [pad_rows(conf_conv_w[l], HALO) for l in range(N_LAYERS)], axis=0)
    wint, wout, wup, wdn = [None] * N_LAYERS, [None] * N_LAYERS, [None] * N_LAYERS, [None] * N_LAYERS
    wint0, conv_all = _exchange([wint_loc[0], conv_loc], [GATHER, GATHER], "gather_first_weights")
    wint[0] = wint0.reshape(D_IN, D_MODEL)
    conv_full = conv_all.transpose(1, 0, 2).reshape(-1, D_GROUP)
    scw = [conv_full[8 * l:8 * (l + 1)] for l in range(N_LAYERS)]
    ccw = [conv_full[8 * N_LAYERS + HALO * l:8 * N_LAYERS + HALO * (l + 1)] for l in range(N_LAYERS)]

    tril = jnp.tril(jnp.ones((CHUNK, CHUNK), dtype=bool))
    wm = [jnp.where(tril, gmlp_w_s[l], 0.0).astype(BF16) for l in range(N_LAYERS)]
    wmt = [w.transpose(0, 2, 1) for w in wm]
    bexp = [jnp.repeat(gmlp_b_s[l].T, HEAD_DIM, axis=1) for l in range(N_LAYERS)]

    def row(vec):
        return vec.reshape(1, -1)

    saved = []
    xc = x0
    for l in range(N_LAYERS):
        first = l == 0
        (z, qkv, hb_in), moved = _in_proj_fwd(xc, row(norm_mix_g[l]), wint[l], tb_wide,
                                              ride=([wout_loc[0]], [GATHER]) if first else None)
        if first:
            wout[0] = moved[0].reshape(D_MODEL, D_MODEL)
        (ya, yb, yd, conv), moved = _mix_fwd(z, row(gmlp_v_g[l]), wm[l], bexp[l], scw[l], ccw[l], row(conf_ln_g[l]),
                                             row(conf_ln_b[l]), min(TB_MIX_FWD, s),
                                             ride=([wup_loc[0]], [GATHER]) if first else None)
        if first:
            wup[0] = moved[0]
        (yc,), moved = _attn_fwd(qkv, TQ, TK, ride=([wdn_loc[0], wint_loc[1]], [GATHER, GATHER]) if first else None)
        if first:
            wdn[0], wint[1] = moved[0], moved[1].reshape(D_IN, D_MODEL)
        ys = (ya, yb, yc, yd)
        x1 = _out_proj_fwd(ys, xc, row(mix_out_g[l]), wout[l], tb_wide)
        (x2, act), moved = _ffn_fwd(x1, row(norm_ffn_g[l]), wup[l], wdn[l], tb_proj,
                                    ride=([wout_loc[1], wup_loc[1], wdn_loc[1]], [GATHER] * 3) if first else None)
        saved.append((xc, z, qkv, ys, x1, act, hb_in, conv))
        xc = x2
        if first:
            wout[1], wup[1], wdn[1] = moved[0].reshape(D_MODEL, D_MODEL), moved[1], moved[2]
    dx, g_final, loss_part = _loss_head(xc, row(final_norm_g), target, tb_wide)

    parts = [None] * (4 * N_LAYERS)
    small_grads = [None] * N_LAYERS
    early_names = ["gmlp_v_g", "gmlp_w_s", "gmlp_b_s", "short_conv_w", "conf_conv_w", "conf_ln_g", "conf_ln_b",
                   "mix_out_g", "norm_ffn_g"]
    for l in reversed(range(N_LAYERS)):
        xin, z, qkv, ys, x1, act, hb_in, conv = saved[l]
        (dx1, hb_ffn, dpre, g_ffn), _ = _ffn_bwd(x1, act, dx, row(norm_ffn_g[l]), wup[l], wdn[l], tb_proj)
        grad_up = _tn_slabs(hb_ffn, dpre, tb_tn, False, "grad_w_up")
        grad_dn = _tn_slabs(dx, act, tb_proj, True, "grad_w_down", square_b=True)
        dya, dyb_mix, dyc, dyd, g_mixout, grad_out = _out_proj_bwd(dx1, ys, row(mix_out_g[l]), wout[l], tb_wide)
        grad_out = grad_out.reshape(N_DEV, D_MODEL // N_DEV, D_MODEL)
        (dq, dk, dv), moved = _attn_bwd(qkv, dyc, ys[2], TQ, TK, ride=([grad_up, grad_dn], [SCATTER] * 2))
        parts[4 * l + 2], parts[4 * l + 3] = moved
        (dz, g_vg, g_ws, g_bs, g_scw, g_ccw, g_lng, g_lnb, grad_in), moved = _mix_bwd(
            z, conv, dya, dyb_mix, dyd, dq, dk, dv, hb_in, row(gmlp_v_g[l]), wm[l], wmt[l], bexp[l], scw[l], ccw[l],
            row(conf_ln_g[l]), row(conf_ln_b[l]), tb_mix, ride=([grad_out], [SCATTER]))
        parts[4 * l + 1] = moved[0]
        small_grads[l] = dict(gmlp_v_g=g_vg[0], gmlp_w_s=g_ws, gmlp_b_s=g_bs[:, :4].T, short_conv_w=g_scw[:K_SHORT],
                              conf_conv_w=g_ccw[:K_CONF], conf_ln_g=g_lng[0], conf_ln_b=g_lnb[0],
                              mix_out_g=g_mixout[0], norm_ffn_g=g_ffn[0])
        riders, modes = [grad_in.reshape(N_DEV, D_IN // N_DEV, D_MODEL)], [SCATTER]
        if l == 0:
            early_list = [jnp.stack([small_grads[k][n] for k in range(N_LAYERS)]) for n in early_names] + [g_final[0]]
            riders, modes = [riders[0].astype(BF16), _pack(early_list).astype(BF16)], modes + [GATHER]
        (dx, g_mix), moved = _in_proj_bwd(xin, dz, dx1, row(norm_mix_g[l]), wint[l], tb_wide, ride=(riders, modes))
        parts[4 * l] = moved[0]
        small_grads[l]["norm_mix_g"] = g_mix[0]

    late_list = [jnp.stack([small_grads[l]["norm_mix_g"] for l in range(N_LAYERS)]), loss_part[0, :1]]
    late_parts = _exchange([_pack(late_list)], [GATHER], "gather_last_grad")[0]
    small_groups = [(early_names + ["final_norm_g"], early_list, moved[1]),
                    (["norm_mix_g", "loss"], late_list, late_parts)]

    given = dict(norm_mix_g=(norm_mix_g, m_norm_mix_g, v_norm_mix_g), gmlp_v_g=(gmlp_v_g, m_gmlp_v_g, v_gmlp_v_g),
                 gmlp_w_s=(gmlp_w_s, m_gmlp_w_s, v_gmlp_w_s), gmlp_b_s=(gmlp_b_s, m_gmlp_b_s, v_gmlp_b_s),
                 short_conv_w=(short_conv_w, m_short_conv_w, v_short_conv_w),
                 conf_conv_w=(conf_conv_w, m_conf_conv_w, v_conf_conv_w),
                 conf_ln_g=(conf_ln_g, m_conf_ln_g, v_conf_ln_g), conf_ln_b=(conf_ln_b, m_conf_ln_b, v_conf_ln_b),
                 mix_out_g=(mix_out_g, m_mix_out_g, v_mix_out_g), norm_ffn_g=(norm_ffn_g, m_norm_ffn_g, v_norm_ffn_g),
                 final_norm_g=(final_norm_g, m_final_norm_g, v_final_norm_g), loss=(jnp.zeros((1,), F32),) * 3)
    sharded_small = ("short_conv_w", "conf_conv_w")

    def widen(a):
        full = jnp.zeros(a.shape[:-1] + (D_GROUP,), a.dtype)
        return lax.dynamic_update_slice(full, a, (0, 0, me * conv_cols))

    small_res = {}
    for names, grads, gathered in small_groups:
        state = [_pack([widen(given[n][k]) if n in sharded_small else given[n][k] for n in names]) for k in range(3)]
        outs = _reduce_adamw(gathered, *state, state[0].shape[0], "adamw_small")
        for kind, packed in zip(("grad", "delta", "new_m", "new_v"), outs):
            for n, val in zip(names, _unpack(packed, [a.shape for a in grads])):
                if n in sharded_small:
                    val = lax.dynamic_slice(val, (0, 0, me * conv_cols), val.shape[:-1] + (conv_cols,))
                small_res[kind, n] = val

    big_names = ["w_in", "w_out", "w_up", "w_down"]
    big_given = dict(w_in=[t.transpose(0, 2, 1) for t in (w_in, m_w_in, v_w_in)], w_out=(w_out, m_w_out, v_w_out),
                     w_up=(w_up, m_w_up, v_w_up), w_down=(w_down, m_w_down, v_w_down))
    big_res = {}
    for j, n in enumerate(big_names):
        outs = None
        for l in range(N_LAYERS):
            outs = _reduce_adamw_layer(parts[4 * l + j], *big_given[n], l, outs,
                                       _adam_rows(*big_given[n][0].shape[1:]), "adamw_" + n)
        for kind, out in zip(("grad", "delta", "new_m", "new_v"), outs):
            big_res[kind, n] = out.transpose(0, 2, 1) if n == "w_in" else out

    order = ["norm_mix_g", "w_in", "gmlp_v_g", "gmlp_w_s", "gmlp_b_s", "short_conv_w", "conf_conv_w", "conf_ln_g",
             "conf_ln_b", "mix_out_g", "w_out", "norm_ffn_g", "w_up", "w_down", "final_norm_g"]
    result = [small_res["grad", "loss"][0], dx.reshape(x.shape)]
    for kind in ("grad", "delta", "new_m", "new_v"):
        for n in order:
            result.append(big_res[kind, n] if n in big_given else small_res[kind, n])
    return tuple(result)
```

```python
import jax
import jax.numpy as jnp
from jax import lax
from jax.experimental import pallas as pl
from jax.experimental.pallas import tpu as pltpu

F32 = jnp.float32
BF16 = jnp.bfloat16

D_MODEL = 1024
D_GROUP = 256
D_IN = 10 * D_GROUP
D_FF = 4 * D_MODEL
N_DEV = 8
N_LAYERS = 2
HEAD_DIM = 64
HEADS_PER_PAIR = 2
PAIR = HEADS_PER_PAIR * HEAD_DIM
CHUNK = 128
K_SHORT = 3
K_CONF = 31
HALO = 32
EPS = 1e-6
ATT_SCALE = HEAD_DIM ** -0.5
LOG_CUT = -104.0
MIB = 2 ** 20

ADAM_LR = 0.001
ADAM_B1 = 0.9
ADAM_B2 = 0.999
ADAM_EPS = 1e-08
ADAM_WD = 0.01
ADAM_STEP = 10

GELU_C = 0.7978845608028654
GELU_A = 0.044715


def _mm(a, b):
    return jnp.dot(a, b, preferred_element_type=F32)


def _mm_nt(a, b):
    return lax.dot_general(a, b, (((1,), (1,)), ((), ())), preferred_element_type=F32)


def _mm_tn(a, b):
    return lax.dot_general(a, b, (((0,), (0,)), ((), ())), preferred_element_type=F32)


def _rms(x):
    return lax.rsqrt(jnp.mean(x * x, axis=-1, keepdims=True) + EPS)


def _rms_bwd(dy, xh, r):
    return r * (dy - xh * jnp.mean(dy * xh, axis=-1, keepdims=True))


def _sigmoid(x):
    return 1.0 / (1.0 + jnp.exp(-x))


def _whole(shape):
    return pl.BlockSpec(shape, lambda *_: (0,) * len(shape))


def _resident(shape):
    return pl.BlockSpec(shape, lambda *_: (0,) * len(shape), pipeline_mode=pl.Buffered(1))


def _rows(tb, width, col=0):
    return pl.BlockSpec((tb, width), lambda i: (i, col))


def _params(semantics, vmem_mib):
    return pltpu.CompilerParams(dimension_semantics=semantics, vmem_limit_bytes=vmem_mib * MIB)


def _sds(shape, dtype=F32):
    return jax.ShapeDtypeStruct(shape, dtype)


def _split_bf16(v):
    hi = v.astype(BF16)
    lo = (v - hi.astype(F32)).astype(BF16)
    return hi, lo


GATHER, SCATTER = "gather", "scatter"
ANY_SPEC = pl.BlockSpec(memory_space=pl.ANY)


def _exchange_copies(ins, outs, modes, send_sems, recv_sems, local_sems, with_arrivals=True):
    x, y, c = lax.axis_index("x"), lax.axis_index("y"), lax.axis_index("c")
    me = 4 * x + 2 * y + c
    local, sends, arrivals = [], [], []
    for a, mode in enumerate(modes):
        local.append(pltpu.make_async_copy(ins[a].at[me] if mode == SCATTER else ins[a], outs[a].at[me], local_sems.at[a]))
    for k in range(N_DEV - 1):
        flip = k + 1
        peer = (1 - x if flip & 4 else x, 1 - y if flip & 2 else y, 1 - c if flip & 1 else c)
        pf = 4 * peer[0] + 2 * peer[1] + peer[2]
        for a, mode in enumerate(modes):
            src = ins[a].at[pf] if mode == SCATTER else ins[a]
            for dst, group in ((outs[a].at[me], sends), (outs[a].at[pf], arrivals)):
                if group is sends or with_arrivals:
                    group.append(pltpu.make_async_remote_copy(
                        src_ref=src, dst_ref=dst, send_sem=send_sems.at[a, k], recv_sem=recv_sems.at[a, k],
                        device_id=peer, device_id_type=pl.DeviceIdType.MESH))
    return local, sends, arrivals


def _exchange_start(*refs_and_modes):
    local, sends, _ = _exchange_copies(*refs_and_modes, with_arrivals=False)
    for cp in local + sends:
        cp.start()


def _exchange_wait(*refs_and_modes):
    local, sends, arrivals = _exchange_copies(*refs_and_modes)
    for cp in sends:
        cp.wait_send()
    for cp in arrivals:
        cp.wait_recv()
    for cp in local:
        cp.wait()


def _exchange_shapes(arrays, modes):
    out_shape = [_sds(a.shape if mode == SCATTER else (N_DEV,) + a.shape, a.dtype) for a, mode in zip(arrays, modes)]
    n = len(arrays)
    sems = [pltpu.SemaphoreType.DMA((n, N_DEV - 1)), pltpu.SemaphoreType.DMA((n, N_DEV - 1)), pltpu.SemaphoreType.DMA((n,))]
    return out_shape, sems


def _exchange(arrays, modes, name):
    n = len(arrays)
    out_shape, sems = _exchange_shapes(arrays, modes)

    def body(*refs):
        _exchange_start(refs[:n], refs[n:2 * n], modes, *refs[2 * n:])
        _exchange_wait(refs[:n], refs[n:2 * n], modes, *refs[2 * n:])

    return pl.pallas_call(body, name=name, out_shape=out_shape, in_specs=[ANY_SPEC] * n, out_specs=[ANY_SPEC] * n,
                          scratch_shapes=sems)(*arrays)


def _call(body, args, ride, *, name, grid, in_specs, out_specs, out_shape, scratch_shapes=(), compiler_params):
    if ride is None:
        outs = pl.pallas_call(body, name=name, grid=grid, in_specs=in_specs, out_specs=out_specs, out_shape=out_shape,
                              scratch_shapes=scratch_shapes, compiler_params=compiler_params)(*args)
        return outs, []
    arrays, modes = ride
    n, n_in, n_out, n_scratch = len(arrays), len(in_specs), len(out_specs), len(scratch_shapes)
    moved_shape, sems = _exchange_shapes(arrays, modes)
    n_steps = 1
    for g in grid:
        n_steps *= g

    def riding(*refs):
        ins, refs = refs[:n_in], refs[n_in:]
        r_ins, refs = refs[:n], refs[n:]
        outs, refs = refs[:n_out], refs[n_out:]
        r_outs, refs = refs[:n], refs[n:]
        scratch, r_sems = refs[:n_scratch], refs[n_scratch:]
        step = pl.program_id(0)
        for axis in range(1, len(grid)):
            step = step * grid[axis] + pl.program_id(axis)

        @pl.when(step == 0)
        def _():
            _exchange_start(r_ins, r_outs, modes, *r_sems)

        body(*ins, *outs, *scratch)

        @pl.when(step == n_steps - 1)
        def _():
            _exchange_wait(r_ins, r_outs, modes, *r_sems)

    outs = pl.pallas_call(
        riding, name=name, grid=grid, in_specs=list(in_specs) + [ANY_SPEC] * n,
        out_specs=list(out_specs) + [ANY_SPEC] * n, out_shape=list(out_shape) + moved_shape,
        scratch_shapes=list(scratch_shapes) + sems, compiler_params=compiler_params)(*args, *arrays)
    return outs[:n_out], outs[n_out:]


def _in_proj_fwd(x, g, wint, tb, ride=None):
    s = x.shape[0]

    def body(x_ref, g_ref, w_ref, z_ref, qkv_ref, hb_ref):
        xv = x_ref[...]
        h = (xv * _rms(xv) * g_ref[...]).astype(BF16)
        hb_ref[...] = h
        z = _mm_nt(h, w_ref[...])
        z_ref[...] = z
        qkv_ref[:, 0:D_GROUP] = (z[:, 5 * D_GROUP:6 * D_GROUP] * ATT_SCALE).astype(BF16)
        qkv_ref[:, D_GROUP:3 * D_GROUP] = z[:, 6 * D_GROUP:8 * D_GROUP].astype(BF16)

    return _call(
        body, (x, g, wint), ride, name="in_proj_fwd", grid=(s // tb,),
        in_specs=[_rows(tb, D_MODEL), _whole((1, D_MODEL)), _resident((D_IN, D_MODEL))],
        out_specs=[_rows(tb, D_IN), _rows(tb, 3 * D_GROUP), _rows(tb, D_MODEL)],
        out_shape=[_sds((s, D_IN)), _sds((s, 3 * D_GROUP), BF16), _sds((s, D_MODEL), BF16)],
        compiler_params=_params(("arbitrary",), 58))


def _gelu(x):
    return 0.5 * x * (1.0 + jnp.tanh(GELU_C * (x + GELU_A * x * x * x)))


def _gelu_grad(x):
    t = jnp.tanh(GELU_C * (x + GELU_A * x * x * x))
    return 0.5 * (1.0 + t) + 0.5 * x * (1.0 - t * t) * GELU_C * (1.0 + 3.0 * GELU_A * x * x)


def _head_lane(width):
    return lax.broadcasted_iota(jnp.int32, (1, width), 1) // HEAD_DIM


def _gating_chunk(wm_ref, bexp_ref, vc, lane_h):
    f = bexp_ref[...]
    for h in range(D_GROUP // HEAD_DIM):
        f = f + _mm(wm_ref[h], jnp.where(lane_h == h, vc, 0))
    return f


def _halo_specs(s, tb, width_blocks):
    per = tb // HALO
    prev = pl.BlockSpec((HALO, width_blocks), lambda i: (jnp.maximum(i * per - 1, 0), 0))
    nxt = pl.BlockSpec((HALO, width_blocks), lambda i: (jnp.minimum((i + 1) * per, s // HALO - 1), 0))
    return prev, nxt


SUBLANES = 8
TAP_SLACK = 24


ROW_CHUNK = 32


def _shifted_copies(buf_ref, copies_ref, first, n_taps, rows):
    where = {}
    for residue in range(SUBLANES):
        taps = [j for j in range(n_taps) if (first + j) % SUBLANES == residue]
        if not taps:
            continue
        lo = first + taps[0]
        span = first + taps[-1] - lo + rows
        copies_ref[residue, 0:span, :] = buf_ref[pl.ds(lo, span), :]
        for j in taps:
            where[j] = (residue, first + j - lo)
    return where


def _conv_chunks(copies_ref, where, rows, weight, out_ref):
    for c0 in range(0, rows, ROW_CHUNK):
        acc = jnp.zeros((ROW_CHUNK, D_GROUP), F32)
        for j, (residue, offset) in sorted(where.items()):
            acc = acc + weight(j) * copies_ref[residue, pl.ds(offset + c0, ROW_CHUNK), :]
        out_ref[pl.ds(c0, ROW_CHUNK), :] = acc


def _corr_chunks(copies_ref, where, rows, dc_ref, sums_ref):
    for c0 in range(0, rows, ROW_CHUNK):
        dc = dc_ref[pl.ds(c0, ROW_CHUNK), :]
        for j, (residue, offset) in sorted(where.items()):
            prod = dc * copies_ref[residue, pl.ds(offset + c0, ROW_CHUNK), :]
            part = prod[0:SUBLANES]
            for r0 in range(SUBLANES, ROW_CHUNK, SUBLANES):
                part = part + prod[r0:r0 + SUBLANES]
            sums_ref[j] += part


def _mix_fwd(z, vg, wm, bexp, scw, ccw, lng, lnb, tb, ride=None):
    s = z.shape[0]
    prev_spec, _ = _halo_specs(s, tb, D_IN)

    def body(z_ref, zp_ref, vg_ref, wm_ref, bexp_ref, scw_ref, ccw_ref, lng_ref, lnb_ref,
             ya_ref, yb_ref, yd_ref, c_ref, pbuf, hbuf, copies):
        keep = (pl.program_id(0) > 0).astype(F32)
        lane_h = _head_lane(D_GROUP)
        ga = _gelu(z_ref[:, 0:2 * D_GROUP])
        u, v = ga[:, :D_GROUP], ga[:, D_GROUP:]
        vn = (v * _rms(v) * vg_ref[...]).astype(BF16)
        for n in range(tb // CHUNK):
            rows = slice(n * CHUNK, (n + 1) * CHUNK)
            ya_ref[rows, :] = u[rows] * _gating_chunk(wm_ref, bexp_ref, vn[rows], lane_h)
        p = z_ref[:, 3 * D_GROUP:4 * D_GROUP] * z_ref[:, 4 * D_GROUP:5 * D_GROUP]
        pbuf[0:HALO, :] = zp_ref[:, 3 * D_GROUP:4 * D_GROUP] * zp_ref[:, 4 * D_GROUP:5 * D_GROUP] * keep
        pbuf[HALO:HALO + tb, :] = p
        cv = scw_ref[K_SHORT - 1:K_SHORT, :] * p
        for k in range(K_SHORT - 1):
            cv = cv + scw_ref[k:k + 1, :] * pbuf[pl.ds(HALO - (K_SHORT - 1) + k, tb), :]
        yb_ref[...] = z_ref[:, 2 * D_GROUP:3 * D_GROUP] * cv
        hbuf[0:HALO, :] = zp_ref[:, 8 * D_GROUP:9 * D_GROUP] * _sigmoid(zp_ref[:, 9 * D_GROUP:10 * D_GROUP]) * keep
        hbuf[HALO:HALO + tb, :] = z_ref[:, 8 * D_GROUP:9 * D_GROUP] * _sigmoid(z_ref[:, 9 * D_GROUP:10 * D_GROUP])
        where = _shifted_copies(hbuf, copies, HALO - (K_CONF - 1), K_CONF, tb)
        _conv_chunks(copies, where, tb, lambda k: ccw_ref[k:k + 1, :], c_ref)
        c = c_ref[...]
        xc = c - jnp.mean(c, axis=-1, keepdims=True)
        ln = xc * lax.rsqrt(jnp.mean(xc * xc, axis=-1, keepdims=True) + EPS) * lng_ref[...] + lnb_ref[...]
        yd_ref[...] = ln * _sigmoid(ln)

    grp = _rows(tb, D_GROUP)
    return _call(
        body, (z, z, vg, wm, bexp, scw, ccw, lng, lnb), ride, name="mix_fwd", grid=(s // tb,),
        in_specs=[_rows(tb, D_IN), prev_spec, _whole((1, D_GROUP)), _whole(wm.shape), _whole(bexp.shape),
                  _whole(scw.shape), _whole(ccw.shape), _whole((1, D_GROUP)), _whole((1, D_GROUP))],
        out_specs=[grp, grp, grp, grp],
        out_shape=[_sds((s, D_GROUP))] * 4,
        scratch_shapes=[pltpu.VMEM((HALO + tb, D_GROUP), F32), pltpu.VMEM((HALO + tb, D_GROUP), F32),
                        pltpu.VMEM((SUBLANES, tb + TAP_SLACK, D_GROUP), F32)],
        compiler_params=_params(("arbitrary",), 52))


def _stick_tile(qh, kt, causal, c, upper):
    x = _mm_nt(qh, kt)
    soft = jnp.log(1.0 + jnp.exp(-jnp.abs(x)))
    lb = jnp.minimum(x, 0.0) - soft
    lom = jnp.where(causal, -jnp.maximum(x, 0.0) - soft, 0.0)
    hi, lo = _split_bf16(lom)
    stick = c + _mm(hi, upper[...]) + _mm(lo, upper[...])
    w = jnp.where(causal, jnp.exp(lb + stick), 0.0)
    return w, lb, lom


def _triangle(n, diagonal):
    return jnp.tri(n, n, diagonal, dtype=BF16)


def _causal_tile(qi, tq, k0, tk):
    qpos = qi * tq + lax.broadcasted_iota(jnp.int32, (tq, 1), 0)
    return k0 + lax.broadcasted_iota(jnp.int32, (1, tk), 1) < qpos


def _sticks_alive(cs):
    longest = cs[0]
    for c in cs[1:]:
        longest = jnp.maximum(longest, c)
    return (jnp.max(longest) > LOG_CUT).astype(jnp.int32)


def _walk(body, qi, tq, tk, init):
    start = (((qi + 1) * tq - 1) // tk, jnp.int32(1)) + tuple(init)
    return lax.while_loop(lambda cr: jnp.logical_and(cr[0] >= 0, cr[1] > 0), body, start)[2:]


def _attn_fwd(qkv, tq, tk, ride=None):
    s = qkv.shape[0]
    n_heads = D_GROUP // HEAD_DIM

    def body(q_ref, k_ref, v_ref, upper, o_ref):
        qi = pl.program_id(0)
        q = q_ref[...]
        lane_h = _head_lane(D_GROUP)
        qhs = [jnp.where(lane_h == h, q, 0) for h in range(n_heads)]

        def step(carry):
            kb, _, acc = carry[:3]
            cs = list(carry[3:])
            k0 = pl.multiple_of(kb * tk, tk)
            kt = k_ref[pl.ds(k0, tk), :]
            vt = v_ref[pl.ds(k0, tk), :]
            causal = _causal_tile(qi, tq, k0, tk)
            for h in range(n_heads):
                w, _, lom = _stick_tile(qhs[h], kt, causal, cs[h], upper)
                acc = acc + _mm(w.astype(BF16), jnp.where(lane_h == h, vt, 0))
                cs[h] = cs[h] + jnp.sum(lom, axis=1, keepdims=True)
            return (kb - 1, _sticks_alive(cs), acc) + tuple(cs)

        init = [jnp.zeros((tq, D_GROUP), F32)] + [jnp.zeros((tq, 1), F32)] * n_heads
        o_ref[...] = _walk(step, qi, tq, tk, init)[0]

    return _call(
        body, (qkv, qkv, qkv, _triangle(tk, -1)), ride, name="attn_fwd", grid=(s // tq,),
        in_specs=[pl.BlockSpec((tq, D_GROUP), lambda qi: (qi, 0)),
                  pl.BlockSpec((s, D_GROUP), lambda qi: (0, 1), pipeline_mode=pl.Buffered(1)),
                  pl.BlockSpec((s, D_GROUP), lambda qi: (0, 2), pipeline_mode=pl.Buffered(1)),
                  _resident((tk, tk))],
        out_specs=[pl.BlockSpec((tq, D_GROUP), lambda qi: (qi, 0))],
        out_shape=[_sds((s, D_GROUP))],
        compiler_params=_params(("arbitrary",), 40))


def _out_proj_fwd(ys, x, mg, wout, tb):
    s = x.shape[0]

    def body(ya_ref, yb_ref, yc_ref, yd_ref, x_ref, mg_ref, w_ref, o_ref):
        acc = x_ref[...]
        for gi, y_ref in enumerate((ya_ref, yb_ref, yc_ref, yd_ref)):
            cols = slice(gi * D_GROUP, (gi + 1) * D_GROUP)
            y = y_ref[...]
            acc = acc + _mm((y * _rms(y) * mg_ref[:, cols]).astype(BF16), w_ref[cols, :])
        o_ref[...] = acc

    grp = _rows(tb, D_GROUP)
    return pl.pallas_call(
        body, name="out_proj_fwd", grid=(s // tb,),
        in_specs=[grp, grp, grp, grp, _rows(tb, D_MODEL), _whole((1, D_MODEL)), _whole((D_MODEL, D_MODEL))],
        out_specs=_rows(tb, D_MODEL), out_shape=_sds((s, D_MODEL)),
        compiler_params=_params(("parallel",), 44),
    )(*ys, x, mg, wout)


def _ffn_fwd(x, g, wup, wdn, tb, ride=None):
    s = x.shape[0]
    ff = D_FF // N_DEV

    def body(x_ref, g_ref, wu_ref, wd_ref, o_ref, a_ref):
        xv = x_ref[...]
        h = (xv * _rms(xv) * g_ref[...]).astype(BF16)
        acc = xv
        for d in range(N_DEV):
            a = jnp.maximum(_mm(h, wu_ref[d]), 0.0)
            a_ref[:, d * ff:(d + 1) * ff] = a.astype(BF16)
            acc = acc + _mm((a * a).astype(BF16), wd_ref[d])
        o_ref[...] = acc

    return _call(
        body, (x, g, wup, wdn), ride, name="ffn_fwd", grid=(s // tb,),
        in_specs=[_rows(tb, D_MODEL), _whole((1, D_MODEL)), _resident((N_DEV, D_MODEL, ff)), _resident((N_DEV, ff, D_MODEL))],
        out_specs=[_rows(tb, D_MODEL), _rows(tb, D_FF)], out_shape=[_sds((s, D_MODEL)), _sds((s, D_FF), BF16)],
        compiler_params=_params(("arbitrary",), 56))


def _loss_head(x, g, tgt, tb):
    s = x.shape[0]

    def body(x_ref, g_ref, t_ref, dx_ref, dg_ref, loss_ref):
        @pl.when(pl.program_id(0) == 0)
        def _():
            dg_ref[...] = jnp.zeros_like(dg_ref)
            loss_ref[...] = jnp.zeros_like(loss_ref)

        xv = x_ref[...]
        r = _rms(xv)
        xh = xv * r
        err = xh * g_ref[...] - t_ref[...]
        loss_ref[...] += 0.5 * jnp.sum(jnp.mean(err * err, axis=-1, keepdims=True))
        dy = err * (1.0 / D_MODEL)
        dg_ref[...] += jnp.sum(dy * xh, axis=0, keepdims=True)
        dx_ref[...] = _rms_bwd(dy * g_ref[...], xh, r)

    return pl.pallas_call(
        body, name="loss_head", grid=(s // tb,),
        in_specs=[_rows(tb, D_MODEL), _whole((1, D_MODEL)), _rows(tb, D_MODEL)],
        out_specs=[_rows(tb, D_MODEL), _whole((1, D_MODEL)), _whole((8, 128))],
        out_shape=[_sds((s, D_MODEL)), _sds((1, D_MODEL)), _sds((8, 128))],
        compiler_params=_params(("arbitrary",), 44),
    )(x, g, tgt)


def _ffn_bwd(x1, act, dx2, g, wup, wdn, tb, ride=None):
    s = x1.shape[0]
    ff = D_FF // N_DEV

    def body(x_ref, a_ref, dy_ref, g_ref, wu_ref, wd_ref, dx_ref, hb_ref, dpre_ref, dg_ref):
        @pl.when(pl.program_id(0) == 0)
        def _():
            dg_ref[...] = jnp.zeros_like(dg_ref)

        xv = x_ref[...]
        r = _rms(xv)
        xh = xv * r
        hb_ref[...] = (xh * g_ref[...]).astype(BF16)
        dyv = dy_ref[...]
        dyb = dyv.astype(BF16)
        dh = jnp.zeros((tb, D_MODEL), F32)
        for d in range(N_DEV):
            cols = slice(d * ff, (d + 1) * ff)
            a = a_ref[:, cols].astype(F32)
            dpre = (_mm_nt(dyb, wd_ref[d]) * (2.0 * a)).astype(BF16)
            dpre_ref[:, cols] = dpre
            dh = dh + _mm_nt(dpre, wu_ref[d])
        dg_ref[...] += jnp.sum(dh * xh, axis=0, keepdims=True)
        dx_ref[...] = dyv + _rms_bwd(dh * g_ref[...], xh, r)

    return _call(
        body, (x1, act, dx2, g, wup, wdn), ride, name="ffn_bwd", grid=(s // tb,),
        in_specs=[_rows(tb, D_MODEL), _rows(tb, D_FF), _rows(tb, D_MODEL), _whole((1, D_MODEL)),
                  _resident((N_DEV, D_MODEL, ff)), _resident((N_DEV, ff, D_MODEL))],
        out_specs=[_rows(tb, D_MODEL), _rows(tb, D_MODEL), _rows(tb, D_FF), _whole((1, D_MODEL))],
        out_shape=[_sds((s, D_MODEL)), _sds((s, D_MODEL), BF16), _sds((s, D_FF), BF16), _sds((1, D_MODEL))],
        compiler_params=_params(("arbitrary",), 58))


def _tn_slabs(a, b, tb, transpose_slabs, name, square_b=False):
    s, m = a.shape
    width = b.shape[1] // N_DEV
    n_steps = s // tb
    slab = (width, m) if transpose_slabs else (m, width)

    def body(a_ref, b_ref, o_hbm, acc, stage, sem):
        step = pl.program_id(0)

        @pl.when(step == 0)
        def _():
            acc[...] = jnp.zeros_like(acc)

        bv = b_ref[...]
        if square_b:
            bv = bv.astype(F32)
            bv = bv * bv
        acc[...] += _mm_tn(a_ref[...].astype(BF16), bv.astype(BF16))

        @pl.when(step == n_steps - 1)
        def _():
            for d in range(N_DEV):
                cols = acc.at[:, pl.ds(d * width, width)]
                if transpose_slabs:
                    stage[...] = cols[...].T
                cp = pltpu.make_async_copy(stage if transpose_slabs else cols, o_hbm.at[d], sem.at[0])
                cp.start()
                cp.wait()

    return pl.pallas_call(
        body, name=name, grid=(n_steps,),
        in_specs=[_rows(tb, m), _rows(tb, b.shape[1])], out_specs=ANY_SPEC, out_shape=_sds((N_DEV,) + slab),
        scratch_shapes=[pltpu.VMEM((m, b.shape[1]), F32), pltpu.VMEM(slab, F32), pltpu.SemaphoreType.DMA((1,))],
        compiler_params=_params(("arbitrary",), 56),
    )(a, b)


def _out_proj_bwd(dx1, ys, mg, wout, tb):
    s = dx1.shape[0]

    def body(dx_ref, ya_ref, yb_ref, yc_ref, yd_ref, mg_ref, w_ref,
             dya_ref, dyb_ref, dyc_ref, dyd_ref, dmg_ref, dw_ref, yn_ref):
        @pl.when(pl.program_id(0) == 0)
        def _():
            dmg_ref[...] = jnp.zeros_like(dmg_ref)
            dw_ref[...] = jnp.zeros_like(dw_ref)

        dxb = dx_ref[...].astype(BF16)
        dyn = _mm_nt(dxb, w_ref[...])
        groups = ((ya_ref, dya_ref), (yb_ref, dyb_ref), (yc_ref, dyc_ref), (yd_ref, dyd_ref))
        for gi, (y_ref, dy_ref) in enumerate(groups):
            cols = slice(gi * D_GROUP, (gi + 1) * D_GROUP)
            y = y_ref[...]
            r = _rms(y)
            n = y * r
            gain = mg_ref[:, cols]
            dn = dyn[:, cols]
            yn_ref[:, cols] = (n * gain).astype(BF16)
            dmg_ref[:, cols] += jnp.sum(dn * n, axis=0, keepdims=True)
            dy_ref[...] = _rms_bwd(dn * gain, n, r)
        dw_ref[...] += _mm_tn(yn_ref[...], dxb)

    grp = _rows(tb, D_GROUP)
    return pl.pallas_call(
        body, name="out_proj_bwd", grid=(s // tb,),
        in_specs=[_rows(tb, D_MODEL), grp, grp, grp, grp, _whole((1, D_MODEL)), _resident((D_MODEL, D_MODEL))],
        out_specs=[grp, grp, grp, grp, _whole((1, D_MODEL)), _whole((D_MODEL, D_MODEL))],
        out_shape=[_sds((s, D_GROUP))] * 4 + [_sds((1, D_MODEL)), _sds((D_MODEL, D_MODEL))],
        scratch_shapes=[pltpu.VMEM((tb, D_MODEL), BF16)],
        compiler_params=_params(("arbitrary",), 52),
    )(dx1, *ys, mg, wout)


def _attn_bwd(qkv, do, o, tq, tk, ride=None):
    s = qkv.shape[0]
    nq = s // tq
    n_pairs = D_GROUP // PAIR

    def body(q_ref, k_ref, v_ref, do_ref, o_ref, upper, upper_eq, dq_ref, dk_hbm, dv_hbm, dk_acc, dv_acc, sems):
        hp, qi = pl.program_id(0), pl.program_id(1)

        @pl.when(qi == 0)
        def _():
            dk_acc[...] = jnp.zeros_like(dk_acc)
            dv_acc[...] = jnp.zeros_like(dv_acc)

        q = q_ref[...]
        dob = do_ref[...].astype(BF16)
        prod = dob.astype(F32) * o_ref[...]
        lane_h = _head_lane(PAIR)
        heads = []
        for h in range(HEADS_PER_PAIR):
            in_head = lane_h == h
            total = jnp.sum(jnp.where(in_head, prod, 0.0), axis=1, keepdims=True)
            heads.append((in_head, jnp.where(in_head, q, 0), jnp.where(in_head, dob, 0), total))

        def step(carry):
            kb, _, acc = carry[:3]
            cs = list(carry[3:3 + HEADS_PER_PAIR])
            nears = list(carry[3 + HEADS_PER_PAIR:])
            k0 = pl.multiple_of(kb * tk, tk)
            kt = k_ref[pl.ds(k0, tk), :]
            vt = v_ref[pl.ds(k0, tk), :]
            causal = _causal_tile(qi, tq, k0, tk)
            dk_t = jnp.zeros((tk, PAIR), F32)
            dv_t = jnp.zeros((tk, PAIR), F32)
            for h, (in_head, qh, doh, total) in enumerate(heads):
                w, lb, lom = _stick_tile(qh, kt, causal, cs[h], upper)
                wb = w.astype(BF16)
                gw = _mm_nt(doh, vt) * wb.astype(F32)
                hi, lo = _split_bf16(gw)
                far = total - nears[h] - _mm(hi, upper_eq[...]) - _mm(lo, upper_eq[...])
                beta = jnp.exp(lb)
                dxb = jnp.where(causal, gw - beta * (gw + far), 0.0).astype(BF16)
                acc = acc + _mm(dxb, jnp.where(in_head, kt, 0))
                dk_t = dk_t + _mm_tn(dxb, qh)
                dv_t = dv_t + _mm_tn(wb, doh)
                cs[h] = cs[h] + jnp.sum(lom, axis=1, keepdims=True)
                nears[h] = nears[h] + jnp.sum(gw, axis=1, keepdims=True)
            dk_acc[pl.ds(k0, tk), :] += dk_t
            dv_acc[pl.ds(k0, tk), :] += dv_t
            return (kb - 1, _sticks_alive(cs), acc) + tuple(cs) + tuple(nears)

        init = [jnp.zeros((tq, PAIR), F32)] + [jnp.zeros((tq, 1), F32)] * (2 * HEADS_PER_PAIR)
        dq_ref[...] = _walk(step, qi, tq, tk, init)[0]

        @pl.when(qi == nq - 1)
        def _():
            ck = pltpu.make_async_copy(dk_acc, dk_hbm.at[hp], sems.at[0])
            cv = pltpu.make_async_copy(dv_acc, dv_hbm.at[hp], sems.at[1])
            ck.start()
            cv.start()
            ck.wait()
            cv.wait()

    blk = pl.BlockSpec((tq, PAIR), lambda hp, qi: (qi, hp))
    return _call(
        body, (qkv, qkv, qkv, do, o, _triangle(tk, -1), _triangle(tk, 0)), ride, name="attn_bwd", grid=(n_pairs, nq),
        in_specs=[blk, pl.BlockSpec((s, PAIR), lambda hp, qi: (0, 2 + hp)),
                  pl.BlockSpec((s, PAIR), lambda hp, qi: (0, 4 + hp)), blk, blk, _resident((tk, tk)), _resident((tk, tk))],
        out_specs=[blk, ANY_SPEC, ANY_SPEC],
        out_shape=[_sds((s, D_GROUP)), _sds((n_pairs, s, PAIR)), _sds((n_pairs, s, PAIR))],
        scratch_shapes=[pltpu.VMEM((s, PAIR), F32), pltpu.VMEM((s, PAIR), F32), pltpu.SemaphoreType.DMA((2,))],
        compiler_params=_params(("arbitrary", "arbitrary"), 56))


def _mix_bwd(z, conv, dya, dyb, dyd, dq, dk, dv, hb, vg, wm, wmt, bexp, scw, ccw, lng, lnb, tb, ride=None):
    s = z.shape[0]
    n_steps = s // tb
    prev_spec, next_spec = _halo_specs(s, tb, D_IN)
    _, next_grp = _halo_specs(s, tb, D_GROUP)
    ext = tb + HALO

    def body(z_ref, zp_ref, zn_ref, dya_ref, dyb_ref, dybn_ref, dyd_ref, dydn_ref, c_ref, cn_ref, dq_ref, dk0_ref, dk1_ref,
             dv0_ref, dv1_ref, hbp_ref, hb_ref, vg_ref, wm_ref, wmt_ref, bexp_ref, scw_ref, ccw_ref, lng_ref, lnb_ref,
             dz_ref, dvg_ref, dws_ref, dbs_ref, dscw_ref, dccw_ref, dlng_ref, dlnb_ref, dwin_hbm,
             pbuf, hbuf, gbuf, cbuf, dubuf, dvnbuf, copies, tap_sums, win_acc, dz_prev, win_sem):
        i = pl.program_id(0)

        @pl.when(i == 0)
        def _():
            for ref in (dvg_ref, dws_ref, dbs_ref, dscw_ref, dccw_ref, dlng_ref, dlnb_ref, win_acc, dz_prev):
                ref[...] = jnp.zeros_like(ref)

        win_acc[...] += _mm_tn(dz_prev[...], hbp_ref[...])

        keep_prev = (i > 0).astype(F32)
        keep_next = (i < n_steps - 1).astype(F32)
        lane_h = _head_lane(D_GROUP)

        za = z_ref[:, 0:2 * D_GROUP]
        ga = _gelu(za)
        u, v = ga[:, :D_GROUP], ga[:, D_GROUP:]
        r = _rms(v)
        vh = v * r
        vn = (vh * vg_ref[...]).astype(BF16)
        tril = lax.broadcasted_iota(jnp.int32, (CHUNK, CHUNK), 0) >= lax.broadcasted_iota(jnp.int32, (CHUNK, CHUNK), 1)
        dbias = jnp.zeros((CHUNK, D_GROUP), F32)
        for n in range(tb // CHUNK):
            rows = slice(n * CHUNK, (n + 1) * CHUNK)
            vc = vn[rows]
            dy = dya_ref[rows, :]
            dubuf[rows, :] = dy * _gating_chunk(wm_ref, bexp_ref, vc, lane_h)
            df = dy * u[rows]
            dfb = df.astype(BF16)
            dvn = jnp.zeros((CHUNK, D_GROUP), F32)
            for h in range(D_GROUP // HEAD_DIM):
                dfh = jnp.where(lane_h == h, dfb, 0)
                dvn = dvn + _mm(wmt_ref[h], dfh)
                dws_ref[h] += jnp.where(tril, _mm_nt(dfh, vc), 0.0)
            dvnbuf[rows, :] = dvn
            dbias = dbias + df
        for h in range(D_GROUP // HEAD_DIM):
            per_head = jnp.sum(jnp.where(lane_h == h, dbias, 0.0), axis=1, keepdims=True)
            dbs_ref[...] += per_head * (lax.broadcasted_iota(jnp.int32, (1, CHUNK), 1) == h).astype(F32)
        dvn = dvnbuf[...]
        dvg_ref[...] += jnp.sum(dvn * vh, axis=0, keepdims=True)
        dgelu = _gelu_grad(za)
        dz_ref[:, 0:D_GROUP] = (dubuf[...] * dgelu[:, :D_GROUP]).astype(BF16)
        dz_ref[:, D_GROUP:2 * D_GROUP] = (_rms_bwd(dvn * vg_ref[...], vh, r) * dgelu[:, D_GROUP:]).astype(BF16)

        gate_b = z_ref[:, 2 * D_GROUP:3 * D_GROUP]
        gate_c = z_ref[:, 3 * D_GROUP:4 * D_GROUP]
        hh = z_ref[:, 4 * D_GROUP:5 * D_GROUP]
        p = gate_c * hh
        pbuf[0:HALO, :] = zp_ref[:, 3 * D_GROUP:4 * D_GROUP] * zp_ref[:, 4 * D_GROUP:5 * D_GROUP] * keep_prev
        pbuf[HALO:HALO + tb, :] = p
        dyb_v = dyb_ref[...]
        dcv = dyb_v * gate_b
        gbuf[0:tb, :] = dcv
        gbuf[tb:ext, :] = dybn_ref[...] * zn_ref[:, 2 * D_GROUP:3 * D_GROUP] * keep_next
        cv = scw_ref[K_SHORT - 1:K_SHORT, :] * p
        dp = scw_ref[K_SHORT - 1:K_SHORT, :] * dcv
        dscw_ref[K_SHORT - 1:K_SHORT, :] += jnp.sum(dcv * p, axis=0, keepdims=True)
        for k in range(K_SHORT - 1):
            earlier = pbuf[pl.ds(HALO - (K_SHORT - 1) + k, tb), :]
            cv = cv + scw_ref[k:k + 1, :] * earlier
            dp = dp + scw_ref[k:k + 1, :] * gbuf[pl.ds(K_SHORT - 1 - k, tb), :]
            dscw_ref[k:k + 1, :] += jnp.sum(dcv * earlier, axis=0, keepdims=True)
        dz_ref[:, 2 * D_GROUP:3 * D_GROUP] = (dyb_v * cv).astype(BF16)
        dz_ref[:, 3 * D_GROUP:4 * D_GROUP] = (dp * hh).astype(BF16)
        dz_ref[:, 4 * D_GROUP:5 * D_GROUP] = (dp * gate_c).astype(BF16)

        dz_ref[:, 5 * D_GROUP:6 * D_GROUP] = (dq_ref[...] * ATT_SCALE).astype(BF16)
        dz_ref[:, 6 * D_GROUP:6 * D_GROUP + PAIR] = dk0_ref[...].astype(BF16)
        dz_ref[:, 6 * D_GROUP + PAIR:7 * D_GROUP] = dk1_ref[...].astype(BF16)
        dz_ref[:, 7 * D_GROUP:7 * D_GROUP + PAIR] = dv0_ref[...].astype(BF16)
        dz_ref[:, 7 * D_GROUP + PAIR:8 * D_GROUP] = dv1_ref[...].astype(BF16)

        a = z_ref[:, 8 * D_GROUP:9 * D_GROUP]
        sg = _sigmoid(z_ref[:, 9 * D_GROUP:10 * D_GROUP])
        hbuf[0:HALO, :] = zp_ref[:, 8 * D_GROUP:9 * D_GROUP] * _sigmoid(zp_ref[:, 9 * D_GROUP:10 * D_GROUP]) * keep_prev
        hbuf[HALO:HALO + tb, :] = a * sg
        c = jnp.concatenate([c_ref[...], cn_ref[...]], axis=0)
        xc = c - jnp.mean(c, axis=-1, keepdims=True)
        rs = lax.rsqrt(jnp.mean(xc * xc, axis=-1, keepdims=True) + EPS)
        xh = xc * rs
        ln = xh * lng_ref[...] + lnb_ref[...]
        sl = _sigmoid(ln)
        dy_ext = jnp.concatenate([dyd_ref[...], dydn_ref[...] * keep_next], axis=0)
        dln = dy_ext * sl * (1.0 + ln * (1.0 - sl))
        dlng_ref[...] += jnp.sum(dln[:tb] * xh[:tb], axis=0, keepdims=True)
        dlnb_ref[...] += jnp.sum(dln[:tb], axis=0, keepdims=True)
        dxh = dln * lng_ref[...]
        dc = rs * (dxh - jnp.mean(dxh, axis=-1, keepdims=True) - xh * jnp.mean(dxh * xh, axis=-1, keepdims=True))
        cbuf[...] = dc
        where = _shifted_copies(cbuf, copies, 0, K_CONF, tb)
        _conv_chunks(copies, where, tb, lambda j: ccw_ref[K_CONF - 1 - j:K_CONF - j, :], dubuf)
        dhd = dubuf[...]
        tap_sums[...] = jnp.zeros_like(tap_sums)
        where = _shifted_copies(hbuf, copies, HALO - (K_CONF - 1), K_CONF, tb)
        _corr_chunks(copies, where, tb, cbuf, tap_sums)
        for k in range(K_CONF):
            dccw_ref[k:k + 1, :] += jnp.sum(tap_sums[k], axis=0, keepdims=True)
        dz_ref[:, 8 * D_GROUP:9 * D_GROUP] = (dhd * sg).astype(BF16)
        dz_ref[:, 9 * D_GROUP:10 * D_GROUP] = (dhd * a * sg * (1.0 - sg)).astype(BF16)

        dz_prev[...] = dz_ref[...]

        @pl.when(i == n_steps - 1)
        def _():
            win_acc[...] += _mm_tn(dz_ref[...], hb_ref[...])
            cp = pltpu.make_async_copy(win_acc, dwin_hbm, win_sem.at[0])
            cp.start()
            cp.wait()

    grp = _rows(tb, D_GROUP)
    pair0 = pl.BlockSpec((None, tb, PAIR), lambda i: (0, i, 0))
    pair1 = pl.BlockSpec((None, tb, PAIR), lambda i: (1, i, 0))
    small = [_sds((1, D_GROUP)), _sds((4, CHUNK, CHUNK)), _sds((CHUNK, CHUNK)), _sds((8, D_GROUP)),
             _sds((HALO, D_GROUP)), _sds((1, D_GROUP)), _sds((1, D_GROUP))]
    return _call(
        body, (z, z, z, dya, dyb, dyb, dyd, dyd, conv, conv, dq, dk, dk, dv, dv, hb, hb, vg, wm, wmt, bexp, scw, ccw, lng, lnb),
        ride, name="mix_bwd", grid=(n_steps,),
        in_specs=[_rows(tb, D_IN), prev_spec, next_spec, grp, grp, next_grp, grp, next_grp, grp, next_grp, grp,
                  pair0, pair1, pair0, pair1,
                  pl.BlockSpec((tb, D_MODEL), lambda i: (jnp.maximum(i - 1, 0), 0)), _rows(tb, D_MODEL),
                  _whole((1, D_GROUP)), _whole(wm.shape), _whole(wmt.shape), _whole(bexp.shape), _whole(scw.shape),
                  _whole(ccw.shape), _whole((1, D_GROUP)), _whole((1, D_GROUP))],
        out_specs=[_rows(tb, D_IN)] + [_whole(t.shape) for t in small] + [ANY_SPEC],
        out_shape=[_sds((s, D_IN), BF16)] + small + [_sds((D_IN, D_MODEL))],
        scratch_shapes=[pltpu.VMEM((HALO + tb, D_GROUP), F32), pltpu.VMEM((HALO + tb, D_GROUP), F32),
                        pltpu.VMEM((ext, D_GROUP), F32), pltpu.VMEM((ext, D_GROUP), F32),
                        pltpu.VMEM((tb, D_GROUP), F32), pltpu.VMEM((tb, D_GROUP), F32),
                        pltpu.VMEM((SUBLANES, tb + TAP_SLACK, D_GROUP), F32),
                        pltpu.VMEM((HALO, SUBLANES, D_GROUP), F32), pltpu.VMEM((D_IN, D_MODEL), F32),
                        pltpu.VMEM((tb, D_IN), BF16), pltpu.SemaphoreType.DMA((1,))],
        compiler_params=_params(("arbitrary",), 56))


def _in_proj_bwd(x, dz, dres, g, wint, tb, ride=None):
    s = x.shape[0]

    def body(x_ref, dz_ref, dr_ref, g_ref, w_ref, dx_ref, dg_ref):
        @pl.when(pl.program_id(0) == 0)
        def _():
            dg_ref[...] = jnp.zeros_like(dg_ref)

        xv = x_ref[...]
        r = _rms(xv)
        xh = xv * r
        dh = _mm(dz_ref[...], w_ref[...])
        dg_ref[...] += jnp.sum(dh * xh, axis=0, keepdims=True)
        dx_ref[...] = dr_ref[...] + _rms_bwd(dh * g_ref[...], xh, r)

    return _call(
        body, (x, dz, dres, g, wint), ride, name="in_proj_bwd", grid=(s // tb,),
        in_specs=[_rows(tb, D_MODEL), _rows(tb, D_IN), _rows(tb, D_MODEL), _whole((1, D_MODEL)), _resident((D_IN, D_MODEL))],
        out_specs=[_rows(tb, D_MODEL), _whole((1, D_MODEL))],
        out_shape=[_sds((s, D_MODEL)), _sds((1, D_MODEL))],
        compiler_params=_params(("arbitrary",), 56))


def _adamw(w, g, m, v):
    m = ADAM_B1 * m + (1.0 - ADAM_B1) * g
    v = ADAM_B2 * v + (1.0 - ADAM_B2) * (g * g)
    m_hat = m / (1.0 - ADAM_B1 ** ADAM_STEP)
    v_hat = v / (1.0 - ADAM_B2 ** ADAM_STEP)
    delta = -ADAM_LR * (m_hat / (jnp.sqrt(v_hat) + ADAM_EPS) + ADAM_WD * w)
    return delta, m, v


def _reduce_adamw(parts, w, m, v, tb, name):
    rows, cols = w.shape

    def body(p_ref, w_ref, m_ref, v_ref, g_ref, d_ref, m2_ref, v2_ref):
        g = p_ref[0].astype(F32)
        for j in range(1, N_DEV):
            g = g + p_ref[j].astype(F32)
        g_ref[...] = g
        d_ref[...], m2_ref[...], v2_ref[...] = _adamw(w_ref[...], g, m_ref[...], v_ref[...])

    blk = _rows(tb, cols)
    return pl.pallas_call(
        body, name=name, grid=(rows // tb,),
        in_specs=[pl.BlockSpec((N_DEV, tb, cols), lambda i: (0, i, 0)), blk, blk, blk],
        out_specs=[blk] * 4, out_shape=[_sds((rows, cols))] * 4,
        compiler_params=_params(("parallel",), 32),
    )(parts, w, m, v)


def _reduce_adamw_layer(parts, w, m, v, layer, taken_over, tb, name):
    _, rows, cols = w.shape
    n_given = 4 if taken_over is None else 8

    def body(*refs):
        p_ref, w_ref, m_ref, v_ref = refs[:4]
        g_ref, d_ref, m2_ref, v2_ref = refs[n_given:]
        g = p_ref[0].astype(F32)
        for j in range(1, N_DEV):
            g = g + p_ref[j].astype(F32)
        g_ref[...] = g
        d_ref[...], m2_ref[...], v2_ref[...] = _adamw(w_ref[...], g, m_ref[...], v_ref[...])

    blk = pl.BlockSpec((None, tb, cols), lambda i: (layer, i, 0))
    return pl.pallas_call(
        body, name=name, grid=(rows // tb,),
        in_specs=[pl.BlockSpec((N_DEV, tb, cols), lambda i: (0, i, 0)), blk, blk, blk] + [ANY_SPEC] * (n_given - 4),
        out_specs=[blk] * 4, out_shape=[_sds(w.shape)] * 4,
        input_output_aliases={} if taken_over is None else {4 + k: k for k in range(4)},
        compiler_params=_params(("arbitrary",), 32),
    )(parts, w, m, v, *(taken_over or ()))


LANES = 128
PACK_ALIGN = 8 * LANES


def _pack(arrays):
    pieces = []
    for a in arrays:
        flat = a.reshape(-1)
        pieces.append(jnp.pad(flat, (0, -flat.shape[0] % PACK_ALIGN)).reshape(-1, LANES))
    return jnp.concatenate(pieces, axis=0)


def _unpack(packed, shapes):
    out, row = [], 0
    for shape in shapes:
        size = 1
        for dim in shape:
            size *= dim
        rows = -(-size // PACK_ALIGN) * 8
        out.append(packed[row:row + rows].reshape(-1)[:size].reshape(shape))
        row += rows
    return out


TB_PROJ = 512
TB_WIDE = 1024
TB_MIX_FWD = 1024
TB_MIX = 512
TB_TN = 1024
TQ = 256
TK = 256
ADAM_PARTS_BLOCK_BYTES = 6 * MIB
BF16_SUBLANES = 16


def _adam_rows(rows, cols):
    best = BF16_SUBLANES
    for tb in range(BF16_SUBLANES, rows + 1, BF16_SUBLANES):
        if rows % tb == 0 and N_DEV * tb * cols * 4 <= ADAM_PARTS_BLOCK_BYTES:
            best = tb
    return best


def kernel(x, norm_mix_g, w_in, gmlp_v_g, gmlp_w_s, gmlp_b_s, short_conv_w, conf_conv_w, conf_ln_g, conf_ln_b, mix_out_g, w_out, norm_ffn_g, w_up, w_down, final_norm_g, loss_target, m_norm_mix_g, m_w_in, m_gmlp_v_g, m_gmlp_w_s, m_gmlp_b_s, m_short_conv_w, m_conf_conv_w, m_conf_ln_g, m_conf_ln_b, m_mix_out_g, m_w_out, m_norm_ffn_g, m_w_up, m_w_down, m_final_norm_g, v_norm_mix_g, v_w_in, v_gmlp_v_g, v_gmlp_w_s, v_gmlp_b_s, v_short_conv_w, v_conf_conv_w, v_conf_ln_g, v_conf_ln_b, v_mix_out_g, v_w_out, v_norm_ffn_g, v_w_up, v_w_down, v_final_norm_g):
    me = 4 * lax.axis_index("x") + 2 * lax.axis_index("y") + lax.axis_index("c")
    x0, target = x[0], loss_target[0]
    s = x0.shape[0]
    tb_proj, tb_mix, tb_tn, tb_wide = min(TB_PROJ, s), min(TB_MIX, s), min(TB_TN, s), min(TB_WIDE, s)
    conv_cols = D_GROUP // N_DEV

    def pad_rows(a, rows):
        return jnp.pad(a, ((0, rows - a.shape[0]), (0, 0)))

    wint_loc = [w_in[l].T.astype(BF16) for l in range(N_LAYERS)]
    wout_loc = [w_out[l].astype(BF16) for l in range(N_LAYERS)]
    wup_loc = [w_up[l].astype(BF16) for l in range(N_LAYERS)]
    wdn_loc = [w_down[l].astype(BF16) for l in range(N_LAYERS)]
    conv_loc = jnp.concatenate([pad_rows(short_conv_w[l], 8) for l in range(N_LAYERS)]
                               + [pad_rows(conf_conv_w[l], HALO) for l in range(N_LAYERS)], axis=0)
    wint, wout, wup, wdn = [None] * N_LAYERS, [None] * N_LAYERS, [None] * N_LAYERS, [None] * N_LAYERS
    wint0, conv_all = _exchange([wint_loc[0], conv_loc], [GATHER, GATHER], "gather_first_weights")
    wint[0] = wint0.reshape(D_IN, D_MODEL)
    conv_full = conv_all.transpose(1, 0, 2).reshape(-1, D_GROUP)
    scw = [conv_full[8 * l:8 * (l + 1)] for l in range(N_LAYERS)]
    ccw = [conv_full[8 * N_LAYERS + HALO * l:8 * N_LAYERS + HALO * (l + 1)] for l in range(N_LAYERS)]

    tril = jnp.tril(jnp.ones((CHUNK, CHUNK), dtype=bool))
    wm = [jnp.where(tril, gmlp_w_s[l], 0.0).astype(BF16) for l in range(N_LAYERS)]
    wmt = [w.transpose(0, 2, 1) for w in wm]
    bexp = [jnp.repeat(gmlp_b_s[l].T, HEAD_DIM, axis=1) for l in range(N_LAYERS)]

    def row(vec):
        return vec.reshape(1, -1)

    saved = []
    xc = x0
    for l in range(N_LAYERS):
        first = l == 0
        (z, qkv, hb_in), moved = _in_proj_fwd(xc, row(norm_mix_g[l]), wint[l], tb_wide,
                                              ride=([wout_loc[0]], [GATHER]) if first else None)
        if first:
            wout[0] = moved[0].reshape(D_MODEL, D_MODEL)
        (ya, yb, yd, conv), moved = _mix_fwd(z, row(gmlp_v_g[l]), wm[l], bexp[l], scw[l], ccw[l], row(conf_ln_g[l]),
                                             row(conf_ln_b[l]), min(TB_MIX_FWD, s),
                                             ride=([wup_loc[0]], [GATHER]) if first else None)
        if first:
            wup[0] = moved[0]
        (yc,), moved = _attn_fwd(qkv, TQ, TK, ride=([wdn_loc[0], wint_loc[1]], [GATHER, GATHER]) if first else None)
        if first:
            wdn[0], wint[1] = moved[0], moved[1].reshape(D_IN, D_MODEL)
        ys = (ya, yb, yc, yd)
        x1 = _out_proj_fwd(ys, xc, row(mix_out_g[l]), wout[l], tb_wide)
        (x2, act), moved = _ffn_fwd(x1, row(norm_ffn_g[l]), wup[l], wdn[l], tb_proj,
                                    ride=([wout_loc[1], wup_loc[1], wdn_loc[1]], [GATHER] * 3) if first else None)
        saved.append((xc, z, qkv, ys, x1, act, hb_in, conv))
        xc = x2
        if first:
            wout[1], wup[1], wdn[1] = moved[0].reshape(D_MODEL, D_MODEL), moved[1], moved[2]
    dx, g_final, loss_part = _loss_head(xc, row(final_norm_g), target, tb_wide)

    parts = [None] * (4 * N_LAYERS)
    small_grads = [None] * N_LAYERS
    early_names = ["gmlp_v_g", "gmlp_w_s", "gmlp_b_s", "short_conv_w", "conf_conv_w", "conf_ln_g", "conf_ln_b",
                   "mix_out_g", "norm_ffn_g"]
    for l in reversed(range(N_LAYERS)):
        xin, z, qkv, ys, x1, act, hb_in, conv = saved[l]
        (dx1, hb_ffn, dpre, g_ffn), _ = _ffn_bwd(x1, act, dx, row(norm_ffn_g[l]), wup[l], wdn[l], tb_proj)
        grad_up = _tn_slabs(hb_ffn, dpre, tb_tn, False, "grad_w_up")
        grad_dn = _tn_slabs(dx, act, tb_proj, True, "grad_w_down", square_b=True)
        dya, dyb_mix, dyc, dyd, g_mixout, grad_out = _out_proj_bwd(dx1, ys, row(mix_out_g[l]), wout[l], tb_wide)
        grad_out = grad_out.reshape(N_DEV, D_MODEL // N_DEV, D_MODEL)
        (dq, dk, dv), moved = _attn_bwd(qkv, dyc, ys[2], TQ, TK, ride=([grad_up, grad_dn], [SCATTER] * 2))
        parts[4 * l + 2], parts[4 * l + 3] = moved
        (dz, g_vg, g_ws, g_bs, g_scw, g_ccw, g_lng, g_lnb, grad_in), moved = _mix_bwd(
            z, conv, dya, dyb_mix, dyd, dq, dk, dv, hb_in, row(gmlp_v_g[l]), wm[l], wmt[l], bexp[l], scw[l], ccw[l],
            row(conf_ln_g[l]), row(conf_ln_b[l]), tb_mix, ride=([grad_out], [SCATTER]))
        parts[4 * l + 1] = moved[0]
        small_grads[l] = dict(gmlp_v_g=g_vg[0], gmlp_w_s=g_ws, gmlp_b_s=g_bs[:, :4].T, short_conv_w=g_scw[:K_SHORT],
                              conf_conv_w=g_ccw[:K_CONF], conf_ln_g=g_lng[0], conf_ln_b=g_lnb[0],
                              mix_out_g=g_mixout[0], norm_ffn_g=g_ffn[0])
        riders, modes = [grad_in.reshape(N_DEV, D_IN // N_DEV, D_MODEL)], [SCATTER]
        if l == 0:
            early_list = [jnp.stack([small_grads[k][n] for k in range(N_LAYERS)]) for n in early_names] + [g_final[0]]
            riders, modes = [riders[0].astype(BF16), _pack(early_list).astype(BF16)], modes + [GATHER]
        (dx, g_mix), moved = _in_proj_bwd(xin, dz, dx1, row(norm_mix_g[l]), wint[l], tb_wide, ride=(riders, modes))
        parts[4 * l] = moved[0]
        small_grads[l]["norm_mix_g"] = g_mix[0]

    late_list = [jnp.stack([small_grads[l]["norm_mix_g"] for l in range(N_LAYERS)]), loss_part[0, :1]]
    late_parts = _exchange([_pack(late_list)], [GATHER], "gather_last_grad")[0]
    small_groups = [(early_names + ["final_norm_g"], early_list, moved[1]),
                    (["norm_mix_g", "loss"], late_list, late_parts)]

    given = dict(norm_mix_g=(norm_mix_g, m_norm_mix_g, v_norm_mix_g), gmlp_v_g=(gmlp_v_g, m_gmlp_v_g, v_gmlp_v_g),
                 gmlp_w_s=(gmlp_w_s, m_gmlp_w_s, v_gmlp_w_s), gmlp_b_s=(gmlp_b_s, m_gmlp_b_s, v_gmlp_b_s),
                 short_conv_w=(short_conv_w, m_short_conv_w, v_short_conv_w),
                 conf_conv_w=(conf_conv_w, m_conf_conv_w, v_conf_conv_w),
                 conf_ln_g=(conf_ln_g, m_conf_ln_g, v_conf_ln_g), conf_ln_b=(conf_ln_b, m_conf_ln_b, v_conf_ln_b),
                 mix_out_g=(mix_out_g, m_mix_out_g, v_mix_out_g), norm_ffn_g=(norm_ffn_g, m_norm_ffn_g, v_norm_ffn_g),
                 final_norm_g=(final_norm_g, m_final_norm_g, v_final_norm_g), loss=(jnp.zeros((1,), F32),) * 3)
    sharded_small = ("short_conv_w", "conf_conv_w")

    def widen(a):
        full = jnp.zeros(a.shape[:-1] + (D_GROUP,), a.dtype)
        return lax.dynamic_update_slice(full, a, (0, 0, me * conv_cols))

    small_res = {}
    for names, grads, gathered in small_groups:
        state = [_pack([widen(given[n][k]) if n in sharded_small else given[n][k] for n in names]) for k in range(3)]
        outs = _reduce_adamw(gathered, *state, state[0].shape[0], "adamw_small")
        for kind, packed in zip(("grad", "delta", "new_m", "new_v"), outs):
            for n, val in zip(names, _unpack(packed, [a.shape for a in grads])):
                if n in sharded_small:
                    val = lax.dynamic_slice(val, (0, 0, me * conv_cols), val.shape[:-1] + (conv_cols,))
                small_res[kind, n] = val

    big_names = ["w_in", "w_out", "w_up", "w_down"]
    big_given = dict(w_in=[t.transpose(0, 2, 1) for t in (w_in, m_w_in, v_w_in)], w_out=(w_out, m_w_out, v_w_out),
                     w_up=(w_up, m_w_up, v_w_up), w_down=(w_down, m_w_down, v_w_down))
    big_res = {}
    for j, n in enumerate(big_names):
        outs = None
        for l in range(N_LAYERS):
            outs = _reduce_adamw_layer(parts[4 * l + j], *big_given[n], l, outs,
                                       _adam_rows(*big_given[n][0].shape[1:]), "adamw_" + n)
        for kind, out in zip(("grad", "delta", "new_m", "new_v"), outs):
            big_res[kind, n] = out.transpose(0, 2, 1) if n == "w_in" else out

    order = ["norm_mix_g", "w_in", "gmlp_v_g", "gmlp_w_s", "gmlp_b_s", "short_conv_w", "conf_conv_w", "conf_ln_g",
             "conf_ln_b", "mix_out_g", "w_out", "norm_ffn_g", "w_up", "w_down", "final_norm_g"]
    result = [small_res["grad", "loss"][0], dx.reshape(x.shape)]
    for kind in ("grad", "delta", "new_m", "new_v"):
        for n in order:
            result.append(big_res[kind, n] if n in big_given else small_res[kind, n])
    return tuple(result)
```
